```python
import jax, jax.numpy as jnp
from jax import lax
import numpy as np

D_MODEL = 2048
BATCH = 4
SEQ = 2048
DEPTH = 2
DEC_BATCH = 8
DEC_SEQ = 8
PAST_LEN = 16384
PAGE_SIZE = 128

HEAD_DIM = 128
NSA_WIDTH = D_MODEL // 2
GM_WIDTH = D_MODEL // 4
CONV_WIDTH = D_MODEL - NSA_WIDTH - GM_WIDTH
NSA_HEADS = NSA_WIDTH // HEAD_DIM
NSA_KV_HEADS = 2
GQA_R = NSA_HEADS // NSA_KV_HEADS
KV_WIDTH = NSA_KV_HEADS * HEAD_DIM
GM_GROUPS = 4
CONV_GROUPS = 4
CMP_BLOCK = 32
SEL_BLOCK = 64
N_SELECT = 16
WINDOW = 512
FORCED_SCORE = 1.0e4
SEL_QBLOCK = 64
WIN_QBLOCK = 128
GM_CHUNK = 128
CONV_K = 31
PEER_HEADS = 8
N_KEYS = 128
N_EXPERTS = N_KEYS * N_KEYS
PEER_TOPK = 16
SUBKEY_DIM = 128
PEER_QDIM = 2 * SUBKEY_DIM
PEER_TOKEN_BLOCK = 128
IN_WIDTH = 2 * GM_WIDTH + NSA_WIDTH + 6 * KV_WIDTH + 3 * NSA_HEADS + 2 * CONV_WIDTH
ALPHA = (2 * DEPTH) ** 0.25
BETA = (8 * DEPTH) ** -0.25
ROPE_THETA = 10000.0
LN_EPS = 1e-5

kernel_name = "hymba_gmlp_nsa_conformer_peer_step"


def _ln(x, g, b):
    xf = x.astype(jnp.float32)
    mu = xf.mean(-1, keepdims=True)
    var = jnp.square(xf - mu).mean(-1, keepdims=True)
    return ((xf - mu) * lax.rsqrt(var + LN_EPS)).astype(x.dtype) * g + b


def _rope(x, pos):
    half = HEAD_DIM // 2
    inv = ROPE_THETA ** (-jnp.arange(half, dtype=jnp.float32) / half)
    ang = pos.astype(jnp.float32)[:, None] * inv[None, :]
    cos, sin = jnp.cos(ang)[:, None, :], jnp.sin(ang)[:, None, :]
    x1, x2 = x[..., :half].astype(jnp.float32), x[..., half:].astype(jnp.float32)
    return jnp.concatenate([x1 * cos - x2 * sin, x2 * cos + x1 * sin], -1).astype(x.dtype)


def _masked_softmax(s, mask):
    s = jnp.where(mask, s.astype(jnp.float32), -jnp.inf)
    m = jnp.max(s, -1, keepdims=True)
    m = jnp.where(jnp.isfinite(m), m, 0.0)
    e = jnp.exp(s - m)
    return e / jnp.maximum(e.sum(-1, keepdims=True), 1e-30)


def _attend(q, k, v, mask):
    s = jnp.einsum("...tgrd,...lgd->...tgrl", q, k).astype(jnp.float32) * HEAD_DIM ** -0.5
    p = _masked_softmax(s, mask[..., :, None, None, :])
    return jnp.einsum("...tgrl,...lgd->...tgrd", p.astype(v.dtype), v), p


def _sel_attend(q, kb, vb, sel, qpos):
    B, T = q.shape[0], q.shape[1]
    bi = jnp.arange(B)[:, None, None, None]
    gi = jnp.arange(NSA_KV_HEADS)[None, None, :, None]
    kg, vg = kb[bi, gi, sel], vb[bi, gi, sel]
    nk = sel.shape[-1] * SEL_BLOCK
    s = jnp.einsum("btgrd,btgksd->btgrks", q, kg).reshape(B, T, NSA_KV_HEADS, GQA_R, nk)
    s = s.astype(jnp.float32) * HEAD_DIM ** -0.5
    kpos = sel[..., None] * SEL_BLOCK + jnp.arange(SEL_BLOCK)
    mask = (kpos <= qpos[None, :, None, None, None]).reshape(B, T, NSA_KV_HEADS, 1, nk)
    p = _masked_softmax(s, mask)
    return jnp.einsum("btgrn,btgnd->btgrd", p.astype(vg.dtype), vg.reshape(B, T, NSA_KV_HEADS, nk, HEAD_DIM))


def _nsa_cmp_slc(q, kc, vc, ks, vs, qpos, cmp_wk, cmp_wv):
    B, T, L = q.shape[0], q.shape[1], kc.shape[1]
    n_cmp = L // CMP_BLOCK

    def compress(t, w):
        tb = t[:, : n_cmp * CMP_BLOCK].reshape(B, n_cmp, CMP_BLOCK, NSA_KV_HEADS, HEAD_DIM)
        return jnp.einsum("bnjgd,gj->bngd", tb, w)

    cmp_last = (jnp.arange(n_cmp) + 1) * CMP_BLOCK - 1
    o_cmp, p_cmp = _attend(q, compress(kc, cmp_wk), compress(vc, cmp_wv), cmp_last[None, :] <= qpos[:, None])
    ratio = SEL_BLOCK // CMP_BLOCK
    n_slc = -(-L // SEL_BLOCK)
    imp = p_cmp.sum(axis=3)
    imp = jnp.pad(imp, ((0, 0), (0, 0), (0, 0), (0, n_slc * ratio - n_cmp)))
    imp = imp.reshape(B, T, NSA_KV_HEADS, n_slc, ratio).sum(-1)
    blk = jnp.arange(n_slc)[None, :]
    cur = (qpos // SEL_BLOCK)[:, None]
    valid = blk * SEL_BLOCK <= qpos[:, None]
    forced = (blk == 0) | (blk == cur) | (blk == cur - 1)
    score = jnp.where(valid[None, :, None, :], jnp.where(forced[None, :, None, :], FORCED_SCORE, imp), -jnp.inf)
    _, sel = lax.top_k(score, min(N_SELECT, n_slc))
    pad_len = n_slc * SEL_BLOCK - L

    def blocks(t):
        t = jnp.pad(t, ((0, 0), (0, pad_len), (0, 0), (0, 0)))
        return t.reshape(B, n_slc, SEL_BLOCK, NSA_KV_HEADS, HEAD_DIM).transpose(0, 3, 1, 2, 4)

    kb, vb = blocks(ks), blocks(vs)
    if T > SEL_QBLOCK and T % SEL_QBLOCK == 0:
        nq = T // SEL_QBLOCK
        qb = q.reshape(B, nq, SEL_QBLOCK, NSA_KV_HEADS, GQA_R, HEAD_DIM).swapaxes(0, 1)
        sb = sel.reshape(B, nq, SEL_QBLOCK, NSA_KV_HEADS, sel.shape[-1]).swapaxes(0, 1)
        pb = qpos.reshape(nq, SEL_QBLOCK)
        o = lax.map(lambda a: _sel_attend(a[0], kb, vb, a[1], a[2]), (qb, sb, pb))
        o_slc = o.swapaxes(0, 1).reshape(B, T, NSA_KV_HEADS, GQA_R, HEAD_DIM)
    else:
        o_slc = _sel_attend(q, kb, vb, sel, qpos)
    return o_cmp, o_slc


def _window_prompt(q, k, v):
    B, S = q.shape[0], q.shape[1]
    nb = S // WIN_QBLOCK
    span = WINDOW + WIN_QBLOCK
    idx = jnp.arange(nb)[:, None] * WIN_QBLOCK + jnp.arange(span)[None, :]
    pad = ((0, 0), (WINDOW, 0), (0, 0), (0, 0))
    kb, vb = jnp.pad(k, pad)[:, idx], jnp.pad(v, pad)[:, idx]
    qb = q.reshape(B, nb, WIN_QBLOCK, NSA_KV_HEADS, GQA_R, HEAD_DIM)
    kpos = (idx - WINDOW)[:, None, :]
    qp = (jnp.arange(nb)[:, None] * WIN_QBLOCK + jnp.arange(WIN_QBLOCK)[None, :])[:, :, None]
    mask = (kpos <= qp) & (kpos >= qp - WINDOW) & (kpos >= 0)
    o, _ = _attend(qb, kb, vb, mask)
    return o.reshape(B, S, NSA_KV_HEADS, GQA_R, HEAD_DIM)


def _gmlp(u, v, g, b, ws, bs):
    B, T, _ = u.shape
    gw = GM_WIDTH // GM_GROUPS
    vn = _ln(v.reshape(B, T, GM_GROUPS, gw), g.reshape(GM_GROUPS, gw), b.reshape(GM_GROUPS, gw))
    C = min(T, GM_CHUNK)
    w = jnp.tril(ws[:, :C, :C])
    s = jnp.einsum("hij,bnjhc->bnihc", w, vn.reshape(B, T // C, C, GM_GROUPS, gw)) + bs[:, :C].T[:, :, None]
    vn = vn.reshape(B, T, GM_WIDTH)
    start = ((T - 1) // GM_CHUNK) * GM_CHUNK
    return u * s.reshape(B, T, GM_WIDTH), vn[:, start:]


def _token_mix(x, qpos, past, w_in, gm_g, gm_b, gm_ws, gm_bs, cmp_wk, cmp_wv,
               conv_dw, conv_db, conv_g, conv_b, conv_pw, w_out):
    B, T, _ = x.shape
    sizes = [GM_WIDTH, GM_WIDTH, NSA_WIDTH] + [KV_WIDTH] * 6 + [3 * NSA_HEADS, CONV_WIDTH, CONV_WIDTH]
    cuts = np.cumsum(sizes)[:-1].tolist()
    gu, gv, q, kc, vc, ks, vs, kw, vw, gate, ca, cb = jnp.split(x @ w_in, cuts, axis=-1)
    gm_out, gm_state = _gmlp(jax.nn.gelu(gu), jax.nn.gelu(gv), gm_g, gm_b, gm_ws, gm_bs)
    kvh = lambda t: t.reshape(B, T, NSA_KV_HEADS, HEAD_DIM)
    q = _rope(q.reshape(B, T, NSA_HEADS, HEAD_DIM), qpos).reshape(B, T, NSA_KV_HEADS, GQA_R, HEAD_DIM)
    kc, ks, kw = _rope(kvh(kc), qpos), _rope(kvh(ks), qpos), _rope(kvh(kw), qpos)
    vc, vs, vw = kvh(vc), kvh(vs), kvh(vw)
    if past is None:
        o_cmp, o_slc = _nsa_cmp_slc(q, kc, vc, ks, vs, qpos, cmp_wk, cmp_wv)
        o_win = _window_prompt(q, kw, vw)
        keep = min(WINDOW, T)
        win_k, win_v = kw[:, -keep:], vw[:, -keep:]
        hist = jnp.zeros((B, CONV_K - 1, CONV_WIDTH), x.dtype)
    else:
        kc_p, vc_p, ks_p, vs_p, wk_buf, wv_buf, hist = past
        P, WB = kc_p.shape[1], wk_buf.shape[1]
        cat = lambda a, b_: jnp.concatenate([a, b_], axis=1)
        o_cmp, o_slc = _nsa_cmp_slc(q, cat(kc_p, kc), cat(vc_p, vc), cat(ks_p, ks), cat(vs_p, vs), qpos, cmp_wk, cmp_wv)
        wk_all, wv_all = cat(wk_buf, kw), cat(wv_buf, vw)
        kpos = P - WB + jnp.arange(WB + T)
        mask = (kpos[None, :] <= qpos[:, None]) & (kpos[None, :] >= qpos[:, None] - WINDOW)
        o_win, _ = _attend(q, wk_all, wv_all, mask)
        keep = min(WINDOW, P + T)
        win_k, win_v = wk_all[:, -keep:], wv_all[:, -keep:]
    g = jax.nn.sigmoid(gate.astype(jnp.float32)).astype(x.dtype).reshape(B, T, NSA_KV_HEADS, GQA_R, 3, 1)
    o_nsa = (g[..., 0, :] * o_cmp + g[..., 1, :] * o_slc + g[..., 2, :] * o_win).reshape(B, T, NSA_WIDTH)
    c = ca * jax.nn.sigmoid(cb)
    cseq = jnp.concatenate([hist, c], axis=1)
    y = lax.conv_general_dilated(cseq, conv_dw[:, None, :], window_strides=(1,), padding="VALID",
                                 dimension_numbers=("NWC", "WIO", "NWC"), feature_group_count=CONV_WIDTH) + conv_db
    cw = CONV_WIDTH // CONV_GROUPS
    y = _ln(y.reshape(B, T, CONV_GROUPS, cw), conv_g.reshape(CONV_GROUPS, cw), conv_b.reshape(CONV_GROUPS, cw))
    y = jax.nn.silu(y.reshape(B, T, CONV_WIDTH)) @ conv_pw
    conv_state = cseq[:, -(CONV_K - 1):]
    mix = jnp.concatenate([gm_out, o_nsa, y], axis=-1) @ w_out
    return mix, (kc, vc, ks, vs, win_k, win_v, conv_state, gm_state)


def _peer_tokens(x, wq, subkeys, u_tab, v_tab):
    n = x.shape[0]
    q = (x @ wq).reshape(n, PEER_HEADS, 2, SUBKEY_DIM)
    s = jnp.einsum("nhpd,hpkd->nhpk", q, subkeys).astype(jnp.float32)
    s1, i1 = lax.top_k(s[:, :, 0], PEER_TOPK)
    s2, i2 = lax.top_k(s[:, :, 1], PEER_TOPK)
    ncand = PEER_TOPK * PEER_TOPK
    cand = (s1[..., :, None] + s2[..., None, :]).reshape(n, PEER_HEADS, ncand)
    cidx = (i1[..., :, None] * N_KEYS + i2[..., None, :]).reshape(n, PEER_HEADS, ncand)
    top, pos = lax.top_k(cand, PEER_TOPK)
    experts = jnp.take_along_axis(cidx, pos, axis=-1)
    gate = jax.nn.softmax(top, axis=-1)
    act = jax.nn.gelu(jnp.einsum("nhkd,nd->nhk", u_tab[experts], x))
    w = (gate * act.astype(jnp.float32)).astype(x.dtype)
    return jnp.einsum("nhk,nhkd->nd", w, v_tab[experts])


def _peer(x, wq, subkeys, u_tab, v_tab):
    B, T, D = x.shape
    n = B * T
    xf = x.reshape(n, D)
    if n > PEER_TOKEN_BLOCK and n % PEER_TOKEN_BLOCK == 0:
        y = lax.map(lambda xb: _peer_tokens(xb, wq, subkeys, u_tab, v_tab),
                    xf.reshape(n // PEER_TOKEN_BLOCK, PEER_TOKEN_BLOCK, D)).reshape(n, D)
    else:
        y = _peer_tokens(xf, wq, subkeys, u_tab, v_tab)
    return y.reshape(B, T, D)


def _gather_pages(cache_l, page_table):
    g = cache_l[page_table]
    return g.reshape(g.shape[0], g.shape[1] * g.shape[2], g.shape[3], g.shape[4])


def setup_inputs(seed: int = 0) -> dict:
    key = jax.random.key(seed)
    ks = iter(jax.random.split(key, 48))
    nrm = lambda shape, scale: jax.random.normal(next(ks), shape, jnp.float32) * scale
    gain = lambda shape: 1.0 + nrm(shape, 0.01)
    n_pages = PAST_LEN // PAGE_SIZE
    used = DEC_BATCH * n_pages
    n_pool = used + max(used // 4, 1)
    wbuf = min(WINDOW, PAST_LEN)
    kvshape = (DEPTH, n_pool, PAGE_SIZE, NSA_KV_HEADS, HEAD_DIM)
    wshape = (DEPTH, DEC_BATCH, wbuf, NSA_KV_HEADS, HEAD_DIM)
    page_table = jax.random.permutation(next(ks), n_pool)[:used].reshape(DEC_BATCH, n_pages).astype(jnp.int32)
    return {
        "x_prompt": nrm((BATCH, SEQ, D_MODEL), 1.0),
        "x_sample": nrm((DEC_BATCH, DEC_SEQ, D_MODEL), 1.0),
        "cache_cmp_k": nrm(kvshape, 1.0),
        "cache_cmp_v": nrm(kvshape, 1.0),
        "cache_slc_k": nrm(kvshape, 1.0),
        "cache_slc_v": nrm(kvshape, 1.0),
        "cache_win_k": nrm(wshape, 1.0),
        "cache_win_v": nrm(wshape, 1.0),
        "state_conv": nrm((DEPTH, DEC_BATCH, CONV_K - 1, CONV_WIDTH), 1.0),
        "page_table": page_table,
        "w_in": nrm((DEPTH, D_MODEL, IN_WIDTH), D_MODEL ** -0.5),
        "gm_ln_g": gain((DEPTH, GM_WIDTH)),
        "gm_ln_b": nrm((DEPTH, GM_WIDTH), 0.01),
        "gm_ws": nrm((DEPTH, GM_GROUPS, GM_CHUNK, GM_CHUNK), 0.5 * GM_CHUNK ** -0.5),
        "gm_bs": gain((DEPTH, GM_GROUPS, GM_CHUNK)),
        "cmp_wk": nrm((DEPTH, NSA_KV_HEADS, CMP_BLOCK), CMP_BLOCK ** -0.5),
        "cmp_wv": nrm((DEPTH, NSA_KV_HEADS, CMP_BLOCK), CMP_BLOCK ** -0.5),
        "conv_dw": nrm((DEPTH, CONV_K, CONV_WIDTH), CONV_K ** -0.5),
        "conv_db": nrm((DEPTH, CONV_WIDTH), 0.01),
        "conv_ln_g": gain((DEPTH, CONV_WIDTH)),
        "conv_ln_b": nrm((DEPTH, CONV_WIDTH), 0.01),
        "conv_pw": nrm((DEPTH, CONV_WIDTH, CONV_WIDTH), CONV_WIDTH ** -0.5),
        "w_out": nrm((DEPTH, D_MODEL, D_MODEL), BETA * D_MODEL ** -0.5),
        "ln1_g": gain((DEPTH, D_MODEL)),
        "ln1_b": nrm((DEPTH, D_MODEL), 0.01),
        "peer_wq": nrm((DEPTH, D_MODEL, PEER_HEADS * PEER_QDIM), D_MODEL ** -0.5),
        "peer_subkeys": nrm((DEPTH, PEER_HEADS, 2, N_KEYS, SUBKEY_DIM), SUBKEY_DIM ** -0.5),
        "peer_u": nrm((DEPTH, N_EXPERTS, D_MODEL), D_MODEL ** -0.5),
        "peer_v": nrm((DEPTH, N_EXPERTS, D_MODEL), BETA * PEER_HEADS ** -0.5),
        "ln2_g": gain((DEPTH, D_MODEL)),
        "ln2_b": nrm((DEPTH, D_MODEL), 0.01),
    }


def reference(x_prompt, x_sample, cache_cmp_k, cache_cmp_v, cache_slc_k, cache_slc_v,
              cache_win_k, cache_win_v, state_conv, page_table, w_in, gm_ln_g, gm_ln_b,
              gm_ws, gm_bs, cmp_wk, cmp_wv, conv_dw, conv_db, conv_ln_g, conv_ln_b, conv_pw,
              w_out, ln1_g, ln1_b, peer_wq, peer_subkeys, peer_u, peer_v, ln2_g, ln2_b):
    S, T = x_prompt.shape[1], x_sample.shape[1]
    P = page_table.shape[1] * cache_cmp_k.shape[2]
    pos_p = jnp.arange(S, dtype=jnp.int32)
    pos_s = P + jnp.arange(T, dtype=jnp.int32)
    xp, xs = x_prompt, x_sample
    p_states, s_states = [], []
    for l in range(DEPTH):
        lw = (w_in[l], gm_ln_g[l], gm_ln_b[l], gm_ws[l], gm_bs[l], cmp_wk[l], cmp_wv[l],
              conv_dw[l], conv_db[l], conv_ln_g[l], conv_ln_b[l], conv_pw[l], w_out[l])
        mp, sp = _token_mix(xp, pos_p, None, *lw)
        past = (_gather_pages(cache_cmp_k[l], page_table), _gather_pages(cache_cmp_v[l], page_table),
                _gather_pages(cache_slc_k[l], page_table), _gather_pages(cache_slc_v[l], page_table),
                cache_win_k[l], cache_win_v[l], state_conv[l])
        ms, ss = _token_mix(xs, pos_s, past, *lw)
        xp = _ln(ALPHA * xp + mp, ln1_g[l], ln1_b[l])
        xs = _ln(ALPHA * xs + ms, ln1_g[l], ln1_b[l])
        xp = _ln(ALPHA * xp + _peer(xp, peer_wq[l], peer_subkeys[l], peer_u[l], peer_v[l]), ln2_g[l], ln2_b[l])
        xs = _ln(ALPHA * xs + _peer(xs, peer_wq[l], peer_subkeys[l], peer_u[l], peer_v[l]), ln2_g[l], ln2_b[l])
        p_states.append(sp)
        s_states.append(ss)
    stack = lambda states, i: jnp.stack([st[i] for st in states], axis=0)
    p_cmp_k, p_cmp_v, p_slc_k, p_slc_v = stack(p_states, 0), stack(p_states, 1), stack(p_states, 2), stack(p_states, 3)
    p_win_k, p_win_v, p_conv, p_gm_v = stack(p_states, 4), stack(p_states, 5), stack(p_states, 6), stack(p_states, 7)
    s_cmp_k, s_cmp_v, s_slc_k, s_slc_v = stack(s_states, 0), stack(s_states, 1), stack(s_states, 2), stack(s_states, 3)
    s_win_k, s_win_v, s_conv, s_gm_v = stack(s_states, 4), stack(s_states, 5), stack(s_states, 6), stack(s_states, 7)
    return (xp, xs, p_cmp_k, p_cmp_v, p_slc_k, p_slc_v, p_win_k, p_win_v, p_conv, p_gm_v,
            s_cmp_k, s_cmp_v, s_slc_k, s_slc_v, s_win_k, s_win_v, s_conv, s_gm_v)
```

```python
import functools

import numpy as np
import jax
import jax.numpy as jnp
from jax import lax
from jax.experimental import pallas as pl
from jax.experimental.pallas import tpu as pltpu

F32 = jnp.float32
BF16 = jnp.bfloat16

D_MODEL = 2048
HEAD_DIM = 128
NSA_HEADS = 8
NSA_KV_HEADS = 2
GQA_R = NSA_HEADS // NSA_KV_HEADS
GM_WIDTH = 512
GM_GROUPS = 4
GM_CHUNK = 128
CONV_WIDTH = 512
CONV_GROUPS = 4
CONV_K = 31
NSA_WIDTH = NSA_HEADS * HEAD_DIM
KV_WIDTH = NSA_KV_HEADS * HEAD_DIM
CMP_BLOCK = 32
SEL_BLOCK = 64
N_SELECT = 16
WINDOW = 512
FORCED_SCORE = 1.0e4
PEER_HEADS = 8
N_KEYS = 128
PEER_TOPK = 16
SUBKEY_DIM = 128
ROPE_THETA = 10000.0
LN_EPS = 1e-5
DEPTH = 2
ALPHA = (2 * DEPTH) ** 0.25

C_GU, C_GV, C_CA, C_CB, C_Q = 0, 512, 1024, 1536, 2048
C_KC, C_KS, C_KW, C_VC, C_VS, C_VW, C_GATE = 3072, 3328, 3584, 3840, 4096, 4352, 4608
PROJ_WIDTH = 4864
PROJ_TN = 256
ROPE_LO, ROPE_HI = C_Q // PROJ_TN, C_VC // PROJ_TN

V7X_VMEM_LIMIT = 56 * 1024 * 1024
NEG_INF = float("-inf")


def _cparams(n_axes, vmem=V7X_VMEM_LIMIT):
    return pltpu.CompilerParams(dimension_semantics=("arbitrary",) * n_axes, vmem_limit_bytes=vmem)


def _ln_lanes(z, g, b):
    mu = jnp.mean(z, axis=-1, keepdims=True)
    d = z - mu
    var = jnp.mean(d * d, axis=-1, keepdims=True)
    return d * lax.rsqrt(var + LN_EPS) * g + b


def _dot_nt(a, b):
    return lax.dot_general(a, b, (((1,), (1,)), ((), ())), preferred_element_type=F32)


def _proj_kernel(x_ref, w_ref, cos_ref, sin_ref, o_ref, xb_ref, *, rope_lo, rope_hi, tn):
    j = pl.program_id(1)

    @pl.when(j == 0)
    def _():
        xb_ref[...] = x_ref[...].astype(BF16)

    acc = jnp.dot(xb_ref[...], w_ref[...], preferred_element_type=F32)
    is_rope = jnp.logical_and(j >= rope_lo, j < rope_hi)

    @pl.when(is_rope)
    def _():
        cos, sin = cos_ref[...], sin_ref[...]
        for c in range(tn // HEAD_DIM):
            blk = acc[:, c * HEAD_DIM:(c + 1) * HEAD_DIM]
            o_ref[:, c * HEAD_DIM:(c + 1) * HEAD_DIM] = blk * cos + pltpu.roll(blk, HEAD_DIM // 2, 1) * sin

    @pl.when(jnp.logical_not(is_rope))
    def _():
        o_ref[...] = acc


def _proj(x, w, cos, sin, *, tm, rope_lo, rope_hi, name):
    m, k = x.shape
    n = w.shape[1]
    tn = PROJ_TN
    nper = cos.shape[0] // tm
    kern = functools.partial(_proj_kernel, rope_lo=rope_lo, rope_hi=rope_hi, tn=tn)
    return pl.pallas_call(
        kern,
        grid=(m // tm, n // tn),
        in_specs=[
            pl.BlockSpec((tm, k), lambda i, j: (i, 0)),
            pl.BlockSpec((k, tn), lambda i, j: (0, j)),
            pl.BlockSpec((tm, HEAD_DIM), lambda i, j: (i % nper, 0)),
            pl.BlockSpec((tm, HEAD_DIM), lambda i, j: (i % nper, 0)),
        ],
        out_specs=pl.BlockSpec((tm, tn), lambda i, j: (i, j)),
        out_shape=jax.ShapeDtypeStruct((m, n), F32),
        scratch_shapes=[pltpu.VMEM((tm, k), BF16)],
        compiler_params=_cparams(2),
        name=name,
    )(x, w, cos, sin)


def _gmlp_kernel(gu_ref, gv_ref, g_ref, b_ref, w_ref, bs_ref, o_ref, vn_ref):
    gw = GM_WIDTH // GM_GROUPS
    for h in range(GM_GROUPS):
        sl = slice(h * gw, (h + 1) * gw)
        v = jax.nn.gelu(gv_ref[:, sl])
        vn = _ln_lanes(v, g_ref[:, sl], b_ref[:, sl])
        vn_ref[:, sl] = vn
        s = jnp.dot(w_ref[h], vn.astype(BF16), preferred_element_type=F32) + bs_ref[:, sl]
        o_ref[:, sl] = jax.nn.gelu(gu_ref[:, sl]) * s


def _gmlp(h, g, b, w_tril, bs_full, *, chunk):
    m = h.shape[0]
    blk = lambda c: pl.BlockSpec((chunk, GM_WIDTH), lambda i, c=c: (i, c))
    full = lambda a: pl.BlockSpec(a.shape, lambda i: (0,) * a.ndim)
    return pl.pallas_call(
        _gmlp_kernel,
        grid=(m // chunk,),
        in_specs=[blk(C_GU // GM_WIDTH), blk(C_GV // GM_WIDTH), full(g), full(b), full(w_tril), full(bs_full)],
        out_specs=[pl.BlockSpec((chunk, GM_WIDTH), lambda i: (i, 0))] * 2,
        out_shape=[jax.ShapeDtypeStruct((m, GM_WIDTH), F32)] * 2,
        compiler_params=_cparams(1),
        name="gmlp",
    )(h, h, g, b, w_tril, bs_full)


CONV_HALO = 32


def _conv_kernel(ca_ref, cb_ref, cah_ref, cbh_ref, hist_ref, dw_ref, db_ref, g_ref, b_ref, pw_ref,
                 y_ref, c_ref, buf_ref, *, tq):
    i = pl.program_id(1)
    c = ca_ref[...] * jax.nn.sigmoid(cb_ref[...])
    c_ref[...] = c
    halo = cah_ref[...] * jax.nn.sigmoid(cbh_ref[...])
    halo = jnp.where(i == 0, hist_ref[0], halo)
    buf_ref[0:CONV_HALO, :] = halo
    buf_ref[CONV_HALO:CONV_HALO + tq, :] = c
    lead = CONV_HALO - (CONV_K - 1)
    acc = jnp.zeros((tq, CONV_WIDTH), F32)
    for k in range(CONV_K):
        acc = acc + dw_ref[k:k + 1, :] * buf_ref[lead + k:lead + k + tq, :]
    y = acc + db_ref[...]
    cw = CONV_WIDTH // CONV_GROUPS
    parts = []
    for gi in range(CONV_GROUPS):
        sl = slice(gi * cw, (gi + 1) * cw)
        yn = _ln_lanes(y[:, sl], g_ref[:, sl], b_ref[:, sl])
        parts.append((yn * jax.nn.sigmoid(yn)).astype(BF16))
    y_ref[...] = jnp.dot(jnp.concatenate(parts, axis=-1), pw_ref[...], preferred_element_type=F32)


def _conv(h, hist, dw, db, g, b, pw, *, batch, seq, tq):
    m = h.shape[0]
    nt = seq // tq
    row = lambda bi, i: bi * nt + i
    halo_row = lambda bi, i: jnp.maximum((bi * seq + i * tq) // CONV_HALO - 1, 0)
    full = lambda a: pl.BlockSpec(a.shape, lambda bi, i: (0,) * a.ndim)
    kern = functools.partial(_conv_kernel, tq=tq)
    return pl.pallas_call(
        kern,
        grid=(batch, nt),
        in_specs=[
            pl.BlockSpec((tq, CONV_WIDTH), lambda bi, i: (row(bi, i), C_CA // CONV_WIDTH)),
            pl.BlockSpec((tq, CONV_WIDTH), lambda bi, i: (row(bi, i), C_CB // CONV_WIDTH)),
            pl.BlockSpec((CONV_HALO, CONV_WIDTH), lambda bi, i: (halo_row(bi, i), C_CA // CONV_WIDTH)),
            pl.BlockSpec((CONV_HALO, CONV_WIDTH), lambda bi, i: (halo_row(bi, i), C_CB // CONV_WIDTH)),
            pl.BlockSpec((1, CONV_HALO, CONV_WIDTH), lambda bi, i: (bi, 0, 0)),
            full(dw), full(db), full(g), full(b), full(pw),
        ],
        out_specs=[pl.BlockSpec((tq, CONV_WIDTH), lambda bi, i: (row(bi, i), 0))] * 2,
        out_shape=[jax.ShapeDtypeStruct((m, CONV_WIDTH), F32)] * 2,
        scratch_shapes=[pltpu.VMEM((CONV_HALO + tq, CONV_WIDTH), F32)],
        compiler_params=_cparams(2),
        name="conv",
    )(h, h, h, h, hist, dw, db, g, b, pw)


def _masked_softmax(s, mask):
    s = jnp.where(mask, s, NEG_INF)
    m = jnp.max(s, axis=-1, keepdims=True)
    m = jnp.where(m > NEG_INF, m, 0.0)
    e = jnp.exp(s - m)
    return e / jnp.maximum(jnp.sum(e, axis=-1, keepdims=True), 1e-30)


def _select_blocks(imp, tpos, n_blocks, first_block_pos):
    tq = imp.shape[0]
    blk = lax.broadcasted_iota(jnp.int32, (tq, n_blocks), 1)
    cur = tpos // SEL_BLOCK
    valid = blk * SEL_BLOCK + first_block_pos <= tpos
    forced = jnp.logical_or(blk == 0, jnp.logical_or(blk == cur, blk == cur - 1))
    score = jnp.where(valid, jnp.where(forced, FORCED_SCORE, imp), NEG_INF)
    beaten = jnp.zeros((tq, n_blocks), F32)
    for i in range(n_blocks):
        ci = score[:, i:i + 1]
        earlier = (blk > i).astype(F32)
        beaten = beaten + jnp.where(ci > score, 1.0, jnp.where(ci == score, earlier, 0.0))
    return (beaten < float(min(N_SELECT, n_blocks))).astype(F32)


def _nsa_prompt_kernel(q_ref, kc_ref, ks_ref, kw_ref, vc_ref, vs_ref, vw_ref, gate_ref, cw_ref, o_ref, *, tq, seq):
    qi = pl.program_id(1)
    t0 = qi * tq
    scale = HEAD_DIM ** -0.5
    n_slc = seq // SEL_BLOCK
    tpos = t0 + lax.broadcasted_iota(jnp.int32, (tq, 1), 0)
    lane_c = lax.broadcasted_iota(jnp.int32, (tq, 2 * n_slc), 1)
    cmp_last = jnp.where(lane_c < n_slc, lane_c * SEL_BLOCK + CMP_BLOCK - 1, (lane_c - n_slc) * SEL_BLOCK + SEL_BLOCK - 1)
    cmp_mask = cmp_last <= tpos
    kpos = lax.broadcasted_iota(jnp.int32, (tq, seq), 1)
    causal = kpos <= tpos
    expand = (lax.broadcasted_iota(jnp.int32, (n_slc, seq), 1) // SEL_BLOCK
              == lax.broadcasted_iota(jnp.int32, (n_slc, seq), 0)).astype(BF16)
    win_len = tq + WINDOW
    w0 = pl.multiple_of(jnp.maximum(t0 - WINDOW, 0), tq)
    wpos = w0 + lax.broadcasted_iota(jnp.int32, (tq, win_len), 1)
    win_mask = jnp.logical_and(wpos <= tpos, wpos >= tpos - WINDOW)
    gates = jax.nn.sigmoid(gate_ref[:, 0:HEAD_DIM])

    for g in range(NSA_KV_HEADS):
        gs = slice(g * HEAD_DIM, (g + 1) * HEAD_DIM)

        def compress(ref, which):
            x3 = ref[0, :, gs].reshape(n_slc, SEL_BLOCK, HEAD_DIM)
            ev = jnp.sum(x3 * cw_ref[g, which, 0][None], axis=1)
            od = jnp.sum(x3 * cw_ref[g, which, 1][None], axis=1)
            return jnp.concatenate([ev, od], axis=0).astype(BF16)

        kcmp, vcmp = compress(kc_ref, 0), compress(vc_ref, 1)
        ksel, vsel = ks_ref[0, :, gs].astype(BF16), vs_ref[0, :, gs].astype(BF16)
        kwin = kw_ref[0, pl.ds(w0, win_len), gs].astype(BF16)
        vwin = vw_ref[0, pl.ds(w0, win_len), gs].astype(BF16)

        qs, o_cmp = [], []
        imp = jnp.zeros((tq, 2 * n_slc), F32)
        for r in range(GQA_R):
            hh = g * GQA_R + r
            qh = q_ref[:, hh * HEAD_DIM:(hh + 1) * HEAD_DIM].astype(BF16)
            qs.append(qh)
            p = _masked_softmax(_dot_nt(qh, kcmp) * scale, cmp_mask)
            imp = imp + p
            o_cmp.append(jnp.dot(p.astype(BF16), vcmp, preferred_element_type=F32))
        sel = _select_blocks(imp[:, :n_slc] + imp[:, n_slc:], tpos, n_slc, 0)
        sel_keys = jnp.dot(sel.astype(BF16), expand, preferred_element_type=F32)
        slc_mask = jnp.logical_and(sel_keys > 0.5, causal)
        for r in range(GQA_R):
            hh = g * GQA_R + r
            p = _masked_softmax(_dot_nt(qs[r], ksel) * scale, slc_mask)
            o_slc = jnp.dot(p.astype(BF16), vsel, preferred_element_type=F32)
            p = _masked_softmax(_dot_nt(qs[r], kwin) * scale, win_mask)
            o_win = jnp.dot(p.astype(BF16), vwin, preferred_element_type=F32)
            o_ref[:, hh * HEAD_DIM:(hh + 1) * HEAD_DIM] = (
                gates[:, 3 * hh:3 * hh + 1] * o_cmp[r] + gates[:, 3 * hh + 1:3 * hh + 2] * o_slc
                + gates[:, 3 * hh + 2:3 * hh + 3] * o_win)


def _nsa_prompt(h, cmp_w, *, batch, seq, tq):
    m = h.shape[0]
    nt = seq // tq
    kv = lambda col: pl.BlockSpec((1, seq, KV_WIDTH), lambda bi, i, col=col: (bi, 0, col // KV_WIDTH))
    h3 = h.reshape(batch, seq, PROJ_WIDTH)
    kern = functools.partial(_nsa_prompt_kernel, tq=tq, seq=seq)
    return pl.pallas_call(
        kern,
        grid=(batch, nt),
        in_specs=[
            pl.BlockSpec((tq, NSA_WIDTH), lambda bi, i: (bi * nt + i, C_Q // NSA_WIDTH)),
            kv(C_KC), kv(C_KS), kv(C_KW), kv(C_VC), kv(C_VS), kv(C_VW),
            pl.BlockSpec((tq, KV_WIDTH), lambda bi, i: (bi * nt + i, C_GATE // KV_WIDTH)),
            pl.BlockSpec(cmp_w.shape, lambda bi, i: (0,) * cmp_w.ndim),
        ],
        out_specs=pl.BlockSpec((tq, NSA_WIDTH), lambda bi, i: (bi * nt + i, 0)),
        out_shape=jax.ShapeDtypeStruct((m, NSA_WIDTH), F32),
        compiler_params=_cparams(2),
        name="nsa_prompt",
    )(h, h3, h3, h3, h3, h3, h3, h, cmp_w)


def _out_ln_kernel(gm_ref, nsa_ref, y_ref, x_ref, w_ref, g_ref, b_ref, o_ref):
    cat = jnp.concatenate([gm_ref[...].astype(BF16), nsa_ref[...].astype(BF16), y_ref[...].astype(BF16)], axis=-1)
    mix = jnp.dot(cat, w_ref[...], preferred_element_type=F32)
    o_ref[...] = _ln_lanes(ALPHA * x_ref[...] + mix, g_ref[...], b_ref[...])


def _out_ln(gm, nsa, y, x, w, g, b, *, tm):
    m = x.shape[0]
    row = lambda width: pl.BlockSpec((tm, width), lambda i: (i, 0))
    full = lambda a: pl.BlockSpec(a.shape, lambda i: (0,) * a.ndim)
    return pl.pallas_call(
        _out_ln_kernel,
        grid=(m // tm,),
        in_specs=[row(GM_WIDTH), row(NSA_WIDTH), row(CONV_WIDTH), row(D_MODEL), full(w), full(g), full(b)],
        out_specs=row(D_MODEL),
        out_shape=jax.ShapeDtypeStruct((m, D_MODEL), F32),
        compiler_params=_cparams(1),
        name="out_ln",
    )(gm, nsa, y, x, w, g, b)


def _extract_topk(s, n_rows):
    tm = s.shape[1]
    rows = lax.broadcasted_iota(jnp.int32, (n_rows, tm), 0)
    krow = lax.broadcasted_iota(jnp.int32, (PEER_TOPK, tm), 0)
    work = s
    rank = jnp.full((n_rows, tm), float(PEER_TOPK), F32)
    vals = jnp.zeros((PEER_TOPK, tm), F32)
    for r in range(PEER_TOPK):
        mx = jnp.max(work, axis=0, keepdims=True)
        idx = jnp.min(jnp.where(work == mx, rows, n_rows), axis=0, keepdims=True)
        hit = rows == idx
        rank = jnp.where(hit, float(r), rank)
        vals = jnp.where(krow == r, mx, vals)
        work = jnp.where(hit, NEG_INF, work)
    return vals, rank


def _route_kernel(qp_ref, sk_ref, r2_ref, cnt_ref, f_ref, e2_ref):
    tm = qp_ref.shape[0]
    q1 = qp_ref[:, 0:SUBKEY_DIM].astype(BF16)
    q2 = qp_ref[:, SUBKEY_DIM:2 * SUBKEY_DIM].astype(BF16)
    s1 = _dot_nt(sk_ref[0, 0], q1)
    s2 = _dot_nt(sk_ref[0, 1], q2)
    v1, rank1 = _extract_topk(s1, N_KEYS)
    v2, rank2 = _extract_topk(s2, N_KEYS)
    krow = lax.broadcasted_iota(jnp.int32, (PEER_TOPK, tm), 0)
    cnt = jnp.zeros((PEER_TOPK, tm), F32)
    front = v1 + v2[0:1]
    m0 = v1[0:1] + v2[0:1]
    z = jnp.zeros((1, tm), F32)
    for _ in range(PEER_TOPK):
        mx = jnp.max(front, axis=0, keepdims=True)
        idx = jnp.min(jnp.where(front == mx, krow, PEER_TOPK), axis=0, keepdims=True)
        hit = krow == idx
        cnt = cnt + hit.astype(F32)
        z = z + jnp.exp(mx - m0)
        chosen = jnp.sum(jnp.where(hit, cnt, 0.0), axis=0, keepdims=True)
        nxt = jnp.full((1, tm), NEG_INF, F32)
        for c in range(1, PEER_TOPK):
            nxt = jnp.where(chosen == float(c), v2[c:c + 1], nxt)
        front = jnp.where(hit, v1 + nxt, front)
    cnt_a = jnp.zeros((N_KEYS, tm), F32)
    for r in range(PEER_TOPK):
        cnt_a = jnp.where(rank1 == float(r), cnt[r:r + 1], cnt_a)
    r2_ref[0] = rank2
    cnt_ref[0] = cnt_a
    f_ref[0] = jnp.exp(s1 - v1[0:1]) / z
    e2_ref[0] = jnp.exp(s2 - v2[0:1])


def _route(qp, subkeys, *, tm):
    m = qp.shape[0]
    out = jax.ShapeDtypeStruct((PEER_HEADS, N_KEYS, m), F32)
    ospec = pl.BlockSpec((1, N_KEYS, tm), lambda i, hd: (hd, 0, i))
    return pl.pallas_call(
        _route_kernel,
        grid=(m // tm, PEER_HEADS),
        in_specs=[
            pl.BlockSpec((tm, 2 * SUBKEY_DIM), lambda i, hd: (i, hd)),
            pl.BlockSpec((1, 2, N_KEYS, SUBKEY_DIM), lambda i, hd: (hd, 0, 0, 0)),
        ],
        out_specs=[ospec] * 4,
        out_shape=[out] * 4,
        compiler_params=_cparams(2),
        name="peer_route",
    )(qp, subkeys)


def _peer_dense_kernel(x_ref, u_ref, vt_ref, r2_ref, cnt_ref, f_ref, e2_ref, g_ref, b_ref, o_ref,
                       xb_ref, acc_ref, *, te):
    j = pl.program_id(1)
    na = te // N_KEYS

    @pl.when(j == 0)
    def _():
        xb_ref[...] = x_ref[...].astype(BF16)
        acc_ref[...] = jnp.zeros_like(acc_ref)

    act = jax.nn.gelu(_dot_nt(u_ref[...], xb_ref[...]))
    parts = []
    for al in range(na):
        a = j * na + al
        gate = jnp.zeros((N_KEYS, act.shape[1]), F32)
        for hd in range(PEER_HEADS):
            cnt = cnt_ref[hd, pl.ds(a, 1), :]
            f = f_ref[hd, pl.ds(a, 1), :]
            gate = gate + jnp.where(r2_ref[hd] < cnt, e2_ref[hd], 0.0) * f
        parts.append((act[al * N_KEYS:(al + 1) * N_KEYS] * gate).astype(BF16))
    wt = jnp.concatenate(parts, axis=0)
    acc_ref[...] += jnp.dot(vt_ref[...], wt, preferred_element_type=F32)

    @pl.when(j == pl.num_programs(1) - 1)
    def _():
        o_ref[...] = _ln_lanes(ALPHA * x_ref[...] + acc_ref[...].T, g_ref[...], b_ref[...])


def _peer_dense(x, u, vt, r2, cnt, f, e2, g, b, *, tm, te):
    m = x.shape[0]
    n_exp = u.shape[0]
    rt = pl.BlockSpec((PEER_HEADS, N_KEYS, tm), lambda i, j: (0, 0, i))
    full = lambda a: pl.BlockSpec(a.shape, lambda i, j: (0,) * a.ndim)
    kern = functools.partial(_peer_dense_kernel, te=te)
    return pl.pallas_call(
        kern,
        grid=(m // tm, n_exp // te),
        in_specs=[
            pl.BlockSpec((tm, D_MODEL), lambda i, j: (i, 0)),
            pl.BlockSpec((te, D_MODEL), lambda i, j: (j, 0)),
            pl.BlockSpec((D_MODEL, te), lambda i, j: (0, j)),
            rt, rt, rt, rt, full(g), full(b),
        ],
        out_specs=pl.BlockSpec((tm, D_MODEL), lambda i, j: (i, 0)),
        out_shape=jax.ShapeDtypeStruct((m, D_MODEL), F32),
        scratch_shapes=[pltpu.VMEM((tm, D_MODEL), BF16), pltpu.VMEM((D_MODEL, tm), F32)],
        compiler_params=_cparams(2),
        name="peer_dense",
    )(x, u, vt, r2, cnt, f, e2, g, b)


def _nsa_sample(q, kc, vc, ks, vs, kw, vw, gate, past, qpos, cmp_wk, cmp_wv):
    kc_p, vc_p, ks_p, vs_p, wk_buf, wv_buf = past
    bsz, t = q.shape[0], q.shape[1]
    p_len, wb = kc_p.shape[1], wk_buf.shape[1]
    cat = lambda a, b_: jnp.concatenate([a, b_], axis=1)
    kc_all, vc_all, ks_all, vs_all = cat(kc_p, kc), cat(vc_p, vc), cat(ks_p, ks), cat(vs_p, vs)
    length = p_len + t
    n_cmp = length // CMP_BLOCK
    scale = HEAD_DIM ** -0.5

    def compress(x, w):
        xb = x[:, :n_cmp * CMP_BLOCK].reshape(bsz, n_cmp, CMP_BLOCK, NSA_KV_HEADS, HEAD_DIM)
        return jnp.sum(xb * w.T[None, None, :, :, None], axis=2)

    def attend(k, v, mask):
        s = jnp.einsum("btgrd,blgd->btgrl", q.astype(BF16), k.astype(BF16), preferred_element_type=F32) * scale
        s = jnp.where(mask, s, NEG_INF)
        mx = jnp.max(s, -1, keepdims=True)
        mx = jnp.where(mx > NEG_INF, mx, 0.0)
        e = jnp.exp(s - mx)
        p = e / jnp.maximum(e.sum(-1, keepdims=True), 1e-30)
        return jnp.einsum("btgrl,blgd->btgrd", p.astype(BF16), v.astype(BF16), preferred_element_type=F32), p

    cmp_last = (jnp.arange(n_cmp) + 1) * CMP_BLOCK - 1
    o_cmp, p_cmp = attend(compress(kc_all, cmp_wk), compress(vc_all, cmp_wv),
                          (cmp_last[None, :] <= qpos[:, None])[None, :, None, None, :])
    ratio = SEL_BLOCK // CMP_BLOCK
    n_slc = -(-length // SEL_BLOCK)
    imp = p_cmp.sum(axis=3)
    imp = jnp.pad(imp, ((0, 0), (0, 0), (0, 0), (0, n_slc * ratio - n_cmp)))
    imp = imp.reshape(bsz, t, NSA_KV_HEADS, n_slc, ratio).sum(-1)
    blk = jnp.arange(n_slc)[None, :]
    cur = (qpos // SEL_BLOCK)[:, None]
    valid = blk * SEL_BLOCK <= qpos[:, None]
    forced = (blk == 0) | (blk == cur) | (blk == cur - 1)
    score = jnp.where(valid[None, :, None, :], jnp.where(forced[None, :, None, :], FORCED_SCORE, imp), NEG_INF)
    _, sel = lax.top_k(score, min(N_SELECT, n_slc))
    sel_mask = jnp.zeros((bsz, t, NSA_KV_HEADS, n_slc), F32)
    sel_mask = jnp.max(jax.nn.one_hot(sel, n_slc, dtype=F32), axis=3)
    key_blk = jnp.arange(length) // SEL_BLOCK
    key_mask = (sel_mask[..., key_blk] > 0.5) & (jnp.arange(length)[None, :] <= qpos[:, None])[None, :, None, :]
    o_slc, _ = attend(ks_all, vs_all, key_mask[:, :, :, None, :])
    wk_all, wv_all = cat(wk_buf, kw), cat(wv_buf, vw)
    kpos = p_len - wb + jnp.arange(wb + t)
    wmask = (kpos[None, :] <= qpos[:, None]) & (kpos[None, :] >= qpos[:, None] - WINDOW)
    o_win, _ = attend(wk_all, wv_all, wmask[None, :, None, None, :])
    g = jax.nn.sigmoid(gate).reshape(bsz, t, NSA_KV_HEADS, GQA_R, 3, 1)
    o = g[..., 0, :] * o_cmp + g[..., 1, :] * o_slc + g[..., 2, :] * o_win
    keep = min(WINDOW, p_len + t)
    return o.reshape(bsz * t, NSA_WIDTH), wk_all[:, -keep:], wv_all[:, -keep:]


def _rope_tables(pos):
    half = HEAD_DIM // 2
    inv = ROPE_THETA ** (-jnp.arange(half, dtype=F32) / half)
    ang = pos.astype(F32)[:, None] * inv[None, :]
    cos, sin = jnp.cos(ang), jnp.sin(ang)
    return jnp.concatenate([cos, cos], -1), jnp.concatenate([-sin, sin], -1)


def _prep_weights(l, w_in, gm_ln_g, gm_ln_b, gm_ws, gm_bs, cmp_wk, cmp_wv, conv_dw, conv_db, conv_ln_g,
                  conv_ln_b, conv_pw, w_out, ln1_g, ln1_b, peer_wq, peer_subkeys, peer_u, peer_v, ln2_g, ln2_b):
    w = w_in[l]
    o = np.cumsum([0, GM_WIDTH, GM_WIDTH, NSA_WIDTH] + [KV_WIDTH] * 6 + [3 * NSA_HEADS, CONV_WIDTH, CONV_WIDTH])
    gu, gv, q, kc, vc, ks, vs, kw, vw, gate, ca, cb = [w[:, o[i]:o[i + 1]] for i in range(12)]
    pad = jnp.zeros((w.shape[0], PROJ_WIDTH - C_GATE - 3 * NSA_HEADS), w.dtype)
    w_proj = jnp.concatenate([gu, gv, ca, cb, q, kc, ks, kw, vc, vs, vw, gate, pad], axis=1).astype(BF16)
    row = lambda a: a.reshape(1, -1)
    cw = jnp.stack([cmp_wk[l], cmp_wv[l]], axis=1)
    zero = jnp.zeros_like(cw)
    cw = jnp.stack([jnp.concatenate([cw, zero], -1), jnp.concatenate([zero, cw], -1)], axis=2)
    cw = jnp.broadcast_to(cw[..., None], cw.shape + (HEAD_DIM,))
    dw = jnp.concatenate([conv_dw[l], jnp.zeros((CONV_HALO - CONV_K, CONV_WIDTH), F32)], axis=0)
    return dict(
        w_proj=w_proj, gm_g=row(gm_ln_g[l]), gm_b=row(gm_ln_b[l]), gm_ws=gm_ws[l], gm_bs=gm_bs[l], cmp_w=cw,
        cmp_wk=cmp_wk[l], cmp_wv=cmp_wv[l],
        conv_dw=dw, conv_db=row(conv_db[l]), conv_g=row(conv_ln_g[l]), conv_b=row(conv_ln_b[l]),
        conv_pw=conv_pw[l].astype(BF16), w_out=w_out[l].astype(BF16), ln1_g=row(ln1_g[l]), ln1_b=row(ln1_b[l]),
        wq=peer_wq[l].astype(BF16), subkeys=peer_subkeys[l].astype(BF16), u=peer_u[l].astype(BF16),
        vt=peer_v[l].T.astype(BF16), ln2_g=row(ln2_g[l]), ln2_b=row(ln2_b[l]))


def _gm_weights(p, chunk):
    w = jnp.tril(p["gm_ws"][:, :chunk, :chunk])
    bs = jnp.repeat(p["gm_bs"][:, :chunk].T, GM_WIDTH // GM_GROUPS, axis=1)
    return w, bs


def _peer(x, p, *, tm_proj, tm_route, tm_dense, te):
    m = x.shape[0]
    dummy = jnp.zeros((tm_proj, HEAD_DIM), F32)
    qp = _proj(x, p["wq"], dummy, dummy, tm=tm_proj, rope_lo=0, rope_hi=0, name="peer_q")
    r2, cnt, f, e2 = _route(qp, p["subkeys"], tm=tm_route)
    return _peer_dense(x, p["u"], p["vt"], r2, cnt, f, e2, p["ln2_g"], p["ln2_b"], tm=tm_dense, te=te)


def _kv_rows(h, col, batch, seq):
    return h[:, col:col + KV_WIDTH].reshape(batch, seq, NSA_KV_HEADS, HEAD_DIM)


def _layer_prompt(x, p, cos, sin, *, batch, seq):
    m = batch * seq
    h = _proj(x, p["w_proj"], cos, sin, tm=1024, rope_lo=ROPE_LO, rope_hi=ROPE_HI, name="proj_in")
    gw, gbs = _gm_weights(p, GM_CHUNK)
    gm, vn = _gmlp(h, p["gm_g"], p["gm_b"], gw.astype(BF16), gbs, chunk=GM_CHUNK)
    nsa = _nsa_prompt(h, p["cmp_w"], batch=batch, seq=seq, tq=256)
    hist = jnp.zeros((batch, CONV_HALO, CONV_WIDTH), F32)
    y, c = _conv(h, hist, p["conv_dw"], p["conv_db"], p["conv_g"], p["conv_b"], p["conv_pw"],
                 batch=batch, seq=seq, tq=256)
    x1 = _out_ln(gm, nsa, y, x, p["w_out"], p["ln1_g"], p["ln1_b"], tm=256)
    x2 = _peer(x1, p, tm_proj=1024, tm_route=512, tm_dense=256, te=512)
    keep = min(WINDOW, seq)
    kw, vw = _kv_rows(h, C_KW, batch, seq), _kv_rows(h, C_VW, batch, seq)
    start = ((seq - 1) // GM_CHUNK) * GM_CHUNK
    states = (_kv_rows(h, C_KC, batch, seq), _kv_rows(h, C_VC, batch, seq), _kv_rows(h, C_KS, batch, seq),
              _kv_rows(h, C_VS, batch, seq), kw[:, -keep:], vw[:, -keep:],
              c.reshape(batch, seq, CONV_WIDTH)[:, -(CONV_K - 1):], vn.reshape(batch, seq, GM_WIDTH)[:, start:])
    return x2, states


def _layer_sample(x, p, cos, sin, past, hist30, qpos, *, batch, seq):
    m = batch * seq
    h = _proj(x, p["w_proj"], cos, sin, tm=m, rope_lo=ROPE_LO, rope_hi=ROPE_HI, name="proj_in_s")
    pad_rows = lambda a: jnp.pad(a.reshape(batch, seq, -1), ((0, 0), (0, GM_CHUNK - seq), (0, 0))).reshape(batch * GM_CHUNK, -1)
    gw, gbs = _gm_weights(p, GM_CHUNK)
    hp = pad_rows(h[:, :2 * GM_WIDTH])
    gm, vn = _gmlp(hp, p["gm_g"], p["gm_b"], gw.astype(BF16), gbs, chunk=GM_CHUNK)
    unpad = lambda a: a.reshape(batch, GM_CHUNK, -1)[:, :seq].reshape(m, -1)
    gm, vn = unpad(gm), unpad(vn)
    q = h[:, C_Q:C_Q + NSA_WIDTH].reshape(batch, seq, NSA_KV_HEADS, GQA_R, HEAD_DIM)
    kvs = [_kv_rows(h, col, batch, seq) for col in (C_KC, C_VC, C_KS, C_VS, C_KW, C_VW)]
    gate = h[:, C_GATE:C_GATE + 3 * NSA_HEADS]
    nsa, win_k, win_v = _nsa_sample(q, *kvs, gate, past, qpos, p["cmp_wk"], p["cmp_wv"])
    hist = jnp.pad(hist30, ((0, 0), (CONV_HALO - (CONV_K - 1), 0), (0, 0)))
    y, c = _conv(h, hist, p["conv_dw"], p["conv_db"], p["conv_g"], p["conv_b"], p["conv_pw"],
                 batch=batch, seq=seq, tq=seq)
    x1 = _out_ln(gm, nsa, y, x, p["w_out"], p["ln1_g"], p["ln1_b"], tm=m)
    x2 = _peer(x1, p, tm_proj=m, tm_route=m, tm_dense=m, te=512)
    conv_state = jnp.concatenate([hist30, c.reshape(batch, seq, CONV_WIDTH)], axis=1)[:, -(CONV_K - 1):]
    start = ((seq - 1) // GM_CHUNK) * GM_CHUNK
    states = (kvs[0], kvs[1], kvs[2], kvs[3], win_k, win_v, conv_state, vn.reshape(batch, seq, GM_WIDTH)[:, start:])
    return x2, states


def _gather_pages(cache_l, page_table):
    g = cache_l[page_table]
    return g.reshape(g.shape[0], g.shape[1] * g.shape[2], g.shape[3], g.shape[4])


def kernel(x_prompt, x_sample, cache_cmp_k, cache_cmp_v, cache_slc_k, cache_slc_v, cache_win_k, cache_win_v,
           state_conv, page_table, w_in, gm_ln_g, gm_ln_b, gm_ws, gm_bs, cmp_wk, cmp_wv, conv_dw, conv_db,
           conv_ln_g, conv_ln_b, conv_pw, w_out, ln1_g, ln1_b, peer_wq, peer_subkeys, peer_u, peer_v, ln2_g, ln2_b):
    bp, sp, _ = x_prompt.shape
    bs, ss, _ = x_sample.shape
    depth = w_in.shape[0]
    past_len = page_table.shape[1] * cache_cmp_k.shape[2]
    pos_p = jnp.arange(sp, dtype=jnp.int32)
    pos_s = past_len + jnp.arange(ss, dtype=jnp.int32)
    cos_p, sin_p = _rope_tables(pos_p)
    cos_s, sin_s = _rope_tables(jnp.tile(pos_s, bs))
    xp = x_prompt.reshape(bp * sp, D_MODEL)
    xs = x_sample.reshape(bs * ss, D_MODEL)
    p_states, s_states = [], []
    for l in range(depth):
        p = _prep_weights(l, w_in, gm_ln_g, gm_ln_b, gm_ws, gm_bs, cmp_wk, cmp_wv, conv_dw, conv_db, conv_ln_g,
                          conv_ln_b, conv_pw, w_out, ln1_g, ln1_b, peer_wq, peer_subkeys, peer_u, peer_v, ln2_g, ln2_b)
        xp, sp_state = _layer_prompt(xp, p, cos_p, sin_p, batch=bp, seq=sp)
        past = (_gather_pages(cache_cmp_k[l], page_table), _gather_pages(cache_cmp_v[l], page_table),
                _gather_pages(cache_slc_k[l], page_table), _gather_pages(cache_slc_v[l], page_table),
                cache_win_k[l], cache_win_v[l])
        xs, ss_state = _layer_sample(xs, p, cos_s, sin_s, past, state_conv[l], pos_s, batch=bs, seq=ss)
        p_states.append(sp_state)
        s_states.append(ss_state)
    stack = lambda states, i: jnp.stack([st[i] for st in states], axis=0)
    outs = [xp.reshape(bp, sp, D_MODEL), xs.reshape(bs, ss, D_MODEL)]
    outs += [stack(p_states, i) for i in range(8)]
    outs += [stack(s_states, i) for i in range(8)]
    return tuple(outs)
```

```python
import functools

import numpy as np
import jax
import jax.numpy as jnp
from jax import lax
from jax.experimental import pallas as pl
from jax.experimental.pallas import tpu as pltpu

F32 = jnp.float32
BF16 = jnp.bfloat16

D_MODEL = 2048
HEAD_DIM = 128
NSA_HEADS = 8
NSA_KV_HEADS = 2
GQA_R = NSA_HEADS // NSA_KV_HEADS
GM_WIDTH = 512
GM_GROUPS = 4
GM_CHUNK = 128
CONV_WIDTH = 512
CONV_GROUPS = 4
CONV_K = 31
NSA_WIDTH = NSA_HEADS * HEAD_DIM
KV_WIDTH = NSA_KV_HEADS * HEAD_DIM
CMP_BLOCK = 32
SEL_BLOCK = 64
N_SELECT = 16
WINDOW = 512
FORCED_SCORE = 1.0e4
PEER_HEADS = 8
N_KEYS = 128
PEER_TOPK = 16
SUBKEY_DIM = 128
ROPE_THETA = 10000.0
LN_EPS = 1e-5
DEPTH = 2
ALPHA = (2 * DEPTH) ** 0.25

C_GU, C_GV, C_CA, C_CB, C_Q = 0, 512, 1024, 1536, 2048
C_KC, C_KS, C_KW, C_VC, C_VS, C_VW, C_GATE = 3072, 3328, 3584, 3840, 4096, 4352, 4608
PROJ_WIDTH = 4864
PROJ_TN = 256
ROPE_LO, ROPE_HI = C_Q // PROJ_TN, C_VC // PROJ_TN

V7X_VMEM_LIMIT = 56 * 1024 * 1024
NEG_INF = float("-inf")


def _cparams(n_axes, vmem=V7X_VMEM_LIMIT):
    return pltpu.CompilerParams(dimension_semantics=("arbitrary",) * n_axes, vmem_limit_bytes=vmem)


def _ln_lanes(z, g, b):
    mu = jnp.mean(z, axis=-1, keepdims=True)
    d = z - mu
    var = jnp.mean(d * d, axis=-1, keepdims=True)
    return d * lax.rsqrt(var + LN_EPS) * g + b


def _dot_nt(a, b):
    return lax.dot_general(a, b, (((1,), (1,)), ((), ())), preferred_element_type=F32)


def _proj_kernel(x_ref, w_ref, cos_ref, sin_ref, o_ref, xb_ref, *, rope_lo, rope_hi, tn):
    j = pl.program_id(1)

    @pl.when(j == 0)
    def _():
        xb_ref[...] = x_ref[...].astype(BF16)

    acc = jnp.dot(xb_ref[...], w_ref[...], preferred_element_type=F32)
    is_rope = jnp.logical_and(j >= rope_lo, j < rope_hi)

    @pl.when(is_rope)
    def _():
        cos, sin = cos_ref[...], sin_ref[...]
        for c in range(tn // HEAD_DIM):
            blk = acc[:, c * HEAD_DIM:(c + 1) * HEAD_DIM]
            o_ref[:, c * HEAD_DIM:(c + 1) * HEAD_DIM] = blk * cos + pltpu.roll(blk, HEAD_DIM // 2, 1) * sin

    @pl.when(jnp.logical_not(is_rope))
    def _():
        o_ref[...] = acc


def _proj(x, w, cos, sin, *, tm, rope_lo, rope_hi, name):
    m, k = x.shape
    n = w.shape[1]
    tn = PROJ_TN
    nper = cos.shape[0] // tm
    kern = functools.partial(_proj_kernel, rope_lo=rope_lo, rope_hi=rope_hi, tn=tn)
    return pl.pallas_call(
        kern,
        grid=(m // tm, n // tn),
        in_specs=[
            pl.BlockSpec((tm, k), lambda i, j: (i, 0)),
            pl.BlockSpec((k, tn), lambda i, j: (0, j)),
            pl.BlockSpec((tm, HEAD_DIM), lambda i, j: (i % nper, 0)),
            pl.BlockSpec((tm, HEAD_DIM), lambda i, j: (i % nper, 0)),
        ],
        out_specs=pl.BlockSpec((tm, tn), lambda i, j: (i, j)),
        out_shape=jax.ShapeDtypeStruct((m, n), F32),
        scratch_shapes=[pltpu.VMEM((tm, k), BF16)],
        compiler_params=_cparams(2),
        name=name,
    )(x, w, cos, sin)


def _gmlp_kernel(gu_ref, gv_ref, g_ref, b_ref, w_ref, bs_ref, o_ref, vn_ref):
    gw = GM_WIDTH // GM_GROUPS
    for h in range(GM_GROUPS):
        sl = slice(h * gw, (h + 1) * gw)
        v = jax.nn.gelu(gv_ref[:, sl])
        vn = _ln_lanes(v, g_ref[:, sl], b_ref[:, sl])
        vn_ref[:, sl] = vn
        s = jnp.dot(w_ref[h], vn.astype(BF16), preferred_element_type=F32) + bs_ref[:, sl]
        o_ref[:, sl] = jax.nn.gelu(gu_ref[:, sl]) * s


def _gmlp(h, g, b, w_tril, bs_full, *, chunk):
    m = h.shape[0]
    blk = lambda c: pl.BlockSpec((chunk, GM_WIDTH), lambda i, c=c: (i, c))
    full = lambda a: pl.BlockSpec(a.shape, lambda i: (0,) * a.ndim)
    return pl.pallas_call(
        _gmlp_kernel,
        grid=(m // chunk,),
        in_specs=[blk(C_GU // GM_WIDTH), blk(C_GV // GM_WIDTH), full(g), full(b), full(w_tril), full(bs_full)],
        out_specs=[pl.BlockSpec((chunk, GM_WIDTH), lambda i: (i, 0))] * 2,
        out_shape=[jax.ShapeDtypeStruct((m, GM_WIDTH), F32)] * 2,
        compiler_params=_cparams(1),
        name="gmlp",
    )(h, h, g, b, w_tril, bs_full)


CONV_HALO = 32


def _conv_kernel(ca_ref, cb_ref, cah_ref, cbh_ref, hist_ref, dw_ref, db_ref, g_ref, b_ref, pw_ref,
                 y_ref, c_ref, buf_ref, *, tq):
    i = pl.program_id(1)
    c = ca_ref[...] * jax.nn.sigmoid(cb_ref[...])
    c_ref[...] = c
    halo = cah_ref[...] * jax.nn.sigmoid(cbh_ref[...])
    halo = jnp.where(i == 0, hist_ref[0], halo)
    buf_ref[0:CONV_HALO, :] = halo
    buf_ref[CONV_HALO:CONV_HALO + tq, :] = c
    lead = CONV_HALO - (CONV_K - 1)
    acc = jnp.zeros((tq, CONV_WIDTH), F32)
    for k in range(CONV_K):
        acc = acc + dw_ref[k:k + 1, :] * buf_ref[lead + k:lead + k + tq, :]
    y = acc + db_ref[...]
    cw = CONV_WIDTH // CONV_GROUPS
    parts = []
    for gi in range(CONV_GROUPS):
        sl = slice(gi * cw, (gi + 1) * cw)
        yn = _ln_lanes(y[:, sl], g_ref[:, sl], b_ref[:, sl])
        parts.append((yn * jax.nn.sigmoid(yn)).astype(BF16))
    y_ref[...] = jnp.dot(jnp.concatenate(parts, axis=-1), pw_ref[...], preferred_element_type=F32)


def _conv(h, hist, dw, db, g, b, pw, *, batch, seq, tq):
    m = h.shape[0]
    nt = seq // tq
    row = lambda bi, i: bi * nt + i
    halo_row = lambda bi, i: jnp.maximum((bi * seq + i * tq) // CONV_HALO - 1, 0)
    full = lambda a: pl.BlockSpec(a.shape, lambda bi, i: (0,) * a.ndim)
    kern = functools.partial(_conv_kernel, tq=tq)
    return pl.pallas_call(
        kern,
        grid=(batch, nt),
        in_specs=[
            pl.BlockSpec((tq, CONV_WIDTH), lambda bi, i: (row(bi, i), C_CA // CONV_WIDTH)),
            pl.BlockSpec((tq, CONV_WIDTH), lambda bi, i: (row(bi, i), C_CB // CONV_WIDTH)),
            pl.BlockSpec((CONV_HALO, CONV_WIDTH), lambda bi, i: (halo_row(bi, i), C_CA // CONV_WIDTH)),
            pl.BlockSpec((CONV_HALO, CONV_WIDTH), lambda bi, i: (halo_row(bi, i), C_CB // CONV_WIDTH)),
            pl.BlockSpec((1, CONV_HALO, CONV_WIDTH), lambda bi, i: (bi, 0, 0)),
            full(dw), full(db), full(g), full(b), full(pw),
        ],
        out_specs=[pl.BlockSpec((tq, CONV_WIDTH), lambda bi, i: (row(bi, i), 0))] * 2,
        out_shape=[jax.ShapeDtypeStruct((m, CONV_WIDTH), F32)] * 2,
        scratch_shapes=[pltpu.VMEM((CONV_HALO + tq, CONV_WIDTH), F32)],
        compiler_params=_cparams(2),
        name="conv",
    )(h, h, h, h, hist, dw, db, g, b, pw)


def _masked_softmax(s, mask):
    s = jnp.where(mask, s, NEG_INF)
    m = jnp.max(s, axis=-1, keepdims=True)
    m = jnp.where(m > NEG_INF, m, 0.0)
    e = jnp.exp(s - m)
    return e / jnp.maximum(jnp.sum(e, axis=-1, keepdims=True), 1e-30)


def _select_blocks(imp, tpos, n_blocks):
    tq, lanes = imp.shape
    blk = lax.broadcasted_iota(jnp.int32, (tq, lanes), 1)
    cur = tpos // SEL_BLOCK
    valid = blk * SEL_BLOCK <= tpos
    forced = jnp.logical_or(blk == 0, jnp.logical_or(blk == cur, blk == cur - 1))
    score = jnp.where(valid, jnp.where(forced, FORCED_SCORE, imp), NEG_INF)
    beaten = jnp.zeros((tq, lanes), F32)
    for i in range(n_blocks):
        ci = score[:, i:i + 1]
        earlier = (blk > i).astype(F32)
        beaten = beaten + jnp.where(ci > score, 1.0, jnp.where(ci == score, earlier, 0.0))
    picked = jnp.logical_and(beaten < float(min(N_SELECT, n_blocks)), blk < n_blocks)
    return picked.astype(F32)


def _nsa_prompt_kernel(q_ref, kc_ref, ks_ref, kw_ref, vc_ref, vs_ref, vw_ref, gate_ref, cw_ref, o_ref, *, tq, seq):
    qi = pl.program_id(1)
    t0 = qi * tq
    scale = HEAD_DIM ** -0.5
    n_slc = seq // SEL_BLOCK
    tpos = t0 + lax.broadcasted_iota(jnp.int32, (tq, 1), 0)
    lane_c = lax.broadcasted_iota(jnp.int32, (tq, 2 * n_slc), 1)
    cmp_last = jnp.where(lane_c < n_slc, lane_c * SEL_BLOCK + CMP_BLOCK - 1, (lane_c - n_slc) * SEL_BLOCK + SEL_BLOCK - 1)
    cmp_mask = cmp_last <= tpos
    kpos = lax.broadcasted_iota(jnp.int32, (tq, seq), 1)
    causal = kpos <= tpos
    expand = (lax.broadcasted_iota(jnp.int32, (n_slc, seq), 1) // SEL_BLOCK
              == lax.broadcasted_iota(jnp.int32, (n_slc, seq), 0)).astype(BF16)
    win_len = tq + WINDOW
    w0 = pl.multiple_of(jnp.maximum(t0 - WINDOW, 0), tq)
    wpos = w0 + lax.broadcasted_iota(jnp.int32, (tq, win_len), 1)
    win_mask = jnp.logical_and(wpos <= tpos, wpos >= tpos - WINDOW)
    gates = jax.nn.sigmoid(gate_ref[:, 0:HEAD_DIM])

    for g in range(NSA_KV_HEADS):
        gs = slice(g * HEAD_DIM, (g + 1) * HEAD_DIM)

        def compress(ref, which):
            x3 = ref[0, :, gs].reshape(n_slc, SEL_BLOCK, HEAD_DIM)
            ev = jnp.sum(x3 * cw_ref[g, which, 0][None], axis=1)
            od = jnp.sum(x3 * cw_ref[g, which, 1][None], axis=1)
            return jnp.concatenate([ev, od], axis=0).astype(BF16)

        kcmp, vcmp = compress(kc_ref, 0), compress(vc_ref, 1)
        ksel, vsel = ks_ref[0, :, gs].astype(BF16), vs_ref[0, :, gs].astype(BF16)
        kwin = kw_ref[0, pl.ds(w0, win_len), gs].astype(BF16)
        vwin = vw_ref[0, pl.ds(w0, win_len), gs].astype(BF16)

        qs, o_cmp = [], []
        imp = jnp.zeros((tq, 2 * n_slc), F32)
        for r in range(GQA_R):
            hh = g * GQA_R + r
            qh = q_ref[:, hh * HEAD_DIM:(hh + 1) * HEAD_DIM].astype(BF16)
            qs.append(qh)
            p = _masked_softmax(_dot_nt(qh, kcmp) * scale, cmp_mask)
            imp = imp + p
            o_cmp.append(jnp.dot(p.astype(BF16), vcmp, preferred_element_type=F32))
        sel = _select_blocks(imp[:, :n_slc] + imp[:, n_slc:], tpos, n_slc)
        sel_keys = jnp.dot(sel.astype(BF16), expand, preferred_element_type=F32)
        slc_mask = jnp.logical_and(sel_keys > 0.5, causal)
        for r in range(GQA_R):
            hh = g * GQA_R + r
            p = _masked_softmax(_dot_nt(qs[r], ksel) * scale, slc_mask)
            o_slc = jnp.dot(p.astype(BF16), vsel, preferred_element_type=F32)
            p = _masked_softmax(_dot_nt(qs[r], kwin) * scale, win_mask)
            o_win = jnp.dot(p.astype(BF16), vwin, preferred_element_type=F32)
            o_ref[:, hh * HEAD_DIM:(hh + 1) * HEAD_DIM] = (
                gates[:, 3 * hh:3 * hh + 1] * o_cmp[r] + gates[:, 3 * hh + 1:3 * hh + 2] * o_slc
                + gates[:, 3 * hh + 2:3 * hh + 3] * o_win)


def _nsa_prompt(h, cmp_w, *, batch, seq, tq):
    m = h.shape[0]
    nt = seq // tq
    kv = lambda col: pl.BlockSpec((1, seq, KV_WIDTH), lambda bi, i, col=col: (bi, 0, col // KV_WIDTH))
    h3 = h.reshape(batch, seq, PROJ_WIDTH)
    kern = functools.partial(_nsa_prompt_kernel, tq=tq, seq=seq)
    return pl.pallas_call(
        kern,
        grid=(batch, nt),
        in_specs=[
            pl.BlockSpec((tq, NSA_WIDTH), lambda bi, i: (bi * nt + i, C_Q // NSA_WIDTH)),
            kv(C_KC), kv(C_KS), kv(C_KW), kv(C_VC), kv(C_VS), kv(C_VW),
            pl.BlockSpec((tq, KV_WIDTH), lambda bi, i: (bi * nt + i, C_GATE // KV_WIDTH)),
            pl.BlockSpec(cmp_w.shape, lambda bi, i: (0,) * cmp_w.ndim),
        ],
        out_specs=pl.BlockSpec((tq, NSA_WIDTH), lambda bi, i: (bi * nt + i, 0)),
        out_shape=jax.ShapeDtypeStruct((m, NSA_WIDTH), F32),
        compiler_params=_cparams(2),
        name="nsa_prompt",
    )(h, h3, h3, h3, h3, h3, h3, h, cmp_w)


def _out_ln_kernel(gm_ref, nsa_ref, y_ref, x_ref, w_ref, g_ref, b_ref, o_ref):
    cat = jnp.concatenate([gm_ref[...].astype(BF16), nsa_ref[...].astype(BF16), y_ref[...].astype(BF16)], axis=-1)
    mix = jnp.dot(cat, w_ref[...], preferred_element_type=F32)
    o_ref[...] = _ln_lanes(ALPHA * x_ref[...] + mix, g_ref[...], b_ref[...])


def _out_ln(gm, nsa, y, x, w, g, b, *, tm):
    m = x.shape[0]
    row = lambda width: pl.BlockSpec((tm, width), lambda i: (i, 0))
    full = lambda a: pl.BlockSpec(a.shape, lambda i: (0,) * a.ndim)
    return pl.pallas_call(
        _out_ln_kernel,
        grid=(m // tm,),
        in_specs=[row(GM_WIDTH), row(NSA_WIDTH), row(CONV_WIDTH), row(D_MODEL), full(w), full(g), full(b)],
        out_specs=row(D_MODEL),
        out_shape=jax.ShapeDtypeStruct((m, D_MODEL), F32),
        compiler_params=_cparams(1),
        name="out_ln",
    )(gm, nsa, y, x, w, g, b)


def _extract_topk(s, n_rows):
    tm = s.shape[1]
    rows = lax.broadcasted_iota(jnp.int32, (n_rows, tm), 0)
    krow = lax.broadcasted_iota(jnp.int32, (PEER_TOPK, tm), 0)
    work = s
    rank = jnp.full((n_rows, tm), float(PEER_TOPK), F32)
    vals = jnp.zeros((PEER_TOPK, tm), F32)
    for r in range(PEER_TOPK):
        mx = jnp.max(work, axis=0, keepdims=True)
        idx = jnp.min(jnp.where(work == mx, rows, n_rows), axis=0, keepdims=True)
        hit = rows == idx
        rank = jnp.where(hit, float(r), rank)
        vals = jnp.where(krow == r, mx, vals)
        work = jnp.where(hit, NEG_INF, work)
    return vals, rank


def _route_kernel(qp_ref, sk_ref, r2_ref, cnt_ref, f_ref, e2_ref):
    tm = qp_ref.shape[0]
    q1 = qp_ref[:, 0:SUBKEY_DIM].astype(BF16)
    q2 = qp_ref[:, SUBKEY_DIM:2 * SUBKEY_DIM].astype(BF16)
    s1 = _dot_nt(sk_ref[0, 0], q1)
    s2 = _dot_nt(sk_ref[0, 1], q2)
    v1, rank1 = _extract_topk(s1, N_KEYS)
    v2, rank2 = _extract_topk(s2, N_KEYS)
    krow = lax.broadcasted_iota(jnp.int32, (PEER_TOPK, tm), 0)
    cnt = jnp.zeros((PEER_TOPK, tm), F32)
    front = v1 + v2[0:1]
    m0 = v1[0:1] + v2[0:1]
    z = jnp.zeros((1, tm), F32)
    for _ in range(PEER_TOPK):
        mx = jnp.max(front, axis=0, keepdims=True)
        idx = jnp.min(jnp.where(front == mx, krow, PEER_TOPK), axis=0, keepdims=True)
        hit = krow == idx
        cnt = cnt + hit.astype(F32)
        z = z + jnp.exp(mx - m0)
        chosen = jnp.sum(jnp.where(hit, cnt, 0.0), axis=0, keepdims=True)
        nxt = jnp.full((1, tm), NEG_INF, F32)
        for c in range(1, PEER_TOPK):
            nxt = jnp.where(chosen == float(c), v2[c:c + 1], nxt)
        front = jnp.where(hit, v1 + nxt, front)
    cnt_a = jnp.zeros((N_KEYS, tm), F32)
    for r in range(PEER_TOPK):
        cnt_a = jnp.where(rank1 == float(r), cnt[r:r + 1], cnt_a)
    r2_ref[0] = rank2.astype(BF16)
    cnt_ref[0] = cnt_a
    f_ref[0] = jnp.exp(s1 - v1[0:1]) / z
    e2_ref[0] = jnp.exp(s2 - v2[0:1]).astype(BF16)


def _route(qp, subkeys, *, tm):
    m = qp.shape[0]
    out = lambda dt: jax.ShapeDtypeStruct((PEER_HEADS, N_KEYS, m), dt)
    ospec = pl.BlockSpec((1, N_KEYS, tm), lambda i, hd: (hd, 0, i))
    return pl.pallas_call(
        _route_kernel,
        grid=(m // tm, PEER_HEADS),
        in_specs=[
            pl.BlockSpec((tm, 2 * SUBKEY_DIM), lambda i, hd: (i, hd)),
            pl.BlockSpec((1, 2, N_KEYS, SUBKEY_DIM), lambda i, hd: (hd, 0, 0, 0)),
        ],
        out_specs=[ospec] * 4,
        out_shape=[out(BF16), out(F32), out(F32), out(BF16)],
        compiler_params=_cparams(2),
        name="peer_route",
    )(qp, subkeys)


PEER_SUB = 256


def _peer_dense_kernel(x_ref, u_ref, v_ref, r2_ref, cnt_ref, f_ref, e2_ref, g_ref, b_ref, o_ref, xb_ref, *, te):
    j = pl.program_id(1)

    @pl.when(j == 0)
    def _():
        xb_ref[...] = x_ref[...].astype(BF16)
        o_ref[...] = jnp.zeros_like(o_ref)

    total = None
    for c in range(te // PEER_SUB):
        act = jax.nn.gelu(_dot_nt(u_ref[c * PEER_SUB:(c + 1) * PEER_SUB, :], xb_ref[...]))
        rows = []
        for al in range(PEER_SUB // N_KEYS):
            a = j * (te // N_KEYS) + c * (PEER_SUB // N_KEYS) + al
            gate = None
            for hd in range(PEER_HEADS):
                cnt = cnt_ref[hd, pl.ds(a, 1), :].astype(BF16)
                f = f_ref[hd, pl.ds(a, 1), :].astype(BF16)
                term = jnp.where(r2_ref[hd] < cnt, e2_ref[hd], jnp.zeros((), BF16)) * f
                gate = term if gate is None else gate + term
            rows.append(act[al * N_KEYS:(al + 1) * N_KEYS].astype(BF16) * gate)
        w = jnp.concatenate(rows, axis=0).T
        part = jnp.dot(w, v_ref[c * PEER_SUB:(c + 1) * PEER_SUB, :], preferred_element_type=F32)
        total = part if total is None else total + part
    o_ref[...] += total

    @pl.when(j == pl.num_programs(1) - 1)
    def _():
        o_ref[...] = _ln_lanes(ALPHA * x_ref[...] + o_ref[...], g_ref[...], b_ref[...])


def _peer_dense(x, u, v, r2, cnt, f, e2, g, b, *, tm, te):
    m = x.shape[0]
    n_exp = u.shape[0]
    rt = pl.BlockSpec((PEER_HEADS, N_KEYS, tm), lambda i, j: (0, 0, i))
    full = lambda a: pl.BlockSpec(a.shape, lambda i, j: (0,) * a.ndim)
    kern = functools.partial(_peer_dense_kernel, te=te)
    return pl.pallas_call(
        kern,
        grid=(m // tm, n_exp // te),
        in_specs=[
            pl.BlockSpec((tm, D_MODEL), lambda i, j: (i, 0)),
            pl.BlockSpec((te, D_MODEL), lambda i, j: (j, 0)),
            pl.BlockSpec((te, D_MODEL), lambda i, j: (j, 0)),
            rt, rt, rt, rt, full(g), full(b),
        ],
        out_specs=pl.BlockSpec((tm, D_MODEL), lambda i, j: (i, 0)),
        out_shape=jax.ShapeDtypeStruct((m, D_MODEL), F32),
        scratch_shapes=[pltpu.VMEM((tm, D_MODEL), BF16)],
        compiler_params=_cparams(2),
        name="peer_dense",
    )(x, u, v, r2, cnt, f, e2, g, b)


PAGES_PER_STEP = 4


def _group_queries(q_ref, g):
    heads = [q_ref[:, (g * GQA_R + r) * HEAD_DIM:(g * GQA_R + r + 1) * HEAD_DIM] for r in range(GQA_R)]
    return jnp.concatenate(heads, axis=0).astype(BF16)


def _nsa_s_cmp_kernel(pt_ref, q_ref, cw_ref, *refs, seq, past_len, n_real, sel_lanes):
    nps = PAGES_PER_STEP
    k_pages, v_pages = refs[:nps], refs[nps:2 * nps]
    ocmp_ref, sel_ref, ke_ref, ko_ref, ve_ref, vo_ref = refs[2 * nps:]
    c = pl.program_id(1)
    n_pb = ke_ref.shape[0]
    rows = nps * k_pages[0].shape[1] // SEL_BLOCK
    off = pl.multiple_of(c * rows, rows)

    def compress(pages, which, even_ref, odd_ref):
        x3 = jnp.concatenate([r[0] for r in pages], axis=0).reshape(rows, SEL_BLOCK, KV_WIDTH)
        even_ref[pl.ds(off, rows), :] = jnp.sum(x3 * cw_ref[which, 0][None], axis=1)
        odd_ref[pl.ds(off, rows), :] = jnp.sum(x3 * cw_ref[which, 1][None], axis=1)

    compress(k_pages, 0, ke_ref, ko_ref)
    compress(v_pages, 1, ve_ref, vo_ref)

    @pl.when(c == pl.num_programs(1) - 1)
    def _():
        scale = HEAD_DIM ** -0.5
        nq = GQA_R * seq
        qrow = lax.broadcasted_iota(jnp.int32, (nq, 1), 0)
        qpos = past_len + qrow % seq
        lane = lax.broadcasted_iota(jnp.int32, (nq, 2 * n_pb), 1)
        cmp_last = jnp.where(lane < n_pb, lane * SEL_BLOCK + CMP_BLOCK - 1, (lane - n_pb) * SEL_BLOCK + SEL_BLOCK - 1)
        cmp_mask = cmp_last <= qpos
        tpos = past_len + lax.broadcasted_iota(jnp.int32, (seq, 1), 0)
        for g in range(NSA_KV_HEADS):
            gs = slice(g * HEAD_DIM, (g + 1) * HEAD_DIM)
            kcat = jnp.concatenate([ke_ref[:, gs], ko_ref[:, gs]], axis=0).astype(BF16)
            vcat = jnp.concatenate([ve_ref[:, gs], vo_ref[:, gs]], axis=0).astype(BF16)
            p = _masked_softmax(_dot_nt(_group_queries(q_ref, g), kcat) * scale, cmp_mask)
            ocmp_ref[0, g] = jnp.dot(p.astype(BF16), vcat, preferred_element_type=F32)
            imp = p[0:seq]
            for r in range(1, GQA_R):
                imp = imp + p[r * seq:(r + 1) * seq]
            imp = imp[:, :n_pb] + imp[:, n_pb:]
            imp = jnp.concatenate([imp, jnp.zeros((seq, sel_lanes - n_pb), F32)], axis=1)
            sel_ref[0, g] = _select_blocks(imp, tpos, n_real)


def _nsa_s_slc_kernel(pt_ref, q_ref, sel_ref, ocmp_ref, ksn_ref, vsn_ref, wk_ref, wv_ref, gate_ref, *refs,
                      seq, past_len, sel_lanes, win_buf):
    nps = PAGES_PER_STEP
    k_pages, v_pages = refs[:nps], refs[nps:2 * nps]
    o_ref, m_ref, l_ref, acc_ref = refs[2 * nps:]
    c = pl.program_id(1)
    scale = HEAD_DIM ** -0.5
    nq = GQA_R * seq
    page = k_pages[0].shape[1]
    chunk = nps * page
    qrow = lax.broadcasted_iota(jnp.int32, (nq, 1), 0)
    qpos = past_len + qrow % seq

    @pl.when(c == 0)
    def _():
        m_ref[...] = jnp.full(m_ref.shape, NEG_INF, F32)
        l_ref[...] = jnp.zeros_like(l_ref)
        acc_ref[...] = jnp.zeros_like(acc_ref)

    def online_update(g, s, mask, v):
        s = jnp.where(mask, s, NEG_INF)
        m_old = m_ref[g]
        m_new = jnp.maximum(m_old, jnp.max(s, axis=-1, keepdims=True))
        m_safe = jnp.where(m_new > NEG_INF, m_new, 0.0)
        alpha = jnp.exp(m_old - m_safe)
        e = jnp.exp(s - m_safe)
        l_ref[g] = alpha * l_ref[g] + jnp.sum(e, axis=-1, keepdims=True)
        acc_ref[g] = alpha * acc_ref[g] + jnp.dot(e.astype(BF16), v, preferred_element_type=F32)
        m_ref[g] = m_new

    def key_mask(g, first_block, first_pos, n_keys):
        key_blk = first_block + lax.broadcasted_iota(jnp.int32, (sel_lanes, n_keys), 1) // SEL_BLOCK
        expand = (lax.broadcasted_iota(jnp.int32, (sel_lanes, n_keys), 0) == key_blk).astype(BF16)
        sel_keys = jnp.dot(sel_ref[0, g].astype(BF16), expand, preferred_element_type=F32)
        sel_keys = jnp.concatenate([sel_keys] * GQA_R, axis=0)
        kpos = first_pos + lax.broadcasted_iota(jnp.int32, (nq, n_keys), 1)
        return jnp.logical_and(sel_keys > 0.5, kpos <= qpos)

    kc = jnp.concatenate([r[0] for r in k_pages], axis=0).astype(BF16)
    vc = jnp.concatenate([r[0] for r in v_pages], axis=0).astype(BF16)
    for g in range(NSA_KV_HEADS):
        gs = slice(g * HEAD_DIM, (g + 1) * HEAD_DIM)
        mask = key_mask(g, c * (chunk // SEL_BLOCK), c * chunk, chunk)
        online_update(g, _dot_nt(_group_queries(q_ref, g), kc[:, gs]) * scale, mask, vc[:, gs])

    @pl.when(c == pl.num_programs(1) - 1)
    def _():
        gates = jax.nn.sigmoid(gate_ref[:, 0:HEAD_DIM])
        n_new = ksn_ref.shape[1]
        n_win = wk_ref.shape[1]
        wpos = past_len - win_buf + lax.broadcasted_iota(jnp.int32, (nq, n_win), 1)
        win_mask = jnp.logical_and(wpos <= qpos, wpos >= qpos - WINDOW)
        for g in range(NSA_KV_HEADS):
            gs = slice(g * HEAD_DIM, (g + 1) * HEAD_DIM)
            qg = _group_queries(q_ref, g)
            online_update(g, _dot_nt(qg, ksn_ref[0, :, gs].astype(BF16)) * scale,
                          key_mask(g, past_len // SEL_BLOCK, past_len, n_new), vsn_ref[0, :, gs].astype(BF16))
            o_slc = acc_ref[g] / jnp.maximum(l_ref[g], 1e-30)
            p = _masked_softmax(_dot_nt(qg, wk_ref[0, :, gs].astype(BF16)) * scale, win_mask)
            o_win = jnp.dot(p.astype(BF16), wv_ref[0, :, gs].astype(BF16), preferred_element_type=F32)
            o_cmp = ocmp_ref[0, g]
            for r in range(GQA_R):
                hh = g * GQA_R + r
                rs = slice(r * seq, (r + 1) * seq)
                o_ref[:, hh * HEAD_DIM:(hh + 1) * HEAD_DIM] = (
                    gates[:, 3 * hh:3 * hh + 1] * o_cmp[rs] + gates[:, 3 * hh + 1:3 * hh + 2] * o_slc[rs]
                    + gates[:, 3 * hh + 2:3 * hh + 3] * o_win[rs])


def _nsa_sample(h, caches, layer, page_table, ks_new, vs_new, wk_all, wv_all, cw, *, batch, seq, past_len, win_buf):
    cmp_k, cmp_v, slc_k, slc_v = caches
    nps = PAGES_PER_STEP
    n_pages, page = page_table.shape[1], cmp_k.shape[1]
    n_pool = cmp_k.shape[0] // DEPTH
    n_chunks = n_pages // nps
    n_pb = past_len // SEL_BLOCK
    n_real = n_pb + 1
    sel_lanes = -(-n_real // 128) * 128
    nq = GQA_R * seq
    page_spec = lambda k: pl.BlockSpec((1, page, KV_WIDTH),
                                       lambda b, c, pt, k=k: (pt[b, c * nps + k] + layer * n_pool, 0, 0))
    pages = [page_spec(k) for k in range(nps)]
    q_spec = pl.BlockSpec((seq, NSA_WIDTH), lambda b, c, pt: (b, C_Q // NSA_WIDTH))
    per_b = lambda shape: pl.BlockSpec((1,) + shape, lambda b, c, pt: (b,) + (0,) * len(shape))
    ocmp, sel = pl.pallas_call(
        functools.partial(_nsa_s_cmp_kernel, seq=seq, past_len=past_len, n_real=n_real, sel_lanes=sel_lanes),
        grid_spec=pltpu.PrefetchScalarGridSpec(
            num_scalar_prefetch=1, grid=(batch, n_chunks),
            in_specs=[q_spec, pl.BlockSpec(cw.shape, lambda b, c, pt: (0,) * cw.ndim)] + pages + pages,
            out_specs=[per_b((NSA_KV_HEADS, nq, HEAD_DIM)), per_b((NSA_KV_HEADS, seq, sel_lanes))],
            scratch_shapes=[pltpu.VMEM((n_pb, KV_WIDTH), F32)] * 4),
        out_shape=[jax.ShapeDtypeStruct((batch, NSA_KV_HEADS, nq, HEAD_DIM), F32),
                   jax.ShapeDtypeStruct((batch, NSA_KV_HEADS, seq, sel_lanes), F32)],
        compiler_params=_cparams(2),
        name="nsa_sample_cmp",
    )(page_table, h, cw, *([cmp_k] * nps), *([cmp_v] * nps))
    return pl.pallas_call(
        functools.partial(_nsa_s_slc_kernel, seq=seq, past_len=past_len, sel_lanes=sel_lanes, win_buf=win_buf),
        grid_spec=pltpu.PrefetchScalarGridSpec(
            num_scalar_prefetch=1, grid=(batch, n_chunks),
            in_specs=[q_spec, per_b((NSA_KV_HEADS, seq, sel_lanes)), per_b((NSA_KV_HEADS, nq, HEAD_DIM)),
                      per_b(ks_new.shape[1:]), per_b(vs_new.shape[1:]), per_b(wk_all.shape[1:]), per_b(wv_all.shape[1:]),
                      pl.BlockSpec((seq, KV_WIDTH), lambda b, c, pt: (b, C_GATE // KV_WIDTH))] + pages + pages,
            out_specs=pl.BlockSpec((seq, NSA_WIDTH), lambda b, c, pt: (b, 0)),
            scratch_shapes=[pltpu.VMEM((NSA_KV_HEADS, nq, 1), F32), pltpu.VMEM((NSA_KV_HEADS, nq, 1), F32),
                            pltpu.VMEM((NSA_KV_HEADS, nq, HEAD_DIM), F32)]),
        out_shape=jax.ShapeDtypeStruct((batch * seq, NSA_WIDTH), F32),
        compiler_params=_cparams(2),
        name="nsa_sample_slc",
    )(page_table, h, sel, ocmp, ks_new, vs_new, wk_all, wv_all, h, *([slc_k] * nps), *([slc_v] * nps))


def _rope_tables(pos):
    half = HEAD_DIM // 2
    inv = ROPE_THETA ** (-jnp.arange(half, dtype=F32) / half)
    ang = pos.astype(F32)[:, None] * inv[None, :]
    cos, sin = jnp.cos(ang), jnp.sin(ang)
    return jnp.concatenate([cos, cos], -1), jnp.concatenate([-sin, sin], -1)


def _prep_weights(l, w_in, gm_ln_g, gm_ln_b, gm_ws, gm_bs, cmp_wk, cmp_wv, conv_dw, conv_db, conv_ln_g,
                  conv_ln_b, conv_pw, w_out, ln1_g, ln1_b, peer_wq, peer_subkeys, peer_u, peer_v, ln2_g, ln2_b):
    w = w_in[l]
    o = np.cumsum([0, GM_WIDTH, GM_WIDTH, NSA_WIDTH] + [KV_WIDTH] * 6 + [3 * NSA_HEADS, CONV_WIDTH, CONV_WIDTH])
    gu, gv, q, kc, vc, ks, vs, kw, vw, gate, ca, cb = [w[:, o[i]:o[i + 1]] for i in range(12)]
    pad = jnp.zeros((w.shape[0], PROJ_WIDTH - C_GATE - 3 * NSA_HEADS), w.dtype)
    w_proj = jnp.concatenate([gu, gv, ca, cb, q, kc, ks, kw, vc, vs, vw, gate, pad], axis=1).astype(BF16)
    row = lambda a: a.reshape(1, -1)
    cw = jnp.stack([cmp_wk[l], cmp_wv[l]], axis=1)
    zero = jnp.zeros_like(cw)
    cw = jnp.stack([jnp.concatenate([cw, zero], -1), jnp.concatenate([zero, cw], -1)], axis=2)
    cw = jnp.broadcast_to(cw[..., None], cw.shape + (HEAD_DIM,))
    cw_lanes = jnp.transpose(cw, (1, 2, 3, 0, 4)).reshape(2, 2, SEL_BLOCK, KV_WIDTH)
    dw = jnp.concatenate([conv_dw[l], jnp.zeros((CONV_HALO - CONV_K, CONV_WIDTH), F32)], axis=0)
    return dict(
        w_proj=w_proj, gm_g=row(gm_ln_g[l]), gm_b=row(gm_ln_b[l]), gm_ws=gm_ws[l], gm_bs=gm_bs[l], cmp_w=cw,
        cmp_w_lanes=cw_lanes,
        conv_dw=dw, conv_db=row(conv_db[l]), conv_g=row(conv_ln_g[l]), conv_b=row(conv_ln_b[l]),
        conv_pw=conv_pw[l].astype(BF16), w_out=w_out[l].astype(BF16), ln1_g=row(ln1_g[l]), ln1_b=row(ln1_b[l]),
        wq=peer_wq[l].astype(BF16), subkeys=peer_subkeys[l].astype(BF16), u=peer_u[l].astype(BF16),
        v=peer_v[l].astype(BF16), ln2_g=row(ln2_g[l]), ln2_b=row(ln2_b[l]))


def _gm_weights(p, chunk):
    w = jnp.tril(p["gm_ws"][:, :chunk, :chunk])
    bs = jnp.repeat(p["gm_bs"][:, :chunk].T, GM_WIDTH // GM_GROUPS, axis=1)
    return w, bs


def _peer(x, p, *, tm_proj, tm_route, tm_dense, te):
    m = x.shape[0]
    dummy = jnp.zeros((tm_proj, HEAD_DIM), F32)
    qp = _proj(x, p["wq"], dummy, dummy, tm=tm_proj, rope_lo=0, rope_hi=0, name="peer_q")
    r2, cnt, f, e2 = _route(qp, p["subkeys"], tm=tm_route)
    return _peer_dense(x, p["u"], p["v"], r2, cnt, f, e2, p["ln2_g"], p["ln2_b"], tm=tm_dense, te=te)


def _kv_rows(h, col, batch, seq):
    return h[:, col:col + KV_WIDTH].reshape(batch, seq, NSA_KV_HEADS, HEAD_DIM)


def _layer_prompt(x, p, cos, sin, *, batch, seq):
    m = batch * seq
    h = _proj(x, p["w_proj"], cos, sin, tm=1024, rope_lo=ROPE_LO, rope_hi=ROPE_HI, name="proj_in")
    gw, gbs = _gm_weights(p, GM_CHUNK)
    gm, vn = _gmlp(h, p["gm_g"], p["gm_b"], gw.astype(BF16), gbs, chunk=GM_CHUNK)
    nsa = _nsa_prompt(h, p["cmp_w"], batch=batch, seq=seq, tq=256)
    hist = jnp.zeros((batch, CONV_HALO, CONV_WIDTH), F32)
    y, c = _conv(h, hist, p["conv_dw"], p["conv_db"], p["conv_g"], p["conv_b"], p["conv_pw"],
                 batch=batch, seq=seq, tq=256)
    x1 = _out_ln(gm, nsa, y, x, p["w_out"], p["ln1_g"], p["ln1_b"], tm=256)
    x2 = _peer(x1, p, tm_proj=1024, tm_route=512, tm_dense=512, te=512)
    keep = min(WINDOW, seq)
    kw, vw = _kv_rows(h, C_KW, batch, seq), _kv_rows(h, C_VW, batch, seq)
    start = ((seq - 1) // GM_CHUNK) * GM_CHUNK
    states = (_kv_rows(h, C_KC, batch, seq), _kv_rows(h, C_VC, batch, seq), _kv_rows(h, C_KS, batch, seq),
              _kv_rows(h, C_VS, batch, seq), kw[:, -keep:], vw[:, -keep:],
              c.reshape(batch, seq, CONV_WIDTH)[:, -(CONV_K - 1):], vn.reshape(batch, seq, GM_WIDTH)[:, start:])
    return x2, states


def _layer_sample(x, p, cos, sin, caches, layer, page_table, win_k, win_v, hist30, *, batch, seq, past_len):
    m = batch * seq
    h = _proj(x, p["w_proj"], cos, sin, tm=m, rope_lo=ROPE_LO, rope_hi=ROPE_HI, name="proj_in_s")
    pad_rows = lambda a: jnp.pad(a.reshape(batch, seq, -1), ((0, 0), (0, GM_CHUNK - seq), (0, 0))).reshape(batch * GM_CHUNK, -1)
    gw, gbs = _gm_weights(p, GM_CHUNK)
    hp = pad_rows(h[:, :2 * GM_WIDTH])
    gm, vn = _gmlp(hp, p["gm_g"], p["gm_b"], gw.astype(BF16), gbs, chunk=GM_CHUNK)
    unpad = lambda a: a.reshape(batch, GM_CHUNK, -1)[:, :seq].reshape(m, -1)
    gm, vn = unpad(gm), unpad(vn)
    kvs = [_kv_rows(h, col, batch, seq) for col in (C_KC, C_VC, C_KS, C_VS, C_KW, C_VW)]
    page = caches[0].shape[1]
    win_buf = win_k.shape[1]
    new_rows = lambda col, rows: jnp.pad(h[:, col:col + KV_WIDTH].reshape(batch, seq, KV_WIDTH),
                                         ((0, 0), (0, rows - seq), (0, 0)))
    flat = lambda a: a.reshape(batch, win_buf, KV_WIDTH)
    wk_all = jnp.concatenate([flat(win_k), new_rows(C_KW, HEAD_DIM)], axis=1)
    wv_all = jnp.concatenate([flat(win_v), new_rows(C_VW, HEAD_DIM)], axis=1)
    nsa = _nsa_sample(h, caches, layer, page_table, new_rows(C_KS, page), new_rows(C_VS, page), wk_all, wv_all,
                      p["cmp_w_lanes"], batch=batch, seq=seq, past_len=past_len, win_buf=win_buf)
    keep = min(WINDOW, past_len + seq)
    unflat = lambda a: a[:, :win_buf + seq][:, -keep:].reshape(batch, keep, NSA_KV_HEADS, HEAD_DIM)
    win_k, win_v = unflat(wk_all), unflat(wv_all)
    hist = jnp.pad(hist30, ((0, 0), (CONV_HALO - (CONV_K - 1), 0), (0, 0)))
    y, c = _conv(h, hist, p["conv_dw"], p["conv_db"], p["conv_g"], p["conv_b"], p["conv_pw"],
                 batch=batch, seq=seq, tq=seq)
    x1 = _out_ln(gm, nsa, y, x, p["w_out"], p["ln1_g"], p["ln1_b"], tm=m)
    x2 = _peer(x1, p, tm_proj=m, tm_route=m, tm_dense=m, te=512)
    conv_state = jnp.concatenate([hist30, c.reshape(batch, seq, CONV_WIDTH)], axis=1)[:, -(CONV_K - 1):]
    start = ((seq - 1) // GM_CHUNK) * GM_CHUNK
    states = (kvs[0], kvs[1], kvs[2], kvs[3], win_k, win_v, conv_state, vn.reshape(batch, seq, GM_WIDTH)[:, start:])
    return x2, states


def kernel(x_prompt, x_sample, cache_cmp_k, cache_cmp_v, cache_slc_k, cache_slc_v, cache_win_k, cache_win_v,
           state_conv, page_table, w_in, gm_ln_g, gm_ln_b, gm_ws, gm_bs, cmp_wk, cmp_wv, conv_dw, conv_db,
           conv_ln_g, conv_ln_b, conv_pw, w_out, ln1_g, ln1_b, peer_wq, peer_subkeys, peer_u, peer_v, ln2_g, ln2_b):
    bp, sp, _ = x_prompt.shape
    bs, ss, _ = x_sample.shape
    depth = w_in.shape[0]
    past_len = page_table.shape[1] * cache_cmp_k.shape[2]
    pos_p = jnp.arange(sp, dtype=jnp.int32)
    pos_s = past_len + jnp.arange(ss, dtype=jnp.int32)
    cos_p, sin_p = _rope_tables(pos_p)
    cos_s, sin_s = _rope_tables(jnp.tile(pos_s, bs))
    xp = x_prompt.reshape(bp * sp, D_MODEL)
    xs = x_sample.reshape(bs * ss, D_MODEL)
    p_states, s_states = [], []
    caches = [c.reshape(c.shape[0] * c.shape[1], c.shape[2], KV_WIDTH)
              for c in (cache_cmp_k, cache_cmp_v, cache_slc_k, cache_slc_v)]
    for l in range(depth):
        p = _prep_weights(l, w_in, gm_ln_g, gm_ln_b, gm_ws, gm_bs, cmp_wk, cmp_wv, conv_dw, conv_db, conv_ln_g,
                          conv_ln_b, conv_pw, w_out, ln1_g, ln1_b, peer_wq, peer_subkeys, peer_u, peer_v, ln2_g, ln2_b)
        xp, sp_state = _layer_prompt(xp, p, cos_p, sin_p, batch=bp, seq=sp)
        xs, ss_state = _layer_sample(xs, p, cos_s, sin_s, caches, l, page_table, cache_win_k[l], cache_win_v[l],
                                     state_conv[l], batch=bs, seq=ss, past_len=past_len)
        p_states.append(sp_state)
        s_states.append(ss_state)
    stack = lambda states, i: jnp.stack([st[i] for st in states], axis=0)
    outs = [xp.reshape(bp, sp, D_MODEL), xs.reshape(bs, ss, D_MODEL)]
    outs += [stack(p_states, i) for i in range(8)]
    outs += [stack(s_states, i) for i in range(8)]
    return tuple(outs)
```

```python
import functools

import numpy as np
import jax
import jax.numpy as jnp
from jax import lax
from jax.experimental import pallas as pl
from jax.experimental.pallas import tpu as pltpu

F32 = jnp.float32
BF16 = jnp.bfloat16

D_MODEL = 2048
HEAD_DIM = 128
NSA_HEADS = 8
NSA_KV_HEADS = 2
GQA_R = NSA_HEADS // NSA_KV_HEADS
GM_WIDTH = 512
GM_GROUPS = 4
GM_CHUNK = 128
CONV_WIDTH = 512
CONV_GROUPS = 4
CONV_K = 31
NSA_WIDTH = NSA_HEADS * HEAD_DIM
KV_WIDTH = NSA_KV_HEADS * HEAD_DIM
CMP_BLOCK = 32
SEL_BLOCK = 64
N_SELECT = 16
WINDOW = 512
FORCED_SCORE = 1.0e4
PEER_HEADS = 8
N_KEYS = 128
PEER_TOPK = 16
SUBKEY_DIM = 128
ROPE_THETA = 10000.0
LN_EPS = 1e-5
DEPTH = 2
ALPHA = (2 * DEPTH) ** 0.25

C_GU, C_GV, C_CA, C_CB, C_Q = 0, 512, 1024, 1536, 2048
C_KC, C_KS, C_KW, C_VC, C_VS, C_VW, C_GATE = 3072, 3328, 3584, 3840, 4096, 4352, 4608
PROJ_WIDTH = 4864
PROJ_TN = 256
ROPE_LO, ROPE_HI = C_Q // PROJ_TN, C_VC // PROJ_TN

V7X_VMEM_LIMIT = 56 * 1024 * 1024
NEG_INF = float("-inf")


def _cparams(n_axes, vmem=V7X_VMEM_LIMIT):
    return pltpu.CompilerParams(dimension_semantics=("arbitrary",) * n_axes, vmem_limit_bytes=vmem)


def _ln_lanes(z, g, b):
    mu = jnp.mean(z, axis=-1, keepdims=True)
    d = z - mu
    var = jnp.mean(d * d, axis=-1, keepdims=True)
    return d * lax.rsqrt(var + LN_EPS) * g + b


def _dot_nt(a, b):
    return lax.dot_general(a, b, (((1,), (1,)), ((), ())), preferred_element_type=F32)


def _proj_kernel(x_ref, w_ref, cos_ref, sin_ref, o_ref, xb_ref, *, rope_lo, rope_hi, tn):
    j = pl.program_id(1)

    @pl.when(j == 0)
    def _():
        xb_ref[...] = x_ref[...].astype(BF16)

    acc = jnp.dot(xb_ref[...], w_ref[...], preferred_element_type=F32)
    is_rope = jnp.logical_and(j >= rope_lo, j < rope_hi)

    @pl.when(is_rope)
    def _():
        cos, sin = cos_ref[...], sin_ref[...]
        for c in range(tn // HEAD_DIM):
            blk = acc[:, c * HEAD_DIM:(c + 1) * HEAD_DIM]
            o_ref[:, c * HEAD_DIM:(c + 1) * HEAD_DIM] = blk * cos + pltpu.roll(blk, HEAD_DIM // 2, 1) * sin

    @pl.when(jnp.logical_not(is_rope))
    def _():
        o_ref[...] = acc


def _proj(x, w, cos, sin, *, tm, rope_lo, rope_hi, name):
    m, k = x.shape
    n = w.shape[1]
    tn = PROJ_TN
    nper = cos.shape[0] // tm
    kern = functools.partial(_proj_kernel, rope_lo=rope_lo, rope_hi=rope_hi, tn=tn)
    return pl.pallas_call(
        kern,
        grid=(m // tm, n // tn),
        in_specs=[
            pl.BlockSpec((tm, k), lambda i, j: (i, 0)),
            pl.BlockSpec((k, tn), lambda i, j: (0, j)),
            pl.BlockSpec((tm, HEAD_DIM), lambda i, j: (i % nper, 0)),
            pl.BlockSpec((tm, HEAD_DIM), lambda i, j: (i % nper, 0)),
        ],
        out_specs=pl.BlockSpec((tm, tn), lambda i, j: (i, j)),
        out_shape=jax.ShapeDtypeStruct((m, n), F32),
        scratch_shapes=[pltpu.VMEM((tm, k), BF16)],
        compiler_params=_cparams(2),
        name=name,
    )(x, w, cos, sin)


def _gmlp_kernel(gu_ref, gv_ref, g_ref, b_ref, w_ref, bs_ref, o_ref, vn_ref):
    gw = GM_WIDTH // GM_GROUPS
    for h in range(GM_GROUPS):
        sl = slice(h * gw, (h + 1) * gw)
        v = jax.nn.gelu(gv_ref[:, sl])
        vn = _ln_lanes(v, g_ref[:, sl], b_ref[:, sl])
        vn_ref[:, sl] = vn
        s = jnp.dot(w_ref[h], vn.astype(BF16), preferred_element_type=F32) + bs_ref[:, sl]
        o_ref[:, sl] = jax.nn.gelu(gu_ref[:, sl]) * s


def _gmlp(h, g, b, w_tril, bs_full, *, chunk):
    m = h.shape[0]
    blk = lambda c: pl.BlockSpec((chunk, GM_WIDTH), lambda i, c=c: (i, c))
    full = lambda a: pl.BlockSpec(a.shape, lambda i: (0,) * a.ndim)
    return pl.pallas_call(
        _gmlp_kernel,
        grid=(m // chunk,),
        in_specs=[blk(C_GU // GM_WIDTH), blk(C_GV // GM_WIDTH), full(g), full(b), full(w_tril), full(bs_full)],
        out_specs=[pl.BlockSpec((chunk, GM_WIDTH), lambda i: (i, 0))] * 2,
        out_shape=[jax.ShapeDtypeStruct((m, GM_WIDTH), F32)] * 2,
        compiler_params=_cparams(1),
        name="gmlp",
    )(h, h, g, b, w_tril, bs_full)


CONV_HALO = 32


def _conv_kernel(ca_ref, cb_ref, cah_ref, cbh_ref, hist_ref, dw_ref, db_ref, g_ref, b_ref, pw_ref,
                 y_ref, c_ref, buf_ref, *, tq):
    i = pl.program_id(1)
    c = ca_ref[...] * jax.nn.sigmoid(cb_ref[...])
    c_ref[...] = c
    halo = cah_ref[...] * jax.nn.sigmoid(cbh_ref[...])
    halo = jnp.where(i == 0, hist_ref[0], halo)
    buf_ref[0:CONV_HALO, :] = halo
    buf_ref[CONV_HALO:CONV_HALO + tq, :] = c
    lead = CONV_HALO - (CONV_K - 1)
    acc = jnp.zeros((tq, CONV_WIDTH), F32)
    for k in range(CONV_K):
        acc = acc + dw_ref[k:k + 1, :] * buf_ref[lead + k:lead + k + tq, :]
    y = acc + db_ref[...]
    cw = CONV_WIDTH // CONV_GROUPS
    parts = []
    for gi in range(CONV_GROUPS):
        sl = slice(gi * cw, (gi + 1) * cw)
        yn = _ln_lanes(y[:, sl], g_ref[:, sl], b_ref[:, sl])
        parts.append((yn * jax.nn.sigmoid(yn)).astype(BF16))
    y_ref[...] = jnp.dot(jnp.concatenate(parts, axis=-1), pw_ref[...], preferred_element_type=F32)


def _conv(h, hist, dw, db, g, b, pw, *, batch, seq, tq):
    m = h.shape[0]
    nt = seq // tq
    row = lambda bi, i: bi * nt + i
    halo_row = lambda bi, i: jnp.maximum((bi * seq + i * tq) // CONV_HALO - 1, 0)
    full = lambda a: pl.BlockSpec(a.shape, lambda bi, i: (0,) * a.ndim)
    kern = functools.partial(_conv_kernel, tq=tq)
    return pl.pallas_call(
        kern,
        grid=(batch, nt),
        in_specs=[
            pl.BlockSpec((tq, CONV_WIDTH), lambda bi, i: (row(bi, i), C_CA // CONV_WIDTH)),
            pl.BlockSpec((tq, CONV_WIDTH), lambda bi, i: (row(bi, i), C_CB // CONV_WIDTH)),
            pl.BlockSpec((CONV_HALO, CONV_WIDTH), lambda bi, i: (halo_row(bi, i), C_CA // CONV_WIDTH)),
            pl.BlockSpec((CONV_HALO, CONV_WIDTH), lambda bi, i: (halo_row(bi, i), C_CB // CONV_WIDTH)),
            pl.BlockSpec((1, CONV_HALO, CONV_WIDTH), lambda bi, i: (bi, 0, 0)),
            full(dw), full(db), full(g), full(b), full(pw),
        ],
        out_specs=[pl.BlockSpec((tq, CONV_WIDTH), lambda bi, i: (row(bi, i), 0))] * 2,
        out_shape=[jax.ShapeDtypeStruct((m, CONV_WIDTH), F32)] * 2,
        scratch_shapes=[pltpu.VMEM((CONV_HALO + tq, CONV_WIDTH), F32)],
        compiler_params=_cparams(2),
        name="conv",
    )(h, h, h, h, hist, dw, db, g, b, pw)


def _masked_softmax(s, mask):
    s = jnp.where(mask, s, NEG_INF)
    m = jnp.max(s, axis=-1, keepdims=True)
    m = jnp.where(m > NEG_INF, m, 0.0)
    e = jnp.exp(s - m)
    return e / jnp.maximum(jnp.sum(e, axis=-1, keepdims=True), 1e-30)


def _select_blocks(imp, tpos, n_blocks):
    tq, lanes = imp.shape
    blk = lax.broadcasted_iota(jnp.int32, (tq, lanes), 1)
    cur = tpos // SEL_BLOCK
    valid = blk * SEL_BLOCK <= tpos
    forced = jnp.logical_or(blk == 0, jnp.logical_or(blk == cur, blk == cur - 1))
    score = jnp.where(valid, jnp.where(forced, FORCED_SCORE, imp), NEG_INF)
    beaten = jnp.zeros((tq, lanes), F32)
    for i in range(n_blocks):
        ci = score[:, i:i + 1]
        earlier = (blk > i).astype(F32)
        beaten = beaten + jnp.where(ci > score, 1.0, jnp.where(ci == score, earlier, 0.0))
    picked = jnp.logical_and(beaten < float(min(N_SELECT, n_blocks)), blk < n_blocks)
    return picked.astype(F32)


SLC_CHUNK = 512


def _online_softmax_step(m_ref, l_ref, acc_ref, r, s, mask, v):
    s = jnp.where(mask, s, NEG_INF)
    m_old = m_ref[r]
    m_new = jnp.maximum(m_old, jnp.max(s, axis=-1, keepdims=True))
    m_safe = jnp.where(m_new > NEG_INF, m_new, 0.0)
    alpha = jnp.exp(m_old - m_safe)
    e = jnp.exp(s - m_safe)
    l_ref[r] = alpha * l_ref[r] + jnp.sum(e, axis=-1, keepdims=True)
    acc_ref[r] = alpha * acc_ref[r] + jnp.dot(e.astype(BF16), v, preferred_element_type=F32)
    m_ref[r] = m_new


def _attend_scaled(qh, k, v, mask):
    s = jnp.where(mask, _dot_nt(qh, k), NEG_INF)
    m = jnp.max(s, axis=-1, keepdims=True)
    m = jnp.where(m > NEG_INF, m, 0.0)
    e = jnp.exp(s - m)
    l = jnp.sum(e, axis=-1, keepdims=True)
    return jnp.dot(e.astype(BF16), v, preferred_element_type=F32) / jnp.maximum(l, 1e-30)


def _nsa_prompt_kernel(q_ref, kc_ref, ks_ref, kw_ref, vc_ref, vs_ref, vw_ref, gate_ref, cw_ref, o_ref,
                       kvb_ref, cmpb_ref, m_ref, l_ref, acc_ref, *, tq, seq):
    qi = pl.program_id(1)
    t0 = qi * tq
    scale = HEAD_DIM ** -0.5
    n_slc = seq // SEL_BLOCK
    tpos = t0 + lax.broadcasted_iota(jnp.int32, (tq, 1), 0)
    lane_c = lax.broadcasted_iota(jnp.int32, (tq, 2 * n_slc), 1)
    cmp_last = jnp.where(lane_c < n_slc, lane_c * SEL_BLOCK + CMP_BLOCK - 1, (lane_c - n_slc) * SEL_BLOCK + SEL_BLOCK - 1)
    cmp_mask = cmp_last <= tpos
    win_len = tq + WINDOW
    w0 = pl.multiple_of(jnp.maximum(t0 - WINDOW, 0), tq)
    wpos = w0 + lax.broadcasted_iota(jnp.int32, (tq, win_len), 1)
    win_mask = jnp.logical_and(wpos <= tpos, wpos >= tpos - WINDOW)
    gates = jax.nn.sigmoid(gate_ref[:, 0:HEAD_DIM])

    @pl.when(qi == 0)
    def _():
        for n, ref in enumerate((ks_ref, vs_ref, kw_ref, vw_ref)):
            kvb_ref[n] = ref[0].astype(BF16)
        for g in range(NSA_KV_HEADS):
            gs = slice(g * HEAD_DIM, (g + 1) * HEAD_DIM)
            for which, ref in enumerate((kc_ref, vc_ref)):
                x3 = ref[0, :, gs].reshape(n_slc, SEL_BLOCK, HEAD_DIM)
                ev = jnp.sum(x3 * cw_ref[g, which, 0][None], axis=1)
                od = jnp.sum(x3 * cw_ref[g, which, 1][None], axis=1)
                cmpb_ref[which, g] = jnp.concatenate([ev, od], axis=0).astype(BF16)

    for g in range(NSA_KV_HEADS):
        gs = slice(g * HEAD_DIM, (g + 1) * HEAD_DIM)
        kcmp, vcmp = cmpb_ref[0, g], cmpb_ref[1, g]
        qs, o_cmp = [], []
        imp = jnp.zeros((tq, 2 * n_slc), F32)
        for r in range(GQA_R):
            hh = g * GQA_R + r
            qh = (q_ref[:, hh * HEAD_DIM:(hh + 1) * HEAD_DIM] * scale).astype(BF16)
            qs.append(qh)
            p = _masked_softmax(_dot_nt(qh, kcmp), cmp_mask)
            imp = imp + p
            o_cmp.append(jnp.dot(p.astype(BF16), vcmp, preferred_element_type=F32))
        sel = _select_blocks(imp[:, :n_slc] + imp[:, n_slc:], tpos, n_slc).astype(BF16)

        m_ref[...] = jnp.full(m_ref.shape, NEG_INF, F32)
        l_ref[...] = jnp.zeros_like(l_ref)
        acc_ref[...] = jnp.zeros_like(acc_ref)
        for j in range(seq // SLC_CHUNK):
            @pl.when(j * SLC_CHUNK < t0 + tq)
            def _():
                ks = slice(j * SLC_CHUNK, (j + 1) * SLC_CHUNK)
                key_blk = (j * SLC_CHUNK + lax.broadcasted_iota(jnp.int32, (n_slc, SLC_CHUNK), 1)) // SEL_BLOCK
                expand = (lax.broadcasted_iota(jnp.int32, (n_slc, SLC_CHUNK), 0) == key_blk).astype(BF16)
                sel_keys = jnp.dot(sel, expand, preferred_element_type=F32)
                kpos = j * SLC_CHUNK + lax.broadcasted_iota(jnp.int32, (tq, SLC_CHUNK), 1)
                mask = jnp.logical_and(sel_keys > 0.5, kpos <= tpos)
                k, v = kvb_ref[0, ks, gs], kvb_ref[1, ks, gs]
                for r in range(GQA_R):
                    _online_softmax_step(m_ref, l_ref, acc_ref, r, _dot_nt(qs[r], k), mask, v)

        kwin = kvb_ref[2, pl.ds(w0, win_len), gs]
        vwin = kvb_ref[3, pl.ds(w0, win_len), gs]
        for r in range(GQA_R):
            hh = g * GQA_R + r
            o_slc = acc_ref[r] / jnp.maximum(l_ref[r], 1e-30)
            o_win = _attend_scaled(qs[r], kwin, vwin, win_mask)
            o_ref[:, hh * HEAD_DIM:(hh + 1) * HEAD_DIM] = (
                gates[:, 3 * hh:3 * hh + 1] * o_cmp[r] + gates[:, 3 * hh + 1:3 * hh + 2] * o_slc
                + gates[:, 3 * hh + 2:3 * hh + 3] * o_win)


def _nsa_prompt(h, cmp_w, *, batch, seq, tq):
    m = h.shape[0]
    nt = seq // tq
    kv = lambda col: pl.BlockSpec((1, seq, KV_WIDTH), lambda bi, i, col=col: (bi, 0, col // KV_WIDTH))
    h3 = h.reshape(batch, seq, PROJ_WIDTH)
    kern = functools.partial(_nsa_prompt_kernel, tq=tq, seq=seq)
    return pl.pallas_call(
        kern,
        grid=(batch, nt),
        in_specs=[
            pl.BlockSpec((tq, NSA_WIDTH), lambda bi, i: (bi * nt + i, C_Q // NSA_WIDTH)),
            kv(C_KC), kv(C_KS), kv(C_KW), kv(C_VC), kv(C_VS), kv(C_VW),
            pl.BlockSpec((tq, KV_WIDTH), lambda bi, i: (bi * nt + i, C_GATE // KV_WIDTH)),
            pl.BlockSpec(cmp_w.shape, lambda bi, i: (0,) * cmp_w.ndim),
        ],
        out_specs=pl.BlockSpec((tq, NSA_WIDTH), lambda bi, i: (bi * nt + i, 0)),
        out_shape=jax.ShapeDtypeStruct((m, NSA_WIDTH), F32),
        scratch_shapes=[
            pltpu.VMEM((4, seq, KV_WIDTH), BF16),
            pltpu.VMEM((2, NSA_KV_HEADS, 2 * (seq // SEL_BLOCK), HEAD_DIM), BF16),
            pltpu.VMEM((GQA_R, tq, 1), F32), pltpu.VMEM((GQA_R, tq, 1), F32), pltpu.VMEM((GQA_R, tq, HEAD_DIM), F32),
        ],
        compiler_params=_cparams(2),
        name="nsa_prompt",
    )(h, h3, h3, h3, h3, h3, h3, h, cmp_w)


def _out_ln_kernel(gm_ref, nsa_ref, y_ref, x_ref, w_ref, g_ref, b_ref, o_ref):
    cat = jnp.concatenate([gm_ref[...].astype(BF16), nsa_ref[...].astype(BF16), y_ref[...].astype(BF16)], axis=-1)
    mix = jnp.dot(cat, w_ref[...], preferred_element_type=F32)
    o_ref[...] = _ln_lanes(ALPHA * x_ref[...] + mix, g_ref[...], b_ref[...])


def _out_ln(gm, nsa, y, x, w, g, b, *, tm):
    m = x.shape[0]
    row = lambda width: pl.BlockSpec((tm, width), lambda i: (i, 0))
    full = lambda a: pl.BlockSpec(a.shape, lambda i: (0,) * a.ndim)
    return pl.pallas_call(
        _out_ln_kernel,
        grid=(m // tm,),
        in_specs=[row(GM_WIDTH), row(NSA_WIDTH), row(CONV_WIDTH), row(D_MODEL), full(w), full(g), full(b)],
        out_specs=row(D_MODEL),
        out_shape=jax.ShapeDtypeStruct((m, D_MODEL), F32),
        compiler_params=_cparams(1),
        name="out_ln",
    )(gm, nsa, y, x, w, g, b)


def _extract_topk(s, vals_ref, rank_ref, exact_ties):
    n_rows, tm = s.shape
    rows = lax.broadcasted_iota(jnp.int32, (n_rows, tm), 0)
    krow = lax.broadcasted_iota(jnp.int32, (PEER_TOPK, tm), 0)
    work = s
    rank = jnp.full((n_rows, tm), float(PEER_TOPK), F32)
    vals = jnp.zeros((PEER_TOPK, tm), F32)
    for r in range(PEER_TOPK):
        mx = jnp.max(work, axis=0, keepdims=True)
        hit = work == mx
        if exact_ties:
            hit = rows == jnp.min(jnp.where(hit, rows, n_rows), axis=0, keepdims=True)
        rank = jnp.where(hit, float(r), rank)
        vals = jnp.where(krow == r, mx, vals)
        work = jnp.where(hit, NEG_INF, work)
    vals_ref[...] = vals
    rank_ref[...] = rank


def _topk_rows(s, vals_ref, rank_ref):
    _extract_topk(s, vals_ref, rank_ref, exact_ties=False)
    taken = jnp.sum((rank_ref[...] < float(PEER_TOPK)).astype(F32), axis=0, keepdims=True)
    tied = jnp.max(taken) > float(PEER_TOPK)

    @pl.when(tied)
    def _():
        _extract_topk(s, vals_ref, rank_ref, exact_ties=True)

    return vals_ref[...], rank_ref[...]


def _route_kernel(qp_ref, sk_ref, r2_ref, cnt_ref, f_ref, e2_ref, v1_ref, v2_ref, rank1_ref, rank2_ref):
    tm = qp_ref.shape[0]
    q1 = qp_ref[:, 0:SUBKEY_DIM].astype(BF16)
    q2 = qp_ref[:, SUBKEY_DIM:2 * SUBKEY_DIM].astype(BF16)
    s1 = _dot_nt(sk_ref[0, 0], q1)
    s2 = _dot_nt(sk_ref[0, 1], q2)
    v1, rank1 = _topk_rows(s1, v1_ref, rank1_ref)
    v2, rank2 = _topk_rows(s2, v2_ref, rank2_ref)
    krow = lax.broadcasted_iota(jnp.int32, (PEER_TOPK, tm), 0)
    cnt = jnp.zeros((PEER_TOPK, tm), F32)
    front = v1 + v2[0:1]
    m0 = v1[0:1] + v2[0:1]
    z = jnp.zeros((1, tm), F32)
    for _ in range(PEER_TOPK):
        mx = jnp.max(front, axis=0, keepdims=True)
        idx = jnp.min(jnp.where(front == mx, krow, PEER_TOPK), axis=0, keepdims=True)
        hit = krow == idx
        cnt = cnt + hit.astype(F32)
        z = z + jnp.exp(mx - m0)
        chosen = jnp.sum(jnp.where(hit, cnt, 0.0), axis=0, keepdims=True)
        nxt = jnp.full((1, tm), NEG_INF, F32)
        for c in range(1, PEER_TOPK):
            nxt = jnp.where(chosen == float(c), v2[c:c + 1], nxt)
        front = jnp.where(hit, v1 + nxt, front)
    cnt_a = jnp.zeros((N_KEYS, tm), F32)
    for r in range(PEER_TOPK):
        cnt_a = jnp.where(rank1 == float(r), cnt[r:r + 1], cnt_a)
    r2_ref[0] = rank2.astype(BF16)
    cnt_ref[0] = cnt_a
    f_ref[0] = jnp.exp(s1 - v1[0:1]) / z
    e2_ref[0] = jnp.exp(s2 - v2[0:1]).astype(BF16)


def _route(qp, subkeys, *, tm):
    m = qp.shape[0]
    out = lambda dt: jax.ShapeDtypeStruct((PEER_HEADS, N_KEYS, m), dt)
    ospec = pl.BlockSpec((1, N_KEYS, tm), lambda i, hd: (hd, 0, i))
    return pl.pallas_call(
        _route_kernel,
        grid=(m // tm, PEER_HEADS),
        in_specs=[
            pl.BlockSpec((tm, 2 * SUBKEY_DIM), lambda i, hd: (i, hd)),
            pl.BlockSpec((1, 2, N_KEYS, SUBKEY_DIM), lambda i, hd: (hd, 0, 0, 0)),
        ],
        out_specs=[ospec] * 4,
        out_shape=[out(BF16), out(F32), out(F32), out(BF16)],
        scratch_shapes=[pltpu.VMEM((PEER_TOPK, tm), F32), pltpu.VMEM((PEER_TOPK, tm), F32),
                        pltpu.VMEM((N_KEYS, tm), F32), pltpu.VMEM((N_KEYS, tm), F32)],
        compiler_params=_cparams(2),
        name="peer_route",
    )(qp, subkeys)


PEER_SUB = 256


def _peer_dense_kernel(x_ref, u_ref, v_ref, r2_ref, cnt_ref, f_ref, e2_ref, g_ref, b_ref, o_ref, xb_ref, *, te):
    j = pl.program_id(1)

    @pl.when(j == 0)
    def _():
        xb_ref[...] = x_ref[...].astype(BF16)
        o_ref[...] = jnp.zeros_like(o_ref)

    total = None
    for c in range(te // PEER_SUB):
        act = jax.nn.gelu(_dot_nt(u_ref[c * PEER_SUB:(c + 1) * PEER_SUB, :], xb_ref[...]))
        rows = []
        for al in range(PEER_SUB // N_KEYS):
            a = j * (te // N_KEYS) + c * (PEER_SUB // N_KEYS) + al
            gate = None
            for hd in range(PEER_HEADS):
                cnt = cnt_ref[hd, pl.ds(a, 1), :].astype(BF16)
                f = f_ref[hd, pl.ds(a, 1), :].astype(BF16)
                term = jnp.where(r2_ref[hd] < cnt, e2_ref[hd], jnp.zeros((), BF16)) * f
                gate = term if gate is None else gate + term
            rows.append(act[al * N_KEYS:(al + 1) * N_KEYS].astype(BF16) * gate)
        w = jnp.concatenate(rows, axis=0).T
        part = jnp.dot(w, v_ref[c * PEER_SUB:(c + 1) * PEER_SUB, :], preferred_element_type=F32)
        total = part if total is None else total + part
    o_ref[...] += total

    @pl.when(j == pl.num_programs(1) - 1)
    def _():
        o_ref[...] = _ln_lanes(ALPHA * x_ref[...] + o_ref[...], g_ref[...], b_ref[...])


def _peer_dense(x, u, v, r2, cnt, f, e2, g, b, *, tm, te):
    m = x.shape[0]
    n_exp = u.shape[0]
    rt = pl.BlockSpec((PEER_HEADS, N_KEYS, tm), lambda i, j: (0, 0, i))
    full = lambda a: pl.BlockSpec(a.shape, lambda i, j: (0,) * a.ndim)
    kern = functools.partial(_peer_dense_kernel, te=te)
    return pl.pallas_call(
        kern,
        grid=(m // tm, n_exp // te),
        in_specs=[
            pl.BlockSpec((tm, D_MODEL), lambda i, j: (i, 0)),
            pl.BlockSpec((te, D_MODEL), lambda i, j: (j, 0)),
            pl.BlockSpec((te, D_MODEL), lambda i, j: (j, 0)),
            rt, rt, rt, rt, full(g), full(b),
        ],
        out_specs=pl.BlockSpec((tm, D_MODEL), lambda i, j: (i, 0)),
        out_shape=jax.ShapeDtypeStruct((m, D_MODEL), F32),
        scratch_shapes=[pltpu.VMEM((tm, D_MODEL), BF16)],
        compiler_params=_cparams(2),
        name="peer_dense",
    )(x, u, v, r2, cnt, f, e2, g, b)


PAGES_PER_STEP = 4


def _group_queries(q_ref, g):
    heads = [q_ref[:, (g * GQA_R + r) * HEAD_DIM:(g * GQA_R + r + 1) * HEAD_DIM] for r in range(GQA_R)]
    return jnp.concatenate(heads, axis=0).astype(BF16)


def _page_rows(pages, g):
    n = pages[0].shape[0] // NSA_KV_HEADS
    return jnp.concatenate([r[pl.ds(g, n, stride=NSA_KV_HEADS), :] for r in pages], axis=0)


def _nsa_s_cmp_kernel(pt_ref, q_ref, cw_ref, *refs, seq, past_len, n_real, sel_lanes):
    nps = PAGES_PER_STEP
    k_pages, v_pages = refs[:nps], refs[nps:2 * nps]
    ocmp_ref, sel_ref, ke_ref, ko_ref, ve_ref, vo_ref = refs[2 * nps:]
    c = pl.program_id(1)
    n_pb = ke_ref.shape[0]
    rows = nps * (k_pages[0].shape[0] // NSA_KV_HEADS) // SEL_BLOCK
    off = pl.multiple_of(c * rows, rows)

    def compress(pages, which, even_ref, odd_ref):
        for g in range(NSA_KV_HEADS):
            gs = slice(g * HEAD_DIM, (g + 1) * HEAD_DIM)
            x3 = _page_rows(pages, g).reshape(rows, SEL_BLOCK, HEAD_DIM)
            even_ref[pl.ds(off, rows), gs] = jnp.sum(x3 * cw_ref[g, which, 0][None], axis=1)
            odd_ref[pl.ds(off, rows), gs] = jnp.sum(x3 * cw_ref[g, which, 1][None], axis=1)

    compress(k_pages, 0, ke_ref, ko_ref)
    compress(v_pages, 1, ve_ref, vo_ref)

    @pl.when(c == pl.num_programs(1) - 1)
    def _():
        scale = HEAD_DIM ** -0.5
        nq = GQA_R * seq
        qrow = lax.broadcasted_iota(jnp.int32, (nq, 1), 0)
        qpos = past_len + qrow % seq
        lane = lax.broadcasted_iota(jnp.int32, (nq, 2 * n_pb), 1)
        cmp_last = jnp.where(lane < n_pb, lane * SEL_BLOCK + CMP_BLOCK - 1, (lane - n_pb) * SEL_BLOCK + SEL_BLOCK - 1)
        cmp_mask = cmp_last <= qpos
        tpos = past_len + lax.broadcasted_iota(jnp.int32, (seq, 1), 0)
        for g in range(NSA_KV_HEADS):
            gs = slice(g * HEAD_DIM, (g + 1) * HEAD_DIM)
            kcat = jnp.concatenate([ke_ref[:, gs], ko_ref[:, gs]], axis=0).astype(BF16)
            vcat = jnp.concatenate([ve_ref[:, gs], vo_ref[:, gs]], axis=0).astype(BF16)
            p = _masked_softmax(_dot_nt(_group_queries(q_ref, g), kcat) * scale, cmp_mask)
            ocmp_ref[0, g] = jnp.dot(p.astype(BF16), vcat, preferred_element_type=F32)
            imp = p[0:seq]
            for r in range(1, GQA_R):
                imp = imp + p[r * seq:(r + 1) * seq]
            imp = imp[:, :n_pb] + imp[:, n_pb:]
            imp = jnp.concatenate([imp, jnp.zeros((seq, sel_lanes - n_pb), F32)], axis=1)
            sel_ref[0, g] = _select_blocks(imp, tpos, n_real)


def _nsa_s_slc_kernel(pt_ref, q_ref, sel_ref, ocmp_ref, ksn_ref, vsn_ref, wk_ref, wv_ref, gate_ref, *refs,
                      seq, past_len, sel_lanes, win_buf):
    nps = PAGES_PER_STEP
    k_pages, v_pages = refs[:nps], refs[nps:2 * nps]
    o_ref, m_ref, l_ref, acc_ref = refs[2 * nps:]
    c = pl.program_id(1)
    scale = HEAD_DIM ** -0.5
    nq = GQA_R * seq
    chunk = nps * (k_pages[0].shape[0] // NSA_KV_HEADS)
    qrow = lax.broadcasted_iota(jnp.int32, (nq, 1), 0)
    qpos = past_len + qrow % seq

    @pl.when(c == 0)
    def _():
        m_ref[...] = jnp.full(m_ref.shape, NEG_INF, F32)
        l_ref[...] = jnp.zeros_like(l_ref)
        acc_ref[...] = jnp.zeros_like(acc_ref)

    def online_update(g, s, mask, v):
        _online_softmax_step(m_ref, l_ref, acc_ref, g, s, mask, v)

    def key_mask(g, first_block, first_pos, n_keys):
        key_blk = first_block + lax.broadcasted_iota(jnp.int32, (sel_lanes, n_keys), 1) // SEL_BLOCK
        expand = (lax.broadcasted_iota(jnp.int32, (sel_lanes, n_keys), 0) == key_blk).astype(BF16)
        sel_keys = jnp.dot(sel_ref[0, g].astype(BF16), expand, preferred_element_type=F32)
        sel_keys = jnp.concatenate([sel_keys] * GQA_R, axis=0)
        kpos = first_pos + lax.broadcasted_iota(jnp.int32, (nq, n_keys), 1)
        return jnp.logical_and(sel_keys > 0.5, kpos <= qpos)

    for g in range(NSA_KV_HEADS):
        mask = key_mask(g, c * (chunk // SEL_BLOCK), c * chunk, chunk)
        scores = _dot_nt(_group_queries(q_ref, g), _page_rows(k_pages, g).astype(BF16)) * scale
        online_update(g, scores, mask, _page_rows(v_pages, g).astype(BF16))

    @pl.when(c == pl.num_programs(1) - 1)
    def _():
        gates = jax.nn.sigmoid(gate_ref[:, 0:HEAD_DIM])
        n_new = ksn_ref.shape[1]
        n_win = wk_ref.shape[1]
        wpos = past_len - win_buf + lax.broadcasted_iota(jnp.int32, (nq, n_win), 1)
        win_mask = jnp.logical_and(wpos <= qpos, wpos >= qpos - WINDOW)
        for g in range(NSA_KV_HEADS):
            gs = slice(g * HEAD_DIM, (g + 1) * HEAD_DIM)
            qg = _group_queries(q_ref, g)
            online_update(g, _dot_nt(qg, ksn_ref[0, :, gs].astype(BF16)) * scale,
                          key_mask(g, past_len // SEL_BLOCK, past_len, n_new), vsn_ref[0, :, gs].astype(BF16))
            o_slc = acc_ref[g] / jnp.maximum(l_ref[g], 1e-30)
            p = _masked_softmax(_dot_nt(qg, wk_ref[0, :, gs].astype(BF16)) * scale, win_mask)
            o_win = jnp.dot(p.astype(BF16), wv_ref[0, :, gs].astype(BF16), preferred_element_type=F32)
            o_cmp = ocmp_ref[0, g]
            for r in range(GQA_R):
                hh = g * GQA_R + r
                rs = slice(r * seq, (r + 1) * seq)
                o_ref[:, hh * HEAD_DIM:(hh + 1) * HEAD_DIM] = (
                    gates[:, 3 * hh:3 * hh + 1] * o_cmp[rs] + gates[:, 3 * hh + 1:3 * hh + 2] * o_slc[rs]
                    + gates[:, 3 * hh + 2:3 * hh + 3] * o_win[rs])


def _nsa_sample(h, caches, layer, page_table, ks_new, vs_new, wk_all, wv_all, cw, *, batch, seq, past_len, win_buf):
    cmp_k, cmp_v, slc_k, slc_v = caches
    nps = PAGES_PER_STEP
    n_pages, page = page_table.shape[1], ks_new.shape[1]
    page_rows = page * NSA_KV_HEADS
    n_pool = cmp_k.shape[0] // (DEPTH * page_rows)
    n_chunks = n_pages // nps
    n_pb = past_len // SEL_BLOCK
    n_real = n_pb + 1
    sel_lanes = -(-n_real // 128) * 128
    nq = GQA_R * seq
    page_spec = lambda k: pl.BlockSpec((page_rows, HEAD_DIM),
                                       lambda b, c, pt, k=k: (pt[b, c * nps + k] + layer * n_pool, 0))
    pages = [page_spec(k) for k in range(nps)]
    q_spec = pl.BlockSpec((seq, NSA_WIDTH), lambda b, c, pt: (b, C_Q // NSA_WIDTH))
    per_b = lambda shape: pl.BlockSpec((1,) + shape, lambda b, c, pt: (b,) + (0,) * len(shape))
    ocmp, sel = pl.pallas_call(
        functools.partial(_nsa_s_cmp_kernel, seq=seq, past_len=past_len, n_real=n_real, sel_lanes=sel_lanes),
        grid_spec=pltpu.PrefetchScalarGridSpec(
            num_scalar_prefetch=1, grid=(batch, n_chunks),
            in_specs=[q_spec, pl.BlockSpec(cw.shape, lambda b, c, pt: (0,) * cw.ndim)] + pages + pages,
            out_specs=[per_b((NSA_KV_HEADS, nq, HEAD_DIM)), per_b((NSA_KV_HEADS, seq, sel_lanes))],
            scratch_shapes=[pltpu.VMEM((n_pb, KV_WIDTH), F32)] * 4),
        out_shape=[jax.ShapeDtypeStruct((batch, NSA_KV_HEADS, nq, HEAD_DIM), F32),
                   jax.ShapeDtypeStruct((batch, NSA_KV_HEADS, seq, sel_lanes), F32)],
        compiler_params=_cparams(2),
        name="nsa_sample_cmp",
    )(page_table, h, cw, *([cmp_k] * nps), *([cmp_v] * nps))
    return pl.pallas_call(
        functools.partial(_nsa_s_slc_kernel, seq=seq, past_len=past_len, sel_lanes=sel_lanes, win_buf=win_buf),
        grid_spec=pltpu.PrefetchScalarGridSpec(
            num_scalar_prefetch=1, grid=(batch, n_chunks),
            in_specs=[q_spec, per_b((NSA_KV_HEADS, seq, sel_lanes)), per_b((NSA_KV_HEADS, nq, HEAD_DIM)),
                      per_b(ks_new.shape[1:]), per_b(vs_new.shape[1:]), per_b(wk_all.shape[1:]), per_b(wv_all.shape[1:]),
                      pl.BlockSpec((seq, KV_WIDTH), lambda b, c, pt: (b, C_GATE // KV_WIDTH))] + pages + pages,
            out_specs=pl.BlockSpec((seq, NSA_WIDTH), lambda b, c, pt: (b, 0)),
            scratch_shapes=[pltpu.VMEM((NSA_KV_HEADS, nq, 1), F32), pltpu.VMEM((NSA_KV_HEADS, nq, 1), F32),
                            pltpu.VMEM((NSA_KV_HEADS, nq, HEAD_DIM), F32)]),
        out_shape=jax.ShapeDtypeStruct((batch * seq, NSA_WIDTH), F32),
        compiler_params=_cparams(2),
        name="nsa_sample_slc",
    )(page_table, h, sel, ocmp, ks_new, vs_new, wk_all, wv_all, h, *([slc_k] * nps), *([slc_v] * nps))


def _rope_tables(pos):
    half = HEAD_DIM // 2
    inv = ROPE_THETA ** (-jnp.arange(half, dtype=F32) / half)
    ang = pos.astype(F32)[:, None] * inv[None, :]
    cos, sin = jnp.cos(ang), jnp.sin(ang)
    return jnp.concatenate([cos, cos], -1), jnp.concatenate([-sin, sin], -1)


def _prep_weights(l, w_in, gm_ln_g, gm_ln_b, gm_ws, gm_bs, cmp_wk, cmp_wv, conv_dw, conv_db, conv_ln_g,
                  conv_ln_b, conv_pw, w_out, ln1_g, ln1_b, peer_wq, peer_subkeys, peer_u, peer_v, ln2_g, ln2_b):
    w = w_in[l]
    o = np.cumsum([0, GM_WIDTH, GM_WIDTH, NSA_WIDTH] + [KV_WIDTH] * 6 + [3 * NSA_HEADS, CONV_WIDTH, CONV_WIDTH])
    gu, gv, q, kc, vc, ks, vs, kw, vw, gate, ca, cb = [w[:, o[i]:o[i + 1]] for i in range(12)]
    pad = jnp.zeros((w.shape[0], PROJ_WIDTH - C_GATE - 3 * NSA_HEADS), w.dtype)
    w_proj = jnp.concatenate([gu, gv, ca, cb, q, kc, ks, kw, vc, vs, vw, gate, pad], axis=1).astype(BF16)
    row = lambda a: a.reshape(1, -1)
    cw = jnp.stack([cmp_wk[l], cmp_wv[l]], axis=1)
    zero = jnp.zeros_like(cw)
    cw = jnp.stack([jnp.concatenate([cw, zero], -1), jnp.concatenate([zero, cw], -1)], axis=2)
    cw = jnp.broadcast_to(cw[..., None], cw.shape + (HEAD_DIM,))
    dw = jnp.concatenate([conv_dw[l], jnp.zeros((CONV_HALO - CONV_K, CONV_WIDTH), F32)], axis=0)
    return dict(
        w_proj=w_proj, gm_g=row(gm_ln_g[l]), gm_b=row(gm_ln_b[l]), gm_ws=gm_ws[l], gm_bs=gm_bs[l], cmp_w=cw,
        conv_dw=dw, conv_db=row(conv_db[l]), conv_g=row(conv_ln_g[l]), conv_b=row(conv_ln_b[l]),
        conv_pw=conv_pw[l].astype(BF16), w_out=w_out[l].astype(BF16), ln1_g=row(ln1_g[l]), ln1_b=row(ln1_b[l]),
        wq=peer_wq[l].astype(BF16), subkeys=peer_subkeys[l].astype(BF16), u=peer_u[l].astype(BF16),
        v=peer_v[l].astype(BF16), ln2_g=row(ln2_g[l]), ln2_b=row(ln2_b[l]))


def _gm_weights(p, chunk):
    w = jnp.tril(p["gm_ws"][:, :chunk, :chunk])
    bs = jnp.repeat(p["gm_bs"][:, :chunk].T, GM_WIDTH // GM_GROUPS, axis=1)
    return w, bs


def _peer(x, p, *, tm_proj, tm_route, tm_dense, te):
    m = x.shape[0]
    dummy = jnp.zeros((tm_proj, HEAD_DIM), F32)
    qp = _proj(x, p["wq"], dummy, dummy, tm=tm_proj, rope_lo=0, rope_hi=0, name="peer_q")
    r2, cnt, f, e2 = _route(qp, p["subkeys"], tm=tm_route)
    return _peer_dense(x, p["u"], p["v"], r2, cnt, f, e2, p["ln2_g"], p["ln2_b"], tm=tm_dense, te=te)


def _kv_rows(h, col, batch, seq):
    return h[:, col:col + KV_WIDTH].reshape(batch, seq, NSA_KV_HEADS, HEAD_DIM)


def _layer_prompt(x, p, cos, sin, *, batch, seq):
    m = batch * seq
    h = _proj(x, p["w_proj"], cos, sin, tm=1024, rope_lo=ROPE_LO, rope_hi=ROPE_HI, name="proj_in")
    gw, gbs = _gm_weights(p, GM_CHUNK)
    gm, vn = _gmlp(h, p["gm_g"], p["gm_b"], gw.astype(BF16), gbs, chunk=GM_CHUNK)
    nsa = _nsa_prompt(h, p["cmp_w"], batch=batch, seq=seq, tq=256)
    hist = jnp.zeros((batch, CONV_HALO, CONV_WIDTH), F32)
    y, c = _conv(h, hist, p["conv_dw"], p["conv_db"], p["conv_g"], p["conv_b"], p["conv_pw"],
                 batch=batch, seq=seq, tq=256)
    x1 = _out_ln(gm, nsa, y, x, p["w_out"], p["ln1_g"], p["ln1_b"], tm=256)
    x2 = _peer(x1, p, tm_proj=1024, tm_route=512, tm_dense=512, te=512)
    keep = min(WINDOW, seq)
    kw, vw = _kv_rows(h, C_KW, batch, seq), _kv_rows(h, C_VW, batch, seq)
    start = ((seq - 1) // GM_CHUNK) * GM_CHUNK
    states = (_kv_rows(h, C_KC, batch, seq), _kv_rows(h, C_VC, batch, seq), _kv_rows(h, C_KS, batch, seq),
              _kv_rows(h, C_VS, batch, seq), kw[:, -keep:], vw[:, -keep:],
              c.reshape(batch, seq, CONV_WIDTH)[:, -(CONV_K - 1):], vn.reshape(batch, seq, GM_WIDTH)[:, start:])
    return x2, states


def _layer_sample(x, p, cos, sin, caches, layer, page_table, win_k, win_v, hist30, *, batch, seq, past_len):
    m = batch * seq
    h = _proj(x, p["w_proj"], cos, sin, tm=m, rope_lo=ROPE_LO, rope_hi=ROPE_HI, name="proj_in_s")
    pad_rows = lambda a: jnp.pad(a.reshape(batch, seq, -1), ((0, 0), (0, GM_CHUNK - seq), (0, 0))).reshape(batch * GM_CHUNK, -1)
    gw, gbs = _gm_weights(p, GM_CHUNK)
    hp = pad_rows(h[:, :2 * GM_WIDTH])
    gm, vn = _gmlp(hp, p["gm_g"], p["gm_b"], gw.astype(BF16), gbs, chunk=GM_CHUNK)
    unpad = lambda a: a.reshape(batch, GM_CHUNK, -1)[:, :seq].reshape(m, -1)
    gm, vn = unpad(gm), unpad(vn)
    kvs = [_kv_rows(h, col, batch, seq) for col in (C_KC, C_VC, C_KS, C_VS, C_KW, C_VW)]
    win_buf = win_k.shape[1]
    page = past_len // page_table.shape[1]
    new_rows = lambda col, rows: jnp.pad(h[:, col:col + KV_WIDTH].reshape(batch, seq, KV_WIDTH),
                                         ((0, 0), (0, rows - seq), (0, 0)))
    flat = lambda a: a.reshape(batch, win_buf, KV_WIDTH)
    wk_all = jnp.concatenate([flat(win_k), new_rows(C_KW, HEAD_DIM)], axis=1)
    wv_all = jnp.concatenate([flat(win_v), new_rows(C_VW, HEAD_DIM)], axis=1)
    nsa = _nsa_sample(h, caches, layer, page_table, new_rows(C_KS, page), new_rows(C_VS, page), wk_all, wv_all,
                      p["cmp_w"], batch=batch, seq=seq, past_len=past_len, win_buf=win_buf)
    keep = min(WINDOW, past_len + seq)
    unflat = lambda a: a[:, :win_buf + seq][:, -keep:].reshape(batch, keep, NSA_KV_HEADS, HEAD_DIM)
    win_k, win_v = unflat(wk_all), unflat(wv_all)
    hist = jnp.pad(hist30, ((0, 0), (CONV_HALO - (CONV_K - 1), 0), (0, 0)))
    y, c = _conv(h, hist, p["conv_dw"], p["conv_db"], p["conv_g"], p["conv_b"], p["conv_pw"],
                 batch=batch, seq=seq, tq=seq)
    x1 = _out_ln(gm, nsa, y, x, p["w_out"], p["ln1_g"], p["ln1_b"], tm=m)
    x2 = _peer(x1, p, tm_proj=m, tm_route=m, tm_dense=m, te=512)
    conv_state = jnp.concatenate([hist30, c.reshape(batch, seq, CONV_WIDTH)], axis=1)[:, -(CONV_K - 1):]
    start = ((seq - 1) // GM_CHUNK) * GM_CHUNK
    states = (kvs[0], kvs[1], kvs[2], kvs[3], win_k, win_v, conv_state, vn.reshape(batch, seq, GM_WIDTH)[:, start:])
    return x2, states


def kernel(x_prompt, x_sample, cache_cmp_k, cache_cmp_v, cache_slc_k, cache_slc_v, cache_win_k, cache_win_v,
           state_conv, page_table, w_in, gm_ln_g, gm_ln_b, gm_ws, gm_bs, cmp_wk, cmp_wv, conv_dw, conv_db,
           conv_ln_g, conv_ln_b, conv_pw, w_out, ln1_g, ln1_b, peer_wq, peer_subkeys, peer_u, peer_v, ln2_g, ln2_b):
    bp, sp, _ = x_prompt.shape
    bs, ss, _ = x_sample.shape
    depth = w_in.shape[0]
    past_len = page_table.shape[1] * cache_cmp_k.shape[2]
    pos_p = jnp.arange(sp, dtype=jnp.int32)
    pos_s = past_len + jnp.arange(ss, dtype=jnp.int32)
    cos_p, sin_p = _rope_tables(pos_p)
    cos_s, sin_s = _rope_tables(jnp.tile(pos_s, bs))
    xp = x_prompt.reshape(bp * sp, D_MODEL)
    xs = x_sample.reshape(bs * ss, D_MODEL)
    p_states, s_states = [], []
    caches = [c.reshape(-1, HEAD_DIM) for c in (cache_cmp_k, cache_cmp_v, cache_slc_k, cache_slc_v)]
    for l in range(depth):
        p = _prep_weights(l, w_in, gm_ln_g, gm_ln_b, gm_ws, gm_bs, cmp_wk, cmp_wv, conv_dw, conv_db, conv_ln_g,
                          conv_ln_b, conv_pw, w_out, ln1_g, ln1_b, peer_wq, peer_subkeys, peer_u, peer_v, ln2_g, ln2_b)
        xp, sp_state = _layer_prompt(xp, p, cos_p, sin_p, batch=bp, seq=sp)
        xs, ss_state = _layer_sample(xs, p, cos_s, sin_s, caches, l, page_table, cache_win_k[l], cache_win_v[l],
                                     state_conv[l], batch=bs, seq=ss, past_len=past_len)
        p_states.append(sp_state)
        s_states.append(ss_state)
    stack = lambda states, i: jnp.stack([st[i] for st in states], axis=0)
    outs = [xp.reshape(bp, sp, D_MODEL), xs.reshape(bs, ss, D_MODEL)]
    outs += [stack(p_states, i) for i in range(8)]
    outs += [stack(s_states, i) for i in range(8)]
    return tuple(outs)
```

```python
import functools

import numpy as np
import jax
import jax.numpy as jnp
from jax import lax
from jax.experimental import pallas as pl
from jax.experimental.pallas import tpu as pltpu

F32 = jnp.float32
BF16 = jnp.bfloat16

D_MODEL = 2048
HEAD_DIM = 128
NSA_HEADS = 8
NSA_KV_HEADS = 2
GQA_R = NSA_HEADS // NSA_KV_HEADS
GM_WIDTH = 512
GM_GROUPS = 4
GM_CHUNK = 128
CONV_WIDTH = 512
CONV_GROUPS = 4
CONV_K = 31
NSA_WIDTH = NSA_HEADS * HEAD_DIM
KV_WIDTH = NSA_KV_HEADS * HEAD_DIM
CMP_BLOCK = 32
SEL_BLOCK = 64
N_SELECT = 16
WINDOW = 512
FORCED_SCORE = 1.0e4
PEER_HEADS = 8
N_KEYS = 128
PEER_TOPK = 16
SUBKEY_DIM = 128
ROPE_THETA = 10000.0
LN_EPS = 1e-5
DEPTH = 2
ALPHA = (2 * DEPTH) ** 0.25

C_GU, C_GV, C_CA, C_CB, C_Q = 0, 512, 1024, 1536, 2048
C_KC, C_KS, C_KW, C_VC, C_VS, C_VW, C_GATE = 3072, 3328, 3584, 3840, 4096, 4352, 4608
PROJ_WIDTH = 4864
PROJ_TN = 256
ROPE_LO, ROPE_HI = C_Q // PROJ_TN, C_VC // PROJ_TN

V7X_VMEM_LIMIT = 56 * 1024 * 1024
NEG_INF = float("-inf")


def _cparams(n_axes, vmem=V7X_VMEM_LIMIT):
    return pltpu.CompilerParams(dimension_semantics=("arbitrary",) * n_axes, vmem_limit_bytes=vmem)


def _ln_lanes(z, g, b):
    mu = jnp.mean(z, axis=-1, keepdims=True)
    d = z - mu
    var = jnp.mean(d * d, axis=-1, keepdims=True)
    return d * lax.rsqrt(var + LN_EPS) * g + b


def _dot_nt(a, b):
    return lax.dot_general(a, b, (((1,), (1,)), ((), ())), preferred_element_type=F32)


def _proj_kernel(x_ref, w_ref, cos_ref, sin_ref, o_ref, xb_ref, *, rope_lo, rope_hi, tn):
    j = pl.program_id(1)

    @pl.when(j == 0)
    def _():
        xb_ref[...] = x_ref[...].astype(BF16)

    acc = jnp.dot(xb_ref[...], w_ref[...], preferred_element_type=F32)
    is_rope = jnp.logical_and(j >= rope_lo, j < rope_hi)

    @pl.when(is_rope)
    def _():
        cos, sin = cos_ref[...], sin_ref[...]
        for c in range(tn // HEAD_DIM):
            blk = acc[:, c * HEAD_DIM:(c + 1) * HEAD_DIM]
            o_ref[:, c * HEAD_DIM:(c + 1) * HEAD_DIM] = blk * cos + pltpu.roll(blk, HEAD_DIM // 2, 1) * sin

    @pl.when(jnp.logical_not(is_rope))
    def _():
        o_ref[...] = acc


def _proj(x, w, cos, sin, *, tm, rope_lo, rope_hi, name):
    m, k = x.shape
    n = w.shape[1]
    tn = PROJ_TN
    nper = cos.shape[0] // tm
    kern = functools.partial(_proj_kernel, rope_lo=rope_lo, rope_hi=rope_hi, tn=tn)
    return pl.pallas_call(
        kern,
        grid=(m // tm, n // tn),
        in_specs=[
            pl.BlockSpec((tm, k), lambda i, j: (i, 0)),
            pl.BlockSpec((k, tn), lambda i, j: (0, j)),
            pl.BlockSpec((tm, HEAD_DIM), lambda i, j: (i % nper, 0)),
            pl.BlockSpec((tm, HEAD_DIM), lambda i, j: (i % nper, 0)),
        ],
        out_specs=pl.BlockSpec((tm, tn), lambda i, j: (i, j)),
        out_shape=jax.ShapeDtypeStruct((m, n), F32),
        scratch_shapes=[pltpu.VMEM((tm, k), BF16)],
        compiler_params=_cparams(2),
        name=name,
    )(x, w, cos, sin)


def _gmlp_kernel(gu_ref, gv_ref, g_ref, b_ref, w_ref, bs_ref, o_ref, vn_ref):
    gw = GM_WIDTH // GM_GROUPS
    for h in range(GM_GROUPS):
        sl = slice(h * gw, (h + 1) * gw)
        v = jax.nn.gelu(gv_ref[:, sl])
        vn = _ln_lanes(v, g_ref[:, sl], b_ref[:, sl])
        vn_ref[:, sl] = vn
        s = jnp.dot(w_ref[h], vn.astype(BF16), preferred_element_type=F32) + bs_ref[:, sl]
        o_ref[:, sl] = jax.nn.gelu(gu_ref[:, sl]) * s


def _gmlp(h, g, b, w_tril, bs_full, *, chunk):
    m = h.shape[0]
    blk = lambda c: pl.BlockSpec((chunk, GM_WIDTH), lambda i, c=c: (i, c))
    full = lambda a: pl.BlockSpec(a.shape, lambda i: (0,) * a.ndim)
    return pl.pallas_call(
        _gmlp_kernel,
        grid=(m // chunk,),
        in_specs=[blk(C_GU // GM_WIDTH), blk(C_GV // GM_WIDTH), full(g), full(b), full(w_tril), full(bs_full)],
        out_specs=[pl.BlockSpec((chunk, GM_WIDTH), lambda i: (i, 0))] * 2,
        out_shape=[jax.ShapeDtypeStruct((m, GM_WIDTH), F32)] * 2,
        compiler_params=_cparams(1),
        name="gmlp",
    )(h, h, g, b, w_tril, bs_full)


CONV_HALO = 32


def _conv_kernel(ca_ref, cb_ref, cah_ref, cbh_ref, hist_ref, dw_ref, db_ref, g_ref, b_ref, pw_ref,
                 y_ref, c_ref, buf_ref, *, tq):
    i = pl.program_id(1)
    c = ca_ref[...] * jax.nn.sigmoid(cb_ref[...])
    c_ref[...] = c
    halo = cah_ref[...] * jax.nn.sigmoid(cbh_ref[...])
    halo = jnp.where(i == 0, hist_ref[0], halo)
    buf_ref[0:CONV_HALO, :] = halo
    buf_ref[CONV_HALO:CONV_HALO + tq, :] = c
    lead = CONV_HALO - (CONV_K - 1)
    acc = jnp.zeros((tq, CONV_WIDTH), F32)
    for k in range(CONV_K):
        acc = acc + dw_ref[k:k + 1, :] * buf_ref[lead + k:lead + k + tq, :]
    y = acc + db_ref[...]
    cw = CONV_WIDTH // CONV_GROUPS
    parts = []
    for gi in range(CONV_GROUPS):
        sl = slice(gi * cw, (gi + 1) * cw)
        yn = _ln_lanes(y[:, sl], g_ref[:, sl], b_ref[:, sl])
        parts.append((yn * jax.nn.sigmoid(yn)).astype(BF16))
    y_ref[...] = jnp.dot(jnp.concatenate(parts, axis=-1), pw_ref[...], preferred_element_type=F32)


def _conv(h, hist, dw, db, g, b, pw, *, batch, seq, tq):
    m = h.shape[0]
    nt = seq // tq
    row = lambda bi, i: bi * nt + i
    halo_row = lambda bi, i: jnp.maximum((bi * seq + i * tq) // CONV_HALO - 1, 0)
    full = lambda a: pl.BlockSpec(a.shape, lambda bi, i: (0,) * a.ndim)
    kern = functools.partial(_conv_kernel, tq=tq)
    return pl.pallas_call(
        kern,
        grid=(batch, nt),
        in_specs=[
            pl.BlockSpec((tq, CONV_WIDTH), lambda bi, i: (row(bi, i), C_CA // CONV_WIDTH)),
            pl.BlockSpec((tq, CONV_WIDTH), lambda bi, i: (row(bi, i), C_CB // CONV_WIDTH)),
            pl.BlockSpec((CONV_HALO, CONV_WIDTH), lambda bi, i: (halo_row(bi, i), C_CA // CONV_WIDTH)),
            pl.BlockSpec((CONV_HALO, CONV_WIDTH), lambda bi, i: (halo_row(bi, i), C_CB // CONV_WIDTH)),
            pl.BlockSpec((1, CONV_HALO, CONV_WIDTH), lambda bi, i: (bi, 0, 0)),
            full(dw), full(db), full(g), full(b), full(pw),
        ],
        out_specs=[pl.BlockSpec((tq, CONV_WIDTH), lambda bi, i: (row(bi, i), 0))] * 2,
        out_shape=[jax.ShapeDtypeStruct((m, CONV_WIDTH), F32)] * 2,
        scratch_shapes=[pltpu.VMEM((CONV_HALO + tq, CONV_WIDTH), F32)],
        compiler_params=_cparams(2),
        name="conv",
    )(h, h, h, h, hist, dw, db, g, b, pw)


def _masked_softmax(s, mask):
    s = jnp.where(mask, s, NEG_INF)
    m = jnp.max(s, axis=-1, keepdims=True)
    m = jnp.where(m > NEG_INF, m, 0.0)
    e = jnp.exp(s - m)
    return e / jnp.maximum(jnp.sum(e, axis=-1, keepdims=True), 1e-30)


def _select_blocks(imp, tpos, n_blocks):
    tq, lanes = imp.shape
    blk = lax.broadcasted_iota(jnp.int32, (tq, lanes), 1)
    cur = tpos // SEL_BLOCK
    valid = blk * SEL_BLOCK <= tpos
    forced = jnp.logical_or(blk == 0, jnp.logical_or(blk == cur, blk == cur - 1))
    score = jnp.where(valid, jnp.where(forced, FORCED_SCORE, imp), NEG_INF)
    beaten = jnp.zeros((tq, lanes), F32)
    for i in range(n_blocks):
        ci = score[:, i:i + 1]
        earlier = (blk > i).astype(F32)
        beaten = beaten + jnp.where(ci > score, 1.0, jnp.where(ci == score, earlier, 0.0))
    picked = jnp.logical_and(beaten < float(min(N_SELECT, n_blocks)), blk < n_blocks)
    return picked.astype(F32)


SLC_CHUNK = 512


def _online_softmax_step(m_ref, l_ref, acc_ref, r, s, mask, v):
    s = jnp.where(mask, s, NEG_INF)
    m_old = m_ref[r]
    m_new = jnp.maximum(m_old, jnp.max(s, axis=-1, keepdims=True))
    m_safe = jnp.where(m_new > NEG_INF, m_new, 0.0)
    alpha = jnp.exp(m_old - m_safe)
    e = jnp.exp(s - m_safe)
    l_ref[r] = alpha * l_ref[r] + jnp.sum(e, axis=-1, keepdims=True)
    acc_ref[r] = alpha * acc_ref[r] + jnp.dot(e.astype(BF16), v, preferred_element_type=F32)
    m_ref[r] = m_new


def _attend_scaled(qh, k, v, mask):
    s = jnp.where(mask, _dot_nt(qh, k), NEG_INF)
    m = jnp.max(s, axis=-1, keepdims=True)
    m = jnp.where(m > NEG_INF, m, 0.0)
    e = jnp.exp(s - m)
    l = jnp.sum(e, axis=-1, keepdims=True)
    return jnp.dot(e.astype(BF16), v, preferred_element_type=F32) / jnp.maximum(l, 1e-30)


def _nsa_prompt_kernel(q_ref, kc_ref, ks_ref, kw_ref, vc_ref, vs_ref, vw_ref, gate_ref, cw_ref, o_ref,
                       kvb_ref, cmpb_ref, m_ref, l_ref, acc_ref, *, tq, seq):
    qi = pl.program_id(1)
    t0 = qi * tq
    scale = HEAD_DIM ** -0.5
    n_slc = seq // SEL_BLOCK
    tpos = t0 + lax.broadcasted_iota(jnp.int32, (tq, 1), 0)
    lane_c = lax.broadcasted_iota(jnp.int32, (tq, 2 * n_slc), 1)
    cmp_last = jnp.where(lane_c < n_slc, lane_c * SEL_BLOCK + CMP_BLOCK - 1, (lane_c - n_slc) * SEL_BLOCK + SEL_BLOCK - 1)
    cmp_mask = cmp_last <= tpos
    win_len = tq + WINDOW
    w0 = pl.multiple_of(jnp.maximum(t0 - WINDOW, 0), tq)
    wpos = w0 + lax.broadcasted_iota(jnp.int32, (tq, win_len), 1)
    win_mask = jnp.logical_and(wpos <= tpos, wpos >= tpos - WINDOW)
    gates = jax.nn.sigmoid(gate_ref[:, 0:HEAD_DIM])

    @pl.when(qi == 0)
    def _():
        for n, ref in enumerate((ks_ref, vs_ref, kw_ref, vw_ref)):
            kvb_ref[n] = ref[0].astype(BF16)
        for g in range(NSA_KV_HEADS):
            gs = slice(g * HEAD_DIM, (g + 1) * HEAD_DIM)
            for which, ref in enumerate((kc_ref, vc_ref)):
                x3 = ref[0, :, gs].reshape(n_slc, SEL_BLOCK, HEAD_DIM)
                ev = jnp.sum(x3 * cw_ref[g, which, 0][None], axis=1)
                od = jnp.sum(x3 * cw_ref[g, which, 1][None], axis=1)
                cmpb_ref[which, g] = jnp.concatenate([ev, od], axis=0).astype(BF16)

    for g in range(NSA_KV_HEADS):
        gs = slice(g * HEAD_DIM, (g + 1) * HEAD_DIM)
        kcmp, vcmp = cmpb_ref[0, g], cmpb_ref[1, g]
        qs, o_cmp = [], []
        imp = jnp.zeros((tq, 2 * n_slc), F32)
        for r in range(GQA_R):
            hh = g * GQA_R + r
            qh = (q_ref[:, hh * HEAD_DIM:(hh + 1) * HEAD_DIM] * scale).astype(BF16)
            qs.append(qh)
            p = _masked_softmax(_dot_nt(qh, kcmp), cmp_mask)
            imp = imp + p
            o_cmp.append(jnp.dot(p.astype(BF16), vcmp, preferred_element_type=F32))
        sel = _select_blocks(imp[:, :n_slc] + imp[:, n_slc:], tpos, n_slc).astype(BF16)

        m_ref[...] = jnp.full(m_ref.shape, NEG_INF, F32)
        l_ref[...] = jnp.zeros_like(l_ref)
        acc_ref[...] = jnp.zeros_like(acc_ref)
        for j in range(seq // SLC_CHUNK):
            @pl.when(j * SLC_CHUNK < t0 + tq)
            def _():
                ks = slice(j * SLC_CHUNK, (j + 1) * SLC_CHUNK)
                key_blk = (j * SLC_CHUNK + lax.broadcasted_iota(jnp.int32, (n_slc, SLC_CHUNK), 1)) // SEL_BLOCK
                expand = (lax.broadcasted_iota(jnp.int32, (n_slc, SLC_CHUNK), 0) == key_blk).astype(BF16)
                sel_keys = jnp.dot(sel, expand, preferred_element_type=F32)
                kpos = j * SLC_CHUNK + lax.broadcasted_iota(jnp.int32, (tq, SLC_CHUNK), 1)
                mask = jnp.logical_and(sel_keys > 0.5, kpos <= tpos)
                k, v = kvb_ref[0, ks, gs], kvb_ref[1, ks, gs]
                for r in range(GQA_R):
                    _online_softmax_step(m_ref, l_ref, acc_ref, r, _dot_nt(qs[r], k), mask, v)

        kwin = kvb_ref[2, pl.ds(w0, win_len), gs]
        vwin = kvb_ref[3, pl.ds(w0, win_len), gs]
        for r in range(GQA_R):
            hh = g * GQA_R + r
            o_slc = acc_ref[r] / jnp.maximum(l_ref[r], 1e-30)
            o_win = _attend_scaled(qs[r], kwin, vwin, win_mask)
            o_ref[:, hh * HEAD_DIM:(hh + 1) * HEAD_DIM] = (
                gates[:, 3 * hh:3 * hh + 1] * o_cmp[r] + gates[:, 3 * hh + 1:3 * hh + 2] * o_slc
                + gates[:, 3 * hh + 2:3 * hh + 3] * o_win)


def _nsa_prompt(h, cmp_w, *, batch, seq, tq):
    m = h.shape[0]
    nt = seq // tq
    kv = lambda col: pl.BlockSpec((1, seq, KV_WIDTH), lambda bi, i, col=col: (bi, 0, col // KV_WIDTH))
    h3 = h.reshape(batch, seq, PROJ_WIDTH)
    kern = functools.partial(_nsa_prompt_kernel, tq=tq, seq=seq)
    return pl.pallas_call(
        kern,
        grid=(batch, nt),
        in_specs=[
            pl.BlockSpec((tq, NSA_WIDTH), lambda bi, i: (bi * nt + i, C_Q // NSA_WIDTH)),
            kv(C_KC), kv(C_KS), kv(C_KW), kv(C_VC), kv(C_VS), kv(C_VW),
            pl.BlockSpec((tq, KV_WIDTH), lambda bi, i: (bi * nt + i, C_GATE // KV_WIDTH)),
            pl.BlockSpec(cmp_w.shape, lambda bi, i: (0,) * cmp_w.ndim),
        ],
        out_specs=pl.BlockSpec((tq, NSA_WIDTH), lambda bi, i: (bi * nt + i, 0)),
        out_shape=jax.ShapeDtypeStruct((m, NSA_WIDTH), F32),
        scratch_shapes=[
            pltpu.VMEM((4, seq, KV_WIDTH), BF16),
            pltpu.VMEM((2, NSA_KV_HEADS, 2 * (seq // SEL_BLOCK), HEAD_DIM), BF16),
            pltpu.VMEM((GQA_R, tq, 1), F32), pltpu.VMEM((GQA_R, tq, 1), F32), pltpu.VMEM((GQA_R, tq, HEAD_DIM), F32),
        ],
        compiler_params=_cparams(2),
        name="nsa_prompt",
    )(h, h3, h3, h3, h3, h3, h3, h, cmp_w)


def _out_ln_kernel(gm_ref, nsa_ref, y_ref, x_ref, w_ref, g_ref, b_ref, o_ref):
    cat = jnp.concatenate([gm_ref[...].astype(BF16), nsa_ref[...].astype(BF16), y_ref[...].astype(BF16)], axis=-1)
    mix = jnp.dot(cat, w_ref[...], preferred_element_type=F32)
    o_ref[...] = _ln_lanes(ALPHA * x_ref[...] + mix, g_ref[...], b_ref[...])


def _out_ln(gm, nsa, y, x, w, g, b, *, tm):
    m = x.shape[0]
    row = lambda width: pl.BlockSpec((tm, width), lambda i: (i, 0))
    full = lambda a: pl.BlockSpec(a.shape, lambda i: (0,) * a.ndim)
    return pl.pallas_call(
        _out_ln_kernel,
        grid=(m // tm,),
        in_specs=[row(GM_WIDTH), row(NSA_WIDTH), row(CONV_WIDTH), row(D_MODEL), full(w), full(g), full(b)],
        out_specs=row(D_MODEL),
        out_shape=jax.ShapeDtypeStruct((m, D_MODEL), F32),
        compiler_params=_cparams(1),
        name="out_ln",
    )(gm, nsa, y, x, w, g, b)


def _extract_topk(s, vals_ref, rank_ref, exact_ties):
    n_rows, tm = s.shape
    rows = lax.broadcasted_iota(jnp.int32, (n_rows, tm), 0)
    krow = lax.broadcasted_iota(jnp.int32, (PEER_TOPK, tm), 0)
    work = s
    rank = jnp.full((n_rows, tm), float(PEER_TOPK), F32)
    vals = jnp.zeros((PEER_TOPK, tm), F32)
    for r in range(PEER_TOPK):
        mx = jnp.max(work, axis=0, keepdims=True)
        hit = work == mx
        if exact_ties:
            hit = rows == jnp.min(jnp.where(hit, rows, n_rows), axis=0, keepdims=True)
        rank = jnp.where(hit, float(r), rank)
        vals = jnp.where(krow == r, mx, vals)
        work = jnp.where(hit, NEG_INF, work)
    vals_ref[...] = vals
    rank_ref[...] = rank


def _topk_rows(s, vals_ref, rank_ref):
    _extract_topk(s, vals_ref, rank_ref, exact_ties=False)
    taken = jnp.sum((rank_ref[...] < float(PEER_TOPK)).astype(F32), axis=0, keepdims=True)
    tied = jnp.max(taken) > float(PEER_TOPK)

    @pl.when(tied)
    def _():
        _extract_topk(s, vals_ref, rank_ref, exact_ties=True)

    return vals_ref[...], rank_ref[...]


def _route_kernel(qp_ref, sk_ref, r2_ref, cnt_ref, f_ref, e2_ref, v1_ref, v2_ref, rank1_ref, rank2_ref):
    tm = qp_ref.shape[0]
    q1 = qp_ref[:, 0:SUBKEY_DIM].astype(BF16)
    q2 = qp_ref[:, SUBKEY_DIM:2 * SUBKEY_DIM].astype(BF16)
    s1 = _dot_nt(sk_ref[0, 0], q1)
    s2 = _dot_nt(sk_ref[0, 1], q2)
    v1, rank1 = _topk_rows(s1, v1_ref, rank1_ref)
    v2, rank2 = _topk_rows(s2, v2_ref, rank2_ref)
    krow = lax.broadcasted_iota(jnp.int32, (PEER_TOPK, tm), 0)
    cnt = jnp.zeros((PEER_TOPK, tm), F32)
    front = v1 + v2[0:1]
    m0 = v1[0:1] + v2[0:1]
    z = jnp.zeros((1, tm), F32)
    for _ in range(PEER_TOPK):
        mx = jnp.max(front, axis=0, keepdims=True)
        idx = jnp.min(jnp.where(front == mx, krow, PEER_TOPK), axis=0, keepdims=True)
        hit = krow == idx
        cnt = cnt + hit.astype(F32)
        z = z + jnp.exp(mx - m0)
        chosen = jnp.sum(jnp.where(hit, cnt, 0.0), axis=0, keepdims=True)
        nxt = jnp.full((1, tm), NEG_INF, F32)
        for c in range(1, PEER_TOPK):
            nxt = jnp.where(chosen == float(c), v2[c:c + 1], nxt)
        front = jnp.where(hit, v1 + nxt, front)
    cnt_a = jnp.zeros((N_KEYS, tm), F32)
    for r in range(PEER_TOPK):
        cnt_a = jnp.where(rank1 == float(r), cnt[r:r + 1], cnt_a)
    r2_ref[0] = rank2.astype(BF16)
    cnt_ref[0] = cnt_a
    f_ref[0] = jnp.exp(s1 - v1[0:1]) / z
    e2_ref[0] = jnp.exp(s2 - v2[0:1]).astype(BF16)


def _route(qp, subkeys, *, tm):
    m = qp.shape[0]
    out = lambda dt: jax.ShapeDtypeStruct((PEER_HEADS, N_KEYS, m), dt)
    ospec = pl.BlockSpec((1, N_KEYS, tm), lambda i, hd: (hd, 0, i))
    return pl.pallas_call(
        _route_kernel,
        grid=(m // tm, PEER_HEADS),
        in_specs=[
            pl.BlockSpec((tm, 2 * SUBKEY_DIM), lambda i, hd: (i, hd)),
            pl.BlockSpec((1, 2, N_KEYS, SUBKEY_DIM), lambda i, hd: (hd, 0, 0, 0)),
        ],
        out_specs=[ospec] * 4,
        out_shape=[out(BF16), out(F32), out(F32), out(BF16)],
        scratch_shapes=[pltpu.VMEM((PEER_TOPK, tm), F32), pltpu.VMEM((PEER_TOPK, tm), F32),
                        pltpu.VMEM((N_KEYS, tm), F32), pltpu.VMEM((N_KEYS, tm), F32)],
        compiler_params=_cparams(2),
        name="peer_route",
    )(qp, subkeys)


PEER_SUB = 256
PEER_TE = 1024


def _peer_dense_kernel(x_ref, u_ref, v_ref, r2_ref, cnt_ref, f_ref, e2_ref, g_ref, b_ref, o_ref, xb_ref, *, te):
    j = pl.program_id(1)

    @pl.when(j == 0)
    def _():
        xb_ref[...] = x_ref[...].astype(BF16)
        o_ref[...] = jnp.zeros_like(o_ref)

    cols = []
    for c in range(te // PEER_SUB):
        act = jax.nn.gelu(_dot_nt(u_ref[c * PEER_SUB:(c + 1) * PEER_SUB, :], xb_ref[...]).astype(BF16))
        rows = []
        for al in range(PEER_SUB // N_KEYS):
            a = c * (PEER_SUB // N_KEYS) + al
            gate = None
            for hd in range(PEER_HEADS):
                cnt = cnt_ref[hd, a:a + 1, :].astype(BF16)
                f = f_ref[hd, a:a + 1, :].astype(BF16)
                term = jnp.where(r2_ref[hd] < cnt, e2_ref[hd], jnp.zeros((), BF16)) * f
                gate = term if gate is None else gate + term
            rows.append(act[al * N_KEYS:(al + 1) * N_KEYS] * gate)
        cols.append(jnp.concatenate(rows, axis=0).T)
    o_ref[...] += jnp.dot(jnp.concatenate(cols, axis=1), v_ref[...], preferred_element_type=F32)

    @pl.when(j == pl.num_programs(1) - 1)
    def _():
        o_ref[...] = _ln_lanes(ALPHA * x_ref[...] + o_ref[...], g_ref[...], b_ref[...])


def _peer_dense(x, u, v, r2, cnt, f, e2, g, b, *, tm, te):
    m = x.shape[0]
    n_exp = u.shape[0]
    by_b = pl.BlockSpec((PEER_HEADS, N_KEYS, tm), lambda i, j: (0, 0, i))
    by_a = pl.BlockSpec((PEER_HEADS, te // N_KEYS, tm), lambda i, j: (0, j, i))
    full = lambda a: pl.BlockSpec(a.shape, lambda i, j: (0,) * a.ndim)
    kern = functools.partial(_peer_dense_kernel, te=te)
    return pl.pallas_call(
        kern,
        grid=(m // tm, n_exp // te),
        in_specs=[
            pl.BlockSpec((tm, D_MODEL), lambda i, j: (i, 0), pipeline_mode=pl.Buffered(1)),
            pl.BlockSpec((te, D_MODEL), lambda i, j: (j, 0)),
            pl.BlockSpec((te, D_MODEL), lambda i, j: (j, 0)),
            by_b, by_a, by_a, by_b, full(g), full(b),
        ],
        out_specs=pl.BlockSpec((tm, D_MODEL), lambda i, j: (i, 0), pipeline_mode=pl.Buffered(1)),
        out_shape=jax.ShapeDtypeStruct((m, D_MODEL), F32),
        scratch_shapes=[pltpu.VMEM((tm, D_MODEL), BF16)],
        compiler_params=_cparams(2),
        name="peer_dense",
    )(x, u, v, r2, cnt, f, e2, g, b)


PAGES_PER_STEP = 8


def _group_queries(q_ref, g):
    heads = [q_ref[:, (g * GQA_R + r) * HEAD_DIM:(g * GQA_R + r + 1) * HEAD_DIM] for r in range(GQA_R)]
    return jnp.concatenate(heads, axis=0).astype(BF16)


def _page_rows(pages, g):
    n = pages[0].shape[0] // NSA_KV_HEADS
    return jnp.concatenate([r[pl.ds(g, n, stride=NSA_KV_HEADS), :] for r in pages], axis=0)


def _nsa_s_cmp_kernel(pt_ref, q_ref, cw_ref, *refs, seq, past_len, n_real, sel_lanes):
    nps = PAGES_PER_STEP
    k_pages, v_pages = refs[:nps], refs[nps:2 * nps]
    ocmp_ref, sel_ref, ke_ref, ko_ref, ve_ref, vo_ref = refs[2 * nps:]
    c = pl.program_id(1)
    n_pb = ke_ref.shape[0]
    rows = nps * (k_pages[0].shape[0] // NSA_KV_HEADS) // SEL_BLOCK
    off = pl.multiple_of(c * rows, rows)

    def compress(pages, which, even_ref, odd_ref):
        for g in range(NSA_KV_HEADS):
            gs = slice(g * HEAD_DIM, (g + 1) * HEAD_DIM)
            x3 = _page_rows(pages, g).reshape(rows, SEL_BLOCK, HEAD_DIM)
            even_ref[pl.ds(off, rows), gs] = jnp.sum(x3 * cw_ref[g, which, 0][None], axis=1)
            odd_ref[pl.ds(off, rows), gs] = jnp.sum(x3 * cw_ref[g, which, 1][None], axis=1)

    compress(k_pages, 0, ke_ref, ko_ref)
    compress(v_pages, 1, ve_ref, vo_ref)

    @pl.when(c == pl.num_programs(1) - 1)
    def _():
        scale = HEAD_DIM ** -0.5
        nq = GQA_R * seq
        qrow = lax.broadcasted_iota(jnp.int32, (nq, 1), 0)
        qpos = past_len + qrow % seq
        lane = lax.broadcasted_iota(jnp.int32, (nq, 2 * n_pb), 1)
        cmp_last = jnp.where(lane < n_pb, lane * SEL_BLOCK + CMP_BLOCK - 1, (lane - n_pb) * SEL_BLOCK + SEL_BLOCK - 1)
        cmp_mask = cmp_last <= qpos
        tpos = past_len + lax.broadcasted_iota(jnp.int32, (seq, 1), 0)
        for g in range(NSA_KV_HEADS):
            gs = slice(g * HEAD_DIM, (g + 1) * HEAD_DIM)
            kcat = jnp.concatenate([ke_ref[:, gs], ko_ref[:, gs]], axis=0).astype(BF16)
            vcat = jnp.concatenate([ve_ref[:, gs], vo_ref[:, gs]], axis=0).astype(BF16)
            p = _masked_softmax(_dot_nt(_group_queries(q_ref, g), kcat) * scale, cmp_mask)
            ocmp_ref[0, g] = jnp.dot(p.astype(BF16), vcat, preferred_element_type=F32)
            imp = p[0:seq]
            for r in range(1, GQA_R):
                imp = imp + p[r * seq:(r + 1) * seq]
            imp = imp[:, :n_pb] + imp[:, n_pb:]
            imp = jnp.concatenate([imp, jnp.zeros((seq, sel_lanes - n_pb), F32)], axis=1)
            sel_ref[0, g] = _select_blocks(imp, tpos, n_real)


def _nsa_s_slc_kernel(pt_ref, q_ref, sel_ref, ocmp_ref, ksn_ref, vsn_ref, wk_ref, wv_ref, gate_ref, *refs,
                      seq, past_len, sel_lanes, win_buf):
    nps = PAGES_PER_STEP
    k_pages, v_pages = refs[:nps], refs[nps:2 * nps]
    o_ref, m_ref, l_ref, acc_ref = refs[2 * nps:]
    c = pl.program_id(1)
    scale = HEAD_DIM ** -0.5
    nq = GQA_R * seq
    chunk = nps * (k_pages[0].shape[0] // NSA_KV_HEADS)
    qrow = lax.broadcasted_iota(jnp.int32, (nq, 1), 0)
    qpos = past_len + qrow % seq

    @pl.when(c == 0)
    def _():
        m_ref[...] = jnp.full(m_ref.shape, NEG_INF, F32)
        l_ref[...] = jnp.zeros_like(l_ref)
        acc_ref[...] = jnp.zeros_like(acc_ref)

    def online_update(g, s, mask, v):
        _online_softmax_step(m_ref, l_ref, acc_ref, g, s, mask, v)

    def key_mask(g, first_block, first_pos, n_keys):
        key_blk = first_block + lax.broadcasted_iota(jnp.int32, (sel_lanes, n_keys), 1) // SEL_BLOCK
        expand = (lax.broadcasted_iota(jnp.int32, (sel_lanes, n_keys), 0) == key_blk).astype(BF16)
        sel_keys = jnp.dot(sel_ref[0, g].astype(BF16), expand, preferred_element_type=F32)
        sel_keys = jnp.concatenate([sel_keys] * GQA_R, axis=0)
        kpos = first_pos + lax.broadcasted_iota(jnp.int32, (nq, n_keys), 1)
        return jnp.logical_and(sel_keys > 0.5, kpos <= qpos)

    for g in range(NSA_KV_HEADS):
        mask = key_mask(g, c * (chunk // SEL_BLOCK), c * chunk, chunk)
        scores = _dot_nt(_group_queries(q_ref, g), _page_rows(k_pages, g).astype(BF16)) * scale
        online_update(g, scores, mask, _page_rows(v_pages, g).astype(BF16))

    @pl.when(c == pl.num_programs(1) - 1)
    def _():
        gates = jax.nn.sigmoid(gate_ref[:, 0:HEAD_DIM])
        n_new = ksn_ref.shape[1]
        n_win = wk_ref.shape[1]
        wpos = past_len - win_buf + lax.broadcasted_iota(jnp.int32, (nq, n_win), 1)
        win_mask = jnp.logical_and(wpos <= qpos, wpos >= qpos - WINDOW)
        for g in range(NSA_KV_HEADS):
            gs = slice(g * HEAD_DIM, (g + 1) * HEAD_DIM)
            qg = _group_queries(q_ref, g)
            online_update(g, _dot_nt(qg, ksn_ref[0, :, gs].astype(BF16)) * scale,
                          key_mask(g, past_len // SEL_BLOCK, past_len, n_new), vsn_ref[0, :, gs].astype(BF16))
            o_slc = acc_ref[g] / jnp.maximum(l_ref[g], 1e-30)
            p = _masked_softmax(_dot_nt(qg, wk_ref[0, :, gs].astype(BF16)) * scale, win_mask)
            o_win = jnp.dot(p.astype(BF16), wv_ref[0, :, gs].astype(BF16), preferred_element_type=F32)
            o_cmp = ocmp_ref[0, g]
            for r in range(GQA_R):
                hh = g * GQA_R + r
                rs = slice(r * seq, (r + 1) * seq)
                o_ref[:, hh * HEAD_DIM:(hh + 1) * HEAD_DIM] = (
                    gates[:, 3 * hh:3 * hh + 1] * o_cmp[rs] + gates[:, 3 * hh + 1:3 * hh + 2] * o_slc[rs]
                    + gates[:, 3 * hh + 2:3 * hh + 3] * o_win[rs])


def _nsa_sample(h, caches, layer, page_table, ks_new, vs_new, wk_all, wv_all, cw, *, batch, seq, past_len, win_buf):
    cmp_k, cmp_v, slc_k, slc_v = caches
    nps = PAGES_PER_STEP
    n_pages, page = page_table.shape[1], ks_new.shape[1]
    page_rows = page * NSA_KV_HEADS
    n_pool = cmp_k.shape[0] // (DEPTH * page_rows)
    n_chunks = n_pages // nps
    n_pb = past_len // SEL_BLOCK
    n_real = n_pb + 1
    sel_lanes = -(-n_real // 128) * 128
    nq = GQA_R * seq
    page_spec = lambda k: pl.BlockSpec((page_rows, HEAD_DIM),
                                       lambda b, c, pt, k=k: (pt[b, c * nps + k] + layer * n_pool, 0))
    pages = [page_spec(k) for k in range(nps)]
    q_spec = pl.BlockSpec((seq, NSA_WIDTH), lambda b, c, pt: (b, C_Q // NSA_WIDTH))
    per_b = lambda shape: pl.BlockSpec((1,) + shape, lambda b, c, pt: (b,) + (0,) * len(shape))
    ocmp, sel = pl.pallas_call(
        functools.partial(_nsa_s_cmp_kernel, seq=seq, past_len=past_len, n_real=n_real, sel_lanes=sel_lanes),
        grid_spec=pltpu.PrefetchScalarGridSpec(
            num_scalar_prefetch=1, grid=(batch, n_chunks),
            in_specs=[q_spec, pl.BlockSpec(cw.shape, lambda b, c, pt: (0,) * cw.ndim)] + pages + pages,
            out_specs=[per_b((NSA_KV_HEADS, nq, HEAD_DIM)), per_b((NSA_KV_HEADS, seq, sel_lanes))],
            scratch_shapes=[pltpu.VMEM((n_pb, KV_WIDTH), F32)] * 4),
        out_shape=[jax.ShapeDtypeStruct((batch, NSA_KV_HEADS, nq, HEAD_DIM), F32),
                   jax.ShapeDtypeStruct((batch, NSA_KV_HEADS, seq, sel_lanes), F32)],
        compiler_params=_cparams(2),
        name="nsa_sample_cmp",
    )(page_table, h, cw, *([cmp_k] * nps), *([cmp_v] * nps))
    return pl.pallas_call(
        functools.partial(_nsa_s_slc_kernel, seq=seq, past_len=past_len, sel_lanes=sel_lanes, win_buf=win_buf),
        grid_spec=pltpu.PrefetchScalarGridSpec(
            num_scalar_prefetch=1, grid=(batch, n_chunks),
            in_specs=[q_spec, per_b((NSA_KV_HEADS, seq, sel_lanes)), per_b((NSA_KV_HEADS, nq, HEAD_DIM)),
                      per_b(ks_new.shape[1:]), per_b(vs_new.shape[1:]), per_b(wk_all.shape[1:]), per_b(wv_all.shape[1:]),
                      pl.BlockSpec((seq, KV_WIDTH), lambda b, c, pt: (b, C_GATE // KV_WIDTH))] + pages + pages,
            out_specs=pl.BlockSpec((seq, NSA_WIDTH), lambda b, c, pt: (b, 0)),
            scratch_shapes=[pltpu.VMEM((NSA_KV_HEADS, nq, 1), F32), pltpu.VMEM((NSA_KV_HEADS, nq, 1), F32),
                            pltpu.VMEM((NSA_KV_HEADS, nq, HEAD_DIM), F32)]),
        out_shape=jax.ShapeDtypeStruct((batch * seq, NSA_WIDTH), F32),
        compiler_params=_cparams(2),
        name="nsa_sample_slc",
    )(page_table, h, sel, ocmp, ks_new, vs_new, wk_all, wv_all, h, *([slc_k] * nps), *([slc_v] * nps))


def _rope_tables(pos):
    half = HEAD_DIM // 2
    inv = ROPE_THETA ** (-jnp.arange(half, dtype=F32) / half)
    ang = pos.astype(F32)[:, None] * inv[None, :]
    cos, sin = jnp.cos(ang), jnp.sin(ang)
    return jnp.concatenate([cos, cos], -1), jnp.concatenate([-sin, sin], -1)


def _prep_weights(l, w_in, gm_ln_g, gm_ln_b, gm_ws, gm_bs, cmp_wk, cmp_wv, conv_dw, conv_db, conv_ln_g,
                  conv_ln_b, conv_pw, w_out, ln1_g, ln1_b, peer_wq, peer_subkeys, peer_u, peer_v, ln2_g, ln2_b):
    w = w_in[l]
    o = np.cumsum([0, GM_WIDTH, GM_WIDTH, NSA_WIDTH] + [KV_WIDTH] * 6 + [3 * NSA_HEADS, CONV_WIDTH, CONV_WIDTH])
    gu, gv, q, kc, vc, ks, vs, kw, vw, gate, ca, cb = [w[:, o[i]:o[i + 1]] for i in range(12)]
    pad = jnp.zeros((w.shape[0], PROJ_WIDTH - C_GATE - 3 * NSA_HEADS), w.dtype)
    w_proj = jnp.concatenate([gu, gv, ca, cb, q, kc, ks, kw, vc, vs, vw, gate, pad], axis=1).astype(BF16)
    row = lambda a: a.reshape(1, -1)
    cw = jnp.stack([cmp_wk[l], cmp_wv[l]], axis=1)
    zero = jnp.zeros_like(cw)
    cw = jnp.stack([jnp.concatenate([cw, zero], -1), jnp.concatenate([zero, cw], -1)], axis=2)
    cw = jnp.broadcast_to(cw[..., None], cw.shape + (HEAD_DIM,))
    dw = jnp.concatenate([conv_dw[l], jnp.zeros((CONV_HALO - CONV_K, CONV_WIDTH), F32)], axis=0)
    return dict(
        w_proj=w_proj, gm_g=row(gm_ln_g[l]), gm_b=row(gm_ln_b[l]), gm_ws=gm_ws[l], gm_bs=gm_bs[l], cmp_w=cw,
        conv_dw=dw, conv_db=row(conv_db[l]), conv_g=row(conv_ln_g[l]), conv_b=row(conv_ln_b[l]),
        conv_pw=conv_pw[l].astype(BF16), w_out=w_out[l].astype(BF16), ln1_g=row(ln1_g[l]), ln1_b=row(ln1_b[l]),
        wq=peer_wq[l].astype(BF16), subkeys=peer_subkeys[l].astype(BF16), u=peer_u[l].astype(BF16),
        v=peer_v[l].astype(BF16), ln2_g=row(ln2_g[l]), ln2_b=row(ln2_b[l]))


def _gm_weights(p, chunk):
    w = jnp.tril(p["gm_ws"][:, :chunk, :chunk])
    bs = jnp.repeat(p["gm_bs"][:, :chunk].T, GM_WIDTH // GM_GROUPS, axis=1)
    return w, bs


def _peer(x, p, *, tm_proj, tm_route, tm_dense, te):
    m = x.shape[0]
    dummy = jnp.zeros((tm_proj, HEAD_DIM), F32)
    qp = _proj(x, p["wq"], dummy, dummy, tm=tm_proj, rope_lo=0, rope_hi=0, name="peer_q")
    r2, cnt, f, e2 = _route(qp, p["subkeys"], tm=tm_route)
    return _peer_dense(x, p["u"], p["v"], r2, cnt, f, e2, p["ln2_g"], p["ln2_b"], tm=tm_dense, te=te)


def _kv_rows(h, col, batch, seq):
    return h[:, col:col + KV_WIDTH].reshape(batch, seq, NSA_KV_HEADS, HEAD_DIM)


def _layer_prompt(x, p, cos, sin, *, batch, seq):
    m = batch * seq
    h = _proj(x, p["w_proj"], cos, sin, tm=1024, rope_lo=ROPE_LO, rope_hi=ROPE_HI, name="proj_in")
    gw, gbs = _gm_weights(p, GM_CHUNK)
    gm, vn = _gmlp(h, p["gm_g"], p["gm_b"], gw.astype(BF16), gbs, chunk=GM_CHUNK)
    nsa = _nsa_prompt(h, p["cmp_w"], batch=batch, seq=seq, tq=256)
    hist = jnp.zeros((batch, CONV_HALO, CONV_WIDTH), F32)
    y, c = _conv(h, hist, p["conv_dw"], p["conv_db"], p["conv_g"], p["conv_b"], p["conv_pw"],
                 batch=batch, seq=seq, tq=256)
    x1 = _out_ln(gm, nsa, y, x, p["w_out"], p["ln1_g"], p["ln1_b"], tm=256)
    x2 = _peer(x1, p, tm_proj=1024, tm_route=512, tm_dense=512, te=PEER_TE)
    keep = min(WINDOW, seq)
    kw, vw = _kv_rows(h, C_KW, batch, seq), _kv_rows(h, C_VW, batch, seq)
    start = ((seq - 1) // GM_CHUNK) * GM_CHUNK
    states = (_kv_rows(h, C_KC, batch, seq), _kv_rows(h, C_VC, batch, seq), _kv_rows(h, C_KS, batch, seq),
              _kv_rows(h, C_VS, batch, seq), kw[:, -keep:], vw[:, -keep:],
              c.reshape(batch, seq, CONV_WIDTH)[:, -(CONV_K - 1):], vn.reshape(batch, seq, GM_WIDTH)[:, start:])
    return x2, states


def _layer_sample(x, p, cos, sin, caches, layer, page_table, win_k, win_v, hist30, *, batch, seq, past_len):
    m = batch * seq
    h = _proj(x, p["w_proj"], cos, sin, tm=m, rope_lo=ROPE_LO, rope_hi=ROPE_HI, name="proj_in_s")
    pad_rows = lambda a: jnp.pad(a.reshape(batch, seq, -1), ((0, 0), (0, GM_CHUNK - seq), (0, 0))).reshape(batch * GM_CHUNK, -1)
    gw, gbs = _gm_weights(p, GM_CHUNK)
    hp = pad_rows(h[:, :2 * GM_WIDTH])
    gm, vn = _gmlp(hp, p["gm_g"], p["gm_b"], gw.astype(BF16), gbs, chunk=GM_CHUNK)
    unpad = lambda a: a.reshape(batch, GM_CHUNK, -1)[:, :seq].reshape(m, -1)
    gm, vn = unpad(gm), unpad(vn)
    kvs = [_kv_rows(h, col, batch, seq) for col in (C_KC, C_VC, C_KS, C_VS, C_KW, C_VW)]
    win_buf = win_k.shape[1]
    page = past_len // page_table.shape[1]
    new_rows = lambda col, rows: jnp.pad(h[:, col:col + KV_WIDTH].reshape(batch, seq, KV_WIDTH),
                                         ((0, 0), (0, rows - seq), (0, 0)))
    flat = lambda a: a.reshape(batch, win_buf, KV_WIDTH)
    wk_all = jnp.concatenate([flat(win_k), new_rows(C_KW, HEAD_DIM)], axis=1)
    wv_all = jnp.concatenate([flat(win_v), new_rows(C_VW, HEAD_DIM)], axis=1)
    nsa = _nsa_sample(h, caches, layer, page_table, new_rows(C_KS, page), new_rows(C_VS, page), wk_all, wv_all,
                      p["cmp_w"], batch=batch, seq=seq, past_len=past_len, win_buf=win_buf)
    keep = min(WINDOW, past_len + seq)
    unflat = lambda a: a[:, :win_buf + seq][:, -keep:].reshape(batch, keep, NSA_KV_HEADS, HEAD_DIM)
    win_k, win_v = unflat(wk_all), unflat(wv_all)
    hist = jnp.pad(hist30, ((0, 0), (CONV_HALO - (CONV_K - 1), 0), (0, 0)))
    y, c = _conv(h, hist, p["conv_dw"], p["conv_db"], p["conv_g"], p["conv_b"], p["conv_pw"],
                 batch=batch, seq=seq, tq=seq)
    x1 = _out_ln(gm, nsa, y, x, p["w_out"], p["ln1_g"], p["ln1_b"], tm=m)
    x2 = _peer(x1, p, tm_proj=m, tm_route=m, tm_dense=m, te=PEER_TE)
    conv_state = jnp.concatenate([hist30, c.reshape(batch, seq, CONV_WIDTH)], axis=1)[:, -(CONV_K - 1):]
    start = ((seq - 1) // GM_CHUNK) * GM_CHUNK
    states = (kvs[0], kvs[1], kvs[2], kvs[3], win_k, win_v, conv_state, vn.reshape(batch, seq, GM_WIDTH)[:, start:])
    return x2, states


def kernel(x_prompt, x_sample, cache_cmp_k, cache_cmp_v, cache_slc_k, cache_slc_v, cache_win_k, cache_win_v,
           state_conv, page_table, w_in, gm_ln_g, gm_ln_b, gm_ws, gm_bs, cmp_wk, cmp_wv, conv_dw, conv_db,
           conv_ln_g, conv_ln_b, conv_pw, w_out, ln1_g, ln1_b, peer_wq, peer_subkeys, peer_u, peer_v, ln2_g, ln2_b):
    bp, sp, _ = x_prompt.shape
    bs, ss, _ = x_sample.shape
    depth = w_in.shape[0]
    past_len = page_table.shape[1] * cache_cmp_k.shape[2]
    pos_p = jnp.arange(sp, dtype=jnp.int32)
    pos_s = past_len + jnp.arange(ss, dtype=jnp.int32)
    cos_p, sin_p = _rope_tables(pos_p)
    cos_s, sin_s = _rope_tables(jnp.tile(pos_s, bs))
    xp = x_prompt.reshape(bp * sp, D_MODEL)
    xs = x_sample.reshape(bs * ss, D_MODEL)
    p_states, s_states = [], []
    caches = [c.reshape(-1, HEAD_DIM) for c in (cache_cmp_k, cache_cmp_v, cache_slc_k, cache_slc_v)]
    for l in range(depth):
        p = _prep_weights(l, w_in, gm_ln_g, gm_ln_b, gm_ws, gm_bs, cmp_wk, cmp_wv, conv_dw, conv_db, conv_ln_g,
                          conv_ln_b, conv_pw, w_out, ln1_g, ln1_b, peer_wq, peer_subkeys, peer_u, peer_v, ln2_g, ln2_b)
        xp, sp_state = _layer_prompt(xp, p, cos_p, sin_p, batch=bp, seq=sp)
        xs, ss_state = _layer_sample(xs, p, cos_s, sin_s, caches, l, page_table, cache_win_k[l], cache_win_v[l],
                                     state_conv[l], batch=bs, seq=ss, past_len=past_len)
        p_states.append(sp_state)
        s_states.append(ss_state)
    stack = lambda states, i: jnp.stack([st[i] for st in states], axis=0)
    outs = [xp.reshape(bp, sp, D_MODEL), xs.reshape(bs, ss, D_MODEL)]
    outs += [stack(p_states, i) for i in range(8)]
    outs += [stack(s_states, i) for i in range(8)]
    return tuple(outs)
```

```python
import functools

import numpy as np
import jax
import jax.numpy as jnp
from jax import lax
from jax.experimental import pallas as pl
from jax.experimental.pallas import tpu as pltpu

F32 = jnp.float32
BF16 = jnp.bfloat16

D_MODEL = 2048
HEAD_DIM = 128
NSA_HEADS = 8
NSA_KV_HEADS = 2
GQA_R = NSA_HEADS // NSA_KV_HEADS
GM_WIDTH = 512
GM_GROUPS = 4
GM_CHUNK = 128
CONV_WIDTH = 512
CONV_GROUPS = 4
CONV_K = 31
NSA_WIDTH = NSA_HEADS * HEAD_DIM
KV_WIDTH = NSA_KV_HEADS * HEAD_DIM
CMP_BLOCK = 32
SEL_BLOCK = 64
N_SELECT = 16
WINDOW = 512
FORCED_SCORE = 1.0e4
PEER_HEADS = 8
N_KEYS = 128
PEER_TOPK = 16
SUBKEY_DIM = 128
ROPE_THETA = 10000.0
LN_EPS = 1e-5
DEPTH = 2
ALPHA = (2 * DEPTH) ** 0.25

C_GU, C_GV, C_CA, C_CB, C_Q = 0, 512, 1024, 1536, 2048
C_KC, C_KS, C_KW, C_ROPE_END = 3072, 3328, 3584, 4096
C_VC, C_VS, C_VW, C_GATE = 4096, 4352, 4608, 4864
PROJ_WIDTH = 5120
PROJ_TN = 512
ROPE_LO, ROPE_HI = C_Q // PROJ_TN, C_ROPE_END // PROJ_TN

V7X_VMEM_LIMIT = 56 * 1024 * 1024
V7X_VMEM_LIMIT_LARGE = 60 * 1024 * 1024
NEG_INF = float("-inf")


def _cparams(n_axes, vmem=V7X_VMEM_LIMIT):
    return pltpu.CompilerParams(dimension_semantics=("arbitrary",) * n_axes, vmem_limit_bytes=vmem)


def _ln_lanes(z, g, b):
    mu = jnp.mean(z, axis=-1, keepdims=True)
    d = z - mu
    var = jnp.mean(d * d, axis=-1, keepdims=True)
    return d * lax.rsqrt(var + LN_EPS) * g + b


def _dot_nt(a, b):
    return lax.dot_general(a, b, (((1,), (1,)), ((), ())), preferred_element_type=F32)


def _proj_kernel(x_ref, w_ref, cos_ref, sin_ref, o_ref, xb_ref, *, rope_lo, rope_hi, tn):
    j = pl.program_id(1)

    @pl.when(j == 0)
    def _():
        xb_ref[...] = x_ref[...].astype(BF16)

    acc = jnp.dot(xb_ref[...], w_ref[...], preferred_element_type=F32)
    is_rope = jnp.logical_and(j >= rope_lo, j < rope_hi)

    @pl.when(is_rope)
    def _():
        cos, sin = cos_ref[...], sin_ref[...]
        for c in range(tn // HEAD_DIM):
            blk = acc[:, c * HEAD_DIM:(c + 1) * HEAD_DIM]
            o_ref[:, c * HEAD_DIM:(c + 1) * HEAD_DIM] = blk * cos + pltpu.roll(blk, HEAD_DIM // 2, 1) * sin

    @pl.when(jnp.logical_not(is_rope))
    def _():
        o_ref[...] = acc


def _proj(x, w, cos, sin, *, tm, rope_lo, rope_hi, name):
    m, k = x.shape
    n = w.shape[1]
    tn = PROJ_TN
    nper = cos.shape[0] // tm
    kern = functools.partial(_proj_kernel, rope_lo=rope_lo, rope_hi=rope_hi, tn=tn)
    return pl.pallas_call(
        kern,
        grid=(m // tm, n // tn),
        in_specs=[
            pl.BlockSpec((tm, k), lambda i, j: (i, 0)),
            pl.BlockSpec((k, tn), lambda i, j: (0, j)),
            pl.BlockSpec((tm, HEAD_DIM), lambda i, j: (i % nper, 0)),
            pl.BlockSpec((tm, HEAD_DIM), lambda i, j: (i % nper, 0)),
        ],
        out_specs=pl.BlockSpec((tm, tn), lambda i, j: (i, j)),
        out_shape=jax.ShapeDtypeStruct((m, n), F32),
        scratch_shapes=[pltpu.VMEM((tm, k), BF16)],
        compiler_params=_cparams(2),
        name=name,
    )(x, w, cos, sin)


def _gmlp_kernel(gu_ref, gv_ref, g_ref, b_ref, w_ref, bs_ref, o_ref, vn_ref):
    gw = GM_WIDTH // GM_GROUPS
    for h in range(GM_GROUPS):
        sl = slice(h * gw, (h + 1) * gw)
        v = jax.nn.gelu(gv_ref[:, sl])
        vn = _ln_lanes(v, g_ref[:, sl], b_ref[:, sl])
        vn_ref[:, sl] = vn
        s = jnp.dot(w_ref[h], vn.astype(BF16), preferred_element_type=F32) + bs_ref[:, sl]
        o_ref[:, sl] = jax.nn.gelu(gu_ref[:, sl]) * s


def _gmlp(h, g, b, w_tril, bs_full, *, chunk):
    m = h.shape[0]
    blk = lambda c: pl.BlockSpec((chunk, GM_WIDTH), lambda i, c=c: (i, c))
    full = lambda a: pl.BlockSpec(a.shape, lambda i: (0,) * a.ndim)
    return pl.pallas_call(
        _gmlp_kernel,
        grid=(m // chunk,),
        in_specs=[blk(C_GU // GM_WIDTH), blk(C_GV // GM_WIDTH), full(g), full(b), full(w_tril), full(bs_full)],
        out_specs=[pl.BlockSpec((chunk, GM_WIDTH), lambda i: (i, 0))] * 2,
        out_shape=[jax.ShapeDtypeStruct((m, GM_WIDTH), F32)] * 2,
        compiler_params=_cparams(1),
        name="gmlp",
    )(h, h, g, b, w_tril, bs_full)


CONV_HALO = 32


def _conv_kernel(ca_ref, cb_ref, cah_ref, cbh_ref, hist_ref, dw_ref, db_ref, g_ref, b_ref, pw_ref,
                 y_ref, c_ref, buf_ref, *, tq):
    i = pl.program_id(1)
    c = ca_ref[...] * jax.nn.sigmoid(cb_ref[...])
    c_ref[...] = c
    halo = cah_ref[...] * jax.nn.sigmoid(cbh_ref[...])
    halo = jnp.where(i == 0, hist_ref[0], halo)
    buf_ref[0:CONV_HALO, :] = halo
    buf_ref[CONV_HALO:CONV_HALO + tq, :] = c
    lead = CONV_HALO - (CONV_K - 1)
    acc = jnp.zeros((tq, CONV_WIDTH), F32)
    for k in range(CONV_K):
        acc = acc + dw_ref[k:k + 1, :] * buf_ref[lead + k:lead + k + tq, :]
    y = acc + db_ref[...]
    cw = CONV_WIDTH // CONV_GROUPS
    parts = []
    for gi in range(CONV_GROUPS):
        sl = slice(gi * cw, (gi + 1) * cw)
        yn = _ln_lanes(y[:, sl], g_ref[:, sl], b_ref[:, sl])
        parts.append((yn * jax.nn.sigmoid(yn)).astype(BF16))
    y_ref[...] = jnp.dot(jnp.concatenate(parts, axis=-1), pw_ref[...], preferred_element_type=F32)


def _conv(h, hist, dw, db, g, b, pw, *, batch, seq, tq):
    m = h.shape[0]
    nt = seq // tq
    row = lambda bi, i: bi * nt + i
    halo_row = lambda bi, i: jnp.maximum((bi * seq + i * tq) // CONV_HALO - 1, 0)
    full = lambda a: pl.BlockSpec(a.shape, lambda bi, i: (0,) * a.ndim)
    kern = functools.partial(_conv_kernel, tq=tq)
    return pl.pallas_call(
        kern,
        grid=(batch, nt),
        in_specs=[
            pl.BlockSpec((tq, CONV_WIDTH), lambda bi, i: (row(bi, i), C_CA // CONV_WIDTH)),
            pl.BlockSpec((tq, CONV_WIDTH), lambda bi, i: (row(bi, i), C_CB // CONV_WIDTH)),
            pl.BlockSpec((CONV_HALO, CONV_WIDTH), lambda bi, i: (halo_row(bi, i), C_CA // CONV_WIDTH)),
            pl.BlockSpec((CONV_HALO, CONV_WIDTH), lambda bi, i: (halo_row(bi, i), C_CB // CONV_WIDTH)),
            pl.BlockSpec((1, CONV_HALO, CONV_WIDTH), lambda bi, i: (bi, 0, 0)),
            full(dw), full(db), full(g), full(b), full(pw),
        ],
        out_specs=[pl.BlockSpec((tq, CONV_WIDTH), lambda bi, i: (row(bi, i), 0))] * 2,
        out_shape=[jax.ShapeDtypeStruct((m, CONV_WIDTH), F32)] * 2,
        scratch_shapes=[pltpu.VMEM((CONV_HALO + tq, CONV_WIDTH), F32)],
        compiler_params=_cparams(2),
        name="conv",
    )(h, h, h, h, hist, dw, db, g, b, pw)


def _masked_softmax(s, mask):
    s = jnp.where(mask, s, NEG_INF)
    m = jnp.max(s, axis=-1, keepdims=True)
    m = jnp.where(m > NEG_INF, m, 0.0)
    e = jnp.exp(s - m)
    return e / jnp.maximum(jnp.sum(e, axis=-1, keepdims=True), 1e-30)


def _select_blocks(imp, tpos, n_blocks):
    tq, lanes = imp.shape
    blk = lax.broadcasted_iota(jnp.int32, (tq, lanes), 1)
    cur = tpos // SEL_BLOCK
    valid = blk * SEL_BLOCK <= tpos
    forced = jnp.logical_or(blk == 0, jnp.logical_or(blk == cur, blk == cur - 1))
    score = jnp.where(valid, jnp.where(forced, FORCED_SCORE, imp), NEG_INF)
    beaten = jnp.zeros((tq, lanes), F32)
    for i in range(n_blocks):
        ci = score[:, i:i + 1]
        earlier = (blk > i).astype(F32)
        beaten = beaten + jnp.where(ci > score, 1.0, jnp.where(ci == score, earlier, 0.0))
    picked = jnp.logical_and(beaten < float(min(N_SELECT, n_blocks)), blk < n_blocks)
    return picked.astype(F32)


SLC_CHUNK = 512


def _online_softmax_step(m_ref, l_ref, acc_ref, r, s, mask, v):
    s = jnp.where(mask, s, NEG_INF)
    m_old = m_ref[r]
    m_new = jnp.maximum(m_old, jnp.max(s, axis=-1, keepdims=True))
    m_safe = jnp.where(m_new > NEG_INF, m_new, 0.0)
    alpha = jnp.exp(m_old - m_safe)
    e = jnp.exp(s - m_safe)
    l_ref[r] = alpha * l_ref[r] + jnp.sum(e, axis=-1, keepdims=True)
    acc_ref[r] = alpha * acc_ref[r] + jnp.dot(e.astype(BF16), v, preferred_element_type=F32)
    m_ref[r] = m_new


def _attend_scaled(qh, k, v, mask):
    s = jnp.where(mask, _dot_nt(qh, k), NEG_INF)
    m = jnp.max(s, axis=-1, keepdims=True)
    m = jnp.where(m > NEG_INF, m, 0.0)
    e = jnp.exp(s - m)
    l = jnp.sum(e, axis=-1, keepdims=True)
    return jnp.dot(e.astype(BF16), v, preferred_element_type=F32) / jnp.maximum(l, 1e-30)


def _nsa_prompt_kernel(q_ref, kc_ref, ks_ref, kw_ref, vc_ref, vs_ref, vw_ref, gate_ref, cw_ref, o_ref,
                       kvb_ref, cmpb_ref, m_ref, l_ref, acc_ref, *, tq, seq):
    qi = pl.program_id(1)
    t0 = qi * tq
    scale = HEAD_DIM ** -0.5
    n_slc = seq // SEL_BLOCK
    tpos = t0 + lax.broadcasted_iota(jnp.int32, (tq, 1), 0)
    lane_c = lax.broadcasted_iota(jnp.int32, (tq, 2 * n_slc), 1)
    cmp_last = jnp.where(lane_c < n_slc, lane_c * SEL_BLOCK + CMP_BLOCK - 1, (lane_c - n_slc) * SEL_BLOCK + SEL_BLOCK - 1)
    cmp_mask = cmp_last <= tpos
    win_len = tq + WINDOW
    w0 = pl.multiple_of(jnp.maximum(t0 - WINDOW, 0), tq)
    wpos = w0 + lax.broadcasted_iota(jnp.int32, (tq, win_len), 1)
    win_mask = jnp.logical_and(wpos <= tpos, wpos >= tpos - WINDOW)
    gates = jax.nn.sigmoid(gate_ref[:, 0:HEAD_DIM])

    @pl.when(qi == 0)
    def _():
        for n, ref in enumerate((ks_ref, vs_ref, kw_ref, vw_ref)):
            kvb_ref[n] = ref[0].astype(BF16)
        for g in range(NSA_KV_HEADS):
            gs = slice(g * HEAD_DIM, (g + 1) * HEAD_DIM)
            for which, ref in enumerate((kc_ref, vc_ref)):
                x3 = ref[0, :, gs].reshape(n_slc, SEL_BLOCK, HEAD_DIM)
                ev = jnp.sum(x3 * cw_ref[g, which, 0][None], axis=1)
                od = jnp.sum(x3 * cw_ref[g, which, 1][None], axis=1)
                cmpb_ref[which, g] = jnp.concatenate([ev, od], axis=0).astype(BF16)

    for g in range(NSA_KV_HEADS):
        gs = slice(g * HEAD_DIM, (g + 1) * HEAD_DIM)
        kcmp, vcmp = cmpb_ref[0, g], cmpb_ref[1, g]
        qs, o_cmp = [], []
        imp = jnp.zeros((tq, 2 * n_slc), F32)
        for r in range(GQA_R):
            hh = g * GQA_R + r
            qh = (q_ref[:, hh * HEAD_DIM:(hh + 1) * HEAD_DIM] * scale).astype(BF16)
            qs.append(qh)
            p = _masked_softmax(_dot_nt(qh, kcmp), cmp_mask)
            imp = imp + p
            o_cmp.append(jnp.dot(p.astype(BF16), vcmp, preferred_element_type=F32))
        sel = _select_blocks(imp[:, :n_slc] + imp[:, n_slc:], tpos, n_slc).astype(BF16)

        m_ref[...] = jnp.full(m_ref.shape, NEG_INF, F32)
        l_ref[...] = jnp.zeros_like(l_ref)
        acc_ref[...] = jnp.zeros_like(acc_ref)
        for j in range(seq // SLC_CHUNK):
            @pl.when(j * SLC_CHUNK < t0 + tq)
            def _():
                ks = slice(j * SLC_CHUNK, (j + 1) * SLC_CHUNK)
                key_blk = (j * SLC_CHUNK + lax.broadcasted_iota(jnp.int32, (n_slc, SLC_CHUNK), 1)) // SEL_BLOCK
                expand = (lax.broadcasted_iota(jnp.int32, (n_slc, SLC_CHUNK), 0) == key_blk).astype(BF16)
                sel_keys = jnp.dot(sel, expand, preferred_element_type=F32)
                kpos = j * SLC_CHUNK + lax.broadcasted_iota(jnp.int32, (tq, SLC_CHUNK), 1)
                mask = jnp.logical_and(sel_keys > 0.5, kpos <= tpos)
                k, v = kvb_ref[0, ks, gs], kvb_ref[1, ks, gs]
                for r in range(GQA_R):
                    _online_softmax_step(m_ref, l_ref, acc_ref, r, _dot_nt(qs[r], k), mask, v)

        kwin = kvb_ref[2, pl.ds(w0, win_len), gs]
        vwin = kvb_ref[3, pl.ds(w0, win_len), gs]
        for r in range(GQA_R):
            hh = g * GQA_R + r
            o_slc = acc_ref[r] / jnp.maximum(l_ref[r], 1e-30)
            o_win = _attend_scaled(qs[r], kwin, vwin, win_mask)
            o_ref[:, hh * HEAD_DIM:(hh + 1) * HEAD_DIM] = (
                gates[:, 3 * hh:3 * hh + 1] * o_cmp[r] + gates[:, 3 * hh + 1:3 * hh + 2] * o_slc
                + gates[:, 3 * hh + 2:3 * hh + 3] * o_win)


def _nsa_prompt(h, cmp_w, *, batch, seq, tq):
    m = h.shape[0]
    nt = seq // tq
    kv = lambda col: pl.BlockSpec((1, seq, KV_WIDTH), lambda bi, i, col=col: (bi, 0, col // KV_WIDTH))
    h3 = h.reshape(batch, seq, PROJ_WIDTH)
    kern = functools.partial(_nsa_prompt_kernel, tq=tq, seq=seq)
    return pl.pallas_call(
        kern,
        grid=(batch, nt),
        in_specs=[
            pl.BlockSpec((tq, NSA_WIDTH), lambda bi, i: (bi * nt + i, C_Q // NSA_WIDTH)),
            kv(C_KC), kv(C_KS), kv(C_KW), kv(C_VC), kv(C_VS), kv(C_VW),
            pl.BlockSpec((tq, KV_WIDTH), lambda bi, i: (bi * nt + i, C_GATE // KV_WIDTH)),
            pl.BlockSpec(cmp_w.shape, lambda bi, i: (0,) * cmp_w.ndim),
        ],
        out_specs=pl.BlockSpec((tq, NSA_WIDTH), lambda bi, i: (bi * nt + i, 0)),
        out_shape=jax.ShapeDtypeStruct((m, NSA_WIDTH), F32),
        scratch_shapes=[
            pltpu.VMEM((4, seq, KV_WIDTH), BF16),
            pltpu.VMEM((2, NSA_KV_HEADS, 2 * (seq // SEL_BLOCK), HEAD_DIM), BF16),
            pltpu.VMEM((GQA_R, tq, 1), F32), pltpu.VMEM((GQA_R, tq, 1), F32), pltpu.VMEM((GQA_R, tq, HEAD_DIM), F32),
        ],
        compiler_params=_cparams(2),
        name="nsa_prompt",
    )(h, h3, h3, h3, h3, h3, h3, h, cmp_w)


def _out_ln_kernel(gm_ref, nsa_ref, y_ref, x_ref, w_ref, g_ref, b_ref, o_ref):
    cat = jnp.concatenate([gm_ref[...].astype(BF16), nsa_ref[...].astype(BF16), y_ref[...].astype(BF16)], axis=-1)
    mix = jnp.dot(cat, w_ref[...], preferred_element_type=F32)
    o_ref[...] = _ln_lanes(ALPHA * x_ref[...] + mix, g_ref[...], b_ref[...])


def _out_ln(gm, nsa, y, x, w, g, b, *, tm):
    m = x.shape[0]
    row = lambda width: pl.BlockSpec((tm, width), lambda i: (i, 0))
    full = lambda a: pl.BlockSpec(a.shape, lambda i: (0,) * a.ndim)
    return pl.pallas_call(
        _out_ln_kernel,
        grid=(m // tm,),
        in_specs=[row(GM_WIDTH), row(NSA_WIDTH), row(CONV_WIDTH), row(D_MODEL), full(w), full(g), full(b)],
        out_specs=row(D_MODEL),
        out_shape=jax.ShapeDtypeStruct((m, D_MODEL), F32),
        compiler_params=_cparams(1),
        name="out_ln",
    )(gm, nsa, y, x, w, g, b)


def _extract_topk(s, vals_ref, rank_ref, exact_ties):
    n_rows, tm = s.shape
    rows = lax.broadcasted_iota(jnp.int32, (n_rows, tm), 0)
    krow = lax.broadcasted_iota(jnp.int32, (PEER_TOPK, tm), 0)
    work = s
    rank = jnp.full((n_rows, tm), float(PEER_TOPK), F32)
    vals = jnp.zeros((PEER_TOPK, tm), F32)
    for r in range(PEER_TOPK):
        mx = jnp.max(work, axis=0, keepdims=True)
        hit = work == mx
        if exact_ties:
            hit = rows == jnp.min(jnp.where(hit, rows, n_rows), axis=0, keepdims=True)
        rank = jnp.where(hit, float(r), rank)
        vals = jnp.where(krow == r, mx, vals)
        work = jnp.where(hit, NEG_INF, work)
    vals_ref[...] = vals
    rank_ref[...] = rank


def _topk_rows(s, vals_ref, rank_ref):
    _extract_topk(s, vals_ref, rank_ref, exact_ties=False)
    taken = jnp.sum((rank_ref[...] < float(PEER_TOPK)).astype(F32), axis=0, keepdims=True)
    tied = jnp.max(taken) > float(PEER_TOPK)

    @pl.when(tied)
    def _():
        _extract_topk(s, vals_ref, rank_ref, exact_ties=True)

    return vals_ref[...], rank_ref[...]


def _route_kernel(qp_ref, sk_ref, r2_ref, cnt_ref, f_ref, e2_ref, v1_ref, v2_ref, rank1_ref, rank2_ref):
    tm = qp_ref.shape[0]
    q1 = qp_ref[:, 0:SUBKEY_DIM].astype(BF16)
    q2 = qp_ref[:, SUBKEY_DIM:2 * SUBKEY_DIM].astype(BF16)
    s1 = _dot_nt(sk_ref[0, 0], q1)
    s2 = _dot_nt(sk_ref[0, 1], q2)
    v1, rank1 = _topk_rows(s1, v1_ref, rank1_ref)
    v2, rank2 = _topk_rows(s2, v2_ref, rank2_ref)
    krow = lax.broadcasted_iota(jnp.int32, (PEER_TOPK, tm), 0)
    cnt = jnp.zeros((PEER_TOPK, tm), F32)
    front = v1 + v2[0:1]
    m0 = v1[0:1] + v2[0:1]
    z = jnp.zeros((1, tm), F32)
    for _ in range(PEER_TOPK):
        mx = jnp.max(front, axis=0, keepdims=True)
        idx = jnp.min(jnp.where(front == mx, krow, PEER_TOPK), axis=0, keepdims=True)
        hit = krow == idx
        cnt = cnt + hit.astype(F32)
        z = z + jnp.exp(mx - m0)
        chosen = jnp.sum(jnp.where(hit, cnt, 0.0), axis=0, keepdims=True)
        nxt = jnp.full((1, tm), NEG_INF, F32)
        for c in range(1, PEER_TOPK):
            nxt = jnp.where(chosen == float(c), v2[c:c + 1], nxt)
        front = jnp.where(hit, v1 + nxt, front)
    cnt_a = jnp.zeros((N_KEYS, tm), F32)
    for r in range(PEER_TOPK):
        cnt_a = jnp.where(rank1 == float(r), cnt[r:r + 1], cnt_a)
    r2_ref[0] = rank2.astype(BF16)
    cnt_ref[0] = cnt_a
    f_ref[0] = jnp.exp(s1 - v1[0:1]) / z
    e2_ref[0] = jnp.exp(s2 - v2[0:1]).astype(BF16)


def _route(qp, subkeys, *, tm):
    m = qp.shape[0]
    out = lambda dt: jax.ShapeDtypeStruct((PEER_HEADS, N_KEYS, m), dt)
    ospec = pl.BlockSpec((1, N_KEYS, tm), lambda i, hd: (hd, 0, i))
    return pl.pallas_call(
        _route_kernel,
        grid=(m // tm, PEER_HEADS),
        in_specs=[
            pl.BlockSpec((tm, 2 * SUBKEY_DIM), lambda i, hd: (i, hd)),
            pl.BlockSpec((1, 2, N_KEYS, SUBKEY_DIM), lambda i, hd: (hd, 0, 0, 0)),
        ],
        out_specs=[ospec] * 4,
        out_shape=[out(BF16), out(F32), out(F32), out(BF16)],
        scratch_shapes=[pltpu.VMEM((PEER_TOPK, tm), F32), pltpu.VMEM((PEER_TOPK, tm), F32),
                        pltpu.VMEM((N_KEYS, tm), F32), pltpu.VMEM((N_KEYS, tm), F32)],
        compiler_params=_cparams(2),
        name="peer_route",
    )(qp, subkeys)


PEER_SUB = 256
PEER_TE = 1024


def _peer_dense_kernel(x_ref, u_ref, v_ref, r2_ref, cnt_ref, f_ref, e2_ref, g_ref, b_ref, o_ref, xb_ref, *, te):
    j = pl.program_id(1)

    @pl.when(j == 0)
    def _():
        xb_ref[...] = x_ref[...].astype(BF16)
        o_ref[...] = jnp.zeros_like(o_ref)

    cols = []
    for c in range(te // PEER_SUB):
        act = jax.nn.gelu(_dot_nt(u_ref[c * PEER_SUB:(c + 1) * PEER_SUB, :], xb_ref[...]).astype(BF16))
        rows = []
        for al in range(PEER_SUB // N_KEYS):
            a = c * (PEER_SUB // N_KEYS) + al
            gate = None
            for hd in range(PEER_HEADS):
                cnt = cnt_ref[hd, a:a + 1, :].astype(BF16)
                f = f_ref[hd, a:a + 1, :].astype(BF16)
                term = jnp.where(r2_ref[hd] < cnt, e2_ref[hd], jnp.zeros((), BF16)) * f
                gate = term if gate is None else gate + term
            rows.append(act[al * N_KEYS:(al + 1) * N_KEYS] * gate)
        cols.append(jnp.concatenate(rows, axis=0).T)
    o_ref[...] += jnp.dot(jnp.concatenate(cols, axis=1), v_ref[...], preferred_element_type=F32)

    @pl.when(j == pl.num_programs(1) - 1)
    def _():
        o_ref[...] = _ln_lanes(ALPHA * x_ref[...] + o_ref[...], g_ref[...], b_ref[...])


def _peer_dense(x, u, v, r2, cnt, f, e2, g, b, *, tm, te):
    m = x.shape[0]
    n_exp = u.shape[0]
    by_b = pl.BlockSpec((PEER_HEADS, N_KEYS, tm), lambda i, j: (0, 0, i))
    by_a = pl.BlockSpec((PEER_HEADS, te // N_KEYS, tm), lambda i, j: (0, j, i))
    full = lambda a: pl.BlockSpec(a.shape, lambda i, j: (0,) * a.ndim)
    kern = functools.partial(_peer_dense_kernel, te=te)
    return pl.pallas_call(
        kern,
        grid=(m // tm, n_exp // te),
        in_specs=[
            pl.BlockSpec((tm, D_MODEL), lambda i, j: (i, 0), pipeline_mode=pl.Buffered(1)),
            pl.BlockSpec((te, D_MODEL), lambda i, j: (j, 0)),
            pl.BlockSpec((te, D_MODEL), lambda i, j: (j, 0)),
            by_b, by_a, by_a, by_b, full(g), full(b),
        ],
        out_specs=pl.BlockSpec((tm, D_MODEL), lambda i, j: (i, 0), pipeline_mode=pl.Buffered(1)),
        out_shape=jax.ShapeDtypeStruct((m, D_MODEL), F32),
        scratch_shapes=[pltpu.VMEM((tm, D_MODEL), BF16)],
        compiler_params=_cparams(2, vmem=V7X_VMEM_LIMIT_LARGE),
        name="peer_dense",
    )(x, u, v, r2, cnt, f, e2, g, b)


PAGES_PER_STEP = 16


def _group_queries(q_ref, g):
    heads = [q_ref[:, (g * GQA_R + r) * HEAD_DIM:(g * GQA_R + r + 1) * HEAD_DIM] for r in range(GQA_R)]
    return jnp.concatenate(heads, axis=0).astype(BF16)


def _page_rows(pages, g):
    n = pages[0].shape[0] // NSA_KV_HEADS
    return jnp.concatenate([r[pl.ds(g, n, stride=NSA_KV_HEADS), :] for r in pages], axis=0)


def _nsa_s_cmp_kernel(pt_ref, q_ref, cw_ref, *refs, seq, past_len, n_real, sel_lanes):
    nps = PAGES_PER_STEP
    k_pages, v_pages = refs[:nps], refs[nps:2 * nps]
    ocmp_ref, sel_ref, ke_ref, ko_ref, ve_ref, vo_ref = refs[2 * nps:]
    c = pl.program_id(1)
    n_pb = ke_ref.shape[0]
    rows = nps * (k_pages[0].shape[0] // NSA_KV_HEADS) // SEL_BLOCK
    off = pl.multiple_of(c * rows, rows)

    def compress(pages, which, even_ref, odd_ref):
        for g in range(NSA_KV_HEADS):
            gs = slice(g * HEAD_DIM, (g + 1) * HEAD_DIM)
            x3 = _page_rows(pages, g).reshape(rows, SEL_BLOCK, HEAD_DIM)
            even_ref[pl.ds(off, rows), gs] = jnp.sum(x3 * cw_ref[g, which, 0][None], axis=1)
            odd_ref[pl.ds(off, rows), gs] = jnp.sum(x3 * cw_ref[g, which, 1][None], axis=1)

    compress(k_pages, 0, ke_ref, ko_ref)
    compress(v_pages, 1, ve_ref, vo_ref)

    @pl.when(c == pl.num_programs(1) - 1)
    def _():
        scale = HEAD_DIM ** -0.5
        nq = GQA_R * seq
        qrow = lax.broadcasted_iota(jnp.int32, (nq, 1), 0)
        qpos = past_len + qrow % seq
        lane = lax.broadcasted_iota(jnp.int32, (nq, 2 * n_pb), 1)
        cmp_last = jnp.where(lane < n_pb, lane * SEL_BLOCK + CMP_BLOCK - 1, (lane - n_pb) * SEL_BLOCK + SEL_BLOCK - 1)
        cmp_mask = cmp_last <= qpos
        tpos = past_len + lax.broadcasted_iota(jnp.int32, (seq, 1), 0)
        for g in range(NSA_KV_HEADS):
            gs = slice(g * HEAD_DIM, (g + 1) * HEAD_DIM)
            kcat = jnp.concatenate([ke_ref[:, gs], ko_ref[:, gs]], axis=0).astype(BF16)
            vcat = jnp.concatenate([ve_ref[:, gs], vo_ref[:, gs]], axis=0).astype(BF16)
            p = _masked_softmax(_dot_nt(_group_queries(q_ref, g), kcat) * scale, cmp_mask)
            ocmp_ref[0, g] = jnp.dot(p.astype(BF16), vcat, preferred_element_type=F32)
            imp = p[0:seq]
            for r in range(1, GQA_R):
                imp = imp + p[r * seq:(r + 1) * seq]
            imp = imp[:, :n_pb] + imp[:, n_pb:]
            imp = jnp.concatenate([imp, jnp.zeros((seq, sel_lanes - n_pb), F32)], axis=1)
            sel_ref[0, g] = _select_blocks(imp, tpos, n_real)


def _nsa_s_slc_kernel(pt_ref, q_ref, sel_ref, ocmp_ref, ksn_ref, vsn_ref, wk_ref, wv_ref, gate_ref, *refs,
                      seq, past_len, sel_lanes, win_buf):
    nps = PAGES_PER_STEP
    k_pages, v_pages = refs[:nps], refs[nps:2 * nps]
    o_ref, m_ref, l_ref, acc_ref = refs[2 * nps:]
    c = pl.program_id(1)
    scale = HEAD_DIM ** -0.5
    nq = GQA_R * seq
    chunk = nps * (k_pages[0].shape[0] // NSA_KV_HEADS)
    qrow = lax.broadcasted_iota(jnp.int32, (nq, 1), 0)
    qpos = past_len + qrow % seq

    @pl.when(c == 0)
    def _():
        m_ref[...] = jnp.full(m_ref.shape, NEG_INF, F32)
        l_ref[...] = jnp.zeros_like(l_ref)
        acc_ref[...] = jnp.zeros_like(acc_ref)

    def online_update(g, s, mask, v):
        _online_softmax_step(m_ref, l_ref, acc_ref, g, s, mask, v)

    def key_mask(g, first_block, first_pos, n_keys):
        key_blk = first_block + lax.broadcasted_iota(jnp.int32, (sel_lanes, n_keys), 1) // SEL_BLOCK
        expand = (lax.broadcasted_iota(jnp.int32, (sel_lanes, n_keys), 0) == key_blk).astype(BF16)
        sel_keys = jnp.dot(sel_ref[0, g].astype(BF16), expand, preferred_element_type=F32)
        sel_keys = jnp.concatenate([sel_keys] * GQA_R, axis=0)
        kpos = first_pos + lax.broadcasted_iota(jnp.int32, (nq, n_keys), 1)
        return jnp.logical_and(sel_keys > 0.5, kpos <= qpos)

    for g in range(NSA_KV_HEADS):
        mask = key_mask(g, c * (chunk // SEL_BLOCK), c * chunk, chunk)
        scores = _dot_nt(_group_queries(q_ref, g), _page_rows(k_pages, g).astype(BF16)) * scale
        online_update(g, scores, mask, _page_rows(v_pages, g).astype(BF16))

    @pl.when(c == pl.num_programs(1) - 1)
    def _():
        gates = jax.nn.sigmoid(gate_ref[:, 0:HEAD_DIM])
        n_new = ksn_ref.shape[1]
        n_win = wk_ref.shape[1]
        wpos = past_len - win_buf + lax.broadcasted_iota(jnp.int32, (nq, n_win), 1)
        win_mask = jnp.logical_and(wpos <= qpos, wpos >= qpos - WINDOW)
        for g in range(NSA_KV_HEADS):
            gs = slice(g * HEAD_DIM, (g + 1) * HEAD_DIM)
            qg = _group_queries(q_ref, g)
            online_update(g, _dot_nt(qg, ksn_ref[0, :, gs].astype(BF16)) * scale,
                          key_mask(g, past_len // SEL_BLOCK, past_len, n_new), vsn_ref[0, :, gs].astype(BF16))
            o_slc = acc_ref[g] / jnp.maximum(l_ref[g], 1e-30)
            p = _masked_softmax(_dot_nt(qg, wk_ref[0, :, gs].astype(BF16)) * scale, win_mask)
            o_win = jnp.dot(p.astype(BF16), wv_ref[0, :, gs].astype(BF16), preferred_element_type=F32)
            o_cmp = ocmp_ref[0, g]
            for r in range(GQA_R):
                hh = g * GQA_R + r
                rs = slice(r * seq, (r + 1) * seq)
                o_ref[:, hh * HEAD_DIM:(hh + 1) * HEAD_DIM] = (
                    gates[:, 3 * hh:3 * hh + 1] * o_cmp[rs] + gates[:, 3 * hh + 1:3 * hh + 2] * o_slc[rs]
                    + gates[:, 3 * hh + 2:3 * hh + 3] * o_win[rs])


def _nsa_sample(h, caches, layer, page_table, ks_new, vs_new, wk_all, wv_all, cw, *, batch, seq, past_len, win_buf):
    cmp_k, cmp_v, slc_k, slc_v = caches
    nps = PAGES_PER_STEP
    n_pages, page = page_table.shape[1], ks_new.shape[1]
    page_rows = page * NSA_KV_HEADS
    n_pool = cmp_k.shape[0] // (DEPTH * page_rows)
    n_chunks = n_pages // nps
    n_pb = past_len // SEL_BLOCK
    n_real = n_pb + 1
    sel_lanes = -(-n_real // 128) * 128
    nq = GQA_R * seq
    page_spec = lambda k: pl.BlockSpec((page_rows, HEAD_DIM),
                                       lambda b, c, pt, k=k: (pt[b, c * nps + k] + layer * n_pool, 0))
    pages = [page_spec(k) for k in range(nps)]
    q_spec = pl.BlockSpec((seq, NSA_WIDTH), lambda b, c, pt: (b, C_Q // NSA_WIDTH))
    per_b = lambda shape: pl.BlockSpec((1,) + shape, lambda b, c, pt: (b,) + (0,) * len(shape))
    ocmp, sel = pl.pallas_call(
        functools.partial(_nsa_s_cmp_kernel, seq=seq, past_len=past_len, n_real=n_real, sel_lanes=sel_lanes),
        grid_spec=pltpu.PrefetchScalarGridSpec(
            num_scalar_prefetch=1, grid=(batch, n_chunks),
            in_specs=[q_spec, pl.BlockSpec(cw.shape, lambda b, c, pt: (0,) * cw.ndim)] + pages + pages,
            out_specs=[per_b((NSA_KV_HEADS, nq, HEAD_DIM)), per_b((NSA_KV_HEADS, seq, sel_lanes))],
            scratch_shapes=[pltpu.VMEM((n_pb, KV_WIDTH), F32)] * 4),
        out_shape=[jax.ShapeDtypeStruct((batch, NSA_KV_HEADS, nq, HEAD_DIM), F32),
                   jax.ShapeDtypeStruct((batch, NSA_KV_HEADS, seq, sel_lanes), F32)],
        compiler_params=_cparams(2),
        name="nsa_sample_cmp",
    )(page_table, h, cw, *([cmp_k] * nps), *([cmp_v] * nps))
    return pl.pallas_call(
        functools.partial(_nsa_s_slc_kernel, seq=seq, past_len=past_len, sel_lanes=sel_lanes, win_buf=win_buf),
        grid_spec=pltpu.PrefetchScalarGridSpec(
            num_scalar_prefetch=1, grid=(batch, n_chunks),
            in_specs=[q_spec, per_b((NSA_KV_HEADS, seq, sel_lanes)), per_b((NSA_KV_HEADS, nq, HEAD_DIM)),
                      per_b(ks_new.shape[1:]), per_b(vs_new.shape[1:]), per_b(wk_all.shape[1:]), per_b(wv_all.shape[1:]),
                      pl.BlockSpec((seq, KV_WIDTH), lambda b, c, pt: (b, C_GATE // KV_WIDTH))] + pages + pages,
            out_specs=pl.BlockSpec((seq, NSA_WIDTH), lambda b, c, pt: (b, 0)),
            scratch_shapes=[pltpu.VMEM((NSA_KV_HEADS, nq, 1), F32), pltpu.VMEM((NSA_KV_HEADS, nq, 1), F32),
                            pltpu.VMEM((NSA_KV_HEADS, nq, HEAD_DIM), F32)]),
        out_shape=jax.ShapeDtypeStruct((batch * seq, NSA_WIDTH), F32),
        compiler_params=_cparams(2),
        name="nsa_sample_slc",
    )(page_table, h, sel, ocmp, ks_new, vs_new, wk_all, wv_all, h, *([slc_k] * nps), *([slc_v] * nps))


def _rope_tables(pos):
    half = HEAD_DIM // 2
    inv = ROPE_THETA ** (-jnp.arange(half, dtype=F32) / half)
    ang = pos.astype(F32)[:, None] * inv[None, :]
    cos, sin = jnp.cos(ang), jnp.sin(ang)
    return jnp.concatenate([cos, cos], -1), jnp.concatenate([-sin, sin], -1)


def _prep_weights(l, w_in, gm_ln_g, gm_ln_b, gm_ws, gm_bs, cmp_wk, cmp_wv, conv_dw, conv_db, conv_ln_g,
                  conv_ln_b, conv_pw, w_out, ln1_g, ln1_b, peer_wq, peer_subkeys, peer_u, peer_v, ln2_g, ln2_b):
    w = w_in[l]
    o = np.cumsum([0, GM_WIDTH, GM_WIDTH, NSA_WIDTH] + [KV_WIDTH] * 6 + [3 * NSA_HEADS, CONV_WIDTH, CONV_WIDTH])
    gu, gv, q, kc, vc, ks, vs, kw, vw, gate, ca, cb = [w[:, o[i]:o[i + 1]] for i in range(12)]
    zeros = lambda n: jnp.zeros((w.shape[0], n), w.dtype)
    w_proj = jnp.concatenate([gu, gv, ca, cb, q, kc, ks, kw, zeros(C_ROPE_END - C_KW - KV_WIDTH), vc, vs, vw, gate,
                              zeros(PROJ_WIDTH - C_GATE - 3 * NSA_HEADS)], axis=1).astype(BF16)
    row = lambda a: a.reshape(1, -1)
    cw = jnp.stack([cmp_wk[l], cmp_wv[l]], axis=1)
    zero = jnp.zeros_like(cw)
    cw = jnp.stack([jnp.concatenate([cw, zero], -1), jnp.concatenate([zero, cw], -1)], axis=2)
    cw = jnp.broadcast_to(cw[..., None], cw.shape + (HEAD_DIM,))
    dw = jnp.concatenate([conv_dw[l], jnp.zeros((CONV_HALO - CONV_K, CONV_WIDTH), F32)], axis=0)
    return dict(
        w_proj=w_proj, gm_g=row(gm_ln_g[l]), gm_b=row(gm_ln_b[l]), gm_ws=gm_ws[l], gm_bs=gm_bs[l], cmp_w=cw,
        conv_dw=dw, conv_db=row(conv_db[l]), conv_g=row(conv_ln_g[l]), conv_b=row(conv_ln_b[l]),
        conv_pw=conv_pw[l].astype(BF16), w_out=w_out[l].astype(BF16), ln1_g=row(ln1_g[l]), ln1_b=row(ln1_b[l]),
        wq=peer_wq[l].astype(BF16), subkeys=peer_subkeys[l].astype(BF16), u=peer_u[l].astype(BF16),
        v=peer_v[l].astype(BF16), ln2_g=row(ln2_g[l]), ln2_b=row(ln2_b[l]))


def _gm_weights(p, chunk):
    w = jnp.tril(p["gm_ws"][:, :chunk, :chunk])
    bs = jnp.repeat(p["gm_bs"][:, :chunk].T, GM_WIDTH // GM_GROUPS, axis=1)
    return w, bs


def _peer(x, p, *, tm_proj, tm_route, tm_dense, te):
    m = x.shape[0]
    dummy = jnp.zeros((tm_proj, HEAD_DIM), F32)
    qp = _proj(x, p["wq"], dummy, dummy, tm=tm_proj, rope_lo=0, rope_hi=0, name="peer_q")
    r2, cnt, f, e2 = _route(qp, p["subkeys"], tm=tm_route)
    return _peer_dense(x, p["u"], p["v"], r2, cnt, f, e2, p["ln2_g"], p["ln2_b"], tm=tm_dense, te=te)


def _kv_rows(h, col, batch, seq):
    return h[:, col:col + KV_WIDTH].reshape(batch, seq, NSA_KV_HEADS, HEAD_DIM)


def _layer_prompt(x, p, cos, sin, *, batch, seq):
    m = batch * seq
    h = _proj(x, p["w_proj"], cos, sin, tm=1024, rope_lo=ROPE_LO, rope_hi=ROPE_HI, name="proj_in")
    gw, gbs = _gm_weights(p, GM_CHUNK)
    gm, vn = _gmlp(h, p["gm_g"], p["gm_b"], gw.astype(BF16), gbs, chunk=GM_CHUNK)
    nsa = _nsa_prompt(h, p["cmp_w"], batch=batch, seq=seq, tq=256)
    hist = jnp.zeros((batch, CONV_HALO, CONV_WIDTH), F32)
    y, c = _conv(h, hist, p["conv_dw"], p["conv_db"], p["conv_g"], p["conv_b"], p["conv_pw"],
                 batch=batch, seq=seq, tq=256)
    x1 = _out_ln(gm, nsa, y, x, p["w_out"], p["ln1_g"], p["ln1_b"], tm=256)
    x2 = _peer(x1, p, tm_proj=1024, tm_route=512, tm_dense=1024, te=PEER_TE)
    keep = min(WINDOW, seq)
    kw, vw = _kv_rows(h, C_KW, batch, seq), _kv_rows(h, C_VW, batch, seq)
    start = ((seq - 1) // GM_CHUNK) * GM_CHUNK
    states = (_kv_rows(h, C_KC, batch, seq), _kv_rows(h, C_VC, batch, seq), _kv_rows(h, C_KS, batch, seq),
              _kv_rows(h, C_VS, batch, seq), kw[:, -keep:], vw[:, -keep:],
              c.reshape(batch, seq, CONV_WIDTH)[:, -(CONV_K - 1):], vn.reshape(batch, seq, GM_WIDTH)[:, start:])
    return x2, states


def _layer_sample(x, p, cos, sin, caches, layer, page_table, win_k, win_v, hist30, *, batch, seq, past_len):
    m = batch * seq
    h = _proj(x, p["w_proj"], cos, sin, tm=m, rope_lo=ROPE_LO, rope_hi=ROPE_HI, name="proj_in_s")
    pad_rows = lambda a: jnp.pad(a.reshape(batch, seq, -1), ((0, 0), (0, GM_CHUNK - seq), (0, 0))).reshape(batch * GM_CHUNK, -1)
    gw, gbs = _gm_weights(p, GM_CHUNK)
    hp = pad_rows(h[:, :2 * GM_WIDTH])
    gm, vn = _gmlp(hp, p["gm_g"], p["gm_b"], gw.astype(BF16), gbs, chunk=GM_CHUNK)
    unpad = lambda a: a.reshape(batch, GM_CHUNK, -1)[:, :seq].reshape(m, -1)
    gm, vn = unpad(gm), unpad(vn)
    kvs = [_kv_rows(h, col, batch, seq) for col in (C_KC, C_VC, C_KS, C_VS, C_KW, C_VW)]
    win_buf = win_k.shape[1]
    page = past_len // page_table.shape[1]
    new_rows = lambda col, rows: jnp.pad(h[:, col:col + KV_WIDTH].reshape(batch, seq, KV_WIDTH),
                                         ((0, 0), (0, rows - seq), (0, 0)))
    flat = lambda a: a.reshape(batch, win_buf, KV_WIDTH)
    wk_all = jnp.concatenate([flat(win_k), new_rows(C_KW, HEAD_DIM)], axis=1)
    wv_all = jnp.concatenate([flat(win_v), new_rows(C_VW, HEAD_DIM)], axis=1)
    nsa = _nsa_sample(h, caches, layer, page_table, new_rows(C_KS, page), new_rows(C_VS, page), wk_all, wv_all,
                      p["cmp_w"], batch=batch, seq=seq, past_len=past_len, win_buf=win_buf)
    keep = min(WINDOW, past_len + seq)
    unflat = lambda a: a[:, :win_buf + seq][:, -keep:].reshape(batch, keep, NSA_KV_HEADS, HEAD_DIM)
    win_k, win_v = unflat(wk_all), unflat(wv_all)
    hist = jnp.pad(hist30, ((0, 0), (CONV_HALO - (CONV_K - 1), 0), (0, 0)))
    y, c = _conv(h, hist, p["conv_dw"], p["conv_db"], p["conv_g"], p["conv_b"], p["conv_pw"],
                 batch=batch, seq=seq, tq=seq)
    x1 = _out_ln(gm, nsa, y, x, p["w_out"], p["ln1_g"], p["ln1_b"], tm=m)
    x2 = _peer(x1, p, tm_proj=m, tm_route=m, tm_dense=m, te=PEER_TE)
    conv_state = jnp.concatenate([hist30, c.reshape(batch, seq, CONV_WIDTH)], axis=1)[:, -(CONV_K - 1):]
    start = ((seq - 1) // GM_CHUNK) * GM_CHUNK
    states = (kvs[0], kvs[1], kvs[2], kvs[3], win_k, win_v, conv_state, vn.reshape(batch, seq, GM_WIDTH)[:, start:])
    return x2, states


def kernel(x_prompt, x_sample, cache_cmp_k, cache_cmp_v, cache_slc_k, cache_slc_v, cache_win_k, cache_win_v,
           state_conv, page_table, w_in, gm_ln_g, gm_ln_b, gm_ws, gm_bs, cmp_wk, cmp_wv, conv_dw, conv_db,
           conv_ln_g, conv_ln_b, conv_pw, w_out, ln1_g, ln1_b, peer_wq, peer_subkeys, peer_u, peer_v, ln2_g, ln2_b):
    bp, sp, _ = x_prompt.shape
    bs, ss, _ = x_sample.shape
    depth = w_in.shape[0]
    past_len = page_table.shape[1] * cache_cmp_k.shape[2]
    pos_p = jnp.arange(sp, dtype=jnp.int32)
    pos_s = past_len + jnp.arange(ss, dtype=jnp.int32)
    cos_p, sin_p = _rope_tables(pos_p)
    cos_s, sin_s = _rope_tables(jnp.tile(pos_s, bs))
    xp = x_prompt.reshape(bp * sp, D_MODEL)
    xs = x_sample.reshape(bs * ss, D_MODEL)
    p_states, s_states = [], []
    caches = [c.reshape(-1, HEAD_DIM) for c in (cache_cmp_k, cache_cmp_v, cache_slc_k, cache_slc_v)]
    for l in range(depth):
        p = _prep_weights(l, w_in, gm_ln_g, gm_ln_b, gm_ws, gm_bs, cmp_wk, cmp_wv, conv_dw, conv_db, conv_ln_g,
                          conv_ln_b, conv_pw, w_out, ln1_g, ln1_b, peer_wq, peer_subkeys, peer_u, peer_v, ln2_g, ln2_b)
        xp, sp_state = _layer_prompt(xp, p, cos_p, sin_p, batch=bp, seq=sp)
        xs, ss_state = _layer_sample(xs, p, cos_s, sin_s, caches, l, page_table, cache_win_k[l], cache_win_v[l],
                                     state_conv[l], batch=bs, seq=ss, past_len=past_len)
        p_states.append(sp_state)
        s_states.append(ss_state)
    stack = lambda states, i: jnp.stack([st[i] for st in states], axis=0)
    outs = [xp.reshape(bp, sp, D_MODEL), xs.reshape(bs, ss, D_MODEL)]
    outs += [stack(p_states, i) for i in range(8)]
    outs += [stack(s_states, i) for i in range(8)]
    return tuple(outs)
```

```python
import functools

import numpy as np
import jax
import jax.numpy as jnp
from jax import lax
from jax.experimental import pallas as pl
from jax.experimental.pallas import tpu as pltpu

F32 = jnp.float32
BF16 = jnp.bfloat16

D_MODEL = 2048
HEAD_DIM = 128
NSA_HEADS = 8
NSA_KV_HEADS = 2
GQA_R = NSA_HEADS // NSA_KV_HEADS
GM_WIDTH = 512
GM_GROUPS = 4
GM_CHUNK = 128
CONV_WIDTH = 512
CONV_GROUPS = 4
CONV_K = 31
NSA_WIDTH = NSA_HEADS * HEAD_DIM
KV_WIDTH = NSA_KV_HEADS * HEAD_DIM
CMP_BLOCK = 32
SEL_BLOCK = 64
N_SELECT = 16
WINDOW = 512
FORCED_SCORE = 1.0e4
PEER_HEADS = 8
N_KEYS = 128
PEER_TOPK = 16
SUBKEY_DIM = 128
ROPE_THETA = 10000.0
LN_EPS = 1e-5
DEPTH = 2
ALPHA = (2 * DEPTH) ** 0.25

C_GU, C_GV, C_CA, C_CB, C_Q = 0, 512, 1024, 1536, 2048
C_KC, C_KS, C_KW, C_ROPE_END = 3072, 3328, 3584, 4096
C_VC, C_VS, C_VW, C_GATE = 4096, 4352, 4608, 4864
PROJ_WIDTH = 5120
PROJ_TN = 512
ROPE_LO, ROPE_HI = C_Q // PROJ_TN, C_ROPE_END // PROJ_TN

V7X_VMEM_LIMIT = 56 * 1024 * 1024
NEG_INF = float("-inf")


def _cparams(n_axes, vmem=V7X_VMEM_LIMIT):
    return pltpu.CompilerParams(dimension_semantics=("arbitrary",) * n_axes, vmem_limit_bytes=vmem)


def _ln_lanes(z, g, b):
    mu = jnp.mean(z, axis=-1, keepdims=True)
    d = z - mu
    var = jnp.mean(d * d, axis=-1, keepdims=True)
    return d * lax.rsqrt(var + LN_EPS) * g + b


def _dot_nt(a, b):
    return lax.dot_general(a, b, (((1,), (1,)), ((), ())), preferred_element_type=F32)


def _proj_kernel(x_ref, w_ref, cos_ref, sin_ref, o_ref, xb_ref, *, rope_lo, rope_hi, tn):
    j = pl.program_id(1)

    @pl.when(j == 0)
    def _():
        xb_ref[...] = x_ref[...].astype(BF16)

    acc = jnp.dot(xb_ref[...], w_ref[...], preferred_element_type=F32)
    is_rope = jnp.logical_and(j >= rope_lo, j < rope_hi)

    @pl.when(is_rope)
    def _():
        cos, sin = cos_ref[...], sin_ref[...]
        for c in range(tn // HEAD_DIM):
            blk = acc[:, c * HEAD_DIM:(c + 1) * HEAD_DIM]
            o_ref[:, c * HEAD_DIM:(c + 1) * HEAD_DIM] = blk * cos + pltpu.roll(blk, HEAD_DIM // 2, 1) * sin

    @pl.when(jnp.logical_not(is_rope))
    def _():
        o_ref[...] = acc


def _proj(x, w, cos, sin, *, tm, rope_lo, rope_hi, name):
    m, k = x.shape
    n = w.shape[1]
    tn = PROJ_TN
    nper = cos.shape[0] // tm
    kern = functools.partial(_proj_kernel, rope_lo=rope_lo, rope_hi=rope_hi, tn=tn)
    return pl.pallas_call(
        kern,
        grid=(m // tm, n // tn),
        in_specs=[
            pl.BlockSpec((tm, k), lambda i, j: (i, 0)),
            pl.BlockSpec((k, tn), lambda i, j: (0, j)),
            pl.BlockSpec((tm, HEAD_DIM), lambda i, j: (i % nper, 0)),
            pl.BlockSpec((tm, HEAD_DIM), lambda i, j: (i % nper, 0)),
        ],
        out_specs=pl.BlockSpec((tm, tn), lambda i, j: (i, j)),
        out_shape=jax.ShapeDtypeStruct((m, n), F32),
        scratch_shapes=[pltpu.VMEM((tm, k), BF16)],
        compiler_params=_cparams(2),
        name=name,
    )(x, w, cos, sin)


def _gmlp_kernel(gu_ref, gv_ref, g_ref, b_ref, w_ref, bs_ref, o_ref, vn_ref):
    gw = GM_WIDTH // GM_GROUPS
    for h in range(GM_GROUPS):
        sl = slice(h * gw, (h + 1) * gw)
        v = jax.nn.gelu(gv_ref[:, sl])
        vn = _ln_lanes(v, g_ref[:, sl], b_ref[:, sl])
        vn_ref[:, sl] = vn
        s = jnp.dot(w_ref[h], vn.astype(BF16), preferred_element_type=F32) + bs_ref[:, sl]
        o_ref[:, sl] = jax.nn.gelu(gu_ref[:, sl]) * s


def _gmlp(h, g, b, w_tril, bs_full, *, chunk):
    m = h.shape[0]
    blk = lambda c: pl.BlockSpec((chunk, GM_WIDTH), lambda i, c=c: (i, c))
    full = lambda a: pl.BlockSpec(a.shape, lambda i: (0,) * a.ndim)
    return pl.pallas_call(
        _gmlp_kernel,
        grid=(m // chunk,),
        in_specs=[blk(C_GU // GM_WIDTH), blk(C_GV // GM_WIDTH), full(g), full(b), full(w_tril), full(bs_full)],
        out_specs=[pl.BlockSpec((chunk, GM_WIDTH), lambda i: (i, 0))] * 2,
        out_shape=[jax.ShapeDtypeStruct((m, GM_WIDTH), F32)] * 2,
        compiler_params=_cparams(1),
        name="gmlp",
    )(h, h, g, b, w_tril, bs_full)


CONV_HALO = 32


def _conv_kernel(ca_ref, cb_ref, cah_ref, cbh_ref, hist_ref, dw_ref, db_ref, g_ref, b_ref, pw_ref,
                 y_ref, c_ref, buf_ref, *, tq):
    i = pl.program_id(1)
    c = ca_ref[...] * jax.nn.sigmoid(cb_ref[...])
    c_ref[...] = c
    halo = cah_ref[...] * jax.nn.sigmoid(cbh_ref[...])
    halo = jnp.where(i == 0, hist_ref[0], halo)
    buf_ref[0:CONV_HALO, :] = halo
    buf_ref[CONV_HALO:CONV_HALO + tq, :] = c
    lead = CONV_HALO - (CONV_K - 1)
    acc = jnp.zeros((tq, CONV_WIDTH), F32)
    for k in range(CONV_K):
        acc = acc + dw_ref[k:k + 1, :] * buf_ref[lead + k:lead + k + tq, :]
    y = acc + db_ref[...]
    cw = CONV_WIDTH // CONV_GROUPS
    parts = []
    for gi in range(CONV_GROUPS):
        sl = slice(gi * cw, (gi + 1) * cw)
        yn = _ln_lanes(y[:, sl], g_ref[:, sl], b_ref[:, sl])
        parts.append((yn * jax.nn.sigmoid(yn)).astype(BF16))
    y_ref[...] = jnp.dot(jnp.concatenate(parts, axis=-1), pw_ref[...], preferred_element_type=F32)


def _conv(h, hist, dw, db, g, b, pw, *, batch, seq, tq):
    m = h.shape[0]
    nt = seq // tq
    row = lambda bi, i: bi * nt + i
    halo_row = lambda bi, i: jnp.maximum((bi * seq + i * tq) // CONV_HALO - 1, 0)
    full = lambda a: pl.BlockSpec(a.shape, lambda bi, i: (0,) * a.ndim)
    kern = functools.partial(_conv_kernel, tq=tq)
    return pl.pallas_call(
        kern,
        grid=(batch, nt),
        in_specs=[
            pl.BlockSpec((tq, CONV_WIDTH), lambda bi, i: (row(bi, i), C_CA // CONV_WIDTH)),
            pl.BlockSpec((tq, CONV_WIDTH), lambda bi, i: (row(bi, i), C_CB // CONV_WIDTH)),
            pl.BlockSpec((CONV_HALO, CONV_WIDTH), lambda bi, i: (halo_row(bi, i), C_CA // CONV_WIDTH)),
            pl.BlockSpec((CONV_HALO, CONV_WIDTH), lambda bi, i: (halo_row(bi, i), C_CB // CONV_WIDTH)),
            pl.BlockSpec((1, CONV_HALO, CONV_WIDTH), lambda bi, i: (bi, 0, 0)),
            full(dw), full(db), full(g), full(b), full(pw),
        ],
        out_specs=[pl.BlockSpec((tq, CONV_WIDTH), lambda bi, i: (row(bi, i), 0))] * 2,
        out_shape=[jax.ShapeDtypeStruct((m, CONV_WIDTH), F32)] * 2,
        scratch_shapes=[pltpu.VMEM((CONV_HALO + tq, CONV_WIDTH), F32)],
        compiler_params=_cparams(2),
        name="conv",
    )(h, h, h, h, hist, dw, db, g, b, pw)


def _masked_softmax(s, mask):
    s = jnp.where(mask, s, NEG_INF)
    m = jnp.max(s, axis=-1, keepdims=True)
    m = jnp.where(m > NEG_INF, m, 0.0)
    e = jnp.exp(s - m)
    return e / jnp.maximum(jnp.sum(e, axis=-1, keepdims=True), 1e-30)


def _select_blocks(imp, tpos, n_blocks):
    tq, lanes = imp.shape
    blk = lax.broadcasted_iota(jnp.int32, (tq, lanes), 1)
    cur = tpos // SEL_BLOCK
    valid = blk * SEL_BLOCK <= tpos
    forced = jnp.logical_or(blk == 0, jnp.logical_or(blk == cur, blk == cur - 1))
    score = jnp.where(valid, jnp.where(forced, FORCED_SCORE, imp), NEG_INF)
    beaten = jnp.zeros((tq, lanes), F32)
    for i in range(n_blocks):
        ci = score[:, i:i + 1]
        earlier = (blk > i).astype(F32)
        beaten = beaten + jnp.where(ci > score, 1.0, jnp.where(ci == score, earlier, 0.0))
    picked = jnp.logical_and(beaten < float(min(N_SELECT, n_blocks)), blk < n_blocks)
    return picked.astype(F32)


SLC_CHUNK = 512


def _online_softmax_step(m_ref, l_ref, acc_ref, r, s, mask, v):
    s = jnp.where(mask, s, NEG_INF)
    m_old = m_ref[r]
    m_new = jnp.maximum(m_old, jnp.max(s, axis=-1, keepdims=True))
    m_safe = jnp.where(m_new > NEG_INF, m_new, 0.0)
    alpha = jnp.exp(m_old - m_safe)
    e = jnp.exp(s - m_safe)
    l_ref[r] = alpha * l_ref[r] + jnp.sum(e, axis=-1, keepdims=True)
    acc_ref[r] = alpha * acc_ref[r] + jnp.dot(e.astype(BF16), v, preferred_element_type=F32)
    m_ref[r] = m_new


VT_CHUNK = 256


def _select_blocks_t(imp_t, tpos_row, n_blocks):
    nb, tq = imp_t.shape
    blk = lax.broadcasted_iota(jnp.int32, (nb, tq), 0)
    cur = tpos_row // SEL_BLOCK
    valid = blk * SEL_BLOCK <= tpos_row
    forced = jnp.logical_or(blk == 0, jnp.logical_or(blk == cur, blk == cur - 1))
    score = jnp.where(valid, jnp.where(forced, FORCED_SCORE, imp_t), NEG_INF)
    beaten = jnp.zeros((nb, tq), F32)
    for i in range(n_blocks):
        ci = score[i:i + 1, :]
        earlier = (blk > i).astype(F32)
        beaten = beaten + jnp.where(ci > score, 1.0, jnp.where(ci == score, earlier, 0.0))
    return (beaten < float(min(N_SELECT, n_blocks))).astype(F32)


def _col_softmax(s_t, mask_t):
    s = jnp.where(mask_t, s_t, NEG_INF)
    m = jnp.max(s, axis=0, keepdims=True)
    m = jnp.where(m > NEG_INF, m, 0.0)
    e = jnp.exp(s - m)
    return e, jnp.sum(e, axis=0, keepdims=True)


def _col_softmax_step(m_ref, l_ref, acc_ref, r, s_t, mask_t, v_t):
    s = jnp.where(mask_t, s_t, NEG_INF)
    m_old = m_ref[r]
    m_new = jnp.maximum(m_old, jnp.max(s, axis=0, keepdims=True))
    m_safe = jnp.where(m_new > NEG_INF, m_new, 0.0)
    alpha = jnp.exp(m_old - m_safe)
    e = jnp.exp(s - m_safe)
    l_ref[r] = alpha * l_ref[r] + jnp.sum(e, axis=0, keepdims=True)
    acc_ref[r] = alpha * acc_ref[r] + jnp.dot(v_t, e.astype(BF16), preferred_element_type=F32)
    m_ref[r] = m_new


def _nsa_prompt_kernel(q_ref, kc_ref, ks_ref, kw_ref, vc_ref, vs_ref, vw_ref, gate_ref, cw_ref, o_ref,
                       kb_ref, vt_ref, kcmp_ref, vcmpt_ref, m_ref, l_ref, acc_ref, *, tq, seq):
    qi = pl.program_id(1)
    t0 = qi * tq
    scale = HEAD_DIM ** -0.5
    n_slc = seq // SEL_BLOCK
    tpos = t0 + lax.broadcasted_iota(jnp.int32, (1, tq), 1)
    row_c = lax.broadcasted_iota(jnp.int32, (2 * n_slc, tq), 0)
    cmp_last = jnp.where(row_c < n_slc, row_c * SEL_BLOCK + CMP_BLOCK - 1, (row_c - n_slc) * SEL_BLOCK + SEL_BLOCK - 1)
    cmp_mask = cmp_last <= tpos
    win_len = tq + WINDOW
    w0 = pl.multiple_of(jnp.maximum(t0 - WINDOW, 0), tq)
    wpos = w0 + lax.broadcasted_iota(jnp.int32, (win_len, tq), 0)
    win_mask = jnp.logical_and(wpos <= tpos, wpos >= tpos - WINDOW)
    gates_t = jax.nn.sigmoid(gate_ref[:, 0:HEAD_DIM]).T

    @pl.when(qi == 0)
    def _():
        kb_ref[0] = ks_ref[0].astype(BF16)
        kb_ref[1] = kw_ref[0].astype(BF16)
        for g in range(NSA_KV_HEADS):
            gs = slice(g * HEAD_DIM, (g + 1) * HEAD_DIM)
            for n, ref in enumerate((vs_ref, vw_ref)):
                for c in range(seq // VT_CHUNK):
                    vt_ref[n, g, c] = ref[0, c * VT_CHUNK:(c + 1) * VT_CHUNK, gs].T.astype(BF16)
            halves = []
            for which, ref in enumerate((kc_ref, vc_ref)):
                x3 = ref[0, :, gs].reshape(n_slc, SEL_BLOCK, HEAD_DIM)
                ev = jnp.sum(x3 * cw_ref[g, which, 0][None], axis=1)
                od = jnp.sum(x3 * cw_ref[g, which, 1][None], axis=1)
                halves.append(jnp.concatenate([ev, od], axis=0))
            kcmp_ref[g] = halves[0].astype(BF16)
            vcmpt_ref[g] = jnp.concatenate([halves[1], jnp.zeros_like(halves[1])], axis=0).T.astype(BF16)

    for g in range(NSA_KV_HEADS):
        gs = slice(g * HEAD_DIM, (g + 1) * HEAD_DIM)
        kcmp, vcmp_t = kcmp_ref[g], vcmpt_ref[g][:, :2 * n_slc]
        qg = jnp.concatenate([(q_ref[:, (g * GQA_R + r) * HEAD_DIM:(g * GQA_R + r + 1) * HEAD_DIM] * scale).astype(BF16)
                              for r in range(GQA_R)], axis=0)
        heads = lambda a: jnp.concatenate([a] * GQA_R, axis=1)
        e, l = _col_softmax(_dot_nt(kcmp, qg), heads(cmp_mask))
        p = e / jnp.maximum(l, 1e-30)
        o_cmp = jnp.dot(vcmp_t, p.astype(BF16), preferred_element_type=F32)
        imp = p[:, 0:tq]
        for r in range(1, GQA_R):
            imp = imp + p[:, r * tq:(r + 1) * tq]
        sel = _select_blocks_t(imp[:n_slc] + imp[n_slc:], tpos, n_slc).astype(BF16)

        m_ref[...] = jnp.full(m_ref.shape, NEG_INF, F32)
        l_ref[...] = jnp.zeros_like(l_ref)
        acc_ref[...] = jnp.zeros_like(acc_ref)
        for j in range(seq // SLC_CHUNK):
            @pl.when(j * SLC_CHUNK < t0 + tq)
            def _():
                kpos = j * SLC_CHUNK + lax.broadcasted_iota(jnp.int32, (SLC_CHUNK, tq), 0)
                key_blk = (j * SLC_CHUNK + lax.broadcasted_iota(jnp.int32, (SLC_CHUNK, n_slc), 0)) // SEL_BLOCK
                expand = (lax.broadcasted_iota(jnp.int32, (SLC_CHUNK, n_slc), 1) == key_blk).astype(BF16)
                sel_keys = jnp.dot(expand, sel, preferred_element_type=F32)
                mask = jnp.logical_and(sel_keys > 0.5, kpos <= tpos)
                k = kb_ref[0, j * SLC_CHUNK:(j + 1) * SLC_CHUNK, gs]
                per = SLC_CHUNK // VT_CHUNK
                v_t = jnp.concatenate([vt_ref[0, g, j * per + c] for c in range(per)], axis=1)
                _col_softmax_step(m_ref, l_ref, acc_ref, 0, _dot_nt(k, qg), heads(mask), v_t)

        kwin = kb_ref[1, pl.ds(w0, win_len), gs]
        wc0 = w0 // VT_CHUNK
        vwin_t = jnp.concatenate([vt_ref[1, g, wc0 + c] for c in range(win_len // VT_CHUNK)], axis=1)
        o_slc = acc_ref[0] / jnp.maximum(l_ref[0], 1e-30)
        e, l = _col_softmax(_dot_nt(kwin, qg), heads(win_mask))
        o_win = jnp.dot(vwin_t, e.astype(BF16), preferred_element_type=F32) / jnp.maximum(l, 1e-30)
        for r in range(GQA_R):
            hh = g * GQA_R + r
            cs = slice(r * tq, (r + 1) * tq)
            o_t = (gates_t[3 * hh:3 * hh + 1] * o_cmp[:, cs] + gates_t[3 * hh + 1:3 * hh + 2] * o_slc[:, cs]
                   + gates_t[3 * hh + 2:3 * hh + 3] * o_win[:, cs])
            o_ref[:, hh * HEAD_DIM:(hh + 1) * HEAD_DIM] = o_t.T


def _nsa_prompt(h, cmp_w, *, batch, seq, tq):
    m = h.shape[0]
    nt = seq // tq
    kv = lambda col: pl.BlockSpec((1, seq, KV_WIDTH), lambda bi, i, col=col: (bi, 0, col // KV_WIDTH))
    h3 = h.reshape(batch, seq, PROJ_WIDTH)
    kern = functools.partial(_nsa_prompt_kernel, tq=tq, seq=seq)
    return pl.pallas_call(
        kern,
        grid=(batch, nt),
        in_specs=[
            pl.BlockSpec((tq, NSA_WIDTH), lambda bi, i: (bi * nt + i, C_Q // NSA_WIDTH)),
            kv(C_KC), kv(C_KS), kv(C_KW), kv(C_VC), kv(C_VS), kv(C_VW),
            pl.BlockSpec((tq, KV_WIDTH), lambda bi, i: (bi * nt + i, C_GATE // KV_WIDTH)),
            pl.BlockSpec(cmp_w.shape, lambda bi, i: (0,) * cmp_w.ndim),
        ],
        out_specs=pl.BlockSpec((tq, NSA_WIDTH), lambda bi, i: (bi * nt + i, 0)),
        out_shape=jax.ShapeDtypeStruct((m, NSA_WIDTH), F32),
        scratch_shapes=[
            pltpu.VMEM((2, seq, KV_WIDTH), BF16),
            pltpu.VMEM((2, NSA_KV_HEADS, seq // VT_CHUNK, HEAD_DIM, VT_CHUNK), BF16),
            pltpu.VMEM((NSA_KV_HEADS, 2 * (seq // SEL_BLOCK), HEAD_DIM), BF16),
            pltpu.VMEM((NSA_KV_HEADS, HEAD_DIM, 4 * (seq // SEL_BLOCK)), BF16),
            pltpu.VMEM((1, 1, GQA_R * tq), F32), pltpu.VMEM((1, 1, GQA_R * tq), F32),
            pltpu.VMEM((1, HEAD_DIM, GQA_R * tq), F32),
        ],
        compiler_params=_cparams(2),
        name="nsa_prompt",
    )(h, h3, h3, h3, h3, h3, h3, h, cmp_w)


def _out_ln_kernel(gm_ref, nsa_ref, y_ref, x_ref, w_ref, g_ref, b_ref, o_ref):
    cat = jnp.concatenate([gm_ref[...].astype(BF16), nsa_ref[...].astype(BF16), y_ref[...].astype(BF16)], axis=-1)
    mix = jnp.dot(cat, w_ref[...], preferred_element_type=F32)
    o_ref[...] = _ln_lanes(ALPHA * x_ref[...] + mix, g_ref[...], b_ref[...])


def _out_ln(gm, nsa, y, x, w, g, b, *, tm):
    m = x.shape[0]
    row = lambda width: pl.BlockSpec((tm, width), lambda i: (i, 0))
    full = lambda a: pl.BlockSpec(a.shape, lambda i: (0,) * a.ndim)
    return pl.pallas_call(
        _out_ln_kernel,
        grid=(m // tm,),
        in_specs=[row(GM_WIDTH), row(NSA_WIDTH), row(CONV_WIDTH), row(D_MODEL), full(w), full(g), full(b)],
        out_specs=row(D_MODEL),
        out_shape=jax.ShapeDtypeStruct((m, D_MODEL), F32),
        compiler_params=_cparams(1),
        name="out_ln",
    )(gm, nsa, y, x, w, g, b)


def _extract_topk(s, vals_ref, rank_ref, exact_ties):
    n_rows, tm = s.shape
    rows = lax.broadcasted_iota(jnp.int32, (n_rows, tm), 0)
    krow = lax.broadcasted_iota(jnp.int32, (PEER_TOPK, tm), 0)
    work = s
    rank = jnp.full((n_rows, tm), float(PEER_TOPK), F32)
    vals = jnp.zeros((PEER_TOPK, tm), F32)
    for r in range(PEER_TOPK):
        mx = jnp.max(work, axis=0, keepdims=True)
        hit = work == mx
        if exact_ties:
            hit = rows == jnp.min(jnp.where(hit, rows, n_rows), axis=0, keepdims=True)
        rank = jnp.where(hit, float(r), rank)
        vals = jnp.where(krow == r, mx, vals)
        work = jnp.where(hit, NEG_INF, work)
    vals_ref[...] = vals
    rank_ref[...] = rank


def _topk_rows(s, vals_ref, rank_ref):
    _extract_topk(s, vals_ref, rank_ref, exact_ties=False)
    taken = jnp.sum((rank_ref[...] < float(PEER_TOPK)).astype(F32), axis=0, keepdims=True)
    tied = jnp.max(taken) > float(PEER_TOPK)

    @pl.when(tied)
    def _():
        _extract_topk(s, vals_ref, rank_ref, exact_ties=True)

    return vals_ref[...], rank_ref[...]


def _route_kernel(qp_ref, sk_ref, r2_ref, cnt_ref, f_ref, e2_ref, v1_ref, v2_ref, rank1_ref, rank2_ref):
    tm = qp_ref.shape[0]
    q1 = qp_ref[:, 0:SUBKEY_DIM].astype(BF16)
    q2 = qp_ref[:, SUBKEY_DIM:2 * SUBKEY_DIM].astype(BF16)
    s1 = _dot_nt(sk_ref[0, 0], q1)
    s2 = _dot_nt(sk_ref[0, 1], q2)
    v1, rank1 = _topk_rows(s1, v1_ref, rank1_ref)
    v2, rank2 = _topk_rows(s2, v2_ref, rank2_ref)
    krow = lax.broadcasted_iota(jnp.int32, (PEER_TOPK, tm), 0)
    cnt = jnp.zeros((PEER_TOPK, tm), F32)
    front = v1 + v2[0:1]
    m0 = v1[0:1] + v2[0:1]
    z = jnp.zeros((1, tm), F32)
    for _ in range(PEER_TOPK):
        mx = jnp.max(front, axis=0, keepdims=True)
        idx = jnp.min(jnp.where(front == mx, krow, PEER_TOPK), axis=0, keepdims=True)
        hit = krow == idx
        cnt = cnt + hit.astype(F32)
        z = z + jnp.exp(mx - m0)
        chosen = jnp.sum(jnp.where(hit, cnt, 0.0), axis=0, keepdims=True)
        nxt = jnp.full((1, tm), NEG_INF, F32)
        for c in range(1, PEER_TOPK):
            nxt = jnp.where(chosen == float(c), v2[c:c + 1], nxt)
        front = jnp.where(hit, v1 + nxt, front)
    cnt_a = jnp.zeros((N_KEYS, tm), F32)
    for r in range(PEER_TOPK):
        cnt_a = jnp.where(rank1 == float(r), cnt[r:r + 1], cnt_a)
    r2_ref[0] = rank2.astype(BF16)
    cnt_ref[0] = cnt_a
    f_ref[0] = jnp.exp(s1 - v1[0:1]) / z
    e2_ref[0] = jnp.exp(s2 - v2[0:1]).astype(BF16)


def _route(qp, subkeys, *, tm):
    m = qp.shape[0]
    out = lambda dt: jax.ShapeDtypeStruct((PEER_HEADS, N_KEYS, m), dt)
    ospec = pl.BlockSpec((1, N_KEYS, tm), lambda i, hd: (hd, 0, i))
    return pl.pallas_call(
        _route_kernel,
        grid=(m // tm, PEER_HEADS),
        in_specs=[
            pl.BlockSpec((tm, 2 * SUBKEY_DIM), lambda i, hd: (i, hd)),
            pl.BlockSpec((1, 2, N_KEYS, SUBKEY_DIM), lambda i, hd: (hd, 0, 0, 0)),
        ],
        out_specs=[ospec] * 4,
        out_shape=[out(BF16), out(F32), out(F32), out(BF16)],
        scratch_shapes=[pltpu.VMEM((PEER_TOPK, tm), F32), pltpu.VMEM((PEER_TOPK, tm), F32),
                        pltpu.VMEM((N_KEYS, tm), F32), pltpu.VMEM((N_KEYS, tm), F32)],
        compiler_params=_cparams(2),
        name="peer_route",
    )(qp, subkeys)


PEER_SUB = 256
PEER_TE = 1024


def _peer_dense_kernel(x_ref, u_ref, v_ref, r2_ref, cnt_ref, f_ref, e2_ref, g_ref, b_ref, o_ref, xb_ref, *, te):
    j = pl.program_id(1)

    @pl.when(j == 0)
    def _():
        xb_ref[...] = x_ref[...].astype(BF16)
        o_ref[...] = jnp.zeros_like(o_ref)

    cols = []
    for c in range(te // PEER_SUB):
        act = jax.nn.gelu(_dot_nt(u_ref[c * PEER_SUB:(c + 1) * PEER_SUB, :], xb_ref[...]).astype(BF16))
        rows = []
        for al in range(PEER_SUB // N_KEYS):
            a = c * (PEER_SUB // N_KEYS) + al
            gate = None
            for hd in range(PEER_HEADS):
                cnt = cnt_ref[hd, a:a + 1, :].astype(BF16)
                f = f_ref[hd, a:a + 1, :].astype(BF16)
                term = jnp.where(r2_ref[hd] < cnt, e2_ref[hd], jnp.zeros((), BF16)) * f
                gate = term if gate is None else gate + term
            rows.append(act[al * N_KEYS:(al + 1) * N_KEYS] * gate)
        cols.append(jnp.concatenate(rows, axis=0).T)
    o_ref[...] += jnp.dot(jnp.concatenate(cols, axis=1), v_ref[...], preferred_element_type=F32)

    @pl.when(j == pl.num_programs(1) - 1)
    def _():
        o_ref[...] = _ln_lanes(ALPHA * x_ref[...] + o_ref[...], g_ref[...], b_ref[...])


def _peer_dense(x, u, v, layer, r2, cnt, f, e2, g, b, *, tm, te):
    m = x.shape[0]
    n_exp = u.shape[0] // DEPTH
    first = layer * (n_exp // te)
    by_b = pl.BlockSpec((PEER_HEADS, N_KEYS, tm), lambda i, j: (0, 0, i))
    by_a = pl.BlockSpec((PEER_HEADS, te // N_KEYS, tm), lambda i, j: (0, j, i))
    full = lambda a: pl.BlockSpec(a.shape, lambda i, j: (0,) * a.ndim)
    kern = functools.partial(_peer_dense_kernel, te=te)
    return pl.pallas_call(
        kern,
        grid=(m // tm, n_exp // te),
        in_specs=[
            pl.BlockSpec((tm, D_MODEL), lambda i, j: (i, 0), pipeline_mode=pl.Buffered(1)),
            pl.BlockSpec((te, D_MODEL), lambda i, j: (first + j, 0)),
            pl.BlockSpec((te, D_MODEL), lambda i, j: (first + j, 0)),
            by_b, by_a, by_a, by_b, full(g), full(b),
        ],
        out_specs=pl.BlockSpec((tm, D_MODEL), lambda i, j: (i, 0), pipeline_mode=pl.Buffered(1)),
        out_shape=jax.ShapeDtypeStruct((m, D_MODEL), F32),
        scratch_shapes=[pltpu.VMEM((tm, D_MODEL), BF16)],
        compiler_params=_cparams(2),
        name="peer_dense",
    )(x, u, v, r2, cnt, f, e2, g, b)


PAGES_PER_STEP = 16


def _group_queries(q_ref, g):
    heads = [q_ref[:, (g * GQA_R + r) * HEAD_DIM:(g * GQA_R + r + 1) * HEAD_DIM] for r in range(GQA_R)]
    return jnp.concatenate(heads, axis=0).astype(BF16)


def _page_rows(pages, g):
    n = pages[0].shape[0] // NSA_KV_HEADS
    return jnp.concatenate([r[pl.ds(g, n, stride=NSA_KV_HEADS), :] for r in pages], axis=0)


def _nsa_s_cmp_kernel(pt_ref, q_ref, cw_ref, *refs, seq, past_len, n_real, sel_lanes):
    nps = PAGES_PER_STEP
    k_pages, v_pages = refs[:nps], refs[nps:2 * nps]
    ocmp_ref, sel_ref, ke_ref, ko_ref, ve_ref, vo_ref = refs[2 * nps:]
    c = pl.program_id(1)
    n_pb = ke_ref.shape[0]
    rows = nps * (k_pages[0].shape[0] // NSA_KV_HEADS) // SEL_BLOCK
    off = pl.multiple_of(c * rows, rows)

    def compress(pages, which, even_ref, odd_ref):
        for g in range(NSA_KV_HEADS):
            gs = slice(g * HEAD_DIM, (g + 1) * HEAD_DIM)
            x3 = _page_rows(pages, g).reshape(rows, SEL_BLOCK, HEAD_DIM)
            even_ref[pl.ds(off, rows), gs] = jnp.sum(x3 * cw_ref[g, which, 0][None], axis=1)
            odd_ref[pl.ds(off, rows), gs] = jnp.sum(x3 * cw_ref[g, which, 1][None], axis=1)

    compress(k_pages, 0, ke_ref, ko_ref)
    compress(v_pages, 1, ve_ref, vo_ref)

    @pl.when(c == pl.num_programs(1) - 1)
    def _():
        scale = HEAD_DIM ** -0.5
        nq = GQA_R * seq
        qrow = lax.broadcasted_iota(jnp.int32, (nq, 1), 0)
        qpos = past_len + qrow % seq
        lane = lax.broadcasted_iota(jnp.int32, (nq, 2 * n_pb), 1)
        cmp_last = jnp.where(lane < n_pb, lane * SEL_BLOCK + CMP_BLOCK - 1, (lane - n_pb) * SEL_BLOCK + SEL_BLOCK - 1)
        cmp_mask = cmp_last <= qpos
        tpos = past_len + lax.broadcasted_iota(jnp.int32, (seq, 1), 0)
        for g in range(NSA_KV_HEADS):
            gs = slice(g * HEAD_DIM, (g + 1) * HEAD_DIM)
            kcat = jnp.concatenate([ke_ref[:, gs], ko_ref[:, gs]], axis=0).astype(BF16)
            vcat = jnp.concatenate([ve_ref[:, gs], vo_ref[:, gs]], axis=0).astype(BF16)
            p = _masked_softmax(_dot_nt(_group_queries(q_ref, g), kcat) * scale, cmp_mask)
            ocmp_ref[0, g] = jnp.dot(p.astype(BF16), vcat, preferred_element_type=F32)
            imp = p[0:seq]
            for r in range(1, GQA_R):
                imp = imp + p[r * seq:(r + 1) * seq]
            imp = imp[:, :n_pb] + imp[:, n_pb:]
            imp = jnp.concatenate([imp, jnp.zeros((seq, sel_lanes - n_pb), F32)], axis=1)
            sel_ref[0, g] = _select_blocks(imp, tpos, n_real)


def _nsa_s_slc_kernel(pt_ref, q_ref, sel_ref, ocmp_ref, ksn_ref, vsn_ref, wk_ref, wv_ref, gate_ref, *refs,
                      seq, past_len, sel_lanes, win_buf):
    nps = PAGES_PER_STEP
    k_pages, v_pages = refs[:nps], refs[nps:2 * nps]
    o_ref, m_ref, l_ref, acc_ref = refs[2 * nps:]
    c = pl.program_id(1)
    scale = HEAD_DIM ** -0.5
    nq = GQA_R * seq
    chunk = nps * (k_pages[0].shape[0] // NSA_KV_HEADS)
    qrow = lax.broadcasted_iota(jnp.int32, (nq, 1), 0)
    qpos = past_len + qrow % seq

    @pl.when(c == 0)
    def _():
        m_ref[...] = jnp.full(m_ref.shape, NEG_INF, F32)
        l_ref[...] = jnp.zeros_like(l_ref)
        acc_ref[...] = jnp.zeros_like(acc_ref)

    def online_update(g, s, mask, v):
        _online_softmax_step(m_ref, l_ref, acc_ref, g, s, mask, v)

    def key_mask(g, first_block, first_pos, n_keys):
        key_blk = first_block + lax.broadcasted_iota(jnp.int32, (sel_lanes, n_keys), 1) // SEL_BLOCK
        expand = (lax.broadcasted_iota(jnp.int32, (sel_lanes, n_keys), 0) == key_blk).astype(BF16)
        sel_keys = jnp.dot(sel_ref[0, g].astype(BF16), expand, preferred_element_type=F32)
        sel_keys = jnp.concatenate([sel_keys] * GQA_R, axis=0)
        kpos = first_pos + lax.broadcasted_iota(jnp.int32, (nq, n_keys), 1)
        return jnp.logical_and(sel_keys > 0.5, kpos <= qpos)

    for g in range(NSA_KV_HEADS):
        mask = key_mask(g, c * (chunk // SEL_BLOCK), c * chunk, chunk)
        scores = _dot_nt(_group_queries(q_ref, g), _page_rows(k_pages, g).astype(BF16)) * scale
        online_update(g, scores, mask, _page_rows(v_pages, g).astype(BF16))

    @pl.when(c == pl.num_programs(1) - 1)
    def _():
        gates = jax.nn.sigmoid(gate_ref[:, 0:HEAD_DIM])
        n_new = ksn_ref.shape[1]
        n_win = wk_ref.shape[1]
        wpos = past_len - win_buf + lax.broadcasted_iota(jnp.int32, (nq, n_win), 1)
        win_mask = jnp.logical_and(wpos <= qpos, wpos >= qpos - WINDOW)
        for g in range(NSA_KV_HEADS):
            gs = slice(g * HEAD_DIM, (g + 1) * HEAD_DIM)
            qg = _group_queries(q_ref, g)
            online_update(g, _dot_nt(qg, ksn_ref[0, :, gs].astype(BF16)) * scale,
                          key_mask(g, past_len // SEL_BLOCK, past_len, n_new), vsn_ref[0, :, gs].astype(BF16))
            o_slc = acc_ref[g] / jnp.maximum(l_ref[g], 1e-30)
            p = _masked_softmax(_dot_nt(qg, wk_ref[0, :, gs].astype(BF16)) * scale, win_mask)
            o_win = jnp.dot(p.astype(BF16), wv_ref[0, :, gs].astype(BF16), preferred_element_type=F32)
            o_cmp = ocmp_ref[0, g]
            for r in range(GQA_R):
                hh = g * GQA_R + r
                rs = slice(r * seq, (r + 1) * seq)
                o_ref[:, hh * HEAD_DIM:(hh + 1) * HEAD_DIM] = (
                    gates[:, 3 * hh:3 * hh + 1] * o_cmp[rs] + gates[:, 3 * hh + 1:3 * hh + 2] * o_slc[rs]
                    + gates[:, 3 * hh + 2:3 * hh + 3] * o_win[rs])


def _nsa_sample(h, caches, layer, page_table, ks_new, vs_new, wk_all, wv_all, cw, *, batch, seq, past_len, win_buf):
    cmp_k, cmp_v, slc_k, slc_v = caches
    nps = PAGES_PER_STEP
    n_pages, page = page_table.shape[1], ks_new.shape[1]
    page_rows = page * NSA_KV_HEADS
    n_pool = cmp_k.shape[0] // (DEPTH * page_rows)
    n_chunks = n_pages // nps
    n_pb = past_len // SEL_BLOCK
    n_real = n_pb + 1
    sel_lanes = -(-n_real // 128) * 128
    nq = GQA_R * seq
    page_spec = lambda k: pl.BlockSpec((page_rows, HEAD_DIM),
                                       lambda b, c, pt, k=k: (pt[b, c * nps + k] + layer * n_pool, 0))
    pages = [page_spec(k) for k in range(nps)]
    q_spec = pl.BlockSpec((seq, NSA_WIDTH), lambda b, c, pt: (b, C_Q // NSA_WIDTH))
    per_b = lambda shape: pl.BlockSpec((1,) + shape, lambda b, c, pt: (b,) + (0,) * len(shape))
    ocmp, sel = pl.pallas_call(
        functools.partial(_nsa_s_cmp_kernel, seq=seq, past_len=past_len, n_real=n_real, sel_lanes=sel_lanes),
        grid_spec=pltpu.PrefetchScalarGridSpec(
            num_scalar_prefetch=1, grid=(batch, n_chunks),
            in_specs=[q_spec, pl.BlockSpec(cw.shape, lambda b, c, pt: (0,) * cw.ndim)] + pages + pages,
            out_specs=[per_b((NSA_KV_HEADS, nq, HEAD_DIM)), per_b((NSA_KV_HEADS, seq, sel_lanes))],
            scratch_shapes=[pltpu.VMEM((n_pb, KV_WIDTH), F32)] * 4),
        out_shape=[jax.ShapeDtypeStruct((batch, NSA_KV_HEADS, nq, HEAD_DIM), F32),
                   jax.ShapeDtypeStruct((batch, NSA_KV_HEADS, seq, sel_lanes), F32)],
        compiler_params=_cparams(2),
        name="nsa_sample_cmp",
    )(page_table, h, cw, *([cmp_k] * nps), *([cmp_v] * nps))
    return pl.pallas_call(
        functools.partial(_nsa_s_slc_kernel, seq=seq, past_len=past_len, sel_lanes=sel_lanes, win_buf=win_buf),
        grid_spec=pltpu.PrefetchScalarGridSpec(
            num_scalar_prefetch=1, grid=(batch, n_chunks),
            in_specs=[q_spec, per_b((NSA_KV_HEADS, seq, sel_lanes)), per_b((NSA_KV_HEADS, nq, HEAD_DIM)),
                      per_b(ks_new.shape[1:]), per_b(vs_new.shape[1:]), per_b(wk_all.shape[1:]), per_b(wv_all.shape[1:]),
                      pl.BlockSpec((seq, KV_WIDTH), lambda b, c, pt: (b, C_GATE // KV_WIDTH))] + pages + pages,
            out_specs=pl.BlockSpec((seq, NSA_WIDTH), lambda b, c, pt: (b, 0)),
            scratch_shapes=[pltpu.VMEM((NSA_KV_HEADS, nq, 1), F32), pltpu.VMEM((NSA_KV_HEADS, nq, 1), F32),
                            pltpu.VMEM((NSA_KV_HEADS, nq, HEAD_DIM), F32)]),
        out_shape=jax.ShapeDtypeStruct((batch * seq, NSA_WIDTH), F32),
        compiler_params=_cparams(2),
        name="nsa_sample_slc",
    )(page_table, h, sel, ocmp, ks_new, vs_new, wk_all, wv_all, h, *([slc_k] * nps), *([slc_v] * nps))


def _rope_tables(pos):
    half = HEAD_DIM // 2
    inv = ROPE_THETA ** (-jnp.arange(half, dtype=F32) / half)
    ang = pos.astype(F32)[:, None] * inv[None, :]
    cos, sin = jnp.cos(ang), jnp.sin(ang)
    return jnp.concatenate([cos, cos], -1), jnp.concatenate([-sin, sin], -1)


def _prep_weights(l, w_in, gm_ln_g, gm_ln_b, gm_ws, gm_bs, cmp_wk, cmp_wv, conv_dw, conv_db, conv_ln_g,
                  conv_ln_b, conv_pw, w_out, ln1_g, ln1_b, peer_wq, peer_subkeys, peer_u, peer_v, ln2_g, ln2_b):
    w = w_in[l]
    o = np.cumsum([0, GM_WIDTH, GM_WIDTH, NSA_WIDTH] + [KV_WIDTH] * 6 + [3 * NSA_HEADS, CONV_WIDTH, CONV_WIDTH])
    gu, gv, q, kc, vc, ks, vs, kw, vw, gate, ca, cb = [w[:, o[i]:o[i + 1]] for i in range(12)]
    zeros = lambda n: jnp.zeros((w.shape[0], n), w.dtype)
    w_proj = jnp.concatenate([gu, gv, ca, cb, q, kc, ks, kw, zeros(C_ROPE_END - C_KW - KV_WIDTH), vc, vs, vw, gate,
                              zeros(PROJ_WIDTH - C_GATE - 3 * NSA_HEADS)], axis=1).astype(BF16)
    row = lambda a: a.reshape(1, -1)
    cw = jnp.stack([cmp_wk[l], cmp_wv[l]], axis=1)
    zero = jnp.zeros_like(cw)
    cw = jnp.stack([jnp.concatenate([cw, zero], -1), jnp.concatenate([zero, cw], -1)], axis=2)
    cw = jnp.broadcast_to(cw[..., None], cw.shape + (HEAD_DIM,))
    dw = jnp.concatenate([conv_dw[l], jnp.zeros((CONV_HALO - CONV_K, CONV_WIDTH), F32)], axis=0)
    return dict(
        w_proj=w_proj, gm_g=row(gm_ln_g[l]), gm_b=row(gm_ln_b[l]), gm_ws=gm_ws[l], gm_bs=gm_bs[l], cmp_w=cw,
        conv_dw=dw, conv_db=row(conv_db[l]), conv_g=row(conv_ln_g[l]), conv_b=row(conv_ln_b[l]),
        conv_pw=conv_pw[l].astype(BF16), w_out=w_out[l].astype(BF16), ln1_g=row(ln1_g[l]), ln1_b=row(ln1_b[l]),
        wq=peer_wq[l].astype(BF16), subkeys=peer_subkeys[l].astype(BF16), layer=l,
        u=peer_u.astype(BF16).reshape(-1, D_MODEL), v=peer_v.astype(BF16).reshape(-1, D_MODEL),
        ln2_g=row(ln2_g[l]), ln2_b=row(ln2_b[l]))


def _gm_weights(p, chunk):
    w = jnp.tril(p["gm_ws"][:, :chunk, :chunk])
    bs = jnp.repeat(p["gm_bs"][:, :chunk].T, GM_WIDTH // GM_GROUPS, axis=1)
    return w, bs


def _peer(x, p, *, tm_proj, tm_route, tm_dense, te):
    m = x.shape[0]
    dummy = jnp.zeros((tm_proj, HEAD_DIM), F32)
    qp = _proj(x, p["wq"], dummy, dummy, tm=tm_proj, rope_lo=0, rope_hi=0, name="peer_q")
    r2, cnt, f, e2 = _route(qp, p["subkeys"], tm=tm_route)
    return _peer_dense(x, p["u"], p["v"], p["layer"], r2, cnt, f, e2, p["ln2_g"], p["ln2_b"], tm=tm_dense, te=te)


def _kv_rows(h, col, batch, seq):
    return h[:, col:col + KV_WIDTH].reshape(batch, seq, NSA_KV_HEADS, HEAD_DIM)


def _layer_prompt(x, p, cos, sin, *, batch, seq):
    m = batch * seq
    h = _proj(x, p["w_proj"], cos, sin, tm=1024, rope_lo=ROPE_LO, rope_hi=ROPE_HI, name="proj_in")
    gw, gbs = _gm_weights(p, GM_CHUNK)
    gm, vn = _gmlp(h, p["gm_g"], p["gm_b"], gw.astype(BF16), gbs, chunk=GM_CHUNK)
    nsa = _nsa_prompt(h, p["cmp_w"], batch=batch, seq=seq, tq=256)
    hist = jnp.zeros((batch, CONV_HALO, CONV_WIDTH), F32)
    y, c = _conv(h, hist, p["conv_dw"], p["conv_db"], p["conv_g"], p["conv_b"], p["conv_pw"],
                 batch=batch, seq=seq, tq=256)
    x1 = _out_ln(gm, nsa, y, x, p["w_out"], p["ln1_g"], p["ln1_b"], tm=256)
    x2 = _peer(x1, p, tm_proj=1024, tm_route=512, tm_dense=512, te=PEER_TE)
    keep = min(WINDOW, seq)
    kw, vw = _kv_rows(h, C_KW, batch, seq), _kv_rows(h, C_VW, batch, seq)
    start = ((seq - 1) // GM_CHUNK) * GM_CHUNK
    states = (_kv_rows(h, C_KC, batch, seq), _kv_rows(h, C_VC, batch, seq), _kv_rows(h, C_KS, batch, seq),
              _kv_rows(h, C_VS, batch, seq), kw[:, -keep:], vw[:, -keep:],
              c.reshape(batch, seq, CONV_WIDTH)[:, -(CONV_K - 1):], vn.reshape(batch, seq, GM_WIDTH)[:, start:])
    return x2, states


def _layer_sample(x, p, cos, sin, caches, layer, page_table, win_k, win_v, hist30, *, batch, seq, past_len):
    m = batch * seq
    h = _proj(x, p["w_proj"], cos, sin, tm=m, rope_lo=ROPE_LO, rope_hi=ROPE_HI, name="proj_in_s")
    pad_rows = lambda a: jnp.pad(a.reshape(batch, seq, -1), ((0, 0), (0, GM_CHUNK - seq), (0, 0))).reshape(batch * GM_CHUNK, -1)
    gw, gbs = _gm_weights(p, GM_CHUNK)
    hp = pad_rows(h[:, :2 * GM_WIDTH])
    gm, vn = _gmlp(hp, p["gm_g"], p["gm_b"], gw.astype(BF16), gbs, chunk=GM_CHUNK)
    unpad = lambda a: a.reshape(batch, GM_CHUNK, -1)[:, :seq].reshape(m, -1)
    gm, vn = unpad(gm), unpad(vn)
    kvs = [_kv_rows(h, col, batch, seq) for col in (C_KC, C_VC, C_KS, C_VS, C_KW, C_VW)]
    win_buf = win_k.shape[1]
    page = past_len // page_table.shape[1]
    new_rows = lambda col, rows: jnp.pad(h[:, col:col + KV_WIDTH].reshape(batch, seq, KV_WIDTH),
                                         ((0, 0), (0, rows - seq), (0, 0)))
    flat = lambda a: a.reshape(batch, win_buf, KV_WIDTH)
    wk_all = jnp.concatenate([flat(win_k), new_rows(C_KW, HEAD_DIM)], axis=1)
    wv_all = jnp.concatenate([flat(win_v), new_rows(C_VW, HEAD_DIM)], axis=1)
    nsa = _nsa_sample(h, caches, layer, page_table, new_rows(C_KS, page), new_rows(C_VS, page), wk_all, wv_all,
                      p["cmp_w"], batch=batch, seq=seq, past_len=past_len, win_buf=win_buf)
    keep = min(WINDOW, past_len + seq)
    unflat = lambda a: a[:, :win_buf + seq][:, -keep:].reshape(batch, keep, NSA_KV_HEADS, HEAD_DIM)
    win_k, win_v = unflat(wk_all), unflat(wv_all)
    hist = jnp.pad(hist30, ((0, 0), (CONV_HALO - (CONV_K - 1), 0), (0, 0)))
    y, c = _conv(h, hist, p["conv_dw"], p["conv_db"], p["conv_g"], p["conv_b"], p["conv_pw"],
                 batch=batch, seq=seq, tq=seq)
    x1 = _out_ln(gm, nsa, y, x, p["w_out"], p["ln1_g"], p["ln1_b"], tm=m)
    x2 = _peer(x1, p, tm_proj=m, tm_route=m, tm_dense=m, te=PEER_TE)
    conv_state = jnp.concatenate([hist30, c.reshape(batch, seq, CONV_WIDTH)], axis=1)[:, -(CONV_K - 1):]
    start = ((seq - 1) // GM_CHUNK) * GM_CHUNK
    states = (kvs[0], kvs[1], kvs[2], kvs[3], win_k, win_v, conv_state, vn.reshape(batch, seq, GM_WIDTH)[:, start:])
    return x2, states


def kernel(x_prompt, x_sample, cache_cmp_k, cache_cmp_v, cache_slc_k, cache_slc_v, cache_win_k, cache_win_v,
           state_conv, page_table, w_in, gm_ln_g, gm_ln_b, gm_ws, gm_bs, cmp_wk, cmp_wv, conv_dw, conv_db,
           conv_ln_g, conv_ln_b, conv_pw, w_out, ln1_g, ln1_b, peer_wq, peer_subkeys, peer_u, peer_v, ln2_g, ln2_b):
    bp, sp, _ = x_prompt.shape
    bs, ss, _ = x_sample.shape
    depth = w_in.shape[0]
    past_len = page_table.shape[1] * cache_cmp_k.shape[2]
    pos_p = jnp.arange(sp, dtype=jnp.int32)
    pos_s = past_len + jnp.arange(ss, dtype=jnp.int32)
    cos_p, sin_p = _rope_tables(pos_p)
    cos_s, sin_s = _rope_tables(jnp.tile(pos_s, bs))
    xp = x_prompt.reshape(bp * sp, D_MODEL)
    xs = x_sample.reshape(bs * ss, D_MODEL)
    p_states, s_states = [], []
    caches = [c.reshape(-1, HEAD_DIM) for c in (cache_cmp_k, cache_cmp_v, cache_slc_k, cache_slc_v)]
    for l in range(depth):
        p = _prep_weights(l, w_in, gm_ln_g, gm_ln_b, gm_ws, gm_bs, cmp_wk, cmp_wv, conv_dw, conv_db, conv_ln_g,
                          conv_ln_b, conv_pw, w_out, ln1_g, ln1_b, peer_wq, peer_subkeys, peer_u, peer_v, ln2_g, ln2_b)
        xp, sp_state = _layer_prompt(xp, p, cos_p, sin_p, batch=bp, seq=sp)
        xs, ss_state = _layer_sample(xs, p, cos_s, sin_s, caches, l, page_table, cache_win_k[l], cache_win_v[l],
                                     state_conv[l], batch=bs, seq=ss, past_len=past_len)
        p_states.append(sp_state)
        s_states.append(ss_state)
    stack = lambda states, i: jnp.stack([st[i] for st in states], axis=0)
    outs = [xp.reshape(bp, sp, D_MODEL), xs.reshape(bs, ss, D_MODEL)]
    outs += [stack(p_states, i) for i in range(8)]
    outs += [stack(s_states, i) for i in range(8)]
    return tuple(outs)
```

```python
import functools

import numpy as np
import jax
import jax.numpy as jnp
from jax import lax
from jax.experimental import pallas as pl
from jax.experimental.pallas import tpu as pltpu

F32 = jnp.float32
BF16 = jnp.bfloat16

D_MODEL = 2048
HEAD_DIM = 128
NSA_HEADS = 8
NSA_KV_HEADS = 2
GQA_R = NSA_HEADS // NSA_KV_HEADS
GM_WIDTH = 512
GM_GROUPS = 4
GM_CHUNK = 128
CONV_WIDTH = 512
CONV_GROUPS = 4
CONV_K = 31
NSA_WIDTH = NSA_HEADS * HEAD_DIM
KV_WIDTH = NSA_KV_HEADS * HEAD_DIM
CMP_BLOCK = 32
SEL_BLOCK = 64
N_SELECT = 16
WINDOW = 512
FORCED_SCORE = 1.0e4
PEER_HEADS = 8
N_KEYS = 128
PEER_TOPK = 16
SUBKEY_DIM = 128
ROPE_THETA = 10000.0
LN_EPS = 1e-5
DEPTH = 2
ALPHA = (2 * DEPTH) ** 0.25

C_GU, C_GV, C_CA, C_CB, C_Q = 0, 512, 1024, 1536, 2048
C_KC, C_KS, C_KW, C_ROPE_END = 3072, 3328, 3584, 4096
C_VC, C_VS, C_VW, C_GATE = 4096, 4352, 4608, 4864
PROJ_WIDTH = 5120
PROJ_TN = 512
ROPE_LO, ROPE_HI = C_Q // PROJ_TN, C_ROPE_END // PROJ_TN

V7X_VMEM_LIMIT = 56 * 1024 * 1024
NEG_INF = float("-inf")


def _cparams(n_axes, vmem=V7X_VMEM_LIMIT):
    return pltpu.CompilerParams(dimension_semantics=("arbitrary",) * n_axes, vmem_limit_bytes=vmem)


def _ln_lanes(z, g, b):
    mu = jnp.mean(z, axis=-1, keepdims=True)
    d = z - mu
    var = jnp.mean(d * d, axis=-1, keepdims=True)
    return d * lax.rsqrt(var + LN_EPS) * g + b


def _dot_nt(a, b):
    return lax.dot_general(a, b, (((1,), (1,)), ((), ())), preferred_element_type=F32)


def _proj_kernel(x_ref, w_ref, cos_ref, sin_ref, o_ref, xb_ref, *, rope_lo, rope_hi, tn):
    j = pl.program_id(1)

    @pl.when(j == 0)
    def _():
        xb_ref[...] = x_ref[...].astype(BF16)

    acc = jnp.dot(xb_ref[...], w_ref[...], preferred_element_type=F32)
    is_rope = jnp.logical_and(j >= rope_lo, j < rope_hi)

    @pl.when(is_rope)
    def _():
        cos, sin = cos_ref[...], sin_ref[...]
        for c in range(tn // HEAD_DIM):
            blk = acc[:, c * HEAD_DIM:(c + 1) * HEAD_DIM]
            o_ref[:, c * HEAD_DIM:(c + 1) * HEAD_DIM] = blk * cos + pltpu.roll(blk, HEAD_DIM // 2, 1) * sin

    @pl.when(jnp.logical_not(is_rope))
    def _():
        o_ref[...] = acc


def _proj(x, w, cos, sin, *, tm, rope_lo, rope_hi, name):
    m, k = x.shape
    n = w.shape[1]
    tn = PROJ_TN
    nper = cos.shape[0] // tm
    kern = functools.partial(_proj_kernel, rope_lo=rope_lo, rope_hi=rope_hi, tn=tn)
    return pl.pallas_call(
        kern,
        grid=(m // tm, n // tn),
        in_specs=[
            pl.BlockSpec((tm, k), lambda i, j: (i, 0)),
            pl.BlockSpec((k, tn), lambda i, j: (0, j)),
            pl.BlockSpec((tm, HEAD_DIM), lambda i, j: (i % nper, 0)),
            pl.BlockSpec((tm, HEAD_DIM), lambda i, j: (i % nper, 0)),
        ],
        out_specs=pl.BlockSpec((tm, tn), lambda i, j: (i, j)),
        out_shape=jax.ShapeDtypeStruct((m, n), F32),
        scratch_shapes=[pltpu.VMEM((tm, k), BF16)],
        compiler_params=_cparams(2),
        name=name,
    )(x, w, cos, sin)


def _gmlp_kernel(gu_ref, gv_ref, g_ref, b_ref, w_ref, bs_ref, o_ref, vn_ref):
    gw = GM_WIDTH // GM_GROUPS
    for h in range(GM_GROUPS):
        sl = slice(h * gw, (h + 1) * gw)
        v = jax.nn.gelu(gv_ref[:, sl])
        vn = _ln_lanes(v, g_ref[:, sl], b_ref[:, sl])
        vn_ref[:, sl] = vn
        s = jnp.dot(w_ref[h], vn.astype(BF16), preferred_element_type=F32) + bs_ref[:, sl]
        o_ref[:, sl] = jax.nn.gelu(gu_ref[:, sl]) * s


def _gmlp(h, g, b, w_tril, bs_full, *, chunk):
    m = h.shape[0]
    blk = lambda c: pl.BlockSpec((chunk, GM_WIDTH), lambda i, c=c: (i, c))
    full = lambda a: pl.BlockSpec(a.shape, lambda i: (0,) * a.ndim)
    return pl.pallas_call(
        _gmlp_kernel,
        grid=(m // chunk,),
        in_specs=[blk(C_GU // GM_WIDTH), blk(C_GV // GM_WIDTH), full(g), full(b), full(w_tril), full(bs_full)],
        out_specs=[pl.BlockSpec((chunk, GM_WIDTH), lambda i: (i, 0))] * 2,
        out_shape=[jax.ShapeDtypeStruct((m, GM_WIDTH), F32)] * 2,
        compiler_params=_cparams(1),
        name="gmlp",
    )(h, h, g, b, w_tril, bs_full)


CONV_HALO = 32


def _conv_kernel(ca_ref, cb_ref, cah_ref, cbh_ref, hist_ref, dw_ref, db_ref, g_ref, b_ref, pw_ref,
                 y_ref, c_ref, buf_ref, *, tq):
    i = pl.program_id(1)
    c = ca_ref[...] * jax.nn.sigmoid(cb_ref[...])
    c_ref[...] = c
    halo = cah_ref[...] * jax.nn.sigmoid(cbh_ref[...])
    halo = jnp.where(i == 0, hist_ref[0], halo)
    buf_ref[0:CONV_HALO, :] = halo
    buf_ref[CONV_HALO:CONV_HALO + tq, :] = c
    lead = CONV_HALO - (CONV_K - 1)
    acc = jnp.zeros((tq, CONV_WIDTH), F32)
    for k in range(CONV_K):
        acc = acc + dw_ref[k:k + 1, :] * buf_ref[lead + k:lead + k + tq, :]
    y = acc + db_ref[...]
    cw = CONV_WIDTH // CONV_GROUPS
    parts = []
    for gi in range(CONV_GROUPS):
        sl = slice(gi * cw, (gi + 1) * cw)
        yn = _ln_lanes(y[:, sl], g_ref[:, sl], b_ref[:, sl])
        parts.append((yn * jax.nn.sigmoid(yn)).astype(BF16))
    y_ref[...] = jnp.dot(jnp.concatenate(parts, axis=-1), pw_ref[...], preferred_element_type=F32)


def _conv(h, hist, dw, db, g, b, pw, *, batch, seq, tq):
    m = h.shape[0]
    nt = seq // tq
    row = lambda bi, i: bi * nt + i
    halo_row = lambda bi, i: jnp.maximum((bi * seq + i * tq) // CONV_HALO - 1, 0)
    full = lambda a: pl.BlockSpec(a.shape, lambda bi, i: (0,) * a.ndim)
    kern = functools.partial(_conv_kernel, tq=tq)
    return pl.pallas_call(
        kern,
        grid=(batch, nt),
        in_specs=[
            pl.BlockSpec((tq, CONV_WIDTH), lambda bi, i: (row(bi, i), C_CA // CONV_WIDTH)),
            pl.BlockSpec((tq, CONV_WIDTH), lambda bi, i: (row(bi, i), C_CB // CONV_WIDTH)),
            pl.BlockSpec((CONV_HALO, CONV_WIDTH), lambda bi, i: (halo_row(bi, i), C_CA // CONV_WIDTH)),
            pl.BlockSpec((CONV_HALO, CONV_WIDTH), lambda bi, i: (halo_row(bi, i), C_CB // CONV_WIDTH)),
            pl.BlockSpec((1, CONV_HALO, CONV_WIDTH), lambda bi, i: (bi, 0, 0)),
            full(dw), full(db), full(g), full(b), full(pw),
        ],
        out_specs=[pl.BlockSpec((tq, CONV_WIDTH), lambda bi, i: (row(bi, i), 0))] * 2,
        out_shape=[jax.ShapeDtypeStruct((m, CONV_WIDTH), F32)] * 2,
        scratch_shapes=[pltpu.VMEM((CONV_HALO + tq, CONV_WIDTH), F32)],
        compiler_params=_cparams(2),
        name="conv",
    )(h, h, h, h, hist, dw, db, g, b, pw)


def _masked_softmax(s, mask):
    s = jnp.where(mask, s, NEG_INF)
    m = jnp.max(s, axis=-1, keepdims=True)
    m = jnp.where(m > NEG_INF, m, 0.0)
    e = jnp.exp(s - m)
    return e / jnp.maximum(jnp.sum(e, axis=-1, keepdims=True), 1e-30)


def _select_blocks(imp, tpos, n_blocks):
    tq, lanes = imp.shape
    blk = lax.broadcasted_iota(jnp.int32, (tq, lanes), 1)
    cur = tpos // SEL_BLOCK
    valid = blk * SEL_BLOCK <= tpos
    forced = jnp.logical_or(blk == 0, jnp.logical_or(blk == cur, blk == cur - 1))
    score = jnp.where(valid, jnp.where(forced, FORCED_SCORE, imp), NEG_INF)
    beaten = jnp.zeros((tq, lanes), F32)
    for i in range(n_blocks):
        ci = score[:, i:i + 1]
        earlier = (blk > i).astype(F32)
        beaten = beaten + jnp.where(ci > score, 1.0, jnp.where(ci == score, earlier, 0.0))
    picked = jnp.logical_and(beaten < float(min(N_SELECT, n_blocks)), blk < n_blocks)
    return picked.astype(F32)


SLC_CHUNK = 512


def _online_softmax_step(m_ref, l_ref, acc_ref, r, s, mask, v):
    s = jnp.where(mask, s, NEG_INF)
    m_old = m_ref[r]
    m_new = jnp.maximum(m_old, jnp.max(s, axis=-1, keepdims=True))
    m_safe = jnp.where(m_new > NEG_INF, m_new, 0.0)
    alpha = jnp.exp(m_old - m_safe)
    e = jnp.exp(s - m_safe)
    l_ref[r] = alpha * l_ref[r] + jnp.sum(e, axis=-1, keepdims=True)
    acc_ref[r] = alpha * acc_ref[r] + jnp.dot(e.astype(BF16), v, preferred_element_type=F32)
    m_ref[r] = m_new


VT_CHUNK = 256


def _select_blocks_t(imp_t, tpos_row, n_blocks):
    nb, tq = imp_t.shape
    blk = lax.broadcasted_iota(jnp.int32, (nb, tq), 0)
    cur = tpos_row // SEL_BLOCK
    valid = blk * SEL_BLOCK <= tpos_row
    forced = jnp.logical_or(blk == 0, jnp.logical_or(blk == cur, blk == cur - 1))
    score = jnp.where(valid, jnp.where(forced, FORCED_SCORE, imp_t), NEG_INF)
    beaten = jnp.zeros((nb, tq), F32)
    for i in range(n_blocks):
        ci = score[i:i + 1, :]
        earlier = (blk > i).astype(F32)
        beaten = beaten + jnp.where(ci > score, 1.0, jnp.where(ci == score, earlier, 0.0))
    return (beaten < float(min(N_SELECT, n_blocks))).astype(F32)


def _col_softmax(s_t, mask_t):
    s = jnp.where(mask_t, s_t, NEG_INF)
    m = jnp.max(s, axis=0, keepdims=True)
    m = jnp.where(m > NEG_INF, m, 0.0)
    e = jnp.exp(s - m)
    return e, jnp.sum(e, axis=0, keepdims=True)


def _col_softmax_step(m_ref, l_ref, acc_ref, r, s_t, mask_t, v_t):
    s = jnp.where(mask_t, s_t, NEG_INF)
    m_old = m_ref[r]
    m_new = jnp.maximum(m_old, jnp.max(s, axis=0, keepdims=True))
    m_safe = jnp.where(m_new > NEG_INF, m_new, 0.0)
    alpha = jnp.exp(m_old - m_safe)
    e = jnp.exp(s - m_safe)
    l_ref[r] = alpha * l_ref[r] + jnp.sum(e, axis=0, keepdims=True)
    acc_ref[r] = alpha * acc_ref[r] + jnp.dot(v_t, e.astype(BF16), preferred_element_type=F32)
    m_ref[r] = m_new


def _nsa_prompt_kernel(q_ref, kc_ref, ks_ref, kw_ref, vc_ref, vs_ref, vw_ref, gate_ref, cw_ref, o_ref,
                       kb_ref, vt_ref, kcmp_ref, vcmpt_ref, m_ref, l_ref, acc_ref, *, tq, seq):
    qi = pl.program_id(1)
    t0 = qi * tq
    scale = HEAD_DIM ** -0.5
    n_slc = seq // SEL_BLOCK
    tpos = t0 + lax.broadcasted_iota(jnp.int32, (1, tq), 1)
    row_c = lax.broadcasted_iota(jnp.int32, (2 * n_slc, tq), 0)
    cmp_last = jnp.where(row_c < n_slc, row_c * SEL_BLOCK + CMP_BLOCK - 1, (row_c - n_slc) * SEL_BLOCK + SEL_BLOCK - 1)
    cmp_mask = cmp_last <= tpos
    win_len = tq + WINDOW
    w0 = pl.multiple_of(jnp.maximum(t0 - WINDOW, 0), tq)
    wpos = w0 + lax.broadcasted_iota(jnp.int32, (win_len, tq), 0)
    win_mask = jnp.logical_and(wpos <= tpos, wpos >= tpos - WINDOW)
    gates_t = jax.nn.sigmoid(gate_ref[:, 0:HEAD_DIM]).T

    @pl.when(qi == 0)
    def _():
        kb_ref[0] = ks_ref[0].astype(BF16)
        kb_ref[1] = kw_ref[0].astype(BF16)
        for g in range(NSA_KV_HEADS):
            gs = slice(g * HEAD_DIM, (g + 1) * HEAD_DIM)
            for n, ref in enumerate((vs_ref, vw_ref)):
                for c in range(seq // VT_CHUNK):
                    vt_ref[n, g, c] = ref[0, c * VT_CHUNK:(c + 1) * VT_CHUNK, gs].T.astype(BF16)
            halves = []
            for which, ref in enumerate((kc_ref, vc_ref)):
                x3 = ref[0, :, gs].reshape(n_slc, SEL_BLOCK, HEAD_DIM)
                ev = jnp.sum(x3 * cw_ref[g, which, 0][None], axis=1)
                od = jnp.sum(x3 * cw_ref[g, which, 1][None], axis=1)
                halves.append(jnp.concatenate([ev, od], axis=0))
            kcmp_ref[g] = halves[0].astype(BF16)
            vcmpt_ref[g] = jnp.concatenate([halves[1], jnp.zeros_like(halves[1])], axis=0).T.astype(BF16)

    for g in range(NSA_KV_HEADS):
        gs = slice(g * HEAD_DIM, (g + 1) * HEAD_DIM)
        kcmp, vcmp_t = kcmp_ref[g], vcmpt_ref[g][:, :2 * n_slc]
        qg = jnp.concatenate([(q_ref[:, (g * GQA_R + r) * HEAD_DIM:(g * GQA_R + r + 1) * HEAD_DIM] * scale).astype(BF16)
                              for r in range(GQA_R)], axis=0)
        heads = lambda a: jnp.concatenate([a] * GQA_R, axis=1)
        e, l = _col_softmax(_dot_nt(kcmp, qg), heads(cmp_mask))
        p = e / jnp.maximum(l, 1e-30)
        o_cmp = jnp.dot(vcmp_t, p.astype(BF16), preferred_element_type=F32)
        imp = p[:, 0:tq]
        for r in range(1, GQA_R):
            imp = imp + p[:, r * tq:(r + 1) * tq]
        sel = _select_blocks_t(imp[:n_slc] + imp[n_slc:], tpos, n_slc).astype(BF16)

        m_ref[...] = jnp.full(m_ref.shape, NEG_INF, F32)
        l_ref[...] = jnp.zeros_like(l_ref)
        acc_ref[...] = jnp.zeros_like(acc_ref)
        for j in range(seq // SLC_CHUNK):
            @pl.when(j * SLC_CHUNK < t0 + tq)
            def _():
                kpos = j * SLC_CHUNK + lax.broadcasted_iota(jnp.int32, (SLC_CHUNK, tq), 0)
                key_blk = (j * SLC_CHUNK + lax.broadcasted_iota(jnp.int32, (SLC_CHUNK, n_slc), 0)) // SEL_BLOCK
                expand = (lax.broadcasted_iota(jnp.int32, (SLC_CHUNK, n_slc), 1) == key_blk).astype(BF16)
                sel_keys = jnp.dot(expand, sel, preferred_element_type=F32)
                mask = jnp.logical_and(sel_keys > 0.5, kpos <= tpos)
                k = kb_ref[0, j * SLC_CHUNK:(j + 1) * SLC_CHUNK, gs]
                per = SLC_CHUNK // VT_CHUNK
                v_t = jnp.concatenate([vt_ref[0, g, j * per + c] for c in range(per)], axis=1)
                _col_softmax_step(m_ref, l_ref, acc_ref, 0, _dot_nt(k, qg), heads(mask), v_t)

        kwin = kb_ref[1, pl.ds(w0, win_len), gs]
        wc0 = w0 // VT_CHUNK
        vwin_t = jnp.concatenate([vt_ref[1, g, wc0 + c] for c in range(win_len // VT_CHUNK)], axis=1)
        o_slc = acc_ref[0] / jnp.maximum(l_ref[0], 1e-30)
        e, l = _col_softmax(_dot_nt(kwin, qg), heads(win_mask))
        o_win = jnp.dot(vwin_t, e.astype(BF16), preferred_element_type=F32) / jnp.maximum(l, 1e-30)
        for r in range(GQA_R):
            hh = g * GQA_R + r
            cs = slice(r * tq, (r + 1) * tq)
            o_t = (gates_t[3 * hh:3 * hh + 1] * o_cmp[:, cs] + gates_t[3 * hh + 1:3 * hh + 2] * o_slc[:, cs]
                   + gates_t[3 * hh + 2:3 * hh + 3] * o_win[:, cs])
            o_ref[:, hh * HEAD_DIM:(hh + 1) * HEAD_DIM] = o_t.T


def _nsa_prompt(h, cmp_w, *, batch, seq, tq):
    m = h.shape[0]
    nt = seq // tq
    kv = lambda col: pl.BlockSpec((1, seq, KV_WIDTH), lambda bi, i, col=col: (bi, 0, col // KV_WIDTH))
    h3 = h.reshape(batch, seq, PROJ_WIDTH)
    kern = functools.partial(_nsa_prompt_kernel, tq=tq, seq=seq)
    return pl.pallas_call(
        kern,
        grid=(batch, nt),
        in_specs=[
            pl.BlockSpec((tq, NSA_WIDTH), lambda bi, i: (bi * nt + i, C_Q // NSA_WIDTH)),
            kv(C_KC), kv(C_KS), kv(C_KW), kv(C_VC), kv(C_VS), kv(C_VW),
            pl.BlockSpec((tq, KV_WIDTH), lambda bi, i: (bi * nt + i, C_GATE // KV_WIDTH)),
            pl.BlockSpec(cmp_w.shape, lambda bi, i: (0,) * cmp_w.ndim),
        ],
        out_specs=pl.BlockSpec((tq, NSA_WIDTH), lambda bi, i: (bi * nt + i, 0)),
        out_shape=jax.ShapeDtypeStruct((m, NSA_WIDTH), F32),
        scratch_shapes=[
            pltpu.VMEM((2, seq, KV_WIDTH), BF16),
            pltpu.VMEM((2, NSA_KV_HEADS, seq // VT_CHUNK, HEAD_DIM, VT_CHUNK), BF16),
            pltpu.VMEM((NSA_KV_HEADS, 2 * (seq // SEL_BLOCK), HEAD_DIM), BF16),
            pltpu.VMEM((NSA_KV_HEADS, HEAD_DIM, 4 * (seq // SEL_BLOCK)), BF16),
            pltpu.VMEM((1, 1, GQA_R * tq), F32), pltpu.VMEM((1, 1, GQA_R * tq), F32),
            pltpu.VMEM((1, HEAD_DIM, GQA_R * tq), F32),
        ],
        compiler_params=_cparams(2),
        name="nsa_prompt",
    )(h, h3, h3, h3, h3, h3, h3, h, cmp_w)


def _out_ln_kernel(gm_ref, nsa_ref, y_ref, x_ref, w_ref, g_ref, b_ref, o_ref):
    cat = jnp.concatenate([gm_ref[...].astype(BF16), nsa_ref[...].astype(BF16), y_ref[...].astype(BF16)], axis=-1)
    mix = jnp.dot(cat, w_ref[...], preferred_element_type=F32)
    o_ref[...] = _ln_lanes(ALPHA * x_ref[...] + mix, g_ref[...], b_ref[...])


def _out_ln(gm, nsa, y, x, w, g, b, *, tm):
    m = x.shape[0]
    row = lambda width: pl.BlockSpec((tm, width), lambda i: (i, 0))
    full = lambda a: pl.BlockSpec(a.shape, lambda i: (0,) * a.ndim)
    return pl.pallas_call(
        _out_ln_kernel,
        grid=(m // tm,),
        in_specs=[row(GM_WIDTH), row(NSA_WIDTH), row(CONV_WIDTH), row(D_MODEL), full(w), full(g), full(b)],
        out_specs=row(D_MODEL),
        out_shape=jax.ShapeDtypeStruct((m, D_MODEL), F32),
        compiler_params=_cparams(1),
        name="out_ln",
    )(gm, nsa, y, x, w, g, b)


def _extract_topk(s, exact_ties, want_rank):
    n_rows, tm = s.shape
    rows = lax.broadcasted_iota(jnp.int32, (n_rows, tm), 0)
    krow = lax.broadcasted_iota(jnp.int32, (PEER_TOPK, tm), 0)
    work = s
    rank = jnp.full((n_rows, tm), float(PEER_TOPK), F32) if want_rank else None
    vals = jnp.zeros((PEER_TOPK, tm), F32)
    for r in range(PEER_TOPK):
        mx = jnp.max(work, axis=0, keepdims=True)
        hit = work == mx
        if exact_ties:
            hit = rows == jnp.min(jnp.where(hit, rows, n_rows), axis=0, keepdims=True)
        if want_rank:
            rank = jnp.where(hit, float(r), rank)
        vals = jnp.where(krow == r, mx, vals)
        work = jnp.where(hit, NEG_INF, work)
    taken = jnp.sum((work == NEG_INF).astype(F32), axis=0, keepdims=True)
    return vals, rank, taken


def _pair_topk(v1, v2):
    tm = v1.shape[1]
    krow = lax.broadcasted_iota(jnp.int32, (PEER_TOPK, tm), 0)
    cnt = jnp.zeros((PEER_TOPK, tm), F32)
    front = v1 + v2[0:1]
    m0 = v1[0:1] + v2[0:1]
    z = jnp.zeros((1, tm), F32)
    for _ in range(PEER_TOPK):
        mx = jnp.max(front, axis=0, keepdims=True)
        idx = jnp.min(jnp.where(front == mx, krow, PEER_TOPK), axis=0, keepdims=True)
        hit = krow == idx
        cnt = cnt + hit.astype(F32)
        z = z + jnp.exp(mx - m0)
        chosen = jnp.sum(jnp.where(hit, cnt, 0.0), axis=0, keepdims=True)
        nxt = jnp.full((1, tm), NEG_INF, F32)
        for c in range(1, PEER_TOPK):
            nxt = jnp.where(chosen == float(c), v2[c:c + 1], nxt)
        front = jnp.where(hit, v1 + nxt, front)
    return cnt, z


def _route_kernel(qp_ref, sk_ref, r2_ref, cnt_ref, f_ref, e2_ref):
    q1 = qp_ref[:, 0:SUBKEY_DIM].astype(BF16)
    q2 = qp_ref[:, SUBKEY_DIM:2 * SUBKEY_DIM].astype(BF16)
    s1 = _dot_nt(sk_ref[0, 0], q1)
    s2 = _dot_nt(sk_ref[0, 1], q2)

    def emit(v1, v2, cnt, z, cnt_a, rank2):
        r2_ref[0] = rank2.astype(BF16)
        cnt_ref[0] = cnt_a
        f_ref[0] = jnp.exp(s1 - v1[0:1]) / z
        e2_ref[0] = jnp.exp(s2 - v2[0:1]).astype(BF16)

    v1, _, taken1 = _extract_topk(s1, exact_ties=False, want_rank=False)
    v2, rank2, taken2 = _extract_topk(s2, exact_ties=False, want_rank=True)
    cnt, z = _pair_topk(v1, v2)
    cnt_a = jnp.zeros(s1.shape, F32)
    for r in range(PEER_TOPK):
        cnt_a = jnp.where(s1 == v1[r:r + 1], cnt[r:r + 1], cnt_a)
    emit(v1, v2, cnt, z, cnt_a, rank2)

    @pl.when(jnp.max(jnp.maximum(taken1, taken2)) > float(PEER_TOPK))
    def _():
        v1, rank1, _ = _extract_topk(s1, exact_ties=True, want_rank=True)
        v2, rank2, _ = _extract_topk(s2, exact_ties=True, want_rank=True)
        cnt, z = _pair_topk(v1, v2)
        cnt_a = jnp.zeros(s1.shape, F32)
        for r in range(PEER_TOPK):
            cnt_a = jnp.where(rank1 == float(r), cnt[r:r + 1], cnt_a)
        emit(v1, v2, cnt, z, cnt_a, rank2)


def _route(qp, subkeys, *, tm):
    m = qp.shape[0]
    out = lambda dt: jax.ShapeDtypeStruct((PEER_HEADS, N_KEYS, m), dt)
    ospec = pl.BlockSpec((1, N_KEYS, tm), lambda i, hd: (hd, 0, i))
    return pl.pallas_call(
        _route_kernel,
        grid=(m // tm, PEER_HEADS),
        in_specs=[
            pl.BlockSpec((tm, 2 * SUBKEY_DIM), lambda i, hd: (i, hd)),
            pl.BlockSpec((1, 2, N_KEYS, SUBKEY_DIM), lambda i, hd: (hd, 0, 0, 0)),
        ],
        out_specs=[ospec] * 4,
        out_shape=[out(BF16), out(F32), out(F32), out(BF16)],
        compiler_params=_cparams(2),
        name="peer_route",
    )(qp, subkeys)


PEER_SUB = 256
PEER_TE = 1024


def _peer_dense_kernel(x_ref, u_ref, v_ref, r2_ref, cnt_ref, f_ref, e2_ref, g_ref, b_ref, o_ref, xb_ref, *, te):
    j = pl.program_id(1)

    @pl.when(j == 0)
    def _():
        xb_ref[...] = x_ref[...].astype(BF16)
        o_ref[...] = jnp.zeros_like(o_ref)

    cols = []
    for c in range(te // PEER_SUB):
        act = jax.nn.gelu(_dot_nt(u_ref[c * PEER_SUB:(c + 1) * PEER_SUB, :], xb_ref[...]).astype(BF16))
        rows = []
        for al in range(PEER_SUB // N_KEYS):
            a = c * (PEER_SUB // N_KEYS) + al
            gate = None
            for hd in range(PEER_HEADS):
                cnt = cnt_ref[hd, a:a + 1, :].astype(BF16)
                f = f_ref[hd, a:a + 1, :].astype(BF16)
                term = jnp.where(r2_ref[hd] < cnt, e2_ref[hd], jnp.zeros((), BF16)) * f
                gate = term if gate is None else gate + term
            rows.append(act[al * N_KEYS:(al + 1) * N_KEYS] * gate)
        cols.append(jnp.concatenate(rows, axis=0).T)
    o_ref[...] += jnp.dot(jnp.concatenate(cols, axis=1), v_ref[...], preferred_element_type=F32)

    @pl.when(j == pl.num_programs(1) - 1)
    def _():
        o_ref[...] = _ln_lanes(ALPHA * x_ref[...] + o_ref[...], g_ref[...], b_ref[...])


def _peer_dense(x, u, v, layer, r2, cnt, f, e2, g, b, *, tm, te):
    m = x.shape[0]
    n_exp = u.shape[0] // DEPTH
    first = layer * (n_exp // te)
    by_b = pl.BlockSpec((PEER_HEADS, N_KEYS, tm), lambda i, j: (0, 0, i))
    by_a = pl.BlockSpec((PEER_HEADS, te // N_KEYS, tm), lambda i, j: (0, j, i))
    full = lambda a: pl.BlockSpec(a.shape, lambda i, j: (0,) * a.ndim)
    kern = functools.partial(_peer_dense_kernel, te=te)
    return pl.pallas_call(
        kern,
        grid=(m // tm, n_exp // te),
        in_specs=[
            pl.BlockSpec((tm, D_MODEL), lambda i, j: (i, 0), pipeline_mode=pl.Buffered(1)),
            pl.BlockSpec((te, D_MODEL), lambda i, j: (first + j, 0)),
            pl.BlockSpec((te, D_MODEL), lambda i, j: (first + j, 0)),
            by_b, by_a, by_a, by_b, full(g), full(b),
        ],
        out_specs=pl.BlockSpec((tm, D_MODEL), lambda i, j: (i, 0), pipeline_mode=pl.Buffered(1)),
        out_shape=jax.ShapeDtypeStruct((m, D_MODEL), F32),
        scratch_shapes=[pltpu.VMEM((tm, D_MODEL), BF16)],
        compiler_params=_cparams(2),
        name="peer_dense",
    )(x, u, v, r2, cnt, f, e2, g, b)


PAGES_PER_STEP = 16


def _group_queries(q_ref, g):
    heads = [q_ref[:, (g * GQA_R + r) * HEAD_DIM:(g * GQA_R + r + 1) * HEAD_DIM] for r in range(GQA_R)]
    return jnp.concatenate(heads, axis=0).astype(BF16)


def _page_rows(pages, g):
    n = pages[0].shape[0] // NSA_KV_HEADS
    return jnp.concatenate([r[pl.ds(g, n, stride=NSA_KV_HEADS), :] for r in pages], axis=0)


def _nsa_s_cmp_kernel(pt_ref, q_ref, cw_ref, *refs, seq, past_len, n_real, sel_lanes):
    nps = PAGES_PER_STEP
    k_pages, v_pages = refs[:nps], refs[nps:2 * nps]
    ocmp_ref, sel_ref, ke_ref, ko_ref, ve_ref, vo_ref = refs[2 * nps:]
    c = pl.program_id(1)
    n_pb = ke_ref.shape[0]
    rows = nps * (k_pages[0].shape[0] // NSA_KV_HEADS) // SEL_BLOCK
    off = pl.multiple_of(c * rows, rows)

    def compress(pages, which, even_ref, odd_ref):
        for g in range(NSA_KV_HEADS):
            gs = slice(g * HEAD_DIM, (g + 1) * HEAD_DIM)
            x3 = _page_rows(pages, g).reshape(rows, SEL_BLOCK, HEAD_DIM)
            even_ref[pl.ds(off, rows), gs] = jnp.sum(x3 * cw_ref[g, which, 0][None], axis=1)
            odd_ref[pl.ds(off, rows), gs] = jnp.sum(x3 * cw_ref[g, which, 1][None], axis=1)

    compress(k_pages, 0, ke_ref, ko_ref)
    compress(v_pages, 1, ve_ref, vo_ref)

    @pl.when(c == pl.num_programs(1) - 1)
    def _():
        scale = HEAD_DIM ** -0.5
        nq = GQA_R * seq
        qrow = lax.broadcasted_iota(jnp.int32, (nq, 1), 0)
        qpos = past_len + qrow % seq
        lane = lax.broadcasted_iota(jnp.int32, (nq, 2 * n_pb), 1)
        cmp_last = jnp.where(lane < n_pb, lane * SEL_BLOCK + CMP_BLOCK - 1, (lane - n_pb) * SEL_BLOCK + SEL_BLOCK - 1)
        cmp_mask = cmp_last <= qpos
        tpos = past_len + lax.broadcasted_iota(jnp.int32, (seq, 1), 0)
        for g in range(NSA_KV_HEADS):
            gs = slice(g * HEAD_DIM, (g + 1) * HEAD_DIM)
            kcat = jnp.concatenate([ke_ref[:, gs], ko_ref[:, gs]], axis=0).astype(BF16)
            vcat = jnp.concatenate([ve_ref[:, gs], vo_ref[:, gs]], axis=0).astype(BF16)
            p = _masked_softmax(_dot_nt(_group_queries(q_ref, g), kcat) * scale, cmp_mask)
            ocmp_ref[0, g] = jnp.dot(p.astype(BF16), vcat, preferred_element_type=F32)
            imp = p[0:seq]
            for r in range(1, GQA_R):
                imp = imp + p[r * seq:(r + 1) * seq]
            imp = imp[:, :n_pb] + imp[:, n_pb:]
            imp = jnp.concatenate([imp, jnp.zeros((seq, sel_lanes - n_pb), F32)], axis=1)
            sel = _select_blocks(imp, tpos, n_real)
            bpc = sel_ref.shape[-1]
            for ch in range(sel_ref.shape[2]):
                sel_ref[0, g, ch] = sel[:, ch * bpc:(ch + 1) * bpc]


def _nsa_s_slc_kernel(pt_ref, q_ref, sel_ref, sel_new_ref, ocmp_ref, ksn_ref, vsn_ref, wk_ref, wv_ref, gate_ref, *refs,
                      seq, past_len, win_buf):
    nps = PAGES_PER_STEP
    k_pages, v_pages = refs[:nps], refs[nps:2 * nps]
    o_ref, m_ref, l_ref, acc_ref = refs[2 * nps:]
    c = pl.program_id(1)
    scale = HEAD_DIM ** -0.5
    nq = GQA_R * seq
    chunk = nps * (k_pages[0].shape[0] // NSA_KV_HEADS)
    qrow = lax.broadcasted_iota(jnp.int32, (nq, 1), 0)
    qpos = past_len + qrow % seq

    @pl.when(c == 0)
    def _():
        m_ref[...] = jnp.full(m_ref.shape, NEG_INF, F32)
        l_ref[...] = jnp.zeros_like(l_ref)
        acc_ref[...] = jnp.zeros_like(acc_ref)

    def online_update(g, s, mask, v):
        _online_softmax_step(m_ref, l_ref, acc_ref, g, s, mask, v)

    def key_mask(sel_blocks, first_pos, n_keys):
        bpc = sel_blocks.shape[1]
        key_blk = lax.broadcasted_iota(jnp.int32, (bpc, n_keys), 1) // SEL_BLOCK
        expand = (lax.broadcasted_iota(jnp.int32, (bpc, n_keys), 0) == key_blk).astype(BF16)
        sel_keys = jnp.dot(sel_blocks.astype(BF16), expand, preferred_element_type=F32)
        sel_keys = jnp.concatenate([sel_keys] * GQA_R, axis=0)
        kpos = first_pos + lax.broadcasted_iota(jnp.int32, (nq, n_keys), 1)
        return jnp.logical_and(sel_keys > 0.5, kpos <= qpos)

    for g in range(NSA_KV_HEADS):
        mask = key_mask(sel_ref[0, g, 0], c * chunk, chunk)
        scores = _dot_nt(_group_queries(q_ref, g), _page_rows(k_pages, g).astype(BF16)) * scale
        online_update(g, scores, mask, _page_rows(v_pages, g).astype(BF16))

    @pl.when(c == pl.num_programs(1) - 1)
    def _():
        gates = jax.nn.sigmoid(gate_ref[:, 0:HEAD_DIM])
        n_new = ksn_ref.shape[1]
        n_win = wk_ref.shape[1]
        wpos = past_len - win_buf + lax.broadcasted_iota(jnp.int32, (nq, n_win), 1)
        win_mask = jnp.logical_and(wpos <= qpos, wpos >= qpos - WINDOW)
        for g in range(NSA_KV_HEADS):
            gs = slice(g * HEAD_DIM, (g + 1) * HEAD_DIM)
            qg = _group_queries(q_ref, g)
            online_update(g, _dot_nt(qg, ksn_ref[0, :, gs].astype(BF16)) * scale,
                          key_mask(sel_new_ref[0, g, 0], past_len, n_new), vsn_ref[0, :, gs].astype(BF16))
            o_slc = acc_ref[g] / jnp.maximum(l_ref[g], 1e-30)
            p = _masked_softmax(_dot_nt(qg, wk_ref[0, :, gs].astype(BF16)) * scale, win_mask)
            o_win = jnp.dot(p.astype(BF16), wv_ref[0, :, gs].astype(BF16), preferred_element_type=F32)
            o_cmp = ocmp_ref[0, g]
            for r in range(GQA_R):
                hh = g * GQA_R + r
                rs = slice(r * seq, (r + 1) * seq)
                o_ref[:, hh * HEAD_DIM:(hh + 1) * HEAD_DIM] = (
                    gates[:, 3 * hh:3 * hh + 1] * o_cmp[rs] + gates[:, 3 * hh + 1:3 * hh + 2] * o_slc[rs]
                    + gates[:, 3 * hh + 2:3 * hh + 3] * o_win[rs])


def _nsa_sample(h, caches, layer, page_table, ks_new, vs_new, wk_all, wv_all, cw, *, batch, seq, past_len, win_buf):
    cmp_k, cmp_v, slc_k, slc_v = caches
    nps = PAGES_PER_STEP
    n_pages, page = page_table.shape[1], ks_new.shape[1]
    page_rows = page * NSA_KV_HEADS
    n_pool = cmp_k.shape[0] // (DEPTH * page_rows)
    n_chunks = n_pages // nps
    n_pb = past_len // SEL_BLOCK
    n_real = n_pb + 1
    bpc = nps * page // SEL_BLOCK
    sel_lanes = -(-max(n_real, (n_chunks + 1) * bpc) // 128) * 128
    sel_shape = (NSA_KV_HEADS, n_chunks + 1, seq, bpc)
    sel_chunk = lambda which: pl.BlockSpec((1, NSA_KV_HEADS, 1, seq, bpc), lambda b, c, pt: (b, 0, which(c), 0, 0))
    nq = GQA_R * seq
    page_spec = lambda k: pl.BlockSpec((page_rows, HEAD_DIM),
                                       lambda b, c, pt, k=k: (pt[b, c * nps + k] + layer * n_pool, 0))
    pages = [page_spec(k) for k in range(nps)]
    q_spec = pl.BlockSpec((seq, NSA_WIDTH), lambda b, c, pt: (b, C_Q // NSA_WIDTH))
    per_b = lambda shape: pl.BlockSpec((1,) + shape, lambda b, c, pt: (b,) + (0,) * len(shape))
    ocmp, sel = pl.pallas_call(
        functools.partial(_nsa_s_cmp_kernel, seq=seq, past_len=past_len, n_real=n_real, sel_lanes=sel_lanes),
        grid_spec=pltpu.PrefetchScalarGridSpec(
            num_scalar_prefetch=1, grid=(batch, n_chunks),
            in_specs=[q_spec, pl.BlockSpec(cw.shape, lambda b, c, pt: (0,) * cw.ndim)] + pages + pages,
            out_specs=[per_b((NSA_KV_HEADS, nq, HEAD_DIM)), per_b(sel_shape)],
            scratch_shapes=[pltpu.VMEM((n_pb, KV_WIDTH), F32)] * 4),
        out_shape=[jax.ShapeDtypeStruct((batch, NSA_KV_HEADS, nq, HEAD_DIM), F32),
                   jax.ShapeDtypeStruct((batch,) + sel_shape, F32)],
        compiler_params=_cparams(2),
        name="nsa_sample_cmp",
    )(page_table, h, cw, *([cmp_k] * nps), *([cmp_v] * nps))
    return pl.pallas_call(
        functools.partial(_nsa_s_slc_kernel, seq=seq, past_len=past_len, win_buf=win_buf),
        grid_spec=pltpu.PrefetchScalarGridSpec(
            num_scalar_prefetch=1, grid=(batch, n_chunks),
            in_specs=[q_spec, sel_chunk(lambda c: c), sel_chunk(lambda c: n_chunks),
                      per_b((NSA_KV_HEADS, nq, HEAD_DIM)),
                      per_b(ks_new.shape[1:]), per_b(vs_new.shape[1:]), per_b(wk_all.shape[1:]), per_b(wv_all.shape[1:]),
                      pl.BlockSpec((seq, KV_WIDTH), lambda b, c, pt: (b, C_GATE // KV_WIDTH))] + pages + pages,
            out_specs=pl.BlockSpec((seq, NSA_WIDTH), lambda b, c, pt: (b, 0)),
            scratch_shapes=[pltpu.VMEM((NSA_KV_HEADS, nq, 1), F32), pltpu.VMEM((NSA_KV_HEADS, nq, 1), F32),
                            pltpu.VMEM((NSA_KV_HEADS, nq, HEAD_DIM), F32)]),
        out_shape=jax.ShapeDtypeStruct((batch * seq, NSA_WIDTH), F32),
        compiler_params=_cparams(2),
        name="nsa_sample_slc",
    )(page_table, h, sel, sel, ocmp, ks_new, vs_new, wk_all, wv_all, h, *([slc_k] * nps), *([slc_v] * nps))


def _rope_tables(pos):
    half = HEAD_DIM // 2
    inv = ROPE_THETA ** (-jnp.arange(half, dtype=F32) / half)
    ang = pos.astype(F32)[:, None] * inv[None, :]
    cos, sin = jnp.cos(ang), jnp.sin(ang)
    return jnp.concatenate([cos, cos], -1), jnp.concatenate([-sin, sin], -1)


def _prep_weights(l, w_in, gm_ln_g, gm_ln_b, gm_ws, gm_bs, cmp_wk, cmp_wv, conv_dw, conv_db, conv_ln_g,
                  conv_ln_b, conv_pw, w_out, ln1_g, ln1_b, peer_wq, peer_subkeys, peer_u, peer_v, ln2_g, ln2_b):
    w = w_in[l]
    o = np.cumsum([0, GM_WIDTH, GM_WIDTH, NSA_WIDTH] + [KV_WIDTH] * 6 + [3 * NSA_HEADS, CONV_WIDTH, CONV_WIDTH])
    gu, gv, q, kc, vc, ks, vs, kw, vw, gate, ca, cb = [w[:, o[i]:o[i + 1]] for i in range(12)]
    zeros = lambda n: jnp.zeros((w.shape[0], n), w.dtype)
    w_proj = jnp.concatenate([gu, gv, ca, cb, q, kc, ks, kw, zeros(C_ROPE_END - C_KW - KV_WIDTH), vc, vs, vw, gate,
                              zeros(PROJ_WIDTH - C_GATE - 3 * NSA_HEADS)], axis=1).astype(BF16)
    row = lambda a: a.reshape(1, -1)
    cw = jnp.stack([cmp_wk[l], cmp_wv[l]], axis=1)
    zero = jnp.zeros_like(cw)
    cw = jnp.stack([jnp.concatenate([cw, zero], -1), jnp.concatenate([zero, cw], -1)], axis=2)
    cw = jnp.broadcast_to(cw[..., None], cw.shape + (HEAD_DIM,))
    dw = jnp.concatenate([conv_dw[l], jnp.zeros((CONV_HALO - CONV_K, CONV_WIDTH), F32)], axis=0)
    return dict(
        w_proj=w_proj, gm_g=row(gm_ln_g[l]), gm_b=row(gm_ln_b[l]), gm_ws=gm_ws[l], gm_bs=gm_bs[l], cmp_w=cw,
        conv_dw=dw, conv_db=row(conv_db[l]), conv_g=row(conv_ln_g[l]), conv_b=row(conv_ln_b[l]),
        conv_pw=conv_pw[l].astype(BF16), w_out=w_out[l].astype(BF16), ln1_g=row(ln1_g[l]), ln1_b=row(ln1_b[l]),
        wq=peer_wq[l].astype(BF16), subkeys=peer_subkeys[l].astype(BF16), layer=l,
        u=peer_u.astype(BF16).reshape(-1, D_MODEL), v=peer_v.astype(BF16).reshape(-1, D_MODEL),
        ln2_g=row(ln2_g[l]), ln2_b=row(ln2_b[l]))


def _gm_weights(p, chunk):
    w = jnp.tril(p["gm_ws"][:, :chunk, :chunk])
    bs = jnp.repeat(p["gm_bs"][:, :chunk].T, GM_WIDTH // GM_GROUPS, axis=1)
    return w, bs


def _peer(x, p, *, tm_proj, tm_route, tm_dense, te):
    m = x.shape[0]
    dummy = jnp.zeros((tm_proj, HEAD_DIM), F32)
    qp = _proj(x, p["wq"], dummy, dummy, tm=tm_proj, rope_lo=0, rope_hi=0, name="peer_q")
    r2, cnt, f, e2 = _route(qp, p["subkeys"], tm=tm_route)
    return _peer_dense(x, p["u"], p["v"], p["layer"], r2, cnt, f, e2, p["ln2_g"], p["ln2_b"], tm=tm_dense, te=te)


def _kv_rows(h, col, batch, seq):
    return h[:, col:col + KV_WIDTH].reshape(batch, seq, NSA_KV_HEADS, HEAD_DIM)


def _layer_prompt(x, p, cos, sin, *, batch, seq):
    m = batch * seq
    h = _proj(x, p["w_proj"], cos, sin, tm=1024, rope_lo=ROPE_LO, rope_hi=ROPE_HI, name="proj_in")
    gw, gbs = _gm_weights(p, GM_CHUNK)
    gm, vn = _gmlp(h, p["gm_g"], p["gm_b"], gw.astype(BF16), gbs, chunk=GM_CHUNK)
    nsa = _nsa_prompt(h, p["cmp_w"], batch=batch, seq=seq, tq=256)
    hist = jnp.zeros((batch, CONV_HALO, CONV_WIDTH), F32)
    y, c = _conv(h, hist, p["conv_dw"], p["conv_db"], p["conv_g"], p["conv_b"], p["conv_pw"],
                 batch=batch, seq=seq, tq=256)
    x1 = _out_ln(gm, nsa, y, x, p["w_out"], p["ln1_g"], p["ln1_b"], tm=256)
    x2 = _peer(x1, p, tm_proj=1024, tm_route=512, tm_dense=512, te=PEER_TE)
    keep = min(WINDOW, seq)
    kw, vw = _kv_rows(h, C_KW, batch, seq), _kv_rows(h, C_VW, batch, seq)
    start = ((seq - 1) // GM_CHUNK) * GM_CHUNK
    states = (_kv_rows(h, C_KC, batch, seq), _kv_rows(h, C_VC, batch, seq), _kv_rows(h, C_KS, batch, seq),
              _kv_rows(h, C_VS, batch, seq), kw[:, -keep:], vw[:, -keep:],
              c.reshape(batch, seq, CONV_WIDTH)[:, -(CONV_K - 1):], vn.reshape(batch, seq, GM_WIDTH)[:, start:])
    return x2, states


def _layer_sample(x, p, cos, sin, caches, layer, page_table, win_k, win_v, hist30, *, batch, seq, past_len):
    m = batch * seq
    h = _proj(x, p["w_proj"], cos, sin, tm=m, rope_lo=ROPE_LO, rope_hi=ROPE_HI, name="proj_in_s")
    pad_rows = lambda a: jnp.pad(a.reshape(batch, seq, -1), ((0, 0), (0, GM_CHUNK - seq), (0, 0))).reshape(batch * GM_CHUNK, -1)
    gw, gbs = _gm_weights(p, GM_CHUNK)
    hp = pad_rows(h[:, :2 * GM_WIDTH])
    gm, vn = _gmlp(hp, p["gm_g"], p["gm_b"], gw.astype(BF16), gbs, chunk=GM_CHUNK)
    unpad = lambda a: a.reshape(batch, GM_CHUNK, -1)[:, :seq].reshape(m, -1)
    gm, vn = unpad(gm), unpad(vn)
    kvs = [_kv_rows(h, col, batch, seq) for col in (C_KC, C_VC, C_KS, C_VS, C_KW, C_VW)]
    win_buf = win_k.shape[1]
    page = past_len // page_table.shape[1]
    new_rows = lambda col, rows: jnp.pad(h[:, col:col + KV_WIDTH].reshape(batch, seq, KV_WIDTH),
                                         ((0, 0), (0, rows - seq), (0, 0)))
    flat = lambda a: a.reshape(batch, win_buf, KV_WIDTH)
    wk_all = jnp.concatenate([flat(win_k), new_rows(C_KW, HEAD_DIM)], axis=1)
    wv_all = jnp.concatenate([flat(win_v), new_rows(C_VW, HEAD_DIM)], axis=1)
    nsa = _nsa_sample(h, caches, layer, page_table, new_rows(C_KS, page), new_rows(C_VS, page), wk_all, wv_all,
                      p["cmp_w"], batch=batch, seq=seq, past_len=past_len, win_buf=win_buf)
    keep = min(WINDOW, past_len + seq)
    unflat = lambda a: a[:, :win_buf + seq][:, -keep:].reshape(batch, keep, NSA_KV_HEADS, HEAD_DIM)
    win_k, win_v = unflat(wk_all), unflat(wv_all)
    hist = jnp.pad(hist30, ((0, 0), (CONV_HALO - (CONV_K - 1), 0), (0, 0)))
    y, c = _conv(h, hist, p["conv_dw"], p["conv_db"], p["conv_g"], p["conv_b"], p["conv_pw"],
                 batch=batch, seq=seq, tq=seq)
    x1 = _out_ln(gm, nsa, y, x, p["w_out"], p["ln1_g"], p["ln1_b"], tm=m)
    x2 = _peer(x1, p, tm_proj=m, tm_route=m, tm_dense=m, te=PEER_TE)
    conv_state = jnp.concatenate([hist30, c.reshape(batch, seq, CONV_WIDTH)], axis=1)[:, -(CONV_K - 1):]
    start = ((seq - 1) // GM_CHUNK) * GM_CHUNK
    states = (kvs[0], kvs[1], kvs[2], kvs[3], win_k, win_v, conv_state, vn.reshape(batch, seq, GM_WIDTH)[:, start:])
    return x2, states


def kernel(x_prompt, x_sample, cache_cmp_k, cache_cmp_v, cache_slc_k, cache_slc_v, cache_win_k, cache_win_v,
           state_conv, page_table, w_in, gm_ln_g, gm_ln_b, gm_ws, gm_bs, cmp_wk, cmp_wv, conv_dw, conv_db,
           conv_ln_g, conv_ln_b, conv_pw, w_out, ln1_g, ln1_b, peer_wq, peer_subkeys, peer_u, peer_v, ln2_g, ln2_b):
    bp, sp, _ = x_prompt.shape
    bs, ss, _ = x_sample.shape
    depth = w_in.shape[0]
    past_len = page_table.shape[1] * cache_cmp_k.shape[2]
    pos_p = jnp.arange(sp, dtype=jnp.int32)
    pos_s = past_len + jnp.arange(ss, dtype=jnp.int32)
    cos_p, sin_p = _rope_tables(pos_p)
    cos_s, sin_s = _rope_tables(jnp.tile(pos_s, bs))
    xp = x_prompt.reshape(bp * sp, D_MODEL)
    xs = x_sample.reshape(bs * ss, D_MODEL)
    p_states, s_states = [], []
    caches = [c.reshape(-1, HEAD_DIM) for c in (cache_cmp_k, cache_cmp_v, cache_slc_k, cache_slc_v)]
    for l in range(depth):
        p = _prep_weights(l, w_in, gm_ln_g, gm_ln_b, gm_ws, gm_bs, cmp_wk, cmp_wv, conv_dw, conv_db, conv_ln_g,
                          conv_ln_b, conv_pw, w_out, ln1_g, ln1_b, peer_wq, peer_subkeys, peer_u, peer_v, ln2_g, ln2_b)
        xp, sp_state = _layer_prompt(xp, p, cos_p, sin_p, batch=bp, seq=sp)
        xs, ss_state = _layer_sample(xs, p, cos_s, sin_s, caches, l, page_table, cache_win_k[l], cache_win_v[l],
                                     state_conv[l], batch=bs, seq=ss, past_len=past_len)
        p_states.append(sp_state)
        s_states.append(ss_state)
    stack = lambda states, i: jnp.stack([st[i] for st in states], axis=0)
    outs = [xp.reshape(bp, sp, D_MODEL), xs.reshape(bs, ss, D_MODEL)]
    outs += [stack(p_states, i) for i in range(8)]
    outs += [stack(s_states, i) for i in range(8)]
    return tuple(outs)
```

```python
import functools

import numpy as np
import jax
import jax.numpy as jnp
from jax import lax
from jax.experimental import pallas as pl
from jax.experimental.pallas import tpu as pltpu

F32 = jnp.float32
BF16 = jnp.bfloat16

D_MODEL = 2048
HEAD_DIM = 128
NSA_HEADS = 8
NSA_KV_HEADS = 2
GQA_R = NSA_HEADS // NSA_KV_HEADS
GM_WIDTH = 512
GM_GROUPS = 4
GM_CHUNK = 128
CONV_WIDTH = 512
CONV_GROUPS = 4
CONV_K = 31
NSA_WIDTH = NSA_HEADS * HEAD_DIM
KV_WIDTH = NSA_KV_HEADS * HEAD_DIM
CMP_BLOCK = 32
SEL_BLOCK = 64
N_SELECT = 16
WINDOW = 512
FORCED_SCORE = 1.0e4
PEER_HEADS = 8
N_KEYS = 128
PEER_TOPK = 16
SUBKEY_DIM = 128
ROPE_THETA = 10000.0
LN_EPS = 1e-5
DEPTH = 2
ALPHA = (2 * DEPTH) ** 0.25

C_GU, C_GV, C_CA, C_CB, C_Q = 0, 512, 1024, 1536, 2048
C_KC, C_KS, C_KW, C_ROPE_END = 3072, 3328, 3584, 4096
C_VC, C_VS, C_VW, C_GATE = 4096, 4352, 4608, 4864
PROJ_WIDTH = 5120
PROJ_TN = 512
ROPE_LO, ROPE_HI = C_Q // PROJ_TN, C_ROPE_END // PROJ_TN

V7X_VMEM_LIMIT = 56 * 1024 * 1024
NEG_INF = float("-inf")


def _cparams(n_axes, vmem=V7X_VMEM_LIMIT):
    return pltpu.CompilerParams(dimension_semantics=("arbitrary",) * n_axes, vmem_limit_bytes=vmem)


def _ln_lanes(z, g, b):
    mu = jnp.mean(z, axis=-1, keepdims=True)
    d = z - mu
    var = jnp.mean(d * d, axis=-1, keepdims=True)
    return d * lax.rsqrt(var + LN_EPS) * g + b


def _dot_nt(a, b):
    return lax.dot_general(a, b, (((1,), (1,)), ((), ())), preferred_element_type=F32)


def _proj_kernel(x_ref, w_ref, cos_ref, sin_ref, o_ref, *rest, rope_lo, rope_hi, tn, kv_cols):
    kv_refs, xb_ref = rest[:-1], rest[-1]
    j = pl.program_id(1)
    tm = x_ref.shape[0]

    @pl.when(j == 0)
    def _():
        xb_ref[...] = x_ref[...].astype(BF16)

    acc = jnp.dot(xb_ref[...], w_ref[...], preferred_element_type=F32)
    is_rope = jnp.logical_and(j >= rope_lo, j < rope_hi)

    def emit(blocks):
        for c, blk in enumerate(blocks):
            o_ref[:, c * HEAD_DIM:(c + 1) * HEAD_DIM] = blk
        for ref, col in zip(kv_refs, kv_cols):
            @pl.when(j == col // tn)
            def _():
                first = (col % tn) // HEAD_DIM
                for g in range(NSA_KV_HEADS):
                    ref[pl.ds(g, tm, stride=NSA_KV_HEADS), :] = blocks[first + g]

    @pl.when(is_rope)
    def _():
        cos, sin = cos_ref[...], sin_ref[...]
        blocks = [acc[:, c * HEAD_DIM:(c + 1) * HEAD_DIM] for c in range(tn // HEAD_DIM)]
        emit([blk * cos + pltpu.roll(blk, HEAD_DIM // 2, 1) * sin for blk in blocks])

    @pl.when(jnp.logical_not(is_rope))
    def _():
        emit([acc[:, c * HEAD_DIM:(c + 1) * HEAD_DIM] for c in range(tn // HEAD_DIM)])


def _proj(x, w, cos, sin, *, tm, rope_lo, rope_hi, name, kv_cols=()):
    m, k = x.shape
    n = w.shape[1]
    tn = PROJ_TN
    nper = cos.shape[0] // tm
    kern = functools.partial(_proj_kernel, rope_lo=rope_lo, rope_hi=rope_hi, tn=tn, kv_cols=tuple(kv_cols))
    kv_spec = pl.BlockSpec((NSA_KV_HEADS * tm, HEAD_DIM), lambda i, j: (i, 0))
    kv_shape = jax.ShapeDtypeStruct((NSA_KV_HEADS * m, HEAD_DIM), F32)
    outs = pl.pallas_call(
        kern,
        grid=(m // tm, n // tn),
        in_specs=[
            pl.BlockSpec((tm, k), lambda i, j: (i, 0)),
            pl.BlockSpec((k, tn), lambda i, j: (0, j)),
            pl.BlockSpec((tm, HEAD_DIM), lambda i, j: (i % nper, 0)),
            pl.BlockSpec((tm, HEAD_DIM), lambda i, j: (i % nper, 0)),
        ],
        out_specs=[pl.BlockSpec((tm, tn), lambda i, j: (i, j))] + [kv_spec] * len(kv_cols),
        out_shape=[jax.ShapeDtypeStruct((m, n), F32)] + [kv_shape] * len(kv_cols),
        scratch_shapes=[pltpu.VMEM((tm, k), BF16)],
        compiler_params=_cparams(2),
        name=name,
    )(x, w, cos, sin)
    return outs if kv_cols else outs[0]


def _gmlp_kernel(gu_ref, gv_ref, g_ref, b_ref, w_ref, bs_ref, o_ref, vn_ref):
    gw = GM_WIDTH // GM_GROUPS
    for h in range(GM_GROUPS):
        sl = slice(h * gw, (h + 1) * gw)
        v = jax.nn.gelu(gv_ref[:, sl])
        vn = _ln_lanes(v, g_ref[:, sl], b_ref[:, sl])
        vn_ref[:, sl] = vn
        s = jnp.dot(w_ref[h], vn.astype(BF16), preferred_element_type=F32) + bs_ref[:, sl]
        o_ref[:, sl] = jax.nn.gelu(gu_ref[:, sl]) * s


def _gmlp(h, g, b, w_tril, bs_full, *, chunk):
    m = h.shape[0]
    blk = lambda c: pl.BlockSpec((chunk, GM_WIDTH), lambda i, c=c: (i, c))
    full = lambda a: pl.BlockSpec(a.shape, lambda i: (0,) * a.ndim)
    return pl.pallas_call(
        _gmlp_kernel,
        grid=(m // chunk,),
        in_specs=[blk(C_GU // GM_WIDTH), blk(C_GV // GM_WIDTH), full(g), full(b), full(w_tril), full(bs_full)],
        out_specs=[pl.BlockSpec((chunk, GM_WIDTH), lambda i: (i, 0))] * 2,
        out_shape=[jax.ShapeDtypeStruct((m, GM_WIDTH), F32)] * 2,
        compiler_params=_cparams(1),
        name="gmlp",
    )(h, h, g, b, w_tril, bs_full)


CONV_HALO = 32


def _conv_kernel(ca_ref, cb_ref, cah_ref, cbh_ref, hist_ref, dw_ref, db_ref, g_ref, b_ref, pw_ref,
                 y_ref, c_ref, buf_ref, *, tq):
    i = pl.program_id(1)
    c = ca_ref[...] * jax.nn.sigmoid(cb_ref[...])
    c_ref[...] = c
    halo = cah_ref[...] * jax.nn.sigmoid(cbh_ref[...])
    halo = jnp.where(i == 0, hist_ref[0], halo)
    buf_ref[0:CONV_HALO, :] = halo
    buf_ref[CONV_HALO:CONV_HALO + tq, :] = c
    lead = CONV_HALO - (CONV_K - 1)
    acc = jnp.zeros((tq, CONV_WIDTH), F32)
    for k in range(CONV_K):
        acc = acc + dw_ref[k:k + 1, :] * buf_ref[lead + k:lead + k + tq, :]
    y = acc + db_ref[...]
    cw = CONV_WIDTH // CONV_GROUPS
    parts = []
    for gi in range(CONV_GROUPS):
        sl = slice(gi * cw, (gi + 1) * cw)
        yn = _ln_lanes(y[:, sl], g_ref[:, sl], b_ref[:, sl])
        parts.append((yn * jax.nn.sigmoid(yn)).astype(BF16))
    y_ref[...] = jnp.dot(jnp.concatenate(parts, axis=-1), pw_ref[...], preferred_element_type=F32)


def _conv(h, hist, dw, db, g, b, pw, *, batch, seq, tq):
    m = h.shape[0]
    nt = seq // tq
    row = lambda bi, i: bi * nt + i
    halo_row = lambda bi, i: jnp.maximum((bi * seq + i * tq) // CONV_HALO - 1, 0)
    full = lambda a: pl.BlockSpec(a.shape, lambda bi, i: (0,) * a.ndim)
    kern = functools.partial(_conv_kernel, tq=tq)
    return pl.pallas_call(
        kern,
        grid=(batch, nt),
        in_specs=[
            pl.BlockSpec((tq, CONV_WIDTH), lambda bi, i: (row(bi, i), C_CA // CONV_WIDTH)),
            pl.BlockSpec((tq, CONV_WIDTH), lambda bi, i: (row(bi, i), C_CB // CONV_WIDTH)),
            pl.BlockSpec((CONV_HALO, CONV_WIDTH), lambda bi, i: (halo_row(bi, i), C_CA // CONV_WIDTH)),
            pl.BlockSpec((CONV_HALO, CONV_WIDTH), lambda bi, i: (halo_row(bi, i), C_CB // CONV_WIDTH)),
            pl.BlockSpec((1, CONV_HALO, CONV_WIDTH), lambda bi, i: (bi, 0, 0)),
            full(dw), full(db), full(g), full(b), full(pw),
        ],
        out_specs=[pl.BlockSpec((tq, CONV_WIDTH), lambda bi, i: (row(bi, i), 0))] * 2,
        out_shape=[jax.ShapeDtypeStruct((m, CONV_WIDTH), F32)] * 2,
        scratch_shapes=[pltpu.VMEM((CONV_HALO + tq, CONV_WIDTH), F32)],
        compiler_params=_cparams(2),
        name="conv",
    )(h, h, h, h, hist, dw, db, g, b, pw)


def _masked_softmax(s, mask):
    s = jnp.where(mask, s, NEG_INF)
    m = jnp.max(s, axis=-1, keepdims=True)
    m = jnp.where(m > NEG_INF, m, 0.0)
    e = jnp.exp(s - m)
    return e / jnp.maximum(jnp.sum(e, axis=-1, keepdims=True), 1e-30)


def _select_blocks(imp, tpos, n_blocks):
    tq, lanes = imp.shape
    blk = lax.broadcasted_iota(jnp.int32, (tq, lanes), 1)
    cur = tpos // SEL_BLOCK
    valid = blk * SEL_BLOCK <= tpos
    forced = jnp.logical_or(blk == 0, jnp.logical_or(blk == cur, blk == cur - 1))
    score = jnp.where(valid, jnp.where(forced, FORCED_SCORE, imp), NEG_INF)
    beaten = jnp.zeros((tq, lanes), F32)
    for i in range(n_blocks):
        ci = score[:, i:i + 1]
        earlier = (blk > i).astype(F32)
        beaten = beaten + jnp.where(ci > score, 1.0, jnp.where(ci == score, earlier, 0.0))
    picked = jnp.logical_and(beaten < float(min(N_SELECT, n_blocks)), blk < n_blocks)
    return picked.astype(F32)


SLC_CHUNK = 512


def _online_softmax_step(m_ref, l_ref, acc_ref, r, s, mask, v):
    s = jnp.where(mask, s, NEG_INF)
    m_old = m_ref[r]
    m_new = jnp.maximum(m_old, jnp.max(s, axis=-1, keepdims=True))
    m_safe = jnp.where(m_new > NEG_INF, m_new, 0.0)
    alpha = jnp.exp(m_old - m_safe)
    e = jnp.exp(s - m_safe)
    l_ref[r] = alpha * l_ref[r] + jnp.sum(e, axis=-1, keepdims=True)
    acc_ref[r] = alpha * acc_ref[r] + jnp.dot(e.astype(BF16), v, preferred_element_type=F32)
    m_ref[r] = m_new


VT_CHUNK = 256


def _select_blocks_t(imp_t, tpos_row, n_blocks):
    nb, tq = imp_t.shape
    blk = lax.broadcasted_iota(jnp.int32, (nb, tq), 0)
    cur = tpos_row // SEL_BLOCK
    valid = blk * SEL_BLOCK <= tpos_row
    forced = jnp.logical_or(blk == 0, jnp.logical_or(blk == cur, blk == cur - 1))
    score = jnp.where(valid, jnp.where(forced, FORCED_SCORE, imp_t), NEG_INF)
    beaten = jnp.zeros((nb, tq), F32)
    for i in range(n_blocks):
        ci = score[i:i + 1, :]
        earlier = (blk > i).astype(F32)
        beaten = beaten + jnp.where(ci > score, 1.0, jnp.where(ci == score, earlier, 0.0))
    return (beaten < float(min(N_SELECT, n_blocks))).astype(F32)


def _col_softmax(s_t, mask_t):
    s = jnp.where(mask_t, s_t, NEG_INF)
    m = jnp.max(s, axis=0, keepdims=True)
    m = jnp.where(m > NEG_INF, m, 0.0)
    e = jnp.exp(s - m)
    return e, jnp.sum(e, axis=0, keepdims=True)


def _col_softmax_step(m_ref, l_ref, acc_ref, r, s_t, mask_t, v_t):
    s = jnp.where(mask_t, s_t, NEG_INF)
    m_old = m_ref[r]
    m_new = jnp.maximum(m_old, jnp.max(s, axis=0, keepdims=True))
    m_safe = jnp.where(m_new > NEG_INF, m_new, 0.0)
    alpha = jnp.exp(m_old - m_safe)
    e = jnp.exp(s - m_safe)
    l_ref[r] = alpha * l_ref[r] + jnp.sum(e, axis=0, keepdims=True)
    acc_ref[r] = alpha * acc_ref[r] + jnp.dot(v_t, e.astype(BF16), preferred_element_type=F32)
    m_ref[r] = m_new


def _nsa_prompt_kernel(q_ref, kc_ref, ks_ref, kw_ref, vc_ref, vs_ref, vw_ref, gate_ref, cw_ref, o_ref,
                       kb_ref, vt_ref, kcmp_ref, vcmpt_ref, m_ref, l_ref, acc_ref, *, tq, seq):
    qi = pl.program_id(1)
    t0 = qi * tq
    scale = HEAD_DIM ** -0.5
    n_slc = seq // SEL_BLOCK
    tpos = t0 + lax.broadcasted_iota(jnp.int32, (1, tq), 1)
    row_c = lax.broadcasted_iota(jnp.int32, (2 * n_slc, tq), 0)
    cmp_last = jnp.where(row_c < n_slc, row_c * SEL_BLOCK + CMP_BLOCK - 1, (row_c - n_slc) * SEL_BLOCK + SEL_BLOCK - 1)
    cmp_mask = cmp_last <= tpos
    win_len = tq + WINDOW
    w0 = pl.multiple_of(jnp.maximum(t0 - WINDOW, 0), tq)
    wpos = w0 + lax.broadcasted_iota(jnp.int32, (win_len, tq), 0)
    win_mask = jnp.logical_and(wpos <= tpos, wpos >= tpos - WINDOW)
    gates_t = jax.nn.sigmoid(gate_ref[:, 0:HEAD_DIM]).T

    @pl.when(qi == 0)
    def _():
        kb_ref[0] = ks_ref[0].astype(BF16)
        kb_ref[1] = kw_ref[0].astype(BF16)
        for g in range(NSA_KV_HEADS):
            gs = slice(g * HEAD_DIM, (g + 1) * HEAD_DIM)
            for n, ref in enumerate((vs_ref, vw_ref)):
                for c in range(seq // VT_CHUNK):
                    vt_ref[n, g, c] = ref[0, c * VT_CHUNK:(c + 1) * VT_CHUNK, gs].T.astype(BF16)
            halves = []
            for which, ref in enumerate((kc_ref, vc_ref)):
                x3 = ref[0, :, gs].reshape(n_slc, SEL_BLOCK, HEAD_DIM)
                ev = jnp.sum(x3 * cw_ref[g, which, 0][None], axis=1)
                od = jnp.sum(x3 * cw_ref[g, which, 1][None], axis=1)
                halves.append(jnp.concatenate([ev, od], axis=0))
            kcmp_ref[g] = halves[0].astype(BF16)
            vcmpt_ref[g] = jnp.concatenate([halves[1], jnp.zeros_like(halves[1])], axis=0).T.astype(BF16)

    for g in range(NSA_KV_HEADS):
        gs = slice(g * HEAD_DIM, (g + 1) * HEAD_DIM)
        kcmp, vcmp_t = kcmp_ref[g], vcmpt_ref[g][:, :2 * n_slc]
        qg = jnp.concatenate([(q_ref[:, (g * GQA_R + r) * HEAD_DIM:(g * GQA_R + r + 1) * HEAD_DIM] * scale).astype(BF16)
                              for r in range(GQA_R)], axis=0)
        heads = lambda a: jnp.concatenate([a] * GQA_R, axis=1)
        e, l = _col_softmax(_dot_nt(kcmp, qg), heads(cmp_mask))
        p = e / jnp.maximum(l, 1e-30)
        o_cmp = jnp.dot(vcmp_t, p.astype(BF16), preferred_element_type=F32)
        imp = p[:, 0:tq]
        for r in range(1, GQA_R):
            imp = imp + p[:, r * tq:(r + 1) * tq]
        sel = _select_blocks_t(imp[:n_slc] + imp[n_slc:], tpos, n_slc).astype(BF16)

        m_ref[...] = jnp.full(m_ref.shape, NEG_INF, F32)
        l_ref[...] = jnp.zeros_like(l_ref)
        acc_ref[...] = jnp.zeros_like(acc_ref)
        for j in range(seq // SLC_CHUNK):
            @pl.when(j * SLC_CHUNK < t0 + tq)
            def _():
                kpos = j * SLC_CHUNK + lax.broadcasted_iota(jnp.int32, (SLC_CHUNK, tq), 0)
                key_blk = (j * SLC_CHUNK + lax.broadcasted_iota(jnp.int32, (SLC_CHUNK, n_slc), 0)) // SEL_BLOCK
                expand = (lax.broadcasted_iota(jnp.int32, (SLC_CHUNK, n_slc), 1) == key_blk).astype(BF16)
                sel_keys = jnp.dot(expand, sel, preferred_element_type=F32)
                mask = jnp.logical_and(sel_keys > 0.5, kpos <= tpos)
                k = kb_ref[0, j * SLC_CHUNK:(j + 1) * SLC_CHUNK, gs]
                per = SLC_CHUNK // VT_CHUNK
                v_t = jnp.concatenate([vt_ref[0, g, j * per + c] for c in range(per)], axis=1)
                _col_softmax_step(m_ref, l_ref, acc_ref, 0, _dot_nt(k, qg), heads(mask), v_t)

        kwin = kb_ref[1, pl.ds(w0, win_len), gs]
        wc0 = w0 // VT_CHUNK
        vwin_t = jnp.concatenate([vt_ref[1, g, wc0 + c] for c in range(win_len // VT_CHUNK)], axis=1)
        o_slc = acc_ref[0] / jnp.maximum(l_ref[0], 1e-30)
        e, l = _col_softmax(_dot_nt(kwin, qg), heads(win_mask))
        o_win = jnp.dot(vwin_t, e.astype(BF16), preferred_element_type=F32) / jnp.maximum(l, 1e-30)
        for r in range(GQA_R):
            hh = g * GQA_R + r
            cs = slice(r * tq, (r + 1) * tq)
            o_t = (gates_t[3 * hh:3 * hh + 1] * o_cmp[:, cs] + gates_t[3 * hh + 1:3 * hh + 2] * o_slc[:, cs]
                   + gates_t[3 * hh + 2:3 * hh + 3] * o_win[:, cs])
            o_ref[:, hh * HEAD_DIM:(hh + 1) * HEAD_DIM] = o_t.T


def _nsa_prompt(h, cmp_w, *, batch, seq, tq):
    m = h.shape[0]
    nt = seq // tq
    kv = lambda col: pl.BlockSpec((1, seq, KV_WIDTH), lambda bi, i, col=col: (bi, 0, col // KV_WIDTH))
    h3 = h.reshape(batch, seq, PROJ_WIDTH)
    kern = functools.partial(_nsa_prompt_kernel, tq=tq, seq=seq)
    return pl.pallas_call(
        kern,
        grid=(batch, nt),
        in_specs=[
            pl.BlockSpec((tq, NSA_WIDTH), lambda bi, i: (bi * nt + i, C_Q // NSA_WIDTH)),
            kv(C_KC), kv(C_KS), kv(C_KW), kv(C_VC), kv(C_VS), kv(C_VW),
            pl.BlockSpec((tq, KV_WIDTH), lambda bi, i: (bi * nt + i, C_GATE // KV_WIDTH)),
            pl.BlockSpec(cmp_w.shape, lambda bi, i: (0,) * cmp_w.ndim),
        ],
        out_specs=pl.BlockSpec((tq, NSA_WIDTH), lambda bi, i: (bi * nt + i, 0)),
        out_shape=jax.ShapeDtypeStruct((m, NSA_WIDTH), F32),
        scratch_shapes=[
            pltpu.VMEM((2, seq, KV_WIDTH), BF16),
            pltpu.VMEM((2, NSA_KV_HEADS, seq // VT_CHUNK, HEAD_DIM, VT_CHUNK), BF16),
            pltpu.VMEM((NSA_KV_HEADS, 2 * (seq // SEL_BLOCK), HEAD_DIM), BF16),
            pltpu.VMEM((NSA_KV_HEADS, HEAD_DIM, 4 * (seq // SEL_BLOCK)), BF16),
            pltpu.VMEM((1, 1, GQA_R * tq), F32), pltpu.VMEM((1, 1, GQA_R * tq), F32),
            pltpu.VMEM((1, HEAD_DIM, GQA_R * tq), F32),
        ],
        compiler_params=_cparams(2),
        name="nsa_prompt",
    )(h, h3, h3, h3, h3, h3, h3, h, cmp_w)


def _out_ln_kernel(gm_ref, nsa_ref, y_ref, x_ref, w_ref, g_ref, b_ref, o_ref):
    cat = jnp.concatenate([gm_ref[...].astype(BF16), nsa_ref[...].astype(BF16), y_ref[...].astype(BF16)], axis=-1)
    mix = jnp.dot(cat, w_ref[...], preferred_element_type=F32)
    o_ref[...] = _ln_lanes(ALPHA * x_ref[...] + mix, g_ref[...], b_ref[...])


def _out_ln(gm, nsa, y, x, w, g, b, *, tm):
    m = x.shape[0]
    row = lambda width: pl.BlockSpec((tm, width), lambda i: (i, 0))
    full = lambda a: pl.BlockSpec(a.shape, lambda i: (0,) * a.ndim)
    return pl.pallas_call(
        _out_ln_kernel,
        grid=(m // tm,),
        in_specs=[row(GM_WIDTH), row(NSA_WIDTH), row(CONV_WIDTH), row(D_MODEL), full(w), full(g), full(b)],
        out_specs=row(D_MODEL),
        out_shape=jax.ShapeDtypeStruct((m, D_MODEL), F32),
        compiler_params=_cparams(1),
        name="out_ln",
    )(gm, nsa, y, x, w, g, b)


def _extract_topk(s, exact_ties, want_rank):
    n_rows, tm = s.shape
    rows = lax.broadcasted_iota(jnp.int32, (n_rows, tm), 0)
    krow = lax.broadcasted_iota(jnp.int32, (PEER_TOPK, tm), 0)
    work = s
    rank = jnp.full((n_rows, tm), float(PEER_TOPK), F32) if want_rank else None
    vals = jnp.zeros((PEER_TOPK, tm), F32)
    for r in range(PEER_TOPK):
        mx = jnp.max(work, axis=0, keepdims=True)
        hit = work == mx
        if exact_ties:
            hit = rows == jnp.min(jnp.where(hit, rows, n_rows), axis=0, keepdims=True)
        if want_rank:
            rank = jnp.where(hit, float(r), rank)
        vals = jnp.where(krow == r, mx, vals)
        work = jnp.where(hit, NEG_INF, work)
    taken = jnp.sum((work == NEG_INF).astype(F32), axis=0, keepdims=True)
    return vals, rank, taken


def _pair_topk(v1, v2):
    tm = v1.shape[1]
    krow = lax.broadcasted_iota(jnp.int32, (PEER_TOPK, tm), 0)
    cnt = jnp.zeros((PEER_TOPK, tm), F32)
    front = v1 + v2[0:1]
    m0 = v1[0:1] + v2[0:1]
    z = jnp.zeros((1, tm), F32)
    for _ in range(PEER_TOPK):
        mx = jnp.max(front, axis=0, keepdims=True)
        idx = jnp.min(jnp.where(front == mx, krow, PEER_TOPK), axis=0, keepdims=True)
        hit = krow == idx
        cnt = cnt + hit.astype(F32)
        z = z + jnp.exp(mx - m0)
        chosen = jnp.sum(jnp.where(hit, cnt, 0.0), axis=0, keepdims=True)
        nxt = jnp.full((1, tm), NEG_INF, F32)
        for c in range(1, PEER_TOPK):
            nxt = jnp.where(chosen == float(c), v2[c:c + 1], nxt)
        front = jnp.where(hit, v1 + nxt, front)
    return cnt, z


def _route_kernel(qp_ref, sk_ref, r2_ref, cnt_ref, f_ref, e2_ref):
    q1 = qp_ref[:, 0:SUBKEY_DIM].astype(BF16)
    q2 = qp_ref[:, SUBKEY_DIM:2 * SUBKEY_DIM].astype(BF16)
    s1 = _dot_nt(sk_ref[0, 0], q1)
    s2 = _dot_nt(sk_ref[0, 1], q2)

    def emit(v1, v2, cnt, z, cnt_a, rank2):
        r2_ref[0] = rank2.astype(BF16)
        cnt_ref[0] = cnt_a
        f_ref[0] = jnp.exp(s1 - v1[0:1]) / z
        e2_ref[0] = jnp.exp(s2 - v2[0:1]).astype(BF16)

    v1, _, taken1 = _extract_topk(s1, exact_ties=False, want_rank=False)
    v2, rank2, taken2 = _extract_topk(s2, exact_ties=False, want_rank=True)
    cnt, z = _pair_topk(v1, v2)
    cnt_a = jnp.zeros(s1.shape, F32)
    for r in range(PEER_TOPK):
        cnt_a = jnp.where(s1 == v1[r:r + 1], cnt[r:r + 1], cnt_a)
    emit(v1, v2, cnt, z, cnt_a, rank2)

    @pl.when(jnp.max(jnp.maximum(taken1, taken2)) > float(PEER_TOPK))
    def _():
        v1, rank1, _ = _extract_topk(s1, exact_ties=True, want_rank=True)
        v2, rank2, _ = _extract_topk(s2, exact_ties=True, want_rank=True)
        cnt, z = _pair_topk(v1, v2)
        cnt_a = jnp.zeros(s1.shape, F32)
        for r in range(PEER_TOPK):
            cnt_a = jnp.where(rank1 == float(r), cnt[r:r + 1], cnt_a)
        emit(v1, v2, cnt, z, cnt_a, rank2)


def _route(qp, subkeys, *, tm):
    m = qp.shape[0]
    out = lambda dt: jax.ShapeDtypeStruct((PEER_HEADS, N_KEYS, m), dt)
    ospec = pl.BlockSpec((1, N_KEYS, tm), lambda i, hd: (hd, 0, i))
    return pl.pallas_call(
        _route_kernel,
        grid=(m // tm, PEER_HEADS),
        in_specs=[
            pl.BlockSpec((tm, 2 * SUBKEY_DIM), lambda i, hd: (i, hd)),
            pl.BlockSpec((1, 2, N_KEYS, SUBKEY_DIM), lambda i, hd: (hd, 0, 0, 0)),
        ],
        out_specs=[ospec] * 4,
        out_shape=[out(BF16), out(F32), out(F32), out(BF16)],
        compiler_params=_cparams(2),
        name="peer_route",
    )(qp, subkeys)


PEER_SUB = 256
PEER_TE = 1024


def _peer_dense_kernel(x_ref, u_ref, v_ref, r2_ref, cnt_ref, f_ref, e2_ref, g_ref, b_ref, o_ref, xb_ref, *, te):
    j = pl.program_id(1)

    @pl.when(j == 0)
    def _():
        xb_ref[...] = x_ref[...].astype(BF16)
        o_ref[...] = jnp.zeros_like(o_ref)

    cols = []
    for c in range(te // PEER_SUB):
        act = jax.nn.gelu(_dot_nt(u_ref[c * PEER_SUB:(c + 1) * PEER_SUB, :], xb_ref[...]).astype(BF16))
        rows = []
        for al in range(PEER_SUB // N_KEYS):
            a = c * (PEER_SUB // N_KEYS) + al
            gate = None
            for hd in range(PEER_HEADS):
                cnt = cnt_ref[hd, a:a + 1, :].astype(BF16)
                f = f_ref[hd, a:a + 1, :].astype(BF16)
                term = jnp.where(r2_ref[hd] < cnt, e2_ref[hd], jnp.zeros((), BF16)) * f
                gate = term if gate is None else gate + term
            rows.append(act[al * N_KEYS:(al + 1) * N_KEYS] * gate)
        cols.append(jnp.concatenate(rows, axis=0).T)
    o_ref[...] += jnp.dot(jnp.concatenate(cols, axis=1), v_ref[...], preferred_element_type=F32)

    @pl.when(j == pl.num_programs(1) - 1)
    def _():
        o_ref[...] = _ln_lanes(ALPHA * x_ref[...] + o_ref[...], g_ref[...], b_ref[...])


def _peer_dense(x, u, v, layer, r2, cnt, f, e2, g, b, *, tm, te):
    m = x.shape[0]
    n_exp = u.shape[0] // DEPTH
    first = layer * (n_exp // te)
    by_b = pl.BlockSpec((PEER_HEADS, N_KEYS, tm), lambda i, j: (0, 0, i))
    by_a = pl.BlockSpec((PEER_HEADS, te // N_KEYS, tm), lambda i, j: (0, j, i))
    full = lambda a: pl.BlockSpec(a.shape, lambda i, j: (0,) * a.ndim)
    kern = functools.partial(_peer_dense_kernel, te=te)
    return pl.pallas_call(
        kern,
        grid=(m // tm, n_exp // te),
        in_specs=[
            pl.BlockSpec((tm, D_MODEL), lambda i, j: (i, 0), pipeline_mode=pl.Buffered(1)),
            pl.BlockSpec((te, D_MODEL), lambda i, j: (first + j, 0)),
            pl.BlockSpec((te, D_MODEL), lambda i, j: (first + j, 0)),
            by_b, by_a, by_a, by_b, full(g), full(b),
        ],
        out_specs=pl.BlockSpec((tm, D_MODEL), lambda i, j: (i, 0), pipeline_mode=pl.Buffered(1)),
        out_shape=jax.ShapeDtypeStruct((m, D_MODEL), F32),
        scratch_shapes=[pltpu.VMEM((tm, D_MODEL), BF16)],
        compiler_params=_cparams(2),
        name="peer_dense",
    )(x, u, v, r2, cnt, f, e2, g, b)


PAGES_PER_STEP = 16


def _group_queries(q_ref, g):
    heads = [q_ref[:, (g * GQA_R + r) * HEAD_DIM:(g * GQA_R + r + 1) * HEAD_DIM] for r in range(GQA_R)]
    return jnp.concatenate(heads, axis=0).astype(BF16)


def _page_rows(pages, g):
    n = pages[0].shape[0] // NSA_KV_HEADS
    return jnp.concatenate([r[pl.ds(g, n, stride=NSA_KV_HEADS), :] for r in pages], axis=0)


def _nsa_s_cmp_kernel(pt_ref, q_ref, cw_ref, *refs, seq, past_len, n_real, sel_lanes):
    nps = PAGES_PER_STEP
    k_pages, v_pages = refs[:nps], refs[nps:2 * nps]
    ocmp_ref, sel_ref, ke_ref, ko_ref, ve_ref, vo_ref = refs[2 * nps:]
    c = pl.program_id(1)
    n_pb = ke_ref.shape[0]
    rows = nps * (k_pages[0].shape[0] // NSA_KV_HEADS) // SEL_BLOCK
    off = pl.multiple_of(c * rows, rows)

    def compress(pages, which, even_ref, odd_ref):
        for g in range(NSA_KV_HEADS):
            gs = slice(g * HEAD_DIM, (g + 1) * HEAD_DIM)
            x3 = _page_rows(pages, g).reshape(rows, SEL_BLOCK, HEAD_DIM)
            even_ref[pl.ds(off, rows), gs] = jnp.sum(x3 * cw_ref[g, which, 0][None], axis=1)
            odd_ref[pl.ds(off, rows), gs] = jnp.sum(x3 * cw_ref[g, which, 1][None], axis=1)

    compress(k_pages, 0, ke_ref, ko_ref)
    compress(v_pages, 1, ve_ref, vo_ref)

    @pl.when(c == pl.num_programs(1) - 1)
    def _():
        scale = HEAD_DIM ** -0.5
        nq = GQA_R * seq
        qrow = lax.broadcasted_iota(jnp.int32, (nq, 1), 0)
        qpos = past_len + qrow % seq
        lane = lax.broadcasted_iota(jnp.int32, (nq, 2 * n_pb), 1)
        cmp_last = jnp.where(lane < n_pb, lane * SEL_BLOCK + CMP_BLOCK - 1, (lane - n_pb) * SEL_BLOCK + SEL_BLOCK - 1)
        cmp_mask = cmp_last <= qpos
        tpos = past_len + lax.broadcasted_iota(jnp.int32, (seq, 1), 0)
        for g in range(NSA_KV_HEADS):
            gs = slice(g * HEAD_DIM, (g + 1) * HEAD_DIM)
            kcat = jnp.concatenate([ke_ref[:, gs], ko_ref[:, gs]], axis=0).astype(BF16)
            vcat = jnp.concatenate([ve_ref[:, gs], vo_ref[:, gs]], axis=0).astype(BF16)
            p = _masked_softmax(_dot_nt(_group_queries(q_ref, g), kcat) * scale, cmp_mask)
            ocmp_ref[0, g] = jnp.dot(p.astype(BF16), vcat, preferred_element_type=F32)
            imp = p[0:seq]
            for r in range(1, GQA_R):
                imp = imp + p[r * seq:(r + 1) * seq]
            imp = imp[:, :n_pb] + imp[:, n_pb:]
            imp = jnp.concatenate([imp, jnp.zeros((seq, sel_lanes - n_pb), F32)], axis=1)
            sel = _select_blocks(imp, tpos, n_real)
            bpc = sel_ref.shape[-1]
            for ch in range(sel_ref.shape[2]):
                sel_ref[0, g, ch] = sel[:, ch * bpc:(ch + 1) * bpc]


def _nsa_s_slc_kernel(pt_ref, q_ref, sel_ref, sel_new_ref, ocmp_ref, ksn_ref, vsn_ref, wk_ref, wv_ref, gate_ref, *refs,
                      seq, past_len, win_buf):
    nps = PAGES_PER_STEP
    k_pages, v_pages = refs[:nps], refs[nps:2 * nps]
    o_ref, m_ref, l_ref, acc_ref = refs[2 * nps:]
    c = pl.program_id(1)
    scale = HEAD_DIM ** -0.5
    nq = GQA_R * seq
    chunk = nps * (k_pages[0].shape[0] // NSA_KV_HEADS)
    qrow = lax.broadcasted_iota(jnp.int32, (nq, 1), 0)
    qpos = past_len + qrow % seq

    @pl.when(c == 0)
    def _():
        m_ref[...] = jnp.full(m_ref.shape, NEG_INF, F32)
        l_ref[...] = jnp.zeros_like(l_ref)
        acc_ref[...] = jnp.zeros_like(acc_ref)

    def online_update(g, s, mask, v):
        _online_softmax_step(m_ref, l_ref, acc_ref, g, s, mask, v)

    def key_mask(sel_blocks, first_pos, n_keys):
        bpc = sel_blocks.shape[1]
        key_blk = lax.broadcasted_iota(jnp.int32, (bpc, n_keys), 1) // SEL_BLOCK
        expand = (lax.broadcasted_iota(jnp.int32, (bpc, n_keys), 0) == key_blk).astype(BF16)
        sel_keys = jnp.dot(sel_blocks.astype(BF16), expand, preferred_element_type=F32)
        sel_keys = jnp.concatenate([sel_keys] * GQA_R, axis=0)
        kpos = first_pos + lax.broadcasted_iota(jnp.int32, (nq, n_keys), 1)
        return jnp.logical_and(sel_keys > 0.5, kpos <= qpos)

    for g in range(NSA_KV_HEADS):
        mask = key_mask(sel_ref[0, g, 0], c * chunk, chunk)
        scores = _dot_nt(_group_queries(q_ref, g), _page_rows(k_pages, g).astype(BF16)) * scale
        online_update(g, scores, mask, _page_rows(v_pages, g).astype(BF16))

    @pl.when(c == pl.num_programs(1) - 1)
    def _():
        gates = jax.nn.sigmoid(gate_ref[:, 0:HEAD_DIM])
        n_new = ksn_ref.shape[1]
        n_win = wk_ref.shape[1]
        wpos = past_len - win_buf + lax.broadcasted_iota(jnp.int32, (nq, n_win), 1)
        win_mask = jnp.logical_and(wpos <= qpos, wpos >= qpos - WINDOW)
        for g in range(NSA_KV_HEADS):
            gs = slice(g * HEAD_DIM, (g + 1) * HEAD_DIM)
            qg = _group_queries(q_ref, g)
            online_update(g, _dot_nt(qg, ksn_ref[0, :, gs].astype(BF16)) * scale,
                          key_mask(sel_new_ref[0, g, 0], past_len, n_new), vsn_ref[0, :, gs].astype(BF16))
            o_slc = acc_ref[g] / jnp.maximum(l_ref[g], 1e-30)
            p = _masked_softmax(_dot_nt(qg, wk_ref[0, :, gs].astype(BF16)) * scale, win_mask)
            o_win = jnp.dot(p.astype(BF16), wv_ref[0, :, gs].astype(BF16), preferred_element_type=F32)
            o_cmp = ocmp_ref[0, g]
            for r in range(GQA_R):
                hh = g * GQA_R + r
                rs = slice(r * seq, (r + 1) * seq)
                o_ref[:, hh * HEAD_DIM:(hh + 1) * HEAD_DIM] = (
                    gates[:, 3 * hh:3 * hh + 1] * o_cmp[rs] + gates[:, 3 * hh + 1:3 * hh + 2] * o_slc[rs]
                    + gates[:, 3 * hh + 2:3 * hh + 3] * o_win[rs])


def _nsa_sample(h, caches, layer, page_table, ks_new, vs_new, wk_all, wv_all, cw, *, batch, seq, past_len, win_buf):
    cmp_k, cmp_v, slc_k, slc_v = caches
    nps = PAGES_PER_STEP
    n_pages, page = page_table.shape[1], ks_new.shape[1]
    page_rows = page * NSA_KV_HEADS
    n_pool = cmp_k.shape[0] // (DEPTH * page_rows)
    n_chunks = n_pages // nps
    n_pb = past_len // SEL_BLOCK
    n_real = n_pb + 1
    bpc = nps * page // SEL_BLOCK
    sel_lanes = -(-max(n_real, (n_chunks + 1) * bpc) // 128) * 128
    sel_shape = (NSA_KV_HEADS, n_chunks + 1, seq, bpc)
    sel_chunk = lambda which: pl.BlockSpec((1, NSA_KV_HEADS, 1, seq, bpc), lambda b, c, pt: (b, 0, which(c), 0, 0))
    nq = GQA_R * seq
    page_spec = lambda k: pl.BlockSpec((page_rows, HEAD_DIM),
                                       lambda b, c, pt, k=k: (pt[b, c * nps + k] + layer * n_pool, 0))
    pages = [page_spec(k) for k in range(nps)]
    q_spec = pl.BlockSpec((seq, NSA_WIDTH), lambda b, c, pt: (b, C_Q // NSA_WIDTH))
    per_b = lambda shape: pl.BlockSpec((1,) + shape, lambda b, c, pt: (b,) + (0,) * len(shape))
    ocmp, sel = pl.pallas_call(
        functools.partial(_nsa_s_cmp_kernel, seq=seq, past_len=past_len, n_real=n_real, sel_lanes=sel_lanes),
        grid_spec=pltpu.PrefetchScalarGridSpec(
            num_scalar_prefetch=1, grid=(batch, n_chunks),
            in_specs=[q_spec, pl.BlockSpec(cw.shape, lambda b, c, pt: (0,) * cw.ndim)] + pages + pages,
            out_specs=[per_b((NSA_KV_HEADS, nq, HEAD_DIM)), per_b(sel_shape)],
            scratch_shapes=[pltpu.VMEM((n_pb, KV_WIDTH), F32)] * 4),
        out_shape=[jax.ShapeDtypeStruct((batch, NSA_KV_HEADS, nq, HEAD_DIM), F32),
                   jax.ShapeDtypeStruct((batch,) + sel_shape, F32)],
        compiler_params=_cparams(2),
        name="nsa_sample_cmp",
    )(page_table, h, cw, *([cmp_k] * nps), *([cmp_v] * nps))
    return pl.pallas_call(
        functools.partial(_nsa_s_slc_kernel, seq=seq, past_len=past_len, win_buf=win_buf),
        grid_spec=pltpu.PrefetchScalarGridSpec(
            num_scalar_prefetch=1, grid=(batch, n_chunks),
            in_specs=[q_spec, sel_chunk(lambda c: c), sel_chunk(lambda c: n_chunks),
                      per_b((NSA_KV_HEADS, nq, HEAD_DIM)),
                      per_b(ks_new.shape[1:]), per_b(vs_new.shape[1:]), per_b(wk_all.shape[1:]), per_b(wv_all.shape[1:]),
                      pl.BlockSpec((seq, KV_WIDTH), lambda b, c, pt: (b, C_GATE // KV_WIDTH))] + pages + pages,
            out_specs=pl.BlockSpec((seq, NSA_WIDTH), lambda b, c, pt: (b, 0)),
            scratch_shapes=[pltpu.VMEM((NSA_KV_HEADS, nq, 1), F32), pltpu.VMEM((NSA_KV_HEADS, nq, 1), F32),
                            pltpu.VMEM((NSA_KV_HEADS, nq, HEAD_DIM), F32)]),
        out_shape=jax.ShapeDtypeStruct((batch * seq, NSA_WIDTH), F32),
        compiler_params=_cparams(2),
        name="nsa_sample_slc",
    )(page_table, h, sel, sel, ocmp, ks_new, vs_new, wk_all, wv_all, h, *([slc_k] * nps), *([slc_v] * nps))


def _rope_tables(pos):
    half = HEAD_DIM // 2
    inv = ROPE_THETA ** (-jnp.arange(half, dtype=F32) / half)
    ang = pos.astype(F32)[:, None] * inv[None, :]
    cos, sin = jnp.cos(ang), jnp.sin(ang)
    return jnp.concatenate([cos, cos], -1), jnp.concatenate([-sin, sin], -1)


def _prep_weights(l, w_in, gm_ln_g, gm_ln_b, gm_ws, gm_bs, cmp_wk, cmp_wv, conv_dw, conv_db, conv_ln_g,
                  conv_ln_b, conv_pw, w_out, ln1_g, ln1_b, peer_wq, peer_subkeys, peer_u, peer_v, ln2_g, ln2_b):
    w = w_in[l]
    o = np.cumsum([0, GM_WIDTH, GM_WIDTH, NSA_WIDTH] + [KV_WIDTH] * 6 + [3 * NSA_HEADS, CONV_WIDTH, CONV_WIDTH])
    gu, gv, q, kc, vc, ks, vs, kw, vw, gate, ca, cb = [w[:, o[i]:o[i + 1]] for i in range(12)]
    zeros = lambda n: jnp.zeros((w.shape[0], n), w.dtype)
    w_proj = jnp.concatenate([gu, gv, ca, cb, q, kc, ks, kw, zeros(C_ROPE_END - C_KW - KV_WIDTH), vc, vs, vw, gate,
                              zeros(PROJ_WIDTH - C_GATE - 3 * NSA_HEADS)], axis=1).astype(BF16)
    row = lambda a: a.reshape(1, -1)
    cw = jnp.stack([cmp_wk[l], cmp_wv[l]], axis=1)
    zero = jnp.zeros_like(cw)
    cw = jnp.stack([jnp.concatenate([cw, zero], -1), jnp.concatenate([zero, cw], -1)], axis=2)
    cw = jnp.broadcast_to(cw[..., None], cw.shape + (HEAD_DIM,))
    dw = jnp.concatenate([conv_dw[l], jnp.zeros((CONV_HALO - CONV_K, CONV_WIDTH), F32)], axis=0)
    return dict(
        w_proj=w_proj, gm_g=row(gm_ln_g[l]), gm_b=row(gm_ln_b[l]), gm_ws=gm_ws[l], gm_bs=gm_bs[l], cmp_w=cw,
        conv_dw=dw, conv_db=row(conv_db[l]), conv_g=row(conv_ln_g[l]), conv_b=row(conv_ln_b[l]),
        conv_pw=conv_pw[l].astype(BF16), w_out=w_out[l].astype(BF16), ln1_g=row(ln1_g[l]), ln1_b=row(ln1_b[l]),
        wq=peer_wq[l].astype(BF16), subkeys=peer_subkeys[l].astype(BF16), layer=l,
        u=peer_u.astype(BF16).reshape(-1, D_MODEL), v=peer_v.astype(BF16).reshape(-1, D_MODEL),
        ln2_g=row(ln2_g[l]), ln2_b=row(ln2_b[l]))


def _gm_weights(p, chunk):
    w = jnp.tril(p["gm_ws"][:, :chunk, :chunk])
    bs = jnp.repeat(p["gm_bs"][:, :chunk].T, GM_WIDTH // GM_GROUPS, axis=1)
    return w, bs


def _peer(x, p, *, tm_proj, tm_route, tm_dense, te):
    m = x.shape[0]
    dummy = jnp.zeros((tm_proj, HEAD_DIM), F32)
    qp = _proj(x, p["wq"], dummy, dummy, tm=tm_proj, rope_lo=0, rope_hi=0, name="peer_q")
    r2, cnt, f, e2 = _route(qp, p["subkeys"], tm=tm_route)
    return _peer_dense(x, p["u"], p["v"], p["layer"], r2, cnt, f, e2, p["ln2_g"], p["ln2_b"], tm=tm_dense, te=te)


def _kv_rows(h, col, batch, seq):
    return h[:, col:col + KV_WIDTH].reshape(batch, seq, NSA_KV_HEADS, HEAD_DIM)


def _layer_prompt(x, p, cos, sin, *, batch, seq):
    m = batch * seq
    h, *kv_out = _proj(x, p["w_proj"], cos, sin, tm=1024, rope_lo=ROPE_LO, rope_hi=ROPE_HI, name="proj_in",
                       kv_cols=(C_KC, C_VC, C_KS, C_VS, C_KW, C_VW))
    kc, vc, ks, vs, kw, vw = [a.reshape(batch, seq, NSA_KV_HEADS, HEAD_DIM) for a in kv_out]
    gw, gbs = _gm_weights(p, GM_CHUNK)
    gm, vn = _gmlp(h, p["gm_g"], p["gm_b"], gw.astype(BF16), gbs, chunk=GM_CHUNK)
    nsa = _nsa_prompt(h, p["cmp_w"], batch=batch, seq=seq, tq=256)
    hist = jnp.zeros((batch, CONV_HALO, CONV_WIDTH), F32)
    y, c = _conv(h, hist, p["conv_dw"], p["conv_db"], p["conv_g"], p["conv_b"], p["conv_pw"],
                 batch=batch, seq=seq, tq=256)
    x1 = _out_ln(gm, nsa, y, x, p["w_out"], p["ln1_g"], p["ln1_b"], tm=256)
    x2 = _peer(x1, p, tm_proj=1024, tm_route=512, tm_dense=512, te=PEER_TE)
    keep = min(WINDOW, seq)
    start = ((seq - 1) // GM_CHUNK) * GM_CHUNK
    states = (kc, vc, ks, vs, kw[:, -keep:], vw[:, -keep:],
              c.reshape(batch, seq, CONV_WIDTH)[:, -(CONV_K - 1):], vn.reshape(batch, seq, GM_WIDTH)[:, start:])
    return x2, states


def _layer_sample(x, p, cos, sin, caches, layer, page_table, win_k, win_v, hist30, *, batch, seq, past_len):
    m = batch * seq
    h = _proj(x, p["w_proj"], cos, sin, tm=m, rope_lo=ROPE_LO, rope_hi=ROPE_HI, name="proj_in_s")
    pad_rows = lambda a: jnp.pad(a.reshape(batch, seq, -1), ((0, 0), (0, GM_CHUNK - seq), (0, 0))).reshape(batch * GM_CHUNK, -1)
    gw, gbs = _gm_weights(p, GM_CHUNK)
    hp = pad_rows(h[:, :2 * GM_WIDTH])
    gm, vn = _gmlp(hp, p["gm_g"], p["gm_b"], gw.astype(BF16), gbs, chunk=GM_CHUNK)
    unpad = lambda a: a.reshape(batch, GM_CHUNK, -1)[:, :seq].reshape(m, -1)
    gm, vn = unpad(gm), unpad(vn)
    kvs = [_kv_rows(h, col, batch, seq) for col in (C_KC, C_VC, C_KS, C_VS, C_KW, C_VW)]
    win_buf = win_k.shape[1]
    page = past_len // page_table.shape[1]
    new_rows = lambda col, rows: jnp.pad(h[:, col:col + KV_WIDTH].reshape(batch, seq, KV_WIDTH),
                                         ((0, 0), (0, rows - seq), (0, 0)))
    flat = lambda a: a.reshape(batch, win_buf, KV_WIDTH)
    wk_all = jnp.concatenate([flat(win_k), new_rows(C_KW, HEAD_DIM)], axis=1)
    wv_all = jnp.concatenate([flat(win_v), new_rows(C_VW, HEAD_DIM)], axis=1)
    nsa = _nsa_sample(h, caches, layer, page_table, new_rows(C_KS, page), new_rows(C_VS, page), wk_all, wv_all,
                      p["cmp_w"], batch=batch, seq=seq, past_len=past_len, win_buf=win_buf)
    keep = min(WINDOW, past_len + seq)
    unflat = lambda a: a[:, :win_buf + seq][:, -keep:].reshape(batch, keep, NSA_KV_HEADS, HEAD_DIM)
    win_k, win_v = unflat(wk_all), unflat(wv_all)
    hist = jnp.pad(hist30, ((0, 0), (CONV_HALO - (CONV_K - 1), 0), (0, 0)))
    y, c = _conv(h, hist, p["conv_dw"], p["conv_db"], p["conv_g"], p["conv_b"], p["conv_pw"],
                 batch=batch, seq=seq, tq=seq)
    x1 = _out_ln(gm, nsa, y, x, p["w_out"], p["ln1_g"], p["ln1_b"], tm=m)
    x2 = _peer(x1, p, tm_proj=m, tm_route=m, tm_dense=m, te=PEER_TE)
    conv_state = jnp.concatenate([hist30, c.reshape(batch, seq, CONV_WIDTH)], axis=1)[:, -(CONV_K - 1):]
    start = ((seq - 1) // GM_CHUNK) * GM_CHUNK
    states = (kvs[0], kvs[1], kvs[2], kvs[3], win_k, win_v, conv_state, vn.reshape(batch, seq, GM_WIDTH)[:, start:])
    return x2, states


def kernel(x_prompt, x_sample, cache_cmp_k, cache_cmp_v, cache_slc_k, cache_slc_v, cache_win_k, cache_win_v,
           state_conv, page_table, w_in, gm_ln_g, gm_ln_b, gm_ws, gm_bs, cmp_wk, cmp_wv, conv_dw, conv_db,
           conv_ln_g, conv_ln_b, conv_pw, w_out, ln1_g, ln1_b, peer_wq, peer_subkeys, peer_u, peer_v, ln2_g, ln2_b):
    bp, sp, _ = x_prompt.shape
    bs, ss, _ = x_sample.shape
    depth = w_in.shape[0]
    past_len = page_table.shape[1] * cache_cmp_k.shape[2]
    pos_p = jnp.arange(sp, dtype=jnp.int32)
    pos_s = past_len + jnp.arange(ss, dtype=jnp.int32)
    cos_p, sin_p = _rope_tables(pos_p)
    cos_s, sin_s = _rope_tables(jnp.tile(pos_s, bs))
    xp = x_prompt.reshape(bp * sp, D_MODEL)
    xs = x_sample.reshape(bs * ss, D_MODEL)
    p_states, s_states = [], []
    caches = [c.reshape(-1, HEAD_DIM) for c in (cache_cmp_k, cache_cmp_v, cache_slc_k, cache_slc_v)]
    for l in range(depth):
        p = _prep_weights(l, w_in, gm_ln_g, gm_ln_b, gm_ws, gm_bs, cmp_wk, cmp_wv, conv_dw, conv_db, conv_ln_g,
                          conv_ln_b, conv_pw, w_out, ln1_g, ln1_b, peer_wq, peer_subkeys, peer_u, peer_v, ln2_g, ln2_b)
        xp, sp_state = _layer_prompt(xp, p, cos_p, sin_p, batch=bp, seq=sp)
        xs, ss_state = _layer_sample(xs, p, cos_s, sin_s, caches, l, page_table, cache_win_k[l], cache_win_v[l],
                                     state_conv[l], batch=bs, seq=ss, past_len=past_len)
        p_states.append(sp_state)
        s_states.append(ss_state)
    stack = lambda states, i: jnp.stack([st[i] for st in states], axis=0)
    outs = [xp.reshape(bp, sp, D_MODEL), xs.reshape(bs, ss, D_MODEL)]
    outs += [stack(p_states, i) for i in range(8)]
    outs += [stack(s_states, i) for i in range(8)]
    return tuple(outs)
```

```python
import functools

import numpy as np
import jax
import jax.numpy as jnp
from jax import lax
from jax.experimental import pallas as pl
from jax.experimental.pallas import tpu as pltpu

F32 = jnp.float32
BF16 = jnp.bfloat16

D_MODEL = 2048
HEAD_DIM = 128
NSA_HEADS = 8
NSA_KV_HEADS = 2
GQA_R = NSA_HEADS // NSA_KV_HEADS
GM_WIDTH = 512
GM_GROUPS = 4
GM_CHUNK = 128
CONV_WIDTH = 512
CONV_GROUPS = 4
CONV_K = 31
NSA_WIDTH = NSA_HEADS * HEAD_DIM
KV_WIDTH = NSA_KV_HEADS * HEAD_DIM
CMP_BLOCK = 32
SEL_BLOCK = 64
N_SELECT = 16
WINDOW = 512
FORCED_SCORE = 1.0e4
PEER_HEADS = 8
N_KEYS = 128
PEER_TOPK = 16
SUBKEY_DIM = 128
ROPE_THETA = 10000.0
LN_EPS = 1e-5
DEPTH = 2
ALPHA = (2 * DEPTH) ** 0.25

C_GU, C_GV, C_CA, C_CB, C_Q = 0, 512, 1024, 1536, 2048
C_KC, C_KS, C_KW, C_ROPE_END = 3072, 3328, 3584, 4096
C_VC, C_VS, C_VW, C_GATE = 4096, 4352, 4608, 4864
PROJ_WIDTH = 5120
PROJ_TN = 512
ROPE_LO, ROPE_HI = C_Q // PROJ_TN, C_ROPE_END // PROJ_TN

V7X_VMEM_LIMIT = 56 * 1024 * 1024
NEG_INF = float("-inf")


def _cparams(n_axes, vmem=V7X_VMEM_LIMIT):
    return pltpu.CompilerParams(dimension_semantics=("arbitrary",) * n_axes, vmem_limit_bytes=vmem)


def _ln_lanes(z, g, b):
    mu = jnp.mean(z, axis=-1, keepdims=True)
    d = z - mu
    var = jnp.mean(d * d, axis=-1, keepdims=True)
    return d * lax.rsqrt(var + LN_EPS) * g + b


def _dot_nt(a, b):
    return lax.dot_general(a, b, (((1,), (1,)), ((), ())), preferred_element_type=F32)


def _proj_kernel(x_ref, w_ref, cos_ref, sin_ref, o_ref, *rest, rope_lo, rope_hi, tn, kv_cols):
    kv_refs, xb_ref = rest[:-1], rest[-1]
    j = pl.program_id(1)
    tm = x_ref.shape[0]

    @pl.when(j == 0)
    def _():
        xb_ref[...] = x_ref[...].astype(BF16)

    acc = jnp.dot(xb_ref[...], w_ref[...], preferred_element_type=F32)
    is_rope = jnp.logical_and(j >= rope_lo, j < rope_hi)

    def emit(blocks):
        for c, blk in enumerate(blocks):
            o_ref[:, c * HEAD_DIM:(c + 1) * HEAD_DIM] = blk
        for ref, col in zip(kv_refs, kv_cols):
            @pl.when(j == col // tn)
            def _():
                first = (col % tn) // HEAD_DIM
                for g in range(NSA_KV_HEADS):
                    ref[pl.ds(g, tm, stride=NSA_KV_HEADS), :] = blocks[first + g]

    @pl.when(is_rope)
    def _():
        cos, sin = cos_ref[...], sin_ref[...]
        blocks = [acc[:, c * HEAD_DIM:(c + 1) * HEAD_DIM] for c in range(tn // HEAD_DIM)]
        emit([blk * cos + pltpu.roll(blk, HEAD_DIM // 2, 1) * sin for blk in blocks])

    @pl.when(jnp.logical_not(is_rope))
    def _():
        emit([acc[:, c * HEAD_DIM:(c + 1) * HEAD_DIM] for c in range(tn // HEAD_DIM)])


def _proj(x, w, cos, sin, *, tm, rope_lo, rope_hi, name, kv_cols=()):
    m, k = x.shape
    n = w.shape[1]
    tn = PROJ_TN
    nper = cos.shape[0] // tm
    kern = functools.partial(_proj_kernel, rope_lo=rope_lo, rope_hi=rope_hi, tn=tn, kv_cols=tuple(kv_cols))
    kv_spec = pl.BlockSpec((NSA_KV_HEADS * tm, HEAD_DIM), lambda i, j: (i, 0))
    kv_shape = jax.ShapeDtypeStruct((NSA_KV_HEADS * m, HEAD_DIM), F32)
    outs = pl.pallas_call(
        kern,
        grid=(m // tm, n // tn),
        in_specs=[
            pl.BlockSpec((tm, k), lambda i, j: (i, 0)),
            pl.BlockSpec((k, tn), lambda i, j: (0, j)),
            pl.BlockSpec((tm, HEAD_DIM), lambda i, j: (i % nper, 0)),
            pl.BlockSpec((tm, HEAD_DIM), lambda i, j: (i % nper, 0)),
        ],
        out_specs=[pl.BlockSpec((tm, tn), lambda i, j: (i, j))] + [kv_spec] * len(kv_cols),
        out_shape=[jax.ShapeDtypeStruct((m, n), F32)] + [kv_shape] * len(kv_cols),
        scratch_shapes=[pltpu.VMEM((tm, k), BF16)],
        compiler_params=_cparams(2),
        name=name,
    )(x, w, cos, sin)
    return outs if kv_cols else outs[0]


GM_CHUNKS_PER_STEP = 4


def _gmlp_kernel(gu_ref, gv_ref, g_ref, b_ref, w_ref, bs_ref, o_ref, vn_ref, *, chunk):
    gw = GM_WIDTH // GM_GROUPS
    for n in range(gu_ref.shape[0] // chunk):
        rows = slice(n * chunk, (n + 1) * chunk)
        for h in range(GM_GROUPS):
            sl = slice(h * gw, (h + 1) * gw)
            v = jax.nn.gelu(gv_ref[rows, sl])
            vn = _ln_lanes(v, g_ref[:, sl], b_ref[:, sl])
            vn_ref[rows, sl] = vn
            s = jnp.dot(w_ref[h], vn.astype(BF16), preferred_element_type=F32) + bs_ref[:, sl]
            o_ref[rows, sl] = jax.nn.gelu(gu_ref[rows, sl]) * s


def _gmlp(h, g, b, w_tril, bs_full, *, chunk):
    m = h.shape[0]
    tile = chunk * (GM_CHUNKS_PER_STEP if (m // chunk) % GM_CHUNKS_PER_STEP == 0 else 1)
    blk = lambda c: pl.BlockSpec((tile, GM_WIDTH), lambda i, c=c: (i, c))
    full = lambda a: pl.BlockSpec(a.shape, lambda i: (0,) * a.ndim)
    return pl.pallas_call(
        functools.partial(_gmlp_kernel, chunk=chunk),
        grid=(m // tile,),
        in_specs=[blk(C_GU // GM_WIDTH), blk(C_GV // GM_WIDTH), full(g), full(b), full(w_tril), full(bs_full)],
        out_specs=[pl.BlockSpec((tile, GM_WIDTH), lambda i: (i, 0))] * 2,
        out_shape=[jax.ShapeDtypeStruct((m, GM_WIDTH), F32)] * 2,
        compiler_params=_cparams(1),
        name="gmlp",
    )(h, h, g, b, w_tril, bs_full)


CONV_HALO = 32


def _conv_kernel(ca_ref, cb_ref, cah_ref, cbh_ref, hist_ref, dw_ref, db_ref, g_ref, b_ref, pw_ref,
                 y_ref, c_ref, buf_ref, *, tq):
    i = pl.program_id(1)
    c = ca_ref[...] * jax.nn.sigmoid(cb_ref[...])
    c_ref[...] = c
    halo = cah_ref[...] * jax.nn.sigmoid(cbh_ref[...])
    halo = jnp.where(i == 0, hist_ref[0], halo)
    buf_ref[0, 0:CONV_HALO, :] = halo
    buf_ref[0, CONV_HALO:CONV_HALO + tq, :] = c
    sub = 8
    span = tq + CONV_HALO - sub
    for r in range(1, sub):
        buf_ref[r, 0:span, :] = buf_ref[0, r:r + span, :]
    lead = CONV_HALO - (CONV_K - 1)
    acc = jnp.zeros((tq, CONV_WIDTH), F32)
    for k in range(CONV_K):
        off = lead + k
        base = off - off % sub
        acc = acc + dw_ref[k:k + 1, :] * buf_ref[off % sub, base:base + tq, :]
    y = acc + db_ref[...]
    cw = CONV_WIDTH // CONV_GROUPS
    parts = []
    for gi in range(CONV_GROUPS):
        sl = slice(gi * cw, (gi + 1) * cw)
        yn = _ln_lanes(y[:, sl], g_ref[:, sl], b_ref[:, sl])
        parts.append((yn * jax.nn.sigmoid(yn)).astype(BF16))
    y_ref[...] = jnp.dot(jnp.concatenate(parts, axis=-1), pw_ref[...], preferred_element_type=F32)


def _conv(h, hist, dw, db, g, b, pw, *, batch, seq, tq):
    m = h.shape[0]
    nt = seq // tq
    row = lambda bi, i: bi * nt + i
    halo_row = lambda bi, i: jnp.maximum((bi * seq + i * tq) // CONV_HALO - 1, 0)
    full = lambda a: pl.BlockSpec(a.shape, lambda bi, i: (0,) * a.ndim)
    kern = functools.partial(_conv_kernel, tq=tq)
    return pl.pallas_call(
        kern,
        grid=(batch, nt),
        in_specs=[
            pl.BlockSpec((tq, CONV_WIDTH), lambda bi, i: (row(bi, i), C_CA // CONV_WIDTH)),
            pl.BlockSpec((tq, CONV_WIDTH), lambda bi, i: (row(bi, i), C_CB // CONV_WIDTH)),
            pl.BlockSpec((CONV_HALO, CONV_WIDTH), lambda bi, i: (halo_row(bi, i), C_CA // CONV_WIDTH)),
            pl.BlockSpec((CONV_HALO, CONV_WIDTH), lambda bi, i: (halo_row(bi, i), C_CB // CONV_WIDTH)),
            pl.BlockSpec((1, CONV_HALO, CONV_WIDTH), lambda bi, i: (bi, 0, 0)),
            full(dw), full(db), full(g), full(b), full(pw),
        ],
        out_specs=[pl.BlockSpec((tq, CONV_WIDTH), lambda bi, i: (row(bi, i), 0))] * 2,
        out_shape=[jax.ShapeDtypeStruct((m, CONV_WIDTH), F32)] * 2,
        scratch_shapes=[pltpu.VMEM((8, CONV_HALO + tq, CONV_WIDTH), F32)],
        compiler_params=_cparams(2),
        name="conv",
    )(h, h, h, h, hist, dw, db, g, b, pw)


def _masked_softmax(s, mask):
    s = jnp.where(mask, s, NEG_INF)
    m = jnp.max(s, axis=-1, keepdims=True)
    m = jnp.where(m > NEG_INF, m, 0.0)
    e = jnp.exp(s - m)
    return e / jnp.maximum(jnp.sum(e, axis=-1, keepdims=True), 1e-30)


def _select_blocks(imp, tpos, n_blocks):
    tq, lanes = imp.shape
    blk = lax.broadcasted_iota(jnp.int32, (tq, lanes), 1)
    cur = tpos // SEL_BLOCK
    valid = blk * SEL_BLOCK <= tpos
    forced = jnp.logical_or(blk == 0, jnp.logical_or(blk == cur, blk == cur - 1))
    score = jnp.where(valid, jnp.where(forced, FORCED_SCORE, imp), NEG_INF)
    beaten = jnp.zeros((tq, lanes), F32)
    for i in range(n_blocks):
        ci = score[:, i:i + 1]
        earlier = (blk > i).astype(F32)
        beaten = beaten + jnp.where(ci > score, 1.0, jnp.where(ci == score, earlier, 0.0))
    picked = jnp.logical_and(beaten < float(min(N_SELECT, n_blocks)), blk < n_blocks)
    return picked.astype(F32)


SLC_CHUNK = 512


def _online_softmax_step(m_ref, l_ref, acc_ref, r, s, mask, v):
    s = jnp.where(mask, s, NEG_INF)
    m_old = m_ref[r]
    m_new = jnp.maximum(m_old, jnp.max(s, axis=-1, keepdims=True))
    m_safe = jnp.where(m_new > NEG_INF, m_new, 0.0)
    alpha = jnp.exp(m_old - m_safe)
    e = jnp.exp(s - m_safe)
    l_ref[r] = alpha * l_ref[r] + jnp.sum(e, axis=-1, keepdims=True)
    acc_ref[r] = alpha * acc_ref[r] + jnp.dot(e.astype(BF16), v, preferred_element_type=F32)
    m_ref[r] = m_new


VT_CHUNK = 256


def _select_blocks_t(imp_t, tpos_row, n_blocks):
    nb, tq = imp_t.shape
    blk = lax.broadcasted_iota(jnp.int32, (nb, tq), 0)
    cur = tpos_row // SEL_BLOCK
    valid = blk * SEL_BLOCK <= tpos_row
    forced = jnp.logical_or(blk == 0, jnp.logical_or(blk == cur, blk == cur - 1))
    score = jnp.where(valid, jnp.where(forced, FORCED_SCORE, imp_t), NEG_INF)
    beaten = jnp.zeros((nb, tq), F32)
    for i in range(n_blocks):
        ci = score[i:i + 1, :]
        earlier = (blk > i).astype(F32)
        beaten = beaten + jnp.where(ci > score, 1.0, jnp.where(ci == score, earlier, 0.0))
    return (beaten < float(min(N_SELECT, n_blocks))).astype(F32)


def _col_softmax(s_t, mask_t):
    s = jnp.where(mask_t, s_t, NEG_INF)
    m = jnp.max(s, axis=0, keepdims=True)
    m = jnp.where(m > NEG_INF, m, 0.0)
    e = jnp.exp(s - m)
    return e, jnp.sum(e, axis=0, keepdims=True)


def _col_softmax_step(m_ref, l_ref, acc_ref, r, s_t, mask_t, v_t):
    s = jnp.where(mask_t, s_t, NEG_INF)
    m_old = m_ref[r]
    m_new = jnp.maximum(m_old, jnp.max(s, axis=0, keepdims=True))
    m_safe = jnp.where(m_new > NEG_INF, m_new, 0.0)
    alpha = jnp.exp(m_old - m_safe)
    e = jnp.exp(s - m_safe)
    l_ref[r] = alpha * l_ref[r] + jnp.sum(e, axis=0, keepdims=True)
    acc_ref[r] = alpha * acc_ref[r] + jnp.dot(v_t, e.astype(BF16), preferred_element_type=F32)
    m_ref[r] = m_new


def _nsa_prompt_kernel(q_ref, kc_ref, ks_ref, kw_ref, vc_ref, vs_ref, vw_ref, gate_ref, cw_ref, o_ref,
                       kb_ref, vt_ref, kcmp_ref, vcmpt_ref, m_ref, l_ref, acc_ref, *, tq, seq):
    qi = pl.program_id(1)
    t0 = qi * tq
    scale = HEAD_DIM ** -0.5
    n_slc = seq // SEL_BLOCK
    tpos = t0 + lax.broadcasted_iota(jnp.int32, (1, tq), 1)
    row_c = lax.broadcasted_iota(jnp.int32, (2 * n_slc, tq), 0)
    cmp_last = jnp.where(row_c < n_slc, row_c * SEL_BLOCK + CMP_BLOCK - 1, (row_c - n_slc) * SEL_BLOCK + SEL_BLOCK - 1)
    cmp_mask = cmp_last <= tpos
    win_len = tq + WINDOW
    w0 = pl.multiple_of(jnp.maximum(t0 - WINDOW, 0), tq)
    wpos = w0 + lax.broadcasted_iota(jnp.int32, (win_len, tq), 0)
    win_mask = jnp.logical_and(wpos <= tpos, wpos >= tpos - WINDOW)
    gates_t = jax.nn.sigmoid(gate_ref[:, 0:HEAD_DIM]).T

    @pl.when(qi == 0)
    def _():
        kb_ref[0] = ks_ref[0].astype(BF16)
        kb_ref[1] = kw_ref[0].astype(BF16)
        for g in range(NSA_KV_HEADS):
            gs = slice(g * HEAD_DIM, (g + 1) * HEAD_DIM)
            for n, ref in enumerate((vs_ref, vw_ref)):
                for c in range(seq // VT_CHUNK):
                    vt_ref[n, g, c] = ref[0, c * VT_CHUNK:(c + 1) * VT_CHUNK, gs].T.astype(BF16)
            halves = []
            for which, ref in enumerate((kc_ref, vc_ref)):
                x3 = ref[0, :, gs].reshape(n_slc, SEL_BLOCK, HEAD_DIM)
                ev = jnp.sum(x3 * cw_ref[g, which, 0][None], axis=1)
                od = jnp.sum(x3 * cw_ref[g, which, 1][None], axis=1)
                halves.append(jnp.concatenate([ev, od], axis=0))
            kcmp_ref[g] = halves[0].astype(BF16)
            vcmpt_ref[g] = jnp.concatenate([halves[1], jnp.zeros_like(halves[1])], axis=0).T.astype(BF16)

    for g in range(NSA_KV_HEADS):
        gs = slice(g * HEAD_DIM, (g + 1) * HEAD_DIM)
        kcmp, vcmp_t = kcmp_ref[g], vcmpt_ref[g][:, :2 * n_slc]
        qg = jnp.concatenate([(q_ref[:, (g * GQA_R + r) * HEAD_DIM:(g * GQA_R + r + 1) * HEAD_DIM] * scale).astype(BF16)
                              for r in range(GQA_R)], axis=0)
        heads = lambda a: jnp.concatenate([a] * GQA_R, axis=1)
        e, l = _col_softmax(_dot_nt(kcmp, qg), heads(cmp_mask))
        p = e / jnp.maximum(l, 1e-30)
        o_cmp = jnp.dot(vcmp_t, p.astype(BF16), preferred_element_type=F32)
        imp = p[:, 0:tq]
        for r in range(1, GQA_R):
            imp = imp + p[:, r * tq:(r + 1) * tq]
        sel = _select_blocks_t(imp[:n_slc] + imp[n_slc:], tpos, n_slc).astype(BF16)

        m_ref[...] = jnp.full(m_ref.shape, NEG_INF, F32)
        l_ref[...] = jnp.zeros_like(l_ref)
        acc_ref[...] = jnp.zeros_like(acc_ref)
        for j in range(seq // SLC_CHUNK):
            @pl.when(j * SLC_CHUNK < t0 + tq)
            def _():
                kpos = j * SLC_CHUNK + lax.broadcasted_iota(jnp.int32, (SLC_CHUNK, tq), 0)
                key_blk = (j * SLC_CHUNK + lax.broadcasted_iota(jnp.int32, (SLC_CHUNK, n_slc), 0)) // SEL_BLOCK
                expand = (lax.broadcasted_iota(jnp.int32, (SLC_CHUNK, n_slc), 1) == key_blk).astype(BF16)
                sel_keys = jnp.dot(expand, sel, preferred_element_type=F32)
                mask = jnp.logical_and(sel_keys > 0.5, kpos <= tpos)
                k = kb_ref[0, j * SLC_CHUNK:(j + 1) * SLC_CHUNK, gs]
                per = SLC_CHUNK // VT_CHUNK
                v_t = jnp.concatenate([vt_ref[0, g, j * per + c] for c in range(per)], axis=1)
                _col_softmax_step(m_ref, l_ref, acc_ref, 0, _dot_nt(k, qg), heads(mask), v_t)

        kwin = kb_ref[1, pl.ds(w0, win_len), gs]
        wc0 = w0 // VT_CHUNK
        vwin_t = jnp.concatenate([vt_ref[1, g, wc0 + c] for c in range(win_len // VT_CHUNK)], axis=1)
        o_slc = acc_ref[0] / jnp.maximum(l_ref[0], 1e-30)
        e, l = _col_softmax(_dot_nt(kwin, qg), heads(win_mask))
        o_win = jnp.dot(vwin_t, e.astype(BF16), preferred_element_type=F32) / jnp.maximum(l, 1e-30)
        for r in range(GQA_R):
            hh = g * GQA_R + r
            cs = slice(r * tq, (r + 1) * tq)
            o_t = (gates_t[3 * hh:3 * hh + 1] * o_cmp[:, cs] + gates_t[3 * hh + 1:3 * hh + 2] * o_slc[:, cs]
                   + gates_t[3 * hh + 2:3 * hh + 3] * o_win[:, cs])
            o_ref[:, hh * HEAD_DIM:(hh + 1) * HEAD_DIM] = o_t.T


def _nsa_prompt(h, cmp_w, *, batch, seq, tq):
    m = h.shape[0]
    nt = seq // tq
    kv = lambda col: pl.BlockSpec((1, seq, KV_WIDTH), lambda bi, i, col=col: (bi, 0, col // KV_WIDTH))
    h3 = h.reshape(batch, seq, PROJ_WIDTH)
    kern = functools.partial(_nsa_prompt_kernel, tq=tq, seq=seq)
    return pl.pallas_call(
        kern,
        grid=(batch, nt),
        in_specs=[
            pl.BlockSpec((tq, NSA_WIDTH), lambda bi, i: (bi * nt + i, C_Q // NSA_WIDTH)),
            kv(C_KC), kv(C_KS), kv(C_KW), kv(C_VC), kv(C_VS), kv(C_VW),
            pl.BlockSpec((tq, KV_WIDTH), lambda bi, i: (bi * nt + i, C_GATE // KV_WIDTH)),
            pl.BlockSpec(cmp_w.shape, lambda bi, i: (0,) * cmp_w.ndim),
        ],
        out_specs=pl.BlockSpec((tq, NSA_WIDTH), lambda bi, i: (bi * nt + i, 0)),
        out_shape=jax.ShapeDtypeStruct((m, NSA_WIDTH), F32),
        scratch_shapes=[
            pltpu.VMEM((2, seq, KV_WIDTH), BF16),
            pltpu.VMEM((2, NSA_KV_HEADS, seq // VT_CHUNK, HEAD_DIM, VT_CHUNK), BF16),
            pltpu.VMEM((NSA_KV_HEADS, 2 * (seq // SEL_BLOCK), HEAD_DIM), BF16),
            pltpu.VMEM((NSA_KV_HEADS, HEAD_DIM, 4 * (seq // SEL_BLOCK)), BF16),
            pltpu.VMEM((1, 1, GQA_R * tq), F32), pltpu.VMEM((1, 1, GQA_R * tq), F32),
            pltpu.VMEM((1, HEAD_DIM, GQA_R * tq), F32),
        ],
        compiler_params=_cparams(2),
        name="nsa_prompt",
    )(h, h3, h3, h3, h3, h3, h3, h, cmp_w)


def _out_ln_kernel(gm_ref, nsa_ref, y_ref, x_ref, w_ref, g_ref, b_ref, o_ref):
    cat = jnp.concatenate([gm_ref[...].astype(BF16), nsa_ref[...].astype(BF16), y_ref[...].astype(BF16)], axis=-1)
    mix = jnp.dot(cat, w_ref[...], preferred_element_type=F32)
    o_ref[...] = _ln_lanes(ALPHA * x_ref[...] + mix, g_ref[...], b_ref[...])


def _out_ln(gm, nsa, y, x, w, g, b, *, tm):
    m = x.shape[0]
    row = lambda width: pl.BlockSpec((tm, width), lambda i: (i, 0))
    full = lambda a: pl.BlockSpec(a.shape, lambda i: (0,) * a.ndim)
    return pl.pallas_call(
        _out_ln_kernel,
        grid=(m // tm,),
        in_specs=[row(GM_WIDTH), row(NSA_WIDTH), row(CONV_WIDTH), row(D_MODEL), full(w), full(g), full(b)],
        out_specs=row(D_MODEL),
        out_shape=jax.ShapeDtypeStruct((m, D_MODEL), F32),
        compiler_params=_cparams(1),
        name="out_ln",
    )(gm, nsa, y, x, w, g, b)


def _extract_topk(s, exact_ties, want_rank):
    n_rows, tm = s.shape
    rows = lax.broadcasted_iota(jnp.int32, (n_rows, tm), 0)
    krow = lax.broadcasted_iota(jnp.int32, (PEER_TOPK, tm), 0)
    work = s
    rank = jnp.full((n_rows, tm), float(PEER_TOPK), F32) if want_rank else None
    vals = jnp.zeros((PEER_TOPK, tm), F32)
    for r in range(PEER_TOPK):
        mx = jnp.max(work, axis=0, keepdims=True)
        hit = work == mx
        if exact_ties:
            hit = rows == jnp.min(jnp.where(hit, rows, n_rows), axis=0, keepdims=True)
        if want_rank:
            rank = jnp.where(hit, float(r), rank)
        vals = jnp.where(krow == r, mx, vals)
        work = jnp.where(hit, NEG_INF, work)
    taken = jnp.sum((work == NEG_INF).astype(F32), axis=0, keepdims=True)
    return vals, rank, taken


def _pair_topk(v1, v2):
    tm = v1.shape[1]
    krow = lax.broadcasted_iota(jnp.int32, (PEER_TOPK, tm), 0)
    cnt = jnp.zeros((PEER_TOPK, tm), F32)
    front = v1 + v2[0:1]
    m0 = v1[0:1] + v2[0:1]
    z = jnp.zeros((1, tm), F32)
    for _ in range(PEER_TOPK):
        mx = jnp.max(front, axis=0, keepdims=True)
        idx = jnp.min(jnp.where(front == mx, krow, PEER_TOPK), axis=0, keepdims=True)
        hit = krow == idx
        cnt = cnt + hit.astype(F32)
        z = z + jnp.exp(mx - m0)
        chosen = jnp.sum(jnp.where(hit, cnt, 0.0), axis=0, keepdims=True)
        nxt = jnp.full((1, tm), NEG_INF, F32)
        for c in range(1, PEER_TOPK):
            nxt = jnp.where(chosen == float(c), v2[c:c + 1], nxt)
        front = jnp.where(hit, v1 + nxt, front)
    return cnt, z


def _route_kernel(qp_ref, sk_ref, r2_ref, cnt_ref, f_ref, e2_ref):
    q1 = qp_ref[:, 0:SUBKEY_DIM].astype(BF16)
    q2 = qp_ref[:, SUBKEY_DIM:2 * SUBKEY_DIM].astype(BF16)
    s1 = _dot_nt(sk_ref[0, 0], q1)
    s2 = _dot_nt(sk_ref[0, 1], q2)

    def emit(v1, v2, cnt, z, cnt_a, rank2):
        r2_ref[0] = rank2.astype(BF16)
        cnt_ref[0] = cnt_a
        f_ref[0] = jnp.exp(s1 - v1[0:1]) / z
        e2_ref[0] = jnp.exp(s2 - v2[0:1]).astype(BF16)

    v1, _, taken1 = _extract_topk(s1, exact_ties=False, want_rank=False)
    v2, rank2, taken2 = _extract_topk(s2, exact_ties=False, want_rank=True)
    cnt, z = _pair_topk(v1, v2)
    cnt_a = jnp.zeros(s1.shape, F32)
    for r in range(PEER_TOPK):
        cnt_a = jnp.where(s1 == v1[r:r + 1], cnt[r:r + 1], cnt_a)
    emit(v1, v2, cnt, z, cnt_a, rank2)

    @pl.when(jnp.max(jnp.maximum(taken1, taken2)) > float(PEER_TOPK))
    def _():
        v1, rank1, _ = _extract_topk(s1, exact_ties=True, want_rank=True)
        v2, rank2, _ = _extract_topk(s2, exact_ties=True, want_rank=True)
        cnt, z = _pair_topk(v1, v2)
        cnt_a = jnp.zeros(s1.shape, F32)
        for r in range(PEER_TOPK):
            cnt_a = jnp.where(rank1 == float(r), cnt[r:r + 1], cnt_a)
        emit(v1, v2, cnt, z, cnt_a, rank2)


def _route(qp, subkeys, *, tm):
    m = qp.shape[0]
    out = lambda dt: jax.ShapeDtypeStruct((PEER_HEADS, N_KEYS, m), dt)
    ospec = pl.BlockSpec((1, N_KEYS, tm), lambda i, hd: (hd, 0, i))
    return pl.pallas_call(
        _route_kernel,
        grid=(m // tm, PEER_HEADS),
        in_specs=[
            pl.BlockSpec((tm, 2 * SUBKEY_DIM), lambda i, hd: (i, hd)),
            pl.BlockSpec((1, 2, N_KEYS, SUBKEY_DIM), lambda i, hd: (hd, 0, 0, 0)),
        ],
        out_specs=[ospec] * 4,
        out_shape=[out(BF16), out(F32), out(F32), out(BF16)],
        compiler_params=_cparams(2),
        name="peer_route",
    )(qp, subkeys)


PEER_SUB = 256
PEER_TE = 1024


def _peer_dense_kernel(x_ref, u_ref, v_ref, r2_ref, cnt_ref, f_ref, e2_ref, g_ref, b_ref, o_ref, xb_ref, *, te):
    j = pl.program_id(1)

    @pl.when(j == 0)
    def _():
        xb_ref[...] = x_ref[...].astype(BF16)
        o_ref[...] = jnp.zeros_like(o_ref)

    cols = []
    for c in range(te // PEER_SUB):
        act = jax.nn.gelu(_dot_nt(u_ref[c * PEER_SUB:(c + 1) * PEER_SUB, :], xb_ref[...]).astype(BF16))
        rows = []
        for al in range(PEER_SUB // N_KEYS):
            a = c * (PEER_SUB // N_KEYS) + al
            gate = None
            for hd in range(PEER_HEADS):
                cnt = cnt_ref[hd, a:a + 1, :].astype(BF16)
                f = f_ref[hd, a:a + 1, :].astype(BF16)
                term = jnp.where(r2_ref[hd] < cnt, e2_ref[hd], jnp.zeros((), BF16)) * f
                gate = term if gate is None else gate + term
            rows.append(act[al * N_KEYS:(al + 1) * N_KEYS] * gate)
        cols.append(jnp.concatenate(rows, axis=0).T)
    o_ref[...] += jnp.dot(jnp.concatenate(cols, axis=1), v_ref[...], preferred_element_type=F32)

    @pl.when(j == pl.num_programs(1) - 1)
    def _():
        o_ref[...] = _ln_lanes(ALPHA * x_ref[...] + o_ref[...], g_ref[...], b_ref[...])


def _peer_dense(x, u, v, layer, r2, cnt, f, e2, g, b, *, tm, te):
    m = x.shape[0]
    n_exp = u.shape[0] // DEPTH
    first = layer * (n_exp // te)
    by_b = pl.BlockSpec((PEER_HEADS, N_KEYS, tm), lambda i, j: (0, 0, i))
    by_a = pl.BlockSpec((PEER_HEADS, te // N_KEYS, tm), lambda i, j: (0, j, i))
    full = lambda a: pl.BlockSpec(a.shape, lambda i, j: (0,) * a.ndim)
    kern = functools.partial(_peer_dense_kernel, te=te)
    return pl.pallas_call(
        kern,
        grid=(m // tm, n_exp // te),
        in_specs=[
            pl.BlockSpec((tm, D_MODEL), lambda i, j: (i, 0), pipeline_mode=pl.Buffered(1)),
            pl.BlockSpec((te, D_MODEL), lambda i, j: (first + j, 0)),
            pl.BlockSpec((te, D_MODEL), lambda i, j: (first + j, 0)),
            by_b, by_a, by_a, by_b, full(g), full(b),
        ],
        out_specs=pl.BlockSpec((tm, D_MODEL), lambda i, j: (i, 0), pipeline_mode=pl.Buffered(1)),
        out_shape=jax.ShapeDtypeStruct((m, D_MODEL), F32),
        scratch_shapes=[pltpu.VMEM((tm, D_MODEL), BF16)],
        compiler_params=_cparams(2),
        name="peer_dense",
    )(x, u, v, r2, cnt, f, e2, g, b)


PAGES_PER_STEP = 16


def _group_queries(q_ref, g):
    heads = [q_ref[:, (g * GQA_R + r) * HEAD_DIM:(g * GQA_R + r + 1) * HEAD_DIM] for r in range(GQA_R)]
    return jnp.concatenate(heads, axis=0).astype(BF16)


def _page_rows(pages, g):
    n = pages[0].shape[0] // NSA_KV_HEADS
    return jnp.concatenate([r[pl.ds(g, n, stride=NSA_KV_HEADS), :] for r in pages], axis=0)


def _nsa_s_cmp_kernel(pt_ref, q_ref, cw_ref, *refs, seq, past_len, n_real, sel_lanes):
    nps = PAGES_PER_STEP
    k_pages, v_pages = refs[:nps], refs[nps:2 * nps]
    ocmp_ref, sel_ref, ke_ref, ko_ref, ve_ref, vo_ref = refs[2 * nps:]
    c = pl.program_id(1)
    n_pb = ke_ref.shape[0]
    rows = nps * (k_pages[0].shape[0] // NSA_KV_HEADS) // SEL_BLOCK
    off = pl.multiple_of(c * rows, rows)

    def compress(pages, which, even_ref, odd_ref):
        for g in range(NSA_KV_HEADS):
            gs = slice(g * HEAD_DIM, (g + 1) * HEAD_DIM)
            x3 = _page_rows(pages, g).reshape(rows, SEL_BLOCK, HEAD_DIM)
            even_ref[pl.ds(off, rows), gs] = jnp.sum(x3 * cw_ref[g, which, 0][None], axis=1)
            odd_ref[pl.ds(off, rows), gs] = jnp.sum(x3 * cw_ref[g, which, 1][None], axis=1)

    compress(k_pages, 0, ke_ref, ko_ref)
    compress(v_pages, 1, ve_ref, vo_ref)

    @pl.when(c == pl.num_programs(1) - 1)
    def _():
        scale = HEAD_DIM ** -0.5
        nq = GQA_R * seq
        qrow = lax.broadcasted_iota(jnp.int32, (nq, 1), 0)
        qpos = past_len + qrow % seq
        lane = lax.broadcasted_iota(jnp.int32, (nq, 2 * n_pb), 1)
        cmp_last = jnp.where(lane < n_pb, lane * SEL_BLOCK + CMP_BLOCK - 1, (lane - n_pb) * SEL_BLOCK + SEL_BLOCK - 1)
        cmp_mask = cmp_last <= qpos
        tpos = past_len + lax.broadcasted_iota(jnp.int32, (seq, 1), 0)
        for g in range(NSA_KV_HEADS):
            gs = slice(g * HEAD_DIM, (g + 1) * HEAD_DIM)
            kcat = jnp.concatenate([ke_ref[:, gs], ko_ref[:, gs]], axis=0).astype(BF16)
            vcat = jnp.concatenate([ve_ref[:, gs], vo_ref[:, gs]], axis=0).astype(BF16)
            p = _masked_softmax(_dot_nt(_group_queries(q_ref, g), kcat) * scale, cmp_mask)
            ocmp_ref[0, g] = jnp.dot(p.astype(BF16), vcat, preferred_element_type=F32)
            imp = p[0:seq]
            for r in range(1, GQA_R):
                imp = imp + p[r * seq:(r + 1) * seq]
            imp = imp[:, :n_pb] + imp[:, n_pb:]
            imp = jnp.concatenate([imp, jnp.zeros((seq, sel_lanes - n_pb), F32)], axis=1)
            sel = _select_blocks(imp, tpos, n_real)
            bpc = sel_ref.shape[-1]
            for ch in range(sel_ref.shape[2]):
                sel_ref[0, g, ch] = sel[:, ch * bpc:(ch + 1) * bpc]


def _nsa_s_slc_kernel(pt_ref, q_ref, sel_ref, sel_new_ref, ocmp_ref, ksn_ref, vsn_ref, wk_ref, wv_ref, gate_ref, *refs,
                      seq, past_len, win_buf):
    nps = PAGES_PER_STEP
    k_pages, v_pages = refs[:nps], refs[nps:2 * nps]
    o_ref, m_ref, l_ref, acc_ref = refs[2 * nps:]
    c = pl.program_id(1)
    scale = HEAD_DIM ** -0.5
    nq = GQA_R * seq
    chunk = nps * (k_pages[0].shape[0] // NSA_KV_HEADS)
    qrow = lax.broadcasted_iota(jnp.int32, (nq, 1), 0)
    qpos = past_len + qrow % seq

    @pl.when(c == 0)
    def _():
        m_ref[...] = jnp.full(m_ref.shape, NEG_INF, F32)
        l_ref[...] = jnp.zeros_like(l_ref)
        acc_ref[...] = jnp.zeros_like(acc_ref)

    def online_update(g, s, mask, v):
        _online_softmax_step(m_ref, l_ref, acc_ref, g, s, mask, v)

    def key_mask(sel_blocks, first_pos, n_keys):
        bpc = sel_blocks.shape[1]
        key_blk = lax.broadcasted_iota(jnp.int32, (bpc, n_keys), 1) // SEL_BLOCK
        expand = (lax.broadcasted_iota(jnp.int32, (bpc, n_keys), 0) == key_blk).astype(BF16)
        sel_keys = jnp.dot(sel_blocks.astype(BF16), expand, preferred_element_type=F32)
        sel_keys = jnp.concatenate([sel_keys] * GQA_R, axis=0)
        kpos = first_pos + lax.broadcasted_iota(jnp.int32, (nq, n_keys), 1)
        return jnp.logical_and(sel_keys > 0.5, kpos <= qpos)

    for g in range(NSA_KV_HEADS):
        mask = key_mask(sel_ref[0, g, 0], c * chunk, chunk)
        scores = _dot_nt(_group_queries(q_ref, g), _page_rows(k_pages, g).astype(BF16)) * scale
        online_update(g, scores, mask, _page_rows(v_pages, g).astype(BF16))

    @pl.when(c == pl.num_programs(1) - 1)
    def _():
        gates = jax.nn.sigmoid(gate_ref[:, 0:HEAD_DIM])
        n_new = ksn_ref.shape[1]
        n_win = wk_ref.shape[1]
        wpos = past_len - win_buf + lax.broadcasted_iota(jnp.int32, (nq, n_win), 1)
        win_mask = jnp.logical_and(wpos <= qpos, wpos >= qpos - WINDOW)
        for g in range(NSA_KV_HEADS):
            gs = slice(g * HEAD_DIM, (g + 1) * HEAD_DIM)
            qg = _group_queries(q_ref, g)
            online_update(g, _dot_nt(qg, ksn_ref[0, :, gs].astype(BF16)) * scale,
                          key_mask(sel_new_ref[0, g, 0], past_len, n_new), vsn_ref[0, :, gs].astype(BF16))
            o_slc = acc_ref[g] / jnp.maximum(l_ref[g], 1e-30)
            p = _masked_softmax(_dot_nt(qg, wk_ref[0, :, gs].astype(BF16)) * scale, win_mask)
            o_win = jnp.dot(p.astype(BF16), wv_ref[0, :, gs].astype(BF16), preferred_element_type=F32)
            o_cmp = ocmp_ref[0, g]
            for r in range(GQA_R):
                hh = g * GQA_R + r
                rs = slice(r * seq, (r + 1) * seq)
                o_ref[:, hh * HEAD_DIM:(hh + 1) * HEAD_DIM] = (
                    gates[:, 3 * hh:3 * hh + 1] * o_cmp[rs] + gates[:, 3 * hh + 1:3 * hh + 2] * o_slc[rs]
                    + gates[:, 3 * hh + 2:3 * hh + 3] * o_win[rs])


def _nsa_sample(h, caches, layer, page_table, ks_new, vs_new, wk_all, wv_all, cw, *, batch, seq, past_len, win_buf):
    cmp_k, cmp_v, slc_k, slc_v = caches
    nps = PAGES_PER_STEP
    n_pages, page = page_table.shape[1], ks_new.shape[1]
    page_rows = page * NSA_KV_HEADS
    n_pool = cmp_k.shape[0] // (DEPTH * page_rows)
    n_chunks = n_pages // nps
    n_pb = past_len // SEL_BLOCK
    n_real = n_pb + 1
    bpc = nps * page // SEL_BLOCK
    sel_lanes = -(-max(n_real, (n_chunks + 1) * bpc) // 128) * 128
    sel_shape = (NSA_KV_HEADS, n_chunks + 1, seq, bpc)
    sel_chunk = lambda which: pl.BlockSpec((1, NSA_KV_HEADS, 1, seq, bpc), lambda b, c, pt: (b, 0, which(c), 0, 0))
    nq = GQA_R * seq
    page_spec = lambda k: pl.BlockSpec((page_rows, HEAD_DIM),
                                       lambda b, c, pt, k=k: (pt[b, c * nps + k] + layer * n_pool, 0))
    pages = [page_spec(k) for k in range(nps)]
    q_spec = pl.BlockSpec((seq, NSA_WIDTH), lambda b, c, pt: (b, C_Q // NSA_WIDTH))
    per_b = lambda shape: pl.BlockSpec((1,) + shape, lambda b, c, pt: (b,) + (0,) * len(shape))
    ocmp, sel = pl.pallas_call(
        functools.partial(_nsa_s_cmp_kernel, seq=seq, past_len=past_len, n_real=n_real, sel_lanes=sel_lanes),
        grid_spec=pltpu.PrefetchScalarGridSpec(
            num_scalar_prefetch=1, grid=(batch, n_chunks),
            in_specs=[q_spec, pl.BlockSpec(cw.shape, lambda b, c, pt: (0,) * cw.ndim)] + pages + pages,
            out_specs=[per_b((NSA_KV_HEADS, nq, HEAD_DIM)), per_b(sel_shape)],
            scratch_shapes=[pltpu.VMEM((n_pb, KV_WIDTH), F32)] * 4),
        out_shape=[jax.ShapeDtypeStruct((batch, NSA_KV_HEADS, nq, HEAD_DIM), F32),
                   jax.ShapeDtypeStruct((batch,) + sel_shape, F32)],
        compiler_params=_cparams(2),
        name="nsa_sample_cmp",
    )(page_table, h, cw, *([cmp_k] * nps), *([cmp_v] * nps))
    return pl.pallas_call(
        functools.partial(_nsa_s_slc_kernel, seq=seq, past_len=past_len, win_buf=win_buf),
        grid_spec=pltpu.PrefetchScalarGridSpec(
            num_scalar_prefetch=1, grid=(batch, n_chunks),
            in_specs=[q_spec, sel_chunk(lambda c: c), sel_chunk(lambda c: n_chunks),
                      per_b((NSA_KV_HEADS, nq, HEAD_DIM)),
                      per_b(ks_new.shape[1:]), per_b(vs_new.shape[1:]), per_b(wk_all.shape[1:]), per_b(wv_all.shape[1:]),
                      pl.BlockSpec((seq, KV_WIDTH), lambda b, c, pt: (b, C_GATE // KV_WIDTH))] + pages + pages,
            out_specs=pl.BlockSpec((seq, NSA_WIDTH), lambda b, c, pt: (b, 0)),
            scratch_shapes=[pltpu.VMEM((NSA_KV_HEADS, nq, 1), F32), pltpu.VMEM((NSA_KV_HEADS, nq, 1), F32),
                            pltpu.VMEM((NSA_KV_HEADS, nq, HEAD_DIM), F32)]),
        out_shape=jax.ShapeDtypeStruct((batch * seq, NSA_WIDTH), F32),
        compiler_params=_cparams(2),
        name="nsa_sample_slc",
    )(page_table, h, sel, sel, ocmp, ks_new, vs_new, wk_all, wv_all, h, *([slc_k] * nps), *([slc_v] * nps))


def _rope_tables(pos):
    half = HEAD_DIM // 2
    inv = ROPE_THETA ** (-jnp.arange(half, dtype=F32) / half)
    ang = pos.astype(F32)[:, None] * inv[None, :]
    cos, sin = jnp.cos(ang), jnp.sin(ang)
    return jnp.concatenate([cos, cos], -1), jnp.concatenate([-sin, sin], -1)


def _prep_weights(l, w_in, gm_ln_g, gm_ln_b, gm_ws, gm_bs, cmp_wk, cmp_wv, conv_dw, conv_db, conv_ln_g,
                  conv_ln_b, conv_pw, w_out, ln1_g, ln1_b, peer_wq, peer_subkeys, peer_u, peer_v, ln2_g, ln2_b):
    w = w_in[l]
    o = np.cumsum([0, GM_WIDTH, GM_WIDTH, NSA_WIDTH] + [KV_WIDTH] * 6 + [3 * NSA_HEADS, CONV_WIDTH, CONV_WIDTH])
    gu, gv, q, kc, vc, ks, vs, kw, vw, gate, ca, cb = [w[:, o[i]:o[i + 1]] for i in range(12)]
    zeros = lambda n: jnp.zeros((w.shape[0], n), w.dtype)
    w_proj = jnp.concatenate([gu, gv, ca, cb, q, kc, ks, kw, zeros(C_ROPE_END - C_KW - KV_WIDTH), vc, vs, vw, gate,
                              zeros(PROJ_WIDTH - C_GATE - 3 * NSA_HEADS)], axis=1).astype(BF16)
    row = lambda a: a.reshape(1, -1)
    cw = jnp.stack([cmp_wk[l], cmp_wv[l]], axis=1)
    zero = jnp.zeros_like(cw)
    cw = jnp.stack([jnp.concatenate([cw, zero], -1), jnp.concatenate([zero, cw], -1)], axis=2)
    cw = jnp.broadcast_to(cw[..., None], cw.shape + (HEAD_DIM,))
    dw = jnp.concatenate([conv_dw[l], jnp.zeros((CONV_HALO - CONV_K, CONV_WIDTH), F32)], axis=0)
    return dict(
        w_proj=w_proj, gm_g=row(gm_ln_g[l]), gm_b=row(gm_ln_b[l]), gm_ws=gm_ws[l], gm_bs=gm_bs[l], cmp_w=cw,
        conv_dw=dw, conv_db=row(conv_db[l]), conv_g=row(conv_ln_g[l]), conv_b=row(conv_ln_b[l]),
        conv_pw=conv_pw[l].astype(BF16), w_out=w_out[l].astype(BF16), ln1_g=row(ln1_g[l]), ln1_b=row(ln1_b[l]),
        wq=peer_wq[l].astype(BF16), subkeys=peer_subkeys[l].astype(BF16), layer=l,
        u=peer_u.astype(BF16).reshape(-1, D_MODEL), v=peer_v.astype(BF16).reshape(-1, D_MODEL),
        ln2_g=row(ln2_g[l]), ln2_b=row(ln2_b[l]))


def _gm_weights(p, chunk):
    w = jnp.tril(p["gm_ws"][:, :chunk, :chunk])
    bs = jnp.repeat(p["gm_bs"][:, :chunk].T, GM_WIDTH // GM_GROUPS, axis=1)
    return w, bs


def _peer(x, p, *, tm_proj, tm_route, tm_dense, te):
    m = x.shape[0]
    dummy = jnp.zeros((tm_proj, HEAD_DIM), F32)
    qp = _proj(x, p["wq"], dummy, dummy, tm=tm_proj, rope_lo=0, rope_hi=0, name="peer_q")
    r2, cnt, f, e2 = _route(qp, p["subkeys"], tm=tm_route)
    return _peer_dense(x, p["u"], p["v"], p["layer"], r2, cnt, f, e2, p["ln2_g"], p["ln2_b"], tm=tm_dense, te=te)


def _kv_rows(h, col, batch, seq):
    return h[:, col:col + KV_WIDTH].reshape(batch, seq, NSA_KV_HEADS, HEAD_DIM)


def _layer_prompt(x, p, cos, sin, *, batch, seq):
    m = batch * seq
    h, *kv_out = _proj(x, p["w_proj"], cos, sin, tm=1024, rope_lo=ROPE_LO, rope_hi=ROPE_HI, name="proj_in",
                       kv_cols=(C_KC, C_VC, C_KS, C_VS, C_KW, C_VW))
    kc, vc, ks, vs, kw, vw = [a.reshape(batch, seq, NSA_KV_HEADS, HEAD_DIM) for a in kv_out]
    gw, gbs = _gm_weights(p, GM_CHUNK)
    gm, vn = _gmlp(h, p["gm_g"], p["gm_b"], gw.astype(BF16), gbs, chunk=GM_CHUNK)
    nsa = _nsa_prompt(h, p["cmp_w"], batch=batch, seq=seq, tq=256)
    hist = jnp.zeros((batch, CONV_HALO, CONV_WIDTH), F32)
    y, c = _conv(h, hist, p["conv_dw"], p["conv_db"], p["conv_g"], p["conv_b"], p["conv_pw"],
                 batch=batch, seq=seq, tq=256)
    x1 = _out_ln(gm, nsa, y, x, p["w_out"], p["ln1_g"], p["ln1_b"], tm=512)
    x2 = _peer(x1, p, tm_proj=1024, tm_route=512, tm_dense=512, te=PEER_TE)
    keep = min(WINDOW, seq)
    start = ((seq - 1) // GM_CHUNK) * GM_CHUNK
    states = (kc, vc, ks, vs, kw[:, -keep:], vw[:, -keep:],
              c.reshape(batch, seq, CONV_WIDTH)[:, -(CONV_K - 1):], vn.reshape(batch, seq, GM_WIDTH)[:, start:])
    return x2, states


def _layer_sample(x, p, cos, sin, caches, layer, page_table, win_k, win_v, hist30, *, batch, seq, past_len):
    m = batch * seq
    h = _proj(x, p["w_proj"], cos, sin, tm=m, rope_lo=ROPE_LO, rope_hi=ROPE_HI, name="proj_in_s")
    pad_rows = lambda a: jnp.pad(a.reshape(batch, seq, -1), ((0, 0), (0, GM_CHUNK - seq), (0, 0))).reshape(batch * GM_CHUNK, -1)
    gw, gbs = _gm_weights(p, GM_CHUNK)
    hp = pad_rows(h[:, :2 * GM_WIDTH])
    gm, vn = _gmlp(hp, p["gm_g"], p["gm_b"], gw.astype(BF16), gbs, chunk=GM_CHUNK)
    unpad = lambda a: a.reshape(batch, GM_CHUNK, -1)[:, :seq].reshape(m, -1)
    gm, vn = unpad(gm), unpad(vn)
    kvs = [_kv_rows(h, col, batch, seq) for col in (C_KC, C_VC, C_KS, C_VS, C_KW, C_VW)]
    win_buf = win_k.shape[1]
    page = past_len // page_table.shape[1]
    new_rows = lambda col, rows: jnp.pad(h[:, col:col + KV_WIDTH].reshape(batch, seq, KV_WIDTH),
                                         ((0, 0), (0, rows - seq), (0, 0)))
    flat = lambda a: a.reshape(batch, win_buf, KV_WIDTH)
    wk_all = jnp.concatenate([flat(win_k), new_rows(C_KW, HEAD_DIM)], axis=1)
    wv_all = jnp.concatenate([flat(win_v), new_rows(C_VW, HEAD_DIM)], axis=1)
    nsa = _nsa_sample(h, caches, layer, page_table, new_rows(C_KS, page), new_rows(C_VS, page), wk_all, wv_all,
                      p["cmp_w"], batch=batch, seq=seq, past_len=past_len, win_buf=win_buf)
    keep = min(WINDOW, past_len + seq)
    unflat = lambda a: a[:, :win_buf + seq][:, -keep:].reshape(batch, keep, NSA_KV_HEADS, HEAD_DIM)
    win_k, win_v = unflat(wk_all), unflat(wv_all)
    hist = jnp.pad(hist30, ((0, 0), (CONV_HALO - (CONV_K - 1), 0), (0, 0)))
    y, c = _conv(h, hist, p["conv_dw"], p["conv_db"], p["conv_g"], p["conv_b"], p["conv_pw"],
                 batch=batch, seq=seq, tq=seq)
    x1 = _out_ln(gm, nsa, y, x, p["w_out"], p["ln1_g"], p["ln1_b"], tm=m)
    x2 = _peer(x1, p, tm_proj=m, tm_route=m, tm_dense=m, te=PEER_TE)
    conv_state = jnp.concatenate([hist30, c.reshape(batch, seq, CONV_WIDTH)], axis=1)[:, -(CONV_K - 1):]
    start = ((seq - 1) // GM_CHUNK) * GM_CHUNK
    states = (kvs[0], kvs[1], kvs[2], kvs[3], win_k, win_v, conv_state, vn.reshape(batch, seq, GM_WIDTH)[:, start:])
    return x2, states


def kernel(x_prompt, x_sample, cache_cmp_k, cache_cmp_v, cache_slc_k, cache_slc_v, cache_win_k, cache_win_v,
           state_conv, page_table, w_in, gm_ln_g, gm_ln_b, gm_ws, gm_bs, cmp_wk, cmp_wv, conv_dw, conv_db,
           conv_ln_g, conv_ln_b, conv_pw, w_out, ln1_g, ln1_b, peer_wq, peer_subkeys, peer_u, peer_v, ln2_g, ln2_b):
    bp, sp, _ = x_prompt.shape
    bs, ss, _ = x_sample.shape
    depth = w_in.shape[0]
    past_len = page_table.shape[1] * cache_cmp_k.shape[2]
    pos_p = jnp.arange(sp, dtype=jnp.int32)
    pos_s = past_len + jnp.arange(ss, dtype=jnp.int32)
    cos_p, sin_p = _rope_tables(pos_p)
    cos_s, sin_s = _rope_tables(jnp.tile(pos_s, bs))
    xp = x_prompt.reshape(bp * sp, D_MODEL)
    xs = x_sample.reshape(bs * ss, D_MODEL)
    p_states, s_states = [], []
    caches = [c.reshape(-1, HEAD_DIM) for c in (cache_cmp_k, cache_cmp_v, cache_slc_k, cache_slc_v)]
    for l in range(depth):
        p = _prep_weights(l, w_in, gm_ln_g, gm_ln_b, gm_ws, gm_bs, cmp_wk, cmp_wv, conv_dw, conv_db, conv_ln_g,
                          conv_ln_b, conv_pw, w_out, ln1_g, ln1_b, peer_wq, peer_subkeys, peer_u, peer_v, ln2_g, ln2_b)
        xp, sp_state = _layer_prompt(xp, p, cos_p, sin_p, batch=bp, seq=sp)
        xs, ss_state = _layer_sample(xs, p, cos_s, sin_s, caches, l, page_table, cache_win_k[l], cache_win_v[l],
                                     state_conv[l], batch=bs, seq=ss, past_len=past_len)
        p_states.append(sp_state)
        s_states.append(ss_state)
    stack = lambda states, i: jnp.stack([st[i] for st in states], axis=0)
    outs = [xp.reshape(bp, sp, D_MODEL), xs.reshape(bs, ss, D_MODEL)]
    outs += [stack(p_states, i) for i in range(8)]
    outs += [stack(s_states, i) for i in range(8)]
    return tuple(outs)
```

```python
import functools

import numpy as np
import jax
import jax.numpy as jnp
from jax import lax
from jax.experimental import pallas as pl
from jax.experimental.pallas import tpu as pltpu

F32 = jnp.float32
BF16 = jnp.bfloat16

D_MODEL = 2048
HEAD_DIM = 128
NSA_HEADS = 8
NSA_KV_HEADS = 2
GQA_R = NSA_HEADS // NSA_KV_HEADS
GM_WIDTH = 512
GM_GROUPS = 4
GM_CHUNK = 128
CONV_WIDTH = 512
CONV_GROUPS = 4
CONV_K = 31
NSA_WIDTH = NSA_HEADS * HEAD_DIM
KV_WIDTH = NSA_KV_HEADS * HEAD_DIM
CMP_BLOCK = 32
SEL_BLOCK = 64
N_SELECT = 16
WINDOW = 512
FORCED_SCORE = 1.0e4
PEER_HEADS = 8
N_KEYS = 128
PEER_TOPK = 16
SUBKEY_DIM = 128
ROPE_THETA = 10000.0
LN_EPS = 1e-5
DEPTH = 2
ALPHA = (2 * DEPTH) ** 0.25

C_GU, C_GV, C_CA, C_CB, C_Q = 0, 512, 1024, 1536, 2048
C_KC, C_KS, C_KW, C_ROPE_END = 3072, 3328, 3584, 4096
C_VC, C_VS, C_VW, C_GATE = 4096, 4352, 4608, 4864
PROJ_WIDTH = 5120
PROJ_TN = 512
ROPE_LO, ROPE_HI = C_Q // PROJ_TN, C_ROPE_END // PROJ_TN

V7X_VMEM_LIMIT = 56 * 1024 * 1024
NEG_INF = float("-inf")


def _cparams(n_axes, vmem=V7X_VMEM_LIMIT):
    return pltpu.CompilerParams(dimension_semantics=("arbitrary",) * n_axes, vmem_limit_bytes=vmem)


def _ln_lanes(z, g, b):
    mu = jnp.mean(z, axis=-1, keepdims=True)
    d = z - mu
    var = jnp.mean(d * d, axis=-1, keepdims=True)
    return d * lax.rsqrt(var + LN_EPS) * g + b


def _dot_nt(a, b):
    return lax.dot_general(a, b, (((1,), (1,)), ((), ())), preferred_element_type=F32)


def _proj_kernel(x_ref, w_ref, cos_ref, sin_ref, o_ref, *rest, rope_lo, rope_hi, tn, kv_cols):
    kv_refs, xb_ref = rest[:-1], rest[-1]
    j = pl.program_id(1)
    tm = x_ref.shape[0]

    @pl.when(j == 0)
    def _():
        xb_ref[...] = x_ref[...].astype(BF16)

    acc = jnp.dot(xb_ref[...], w_ref[...], preferred_element_type=F32)
    is_rope = jnp.logical_and(j >= rope_lo, j < rope_hi)

    def emit(blocks):
        for c, blk in enumerate(blocks):
            o_ref[:, c * HEAD_DIM:(c + 1) * HEAD_DIM] = blk
        for ref, col in zip(kv_refs, kv_cols):
            @pl.when(j == col // tn)
            def _():
                first = (col % tn) // HEAD_DIM
                for g in range(NSA_KV_HEADS):
                    ref[pl.ds(g, tm, stride=NSA_KV_HEADS), :] = blocks[first + g]

    @pl.when(is_rope)
    def _():
        cos, sin = cos_ref[...], sin_ref[...]
        blocks = [acc[:, c * HEAD_DIM:(c + 1) * HEAD_DIM] for c in range(tn // HEAD_DIM)]
        emit([blk * cos + pltpu.roll(blk, HEAD_DIM // 2, 1) * sin for blk in blocks])

    @pl.when(jnp.logical_not(is_rope))
    def _():
        emit([acc[:, c * HEAD_DIM:(c + 1) * HEAD_DIM] for c in range(tn // HEAD_DIM)])


def _proj(x, w, cos, sin, *, tm, rope_lo, rope_hi, name, kv_cols=()):
    m, k = x.shape
    n = w.shape[1]
    tn = PROJ_TN
    nper = cos.shape[0] // tm
    kern = functools.partial(_proj_kernel, rope_lo=rope_lo, rope_hi=rope_hi, tn=tn, kv_cols=tuple(kv_cols))
    kv_spec = pl.BlockSpec((NSA_KV_HEADS * tm, HEAD_DIM), lambda i, j: (i, 0))
    kv_shape = jax.ShapeDtypeStruct((NSA_KV_HEADS * m, HEAD_DIM), F32)
    outs = pl.pallas_call(
        kern,
        grid=(m // tm, n // tn),
        in_specs=[
            pl.BlockSpec((tm, k), lambda i, j: (i, 0)),
            pl.BlockSpec((k, tn), lambda i, j: (0, j)),
            pl.BlockSpec((tm, HEAD_DIM), lambda i, j: (i % nper, 0)),
            pl.BlockSpec((tm, HEAD_DIM), lambda i, j: (i % nper, 0)),
        ],
        out_specs=[pl.BlockSpec((tm, tn), lambda i, j: (i, j))] + [kv_spec] * len(kv_cols),
        out_shape=[jax.ShapeDtypeStruct((m, n), F32)] + [kv_shape] * len(kv_cols),
        scratch_shapes=[pltpu.VMEM((tm, k), BF16)],
        compiler_params=_cparams(2),
        name=name,
    )(x, w, cos, sin)
    return outs if kv_cols else outs[0]


GM_CHUNKS_PER_STEP = 4


def _gmlp_kernel(gu_ref, gv_ref, g_ref, b_ref, w_ref, bs_ref, o_ref, vn_ref, *, chunk):
    gw = GM_WIDTH // GM_GROUPS
    for n in range(gu_ref.shape[0] // chunk):
        rows = slice(n * chunk, (n + 1) * chunk)
        for h in range(GM_GROUPS):
            sl = slice(h * gw, (h + 1) * gw)
            v = jax.nn.gelu(gv_ref[rows, sl])
            vn = _ln_lanes(v, g_ref[:, sl], b_ref[:, sl])
            vn_ref[rows, sl] = vn
            s = jnp.dot(w_ref[h], vn.astype(BF16), preferred_element_type=F32) + bs_ref[:, sl]
            o_ref[rows, sl] = jax.nn.gelu(gu_ref[rows, sl]) * s


def _gmlp(h, g, b, w_tril, bs_full, *, chunk):
    m = h.shape[0]
    tile = chunk * (GM_CHUNKS_PER_STEP if (m // chunk) % GM_CHUNKS_PER_STEP == 0 else 1)
    blk = lambda c: pl.BlockSpec((tile, GM_WIDTH), lambda i, c=c: (i, c))
    full = lambda a: pl.BlockSpec(a.shape, lambda i: (0,) * a.ndim)
    return pl.pallas_call(
        functools.partial(_gmlp_kernel, chunk=chunk),
        grid=(m // tile,),
        in_specs=[blk(C_GU // GM_WIDTH), blk(C_GV // GM_WIDTH), full(g), full(b), full(w_tril), full(bs_full)],
        out_specs=[pl.BlockSpec((tile, GM_WIDTH), lambda i: (i, 0))] * 2,
        out_shape=[jax.ShapeDtypeStruct((m, GM_WIDTH), F32)] * 2,
        compiler_params=_cparams(1),
        name="gmlp",
    )(h, h, g, b, w_tril, bs_full)


CONV_HALO = 32


def _conv_kernel(ca_ref, cb_ref, cah_ref, cbh_ref, hist_ref, dw_ref, db_ref, g_ref, b_ref, pw_ref,
                 y_ref, c_ref, buf_ref, *, tq):
    i = pl.program_id(1)
    c = ca_ref[...] * jax.nn.sigmoid(cb_ref[...])
    c_ref[...] = c
    halo = cah_ref[...] * jax.nn.sigmoid(cbh_ref[...])
    halo = jnp.where(i == 0, hist_ref[0], halo)
    buf_ref[0, 0:CONV_HALO, :] = halo
    buf_ref[0, CONV_HALO:CONV_HALO + tq, :] = c
    sub = 8
    span = tq + CONV_HALO - sub
    for r in range(1, sub):
        buf_ref[r, 0:span, :] = buf_ref[0, r:r + span, :]
    lead = CONV_HALO - (CONV_K - 1)
    acc = jnp.zeros((tq, CONV_WIDTH), F32)
    for k in range(CONV_K):
        off = lead + k
        base = off - off % sub
        acc = acc + dw_ref[k:k + 1, :] * buf_ref[off % sub, base:base + tq, :]
    y = acc + db_ref[...]
    cw = CONV_WIDTH // CONV_GROUPS
    parts = []
    for gi in range(CONV_GROUPS):
        sl = slice(gi * cw, (gi + 1) * cw)
        yn = _ln_lanes(y[:, sl], g_ref[:, sl], b_ref[:, sl])
        parts.append((yn * jax.nn.sigmoid(yn)).astype(BF16))
    y_ref[...] = jnp.dot(jnp.concatenate(parts, axis=-1), pw_ref[...], preferred_element_type=F32)


def _conv(h, hist, dw, db, g, b, pw, *, batch, seq, tq):
    m = h.shape[0]
    nt = seq // tq
    row = lambda bi, i: bi * nt + i
    halo_row = lambda bi, i: jnp.maximum((bi * seq + i * tq) // CONV_HALO - 1, 0)
    full = lambda a: pl.BlockSpec(a.shape, lambda bi, i: (0,) * a.ndim)
    kern = functools.partial(_conv_kernel, tq=tq)
    return pl.pallas_call(
        kern,
        grid=(batch, nt),
        in_specs=[
            pl.BlockSpec((tq, CONV_WIDTH), lambda bi, i: (row(bi, i), C_CA // CONV_WIDTH)),
            pl.BlockSpec((tq, CONV_WIDTH), lambda bi, i: (row(bi, i), C_CB // CONV_WIDTH)),
            pl.BlockSpec((CONV_HALO, CONV_WIDTH), lambda bi, i: (halo_row(bi, i), C_CA // CONV_WIDTH)),
            pl.BlockSpec((CONV_HALO, CONV_WIDTH), lambda bi, i: (halo_row(bi, i), C_CB // CONV_WIDTH)),
            pl.BlockSpec((1, CONV_HALO, CONV_WIDTH), lambda bi, i: (bi, 0, 0)),
            full(dw), full(db), full(g), full(b), full(pw),
        ],
        out_specs=[pl.BlockSpec((tq, CONV_WIDTH), lambda bi, i: (row(bi, i), 0))] * 2,
        out_shape=[jax.ShapeDtypeStruct((m, CONV_WIDTH), F32)] * 2,
        scratch_shapes=[pltpu.VMEM((8, CONV_HALO + tq, CONV_WIDTH), F32)],
        compiler_params=_cparams(2),
        name="conv",
    )(h, h, h, h, hist, dw, db, g, b, pw)


def _masked_softmax(s, mask):
    s = jnp.where(mask, s, NEG_INF)
    m = jnp.max(s, axis=-1, keepdims=True)
    m = jnp.where(m > NEG_INF, m, 0.0)
    e = jnp.exp(s - m)
    return e / jnp.maximum(jnp.sum(e, axis=-1, keepdims=True), 1e-30)


def _select_blocks(imp, tpos, n_blocks):
    tq, lanes = imp.shape
    blk = lax.broadcasted_iota(jnp.int32, (tq, lanes), 1)
    cur = tpos // SEL_BLOCK
    valid = blk * SEL_BLOCK <= tpos
    forced = jnp.logical_or(blk == 0, jnp.logical_or(blk == cur, blk == cur - 1))
    score = jnp.where(valid, jnp.where(forced, FORCED_SCORE, imp), NEG_INF)
    beaten = jnp.zeros((tq, lanes), F32)
    for i in range(n_blocks):
        ci = score[:, i:i + 1]
        earlier = (blk > i).astype(F32)
        beaten = beaten + jnp.where(ci > score, 1.0, jnp.where(ci == score, earlier, 0.0))
    picked = jnp.logical_and(beaten < float(min(N_SELECT, n_blocks)), blk < n_blocks)
    return picked.astype(F32)


SLC_CHUNK = 512


def _online_softmax_step(m_ref, l_ref, acc_ref, r, s, mask, v):
    s = jnp.where(mask, s, NEG_INF)
    m_old = m_ref[r]
    m_new = jnp.maximum(m_old, jnp.max(s, axis=-1, keepdims=True))
    m_safe = jnp.where(m_new > NEG_INF, m_new, 0.0)
    alpha = jnp.exp(m_old - m_safe)
    e = jnp.exp(s - m_safe)
    l_ref[r] = alpha * l_ref[r] + jnp.sum(e, axis=-1, keepdims=True)
    acc_ref[r] = alpha * acc_ref[r] + jnp.dot(e.astype(BF16), v, preferred_element_type=F32)
    m_ref[r] = m_new


VT_CHUNK = 256


def _select_blocks_t(imp_t, tpos_row, n_blocks):
    nb, tq = imp_t.shape
    blk = lax.broadcasted_iota(jnp.int32, (nb, tq), 0)
    cur = tpos_row // SEL_BLOCK
    valid = blk * SEL_BLOCK <= tpos_row
    forced = jnp.logical_or(blk == 0, jnp.logical_or(blk == cur, blk == cur - 1))
    score = jnp.where(valid, jnp.where(forced, FORCED_SCORE, imp_t), NEG_INF)
    beaten = jnp.zeros((nb, tq), F32)
    for i in range(n_blocks):
        ci = score[i:i + 1, :]
        earlier = (blk > i).astype(F32)
        beaten = beaten + jnp.where(ci > score, 1.0, jnp.where(ci == score, earlier, 0.0))
    return (beaten < float(min(N_SELECT, n_blocks))).astype(F32)


def _col_softmax(s_t, mask_t):
    s = jnp.where(mask_t, s_t, NEG_INF)
    m = jnp.max(s, axis=0, keepdims=True)
    m = jnp.where(m > NEG_INF, m, 0.0)
    e = jnp.exp(s - m)
    return e, jnp.sum(e, axis=0, keepdims=True)


def _col_softmax_step(m_ref, l_ref, acc_ref, r, s_t, mask_t, v_t):
    s = jnp.where(mask_t, s_t, NEG_INF)
    m_old = m_ref[r]
    m_new = jnp.maximum(m_old, jnp.max(s, axis=0, keepdims=True))
    m_safe = jnp.where(m_new > NEG_INF, m_new, 0.0)
    alpha = jnp.exp(m_old - m_safe)
    e = jnp.exp(s - m_safe)
    l_ref[r] = alpha * l_ref[r] + jnp.sum(e, axis=0, keepdims=True)
    acc_ref[r] = alpha * acc_ref[r] + jnp.dot(v_t, e.astype(BF16), preferred_element_type=F32)
    m_ref[r] = m_new


def _nsa_prompt_kernel(q_ref, kc_ref, ks_ref, kw_ref, vc_ref, vs_ref, vw_ref, gate_ref, cw_ref, o_ref,
                       kb_ref, vt_ref, kcmp_ref, vcmpt_ref, m_ref, l_ref, acc_ref, *, tq, seq):
    qi = pl.program_id(1)
    t0 = qi * tq
    scale = HEAD_DIM ** -0.5
    n_slc = seq // SEL_BLOCK
    tpos = t0 + lax.broadcasted_iota(jnp.int32, (1, tq), 1)
    row_c = lax.broadcasted_iota(jnp.int32, (2 * n_slc, tq), 0)
    cmp_last = jnp.where(row_c < n_slc, row_c * SEL_BLOCK + CMP_BLOCK - 1, (row_c - n_slc) * SEL_BLOCK + SEL_BLOCK - 1)
    cmp_mask = cmp_last <= tpos
    win_len = tq + WINDOW
    w0 = pl.multiple_of(jnp.maximum(t0 - WINDOW, 0), tq)
    wpos = w0 + lax.broadcasted_iota(jnp.int32, (win_len, tq), 0)
    win_mask = jnp.logical_and(wpos <= tpos, wpos >= tpos - WINDOW)
    gates_t = jax.nn.sigmoid(gate_ref[:, 0:HEAD_DIM]).T

    @pl.when(qi == 0)
    def _():
        kb_ref[0] = ks_ref[0].astype(BF16)
        kb_ref[1] = kw_ref[0].astype(BF16)
        for g in range(NSA_KV_HEADS):
            gs = slice(g * HEAD_DIM, (g + 1) * HEAD_DIM)
            for n, ref in enumerate((vs_ref, vw_ref)):
                for c in range(seq // VT_CHUNK):
                    vt_ref[n, g, c] = ref[0, c * VT_CHUNK:(c + 1) * VT_CHUNK, gs].T.astype(BF16)
            halves = []
            for which, ref in enumerate((kc_ref, vc_ref)):
                x3 = ref[0, :, gs].reshape(n_slc, SEL_BLOCK, HEAD_DIM)
                ev = jnp.sum(x3 * cw_ref[g, which, 0][None], axis=1)
                od = jnp.sum(x3 * cw_ref[g, which, 1][None], axis=1)
                halves.append(jnp.concatenate([ev, od], axis=0))
            kcmp_ref[g] = halves[0].astype(BF16)
            vcmpt_ref[g] = jnp.concatenate([halves[1], jnp.zeros_like(halves[1])], axis=0).T.astype(BF16)

    for g in range(NSA_KV_HEADS):
        gs = slice(g * HEAD_DIM, (g + 1) * HEAD_DIM)
        kcmp, vcmp_t = kcmp_ref[g], vcmpt_ref[g][:, :2 * n_slc]
        qg = jnp.concatenate([(q_ref[:, (g * GQA_R + r) * HEAD_DIM:(g * GQA_R + r + 1) * HEAD_DIM] * scale).astype(BF16)
                              for r in range(GQA_R)], axis=0)
        heads = lambda a: jnp.concatenate([a] * GQA_R, axis=1)
        e, l = _col_softmax(_dot_nt(kcmp, qg), heads(cmp_mask))
        p = e / jnp.maximum(l, 1e-30)
        o_cmp = jnp.dot(vcmp_t, p.astype(BF16), preferred_element_type=F32)
        imp = p[:, 0:tq]
        for r in range(1, GQA_R):
            imp = imp + p[:, r * tq:(r + 1) * tq]
        sel = _select_blocks_t(imp[:n_slc] + imp[n_slc:], tpos, n_slc).astype(BF16)

        m_ref[...] = jnp.full(m_ref.shape, NEG_INF, F32)
        l_ref[...] = jnp.zeros_like(l_ref)
        acc_ref[...] = jnp.zeros_like(acc_ref)
        for j in range(seq // SLC_CHUNK):
            @pl.when(j * SLC_CHUNK < t0 + tq)
            def _():
                kpos = j * SLC_CHUNK + lax.broadcasted_iota(jnp.int32, (SLC_CHUNK, tq), 0)
                key_blk = (j * SLC_CHUNK + lax.broadcasted_iota(jnp.int32, (SLC_CHUNK, n_slc), 0)) // SEL_BLOCK
                expand = (lax.broadcasted_iota(jnp.int32, (SLC_CHUNK, n_slc), 1) == key_blk).astype(BF16)
                sel_keys = jnp.dot(expand, sel, preferred_element_type=F32)
                mask = jnp.logical_and(sel_keys > 0.5, kpos <= tpos)
                k = kb_ref[0, j * SLC_CHUNK:(j + 1) * SLC_CHUNK, gs]
                per = SLC_CHUNK // VT_CHUNK
                v_t = jnp.concatenate([vt_ref[0, g, j * per + c] for c in range(per)], axis=1)
                _col_softmax_step(m_ref, l_ref, acc_ref, 0, _dot_nt(k, qg), heads(mask), v_t)

        kwin = kb_ref[1, pl.ds(w0, win_len), gs]
        wc0 = w0 // VT_CHUNK
        vwin_t = jnp.concatenate([vt_ref[1, g, wc0 + c] for c in range(win_len // VT_CHUNK)], axis=1)
        o_slc = acc_ref[0] / jnp.maximum(l_ref[0], 1e-30)
        e, l = _col_softmax(_dot_nt(kwin, qg), heads(win_mask))
        o_win = jnp.dot(vwin_t, e.astype(BF16), preferred_element_type=F32) / jnp.maximum(l, 1e-30)
        for r in range(GQA_R):
            hh = g * GQA_R + r
            cs = slice(r * tq, (r + 1) * tq)
            o_t = (gates_t[3 * hh:3 * hh + 1] * o_cmp[:, cs] + gates_t[3 * hh + 1:3 * hh + 2] * o_slc[:, cs]
                   + gates_t[3 * hh + 2:3 * hh + 3] * o_win[:, cs])
            o_ref[:, hh * HEAD_DIM:(hh + 1) * HEAD_DIM] = o_t.T


def _nsa_prompt(h, cmp_w, *, batch, seq, tq):
    m = h.shape[0]
    nt = seq // tq
    kv = lambda col: pl.BlockSpec((1, seq, KV_WIDTH), lambda bi, i, col=col: (bi, 0, col // KV_WIDTH))
    h3 = h.reshape(batch, seq, PROJ_WIDTH)
    kern = functools.partial(_nsa_prompt_kernel, tq=tq, seq=seq)
    return pl.pallas_call(
        kern,
        grid=(batch, nt),
        in_specs=[
            pl.BlockSpec((tq, NSA_WIDTH), lambda bi, i: (bi * nt + i, C_Q // NSA_WIDTH)),
            kv(C_KC), kv(C_KS), kv(C_KW), kv(C_VC), kv(C_VS), kv(C_VW),
            pl.BlockSpec((tq, KV_WIDTH), lambda bi, i: (bi * nt + i, C_GATE // KV_WIDTH)),
            pl.BlockSpec(cmp_w.shape, lambda bi, i: (0,) * cmp_w.ndim),
        ],
        out_specs=pl.BlockSpec((tq, NSA_WIDTH), lambda bi, i: (bi * nt + i, 0)),
        out_shape=jax.ShapeDtypeStruct((m, NSA_WIDTH), F32),
        scratch_shapes=[
            pltpu.VMEM((2, seq, KV_WIDTH), BF16),
            pltpu.VMEM((2, NSA_KV_HEADS, seq // VT_CHUNK, HEAD_DIM, VT_CHUNK), BF16),
            pltpu.VMEM((NSA_KV_HEADS, 2 * (seq // SEL_BLOCK), HEAD_DIM), BF16),
            pltpu.VMEM((NSA_KV_HEADS, HEAD_DIM, 4 * (seq // SEL_BLOCK)), BF16),
            pltpu.VMEM((1, 1, GQA_R * tq), F32), pltpu.VMEM((1, 1, GQA_R * tq), F32),
            pltpu.VMEM((1, HEAD_DIM, GQA_R * tq), F32),
        ],
        compiler_params=_cparams(2),
        name="nsa_prompt",
    )(h, h3, h3, h3, h3, h3, h3, h, cmp_w)


def _out_ln_kernel(gm_ref, nsa_ref, y_ref, x_ref, w_ref, g_ref, b_ref, o_ref):
    cat = jnp.concatenate([gm_ref[...].astype(BF16), nsa_ref[...].astype(BF16), y_ref[...].astype(BF16)], axis=-1)
    mix = jnp.dot(cat, w_ref[...], preferred_element_type=F32)
    o_ref[...] = _ln_lanes(ALPHA * x_ref[...] + mix, g_ref[...], b_ref[...])


def _out_ln(gm, nsa, y, x, w, g, b, *, tm):
    m = x.shape[0]
    row = lambda width: pl.BlockSpec((tm, width), lambda i: (i, 0))
    full = lambda a: pl.BlockSpec(a.shape, lambda i: (0,) * a.ndim)
    return pl.pallas_call(
        _out_ln_kernel,
        grid=(m // tm,),
        in_specs=[row(GM_WIDTH), row(NSA_WIDTH), row(CONV_WIDTH), row(D_MODEL), full(w), full(g), full(b)],
        out_specs=row(D_MODEL),
        out_shape=jax.ShapeDtypeStruct((m, D_MODEL), F32),
        compiler_params=_cparams(1),
        name="out_ln",
    )(gm, nsa, y, x, w, g, b)


def _extract_topk(s, exact_ties, want_rank):
    n_rows, tm = s.shape
    rows = lax.broadcasted_iota(jnp.int32, (n_rows, tm), 0)
    krow = lax.broadcasted_iota(jnp.int32, (PEER_TOPK, tm), 0)
    work = s
    rank = jnp.full((n_rows, tm), float(PEER_TOPK), F32) if want_rank else None
    vals = jnp.zeros((PEER_TOPK, tm), F32)
    for r in range(PEER_TOPK):
        mx = jnp.max(work, axis=0, keepdims=True)
        hit = work == mx
        if exact_ties:
            hit = rows == jnp.min(jnp.where(hit, rows, n_rows), axis=0, keepdims=True)
        if want_rank:
            rank = jnp.where(hit, float(r), rank)
        vals = jnp.where(krow == r, mx, vals)
        work = jnp.where(hit, NEG_INF, work)
    taken = jnp.sum((work == NEG_INF).astype(F32), axis=0, keepdims=True)
    return vals, rank, taken


def _pair_topk(v1, v2):
    tm = v1.shape[1]
    krow = lax.broadcasted_iota(jnp.int32, (PEER_TOPK, tm), 0)
    cnt = jnp.zeros((PEER_TOPK, tm), F32)
    front = v1 + v2[0:1]
    m0 = v1[0:1] + v2[0:1]
    z = jnp.zeros((1, tm), F32)
    for _ in range(PEER_TOPK):
        mx = jnp.max(front, axis=0, keepdims=True)
        idx = jnp.min(jnp.where(front == mx, krow, PEER_TOPK), axis=0, keepdims=True)
        hit = krow == idx
        cnt = cnt + hit.astype(F32)
        z = z + jnp.exp(mx - m0)
        chosen = jnp.sum(jnp.where(hit, cnt, 0.0), axis=0, keepdims=True)
        nxt = jnp.full((1, tm), NEG_INF, F32)
        for c in range(1, PEER_TOPK):
            nxt = jnp.where(chosen == float(c), v2[c:c + 1], nxt)
        front = jnp.where(hit, v1 + nxt, front)
    return cnt, z


def _route_kernel(x_ref, wq_ref, sk_ref, r2_ref, cnt_ref, f_ref, e2_ref, xb_ref):
    @pl.when(pl.program_id(1) == 0)
    def _():
        xb_ref[...] = x_ref[...].astype(BF16)

    qp = jnp.dot(xb_ref[...], wq_ref[...], preferred_element_type=F32)
    q1 = qp[:, 0:SUBKEY_DIM].astype(BF16)
    q2 = qp[:, SUBKEY_DIM:2 * SUBKEY_DIM].astype(BF16)
    s1 = _dot_nt(sk_ref[0, 0], q1)
    s2 = _dot_nt(sk_ref[0, 1], q2)

    def emit(v1, v2, cnt, z, cnt_a, rank2):
        r2_ref[0] = rank2.astype(BF16)
        cnt_ref[0] = cnt_a
        f_ref[0] = jnp.exp(s1 - v1[0:1]) / z
        e2_ref[0] = jnp.exp(s2 - v2[0:1]).astype(BF16)

    v1, _, taken1 = _extract_topk(s1, exact_ties=False, want_rank=False)
    v2, rank2, taken2 = _extract_topk(s2, exact_ties=False, want_rank=True)
    cnt, z = _pair_topk(v1, v2)
    cnt_a = jnp.zeros(s1.shape, F32)
    for r in range(PEER_TOPK):
        cnt_a = jnp.where(s1 == v1[r:r + 1], cnt[r:r + 1], cnt_a)
    emit(v1, v2, cnt, z, cnt_a, rank2)

    @pl.when(jnp.max(jnp.maximum(taken1, taken2)) > float(PEER_TOPK))
    def _():
        v1, rank1, _ = _extract_topk(s1, exact_ties=True, want_rank=True)
        v2, rank2, _ = _extract_topk(s2, exact_ties=True, want_rank=True)
        cnt, z = _pair_topk(v1, v2)
        cnt_a = jnp.zeros(s1.shape, F32)
        for r in range(PEER_TOPK):
            cnt_a = jnp.where(rank1 == float(r), cnt[r:r + 1], cnt_a)
        emit(v1, v2, cnt, z, cnt_a, rank2)


def _route(x, wq, subkeys, *, tm):
    m, k = x.shape
    out = lambda dt: jax.ShapeDtypeStruct((PEER_HEADS, N_KEYS, m), dt)
    ospec = pl.BlockSpec((1, N_KEYS, tm), lambda i, hd: (hd, 0, i))
    return pl.pallas_call(
        _route_kernel,
        grid=(m // tm, PEER_HEADS),
        in_specs=[
            pl.BlockSpec((tm, k), lambda i, hd: (i, 0)),
            pl.BlockSpec((k, 2 * SUBKEY_DIM), lambda i, hd: (0, hd)),
            pl.BlockSpec((1, 2, N_KEYS, SUBKEY_DIM), lambda i, hd: (hd, 0, 0, 0)),
        ],
        out_specs=[ospec] * 4,
        out_shape=[out(BF16), out(F32), out(F32), out(BF16)],
        scratch_shapes=[pltpu.VMEM((tm, k), BF16)],
        compiler_params=_cparams(2),
        name="peer_route",
    )(x, wq, subkeys)


PEER_SUB = 256
PEER_TE = 1024


def _peer_dense_kernel(x_ref, u_ref, v_ref, r2_ref, cnt_ref, f_ref, e2_ref, g_ref, b_ref, o_ref, xb_ref, *, te):
    j = pl.program_id(1)

    @pl.when(j == 0)
    def _():
        xb_ref[...] = x_ref[...].astype(BF16)
        o_ref[...] = jnp.zeros_like(o_ref)

    cols = []
    for c in range(te // PEER_SUB):
        act = jax.nn.gelu(_dot_nt(u_ref[c * PEER_SUB:(c + 1) * PEER_SUB, :], xb_ref[...]).astype(BF16))
        rows = []
        for al in range(PEER_SUB // N_KEYS):
            a = c * (PEER_SUB // N_KEYS) + al
            gate = None
            for hd in range(PEER_HEADS):
                cnt = cnt_ref[hd, a:a + 1, :].astype(BF16)
                f = f_ref[hd, a:a + 1, :].astype(BF16)
                term = jnp.where(r2_ref[hd] < cnt, e2_ref[hd], jnp.zeros((), BF16)) * f
                gate = term if gate is None else gate + term
            rows.append(act[al * N_KEYS:(al + 1) * N_KEYS] * gate)
        cols.append(jnp.concatenate(rows, axis=0).T)
    o_ref[...] += jnp.dot(jnp.concatenate(cols, axis=1), v_ref[...], preferred_element_type=F32)

    @pl.when(j == pl.num_programs(1) - 1)
    def _():
        o_ref[...] = _ln_lanes(ALPHA * x_ref[...] + o_ref[...], g_ref[...], b_ref[...])


def _peer_dense(x, u, v, layer, r2, cnt, f, e2, g, b, *, tm, te):
    m = x.shape[0]
    n_exp = u.shape[0] // DEPTH
    first = layer * (n_exp // te)
    by_b = pl.BlockSpec((PEER_HEADS, N_KEYS, tm), lambda i, j: (0, 0, i))
    by_a = pl.BlockSpec((PEER_HEADS, te // N_KEYS, tm), lambda i, j: (0, j, i))
    full = lambda a: pl.BlockSpec(a.shape, lambda i, j: (0,) * a.ndim)
    kern = functools.partial(_peer_dense_kernel, te=te)
    return pl.pallas_call(
        kern,
        grid=(m // tm, n_exp // te),
        in_specs=[
            pl.BlockSpec((tm, D_MODEL), lambda i, j: (i, 0), pipeline_mode=pl.Buffered(1)),
            pl.BlockSpec((te, D_MODEL), lambda i, j: (first + j, 0)),
            pl.BlockSpec((te, D_MODEL), lambda i, j: (first + j, 0)),
            by_b, by_a, by_a, by_b, full(g), full(b),
        ],
        out_specs=pl.BlockSpec((tm, D_MODEL), lambda i, j: (i, 0), pipeline_mode=pl.Buffered(1)),
        out_shape=jax.ShapeDtypeStruct((m, D_MODEL), F32),
        scratch_shapes=[pltpu.VMEM((tm, D_MODEL), BF16)],
        compiler_params=_cparams(2),
        name="peer_dense",
    )(x, u, v, r2, cnt, f, e2, g, b)


PAGES_PER_STEP = 16


def _group_queries(q_ref, g):
    heads = [q_ref[:, (g * GQA_R + r) * HEAD_DIM:(g * GQA_R + r + 1) * HEAD_DIM] for r in range(GQA_R)]
    return jnp.concatenate(heads, axis=0).astype(BF16)


def _page_rows(pages, g):
    n = pages[0].shape[0] // NSA_KV_HEADS
    return jnp.concatenate([r[pl.ds(g, n, stride=NSA_KV_HEADS), :] for r in pages], axis=0)


def _nsa_s_cmp_kernel(pt_ref, q_ref, cw_ref, *refs, seq, past_len, n_real, sel_lanes):
    nps = PAGES_PER_STEP
    k_pages, v_pages = refs[:nps], refs[nps:2 * nps]
    ocmp_ref, sel_ref, ke_ref, ko_ref, ve_ref, vo_ref = refs[2 * nps:]
    c = pl.program_id(1)
    n_pb = ke_ref.shape[0]
    rows = nps * (k_pages[0].shape[0] // NSA_KV_HEADS) // SEL_BLOCK
    off = pl.multiple_of(c * rows, rows)

    def compress(pages, which, even_ref, odd_ref):
        for g in range(NSA_KV_HEADS):
            gs = slice(g * HEAD_DIM, (g + 1) * HEAD_DIM)
            x3 = _page_rows(pages, g).reshape(rows, SEL_BLOCK, HEAD_DIM)
            even_ref[pl.ds(off, rows), gs] = jnp.sum(x3 * cw_ref[g, which, 0][None], axis=1)
            odd_ref[pl.ds(off, rows), gs] = jnp.sum(x3 * cw_ref[g, which, 1][None], axis=1)

    compress(k_pages, 0, ke_ref, ko_ref)
    compress(v_pages, 1, ve_ref, vo_ref)

    @pl.when(c == pl.num_programs(1) - 1)
    def _():
        scale = HEAD_DIM ** -0.5
        nq = GQA_R * seq
        qrow = lax.broadcasted_iota(jnp.int32, (nq, 1), 0)
        qpos = past_len + qrow % seq
        lane = lax.broadcasted_iota(jnp.int32, (nq, 2 * n_pb), 1)
        cmp_last = jnp.where(lane < n_pb, lane * SEL_BLOCK + CMP_BLOCK - 1, (lane - n_pb) * SEL_BLOCK + SEL_BLOCK - 1)
        cmp_mask = cmp_last <= qpos
        tpos = past_len + lax.broadcasted_iota(jnp.int32, (seq, 1), 0)
        for g in range(NSA_KV_HEADS):
            gs = slice(g * HEAD_DIM, (g + 1) * HEAD_DIM)
            kcat = jnp.concatenate([ke_ref[:, gs], ko_ref[:, gs]], axis=0).astype(BF16)
            vcat = jnp.concatenate([ve_ref[:, gs], vo_ref[:, gs]], axis=0).astype(BF16)
            p = _masked_softmax(_dot_nt(_group_queries(q_ref, g), kcat) * scale, cmp_mask)
            ocmp_ref[0, g] = jnp.dot(p.astype(BF16), vcat, preferred_element_type=F32)
            imp = p[0:seq]
            for r in range(1, GQA_R):
                imp = imp + p[r * seq:(r + 1) * seq]
            imp = imp[:, :n_pb] + imp[:, n_pb:]
            imp = jnp.concatenate([imp, jnp.zeros((seq, sel_lanes - n_pb), F32)], axis=1)
            sel = _select_blocks(imp, tpos, n_real)
            bpc = sel_ref.shape[-1]
            for ch in range(sel_ref.shape[2]):
                sel_ref[0, g, ch] = sel[:, ch * bpc:(ch + 1) * bpc]


def _nsa_s_slc_kernel(pt_ref, q_ref, sel_ref, sel_new_ref, ocmp_ref, ksn_ref, vsn_ref, wk_ref, wv_ref, gate_ref, *refs,
                      seq, past_len, win_buf):
    nps = PAGES_PER_STEP
    k_pages, v_pages = refs[:nps], refs[nps:2 * nps]
    o_ref, m_ref, l_ref, acc_ref = refs[2 * nps:]
    c = pl.program_id(1)
    scale = HEAD_DIM ** -0.5
    nq = GQA_R * seq
    chunk = nps * (k_pages[0].shape[0] // NSA_KV_HEADS)
    qrow = lax.broadcasted_iota(jnp.int32, (nq, 1), 0)
    qpos = past_len + qrow % seq

    @pl.when(c == 0)
    def _():
        m_ref[...] = jnp.full(m_ref.shape, NEG_INF, F32)
        l_ref[...] = jnp.zeros_like(l_ref)
        acc_ref[...] = jnp.zeros_like(acc_ref)

    def online_update(g, s, mask, v):
        _online_softmax_step(m_ref, l_ref, acc_ref, g, s, mask, v)

    def key_mask(sel_blocks, first_pos, n_keys):
        bpc = sel_blocks.shape[1]
        key_blk = lax.broadcasted_iota(jnp.int32, (bpc, n_keys), 1) // SEL_BLOCK
        expand = (lax.broadcasted_iota(jnp.int32, (bpc, n_keys), 0) == key_blk).astype(BF16)
        sel_keys = jnp.dot(sel_blocks.astype(BF16), expand, preferred_element_type=F32)
        sel_keys = jnp.concatenate([sel_keys] * GQA_R, axis=0)
        kpos = first_pos + lax.broadcasted_iota(jnp.int32, (nq, n_keys), 1)
        return jnp.logical_and(sel_keys > 0.5, kpos <= qpos)

    for g in range(NSA_KV_HEADS):
        mask = key_mask(sel_ref[0, g, 0], c * chunk, chunk)
        scores = _dot_nt(_group_queries(q_ref, g), _page_rows(k_pages, g).astype(BF16)) * scale
        online_update(g, scores, mask, _page_rows(v_pages, g).astype(BF16))

    @pl.when(c == pl.num_programs(1) - 1)
    def _():
        gates = jax.nn.sigmoid(gate_ref[:, 0:HEAD_DIM])
        n_new = ksn_ref.shape[1]
        n_win = wk_ref.shape[1]
        wpos = past_len - win_buf + lax.broadcasted_iota(jnp.int32, (nq, n_win), 1)
        win_mask = jnp.logical_and(wpos <= qpos, wpos >= qpos - WINDOW)
        for g in range(NSA_KV_HEADS):
            gs = slice(g * HEAD_DIM, (g + 1) * HEAD_DIM)
            qg = _group_queries(q_ref, g)
            online_update(g, _dot_nt(qg, ksn_ref[0, :, gs].astype(BF16)) * scale,
                          key_mask(sel_new_ref[0, g, 0], past_len, n_new), vsn_ref[0, :, gs].astype(BF16))
            o_slc = acc_ref[g] / jnp.maximum(l_ref[g], 1e-30)
            p = _masked_softmax(_dot_nt(qg, wk_ref[0, :, gs].astype(BF16)) * scale, win_mask)
            o_win = jnp.dot(p.astype(BF16), wv_ref[0, :, gs].astype(BF16), preferred_element_type=F32)
            o_cmp = ocmp_ref[0, g]
            for r in range(GQA_R):
                hh = g * GQA_R + r
                rs = slice(r * seq, (r + 1) * seq)
                o_ref[:, hh * HEAD_DIM:(hh + 1) * HEAD_DIM] = (
                    gates[:, 3 * hh:3 * hh + 1] * o_cmp[rs] + gates[:, 3 * hh + 1:3 * hh + 2] * o_slc[rs]
                    + gates[:, 3 * hh + 2:3 * hh + 3] * o_win[rs])


def _nsa_sample(h, caches, layer, page_table, ks_new, vs_new, wk_all, wv_all, cw, *, batch, seq, past_len, win_buf):
    cmp_k, cmp_v, slc_k, slc_v = caches
    nps = PAGES_PER_STEP
    n_pages, page = page_table.shape[1], ks_new.shape[1]
    page_rows = page * NSA_KV_HEADS
    n_pool = cmp_k.shape[0] // (DEPTH * page_rows)
    n_chunks = n_pages // nps
    n_pb = past_len // SEL_BLOCK
    n_real = n_pb + 1
    bpc = nps * page // SEL_BLOCK
    sel_lanes = -(-max(n_real, (n_chunks + 1) * bpc) // 128) * 128
    sel_shape = (NSA_KV_HEADS, n_chunks + 1, seq, bpc)
    sel_chunk = lambda which: pl.BlockSpec((1, NSA_KV_HEADS, 1, seq, bpc), lambda b, c, pt: (b, 0, which(c), 0, 0))
    nq = GQA_R * seq
    page_spec = lambda k: pl.BlockSpec((page_rows, HEAD_DIM),
                                       lambda b, c, pt, k=k: (pt[b, c * nps + k] + layer * n_pool, 0))
    pages = [page_spec(k) for k in range(nps)]
    q_spec = pl.BlockSpec((seq, NSA_WIDTH), lambda b, c, pt: (b, C_Q // NSA_WIDTH))
    per_b = lambda shape: pl.BlockSpec((1,) + shape, lambda b, c, pt: (b,) + (0,) * len(shape))
    ocmp, sel = pl.pallas_call(
        functools.partial(_nsa_s_cmp_kernel, seq=seq, past_len=past_len, n_real=n_real, sel_lanes=sel_lanes),
        grid_spec=pltpu.PrefetchScalarGridSpec(
            num_scalar_prefetch=1, grid=(batch, n_chunks),
            in_specs=[q_spec, pl.BlockSpec(cw.shape, lambda b, c, pt: (0,) * cw.ndim)] + pages + pages,
            out_specs=[per_b((NSA_KV_HEADS, nq, HEAD_DIM)), per_b(sel_shape)],
            scratch_shapes=[pltpu.VMEM((n_pb, KV_WIDTH), F32)] * 4),
        out_shape=[jax.ShapeDtypeStruct((batch, NSA_KV_HEADS, nq, HEAD_DIM), F32),
                   jax.ShapeDtypeStruct((batch,) + sel_shape, F32)],
        compiler_params=_cparams(2),
        name="nsa_sample_cmp",
    )(page_table, h, cw, *([cmp_k] * nps), *([cmp_v] * nps))
    return pl.pallas_call(
        functools.partial(_nsa_s_slc_kernel, seq=seq, past_len=past_len, win_buf=win_buf),
        grid_spec=pltpu.PrefetchScalarGridSpec(
            num_scalar_prefetch=1, grid=(batch, n_chunks),
            in_specs=[q_spec, sel_chunk(lambda c: c), sel_chunk(lambda c: n_chunks),
                      per_b((NSA_KV_HEADS, nq, HEAD_DIM)),
                      per_b(ks_new.shape[1:]), per_b(vs_new.shape[1:]), per_b(wk_all.shape[1:]), per_b(wv_all.shape[1:]),
                      pl.BlockSpec((seq, KV_WIDTH), lambda b, c, pt: (b, C_GATE // KV_WIDTH))] + pages + pages,
            out_specs=pl.BlockSpec((seq, NSA_WIDTH), lambda b, c, pt: (b, 0)),
            scratch_shapes=[pltpu.VMEM((NSA_KV_HEADS, nq, 1), F32), pltpu.VMEM((NSA_KV_HEADS, nq, 1), F32),
                            pltpu.VMEM((NSA_KV_HEADS, nq, HEAD_DIM), F32)]),
        out_shape=jax.ShapeDtypeStruct((batch * seq, NSA_WIDTH), F32),
        compiler_params=_cparams(2),
        name="nsa_sample_slc",
    )(page_table, h, sel, sel, ocmp, ks_new, vs_new, wk_all, wv_all, h, *([slc_k] * nps), *([slc_v] * nps))


def _rope_tables(pos):
    half = HEAD_DIM // 2
    inv = ROPE_THETA ** (-jnp.arange(half, dtype=F32) / half)
    ang = pos.astype(F32)[:, None] * inv[None, :]
    cos, sin = jnp.cos(ang), jnp.sin(ang)
    return jnp.concatenate([cos, cos], -1), jnp.concatenate([-sin, sin], -1)


def _prep_weights(l, w_in, gm_ln_g, gm_ln_b, gm_ws, gm_bs, cmp_wk, cmp_wv, conv_dw, conv_db, conv_ln_g,
                  conv_ln_b, conv_pw, w_out, ln1_g, ln1_b, peer_wq, peer_subkeys, peer_u, peer_v, ln2_g, ln2_b):
    w = w_in[l]
    o = np.cumsum([0, GM_WIDTH, GM_WIDTH, NSA_WIDTH] + [KV_WIDTH] * 6 + [3 * NSA_HEADS, CONV_WIDTH, CONV_WIDTH])
    gu, gv, q, kc, vc, ks, vs, kw, vw, gate, ca, cb = [w[:, o[i]:o[i + 1]] for i in range(12)]
    zeros = lambda n: jnp.zeros((w.shape[0], n), w.dtype)
    w_proj = jnp.concatenate([gu, gv, ca, cb, q, kc, ks, kw, zeros(C_ROPE_END - C_KW - KV_WIDTH), vc, vs, vw, gate,
                              zeros(PROJ_WIDTH - C_GATE - 3 * NSA_HEADS)], axis=1).astype(BF16)
    row = lambda a: a.reshape(1, -1)
    cw = jnp.stack([cmp_wk[l], cmp_wv[l]], axis=1)
    zero = jnp.zeros_like(cw)
    cw = jnp.stack([jnp.concatenate([cw, zero], -1), jnp.concatenate([zero, cw], -1)], axis=2)
    cw = jnp.broadcast_to(cw[..., None], cw.shape + (HEAD_DIM,))
    dw = jnp.concatenate([conv_dw[l], jnp.zeros((CONV_HALO - CONV_K, CONV_WIDTH), F32)], axis=0)
    return dict(
        w_proj=w_proj, gm_g=row(gm_ln_g[l]), gm_b=row(gm_ln_b[l]), gm_ws=gm_ws[l], gm_bs=gm_bs[l], cmp_w=cw,
        conv_dw=dw, conv_db=row(conv_db[l]), conv_g=row(conv_ln_g[l]), conv_b=row(conv_ln_b[l]),
        conv_pw=conv_pw[l].astype(BF16), w_out=w_out[l].astype(BF16), ln1_g=row(ln1_g[l]), ln1_b=row(ln1_b[l]),
        wq=peer_wq[l].astype(BF16), subkeys=peer_subkeys[l].astype(BF16), layer=l,
        u=peer_u.astype(BF16).reshape(-1, D_MODEL), v=peer_v.astype(BF16).reshape(-1, D_MODEL),
        ln2_g=row(ln2_g[l]), ln2_b=row(ln2_b[l]))


def _gm_weights(p, chunk):
    w = jnp.tril(p["gm_ws"][:, :chunk, :chunk])
    bs = jnp.repeat(p["gm_bs"][:, :chunk].T, GM_WIDTH // GM_GROUPS, axis=1)
    return w, bs


def _peer(x, p, *, tm_route, tm_dense, te):
    r2, cnt, f, e2 = _route(x, p["wq"], p["subkeys"], tm=tm_route)
    return _peer_dense(x, p["u"], p["v"], p["layer"], r2, cnt, f, e2, p["ln2_g"], p["ln2_b"], tm=tm_dense, te=te)


def _kv_rows(h, col, batch, seq):
    return h[:, col:col + KV_WIDTH].reshape(batch, seq, NSA_KV_HEADS, HEAD_DIM)


def _layer_prompt(x, p, cos, sin, *, batch, seq):
    m = batch * seq
    h, *kv_out = _proj(x, p["w_proj"], cos, sin, tm=1024, rope_lo=ROPE_LO, rope_hi=ROPE_HI, name="proj_in",
                       kv_cols=(C_KC, C_VC, C_KS, C_VS, C_KW, C_VW))
    kc, vc, ks, vs, kw, vw = [a.reshape(batch, seq, NSA_KV_HEADS, HEAD_DIM) for a in kv_out]
    gw, gbs = _gm_weights(p, GM_CHUNK)
    gm, vn = _gmlp(h, p["gm_g"], p["gm_b"], gw.astype(BF16), gbs, chunk=GM_CHUNK)
    nsa = _nsa_prompt(h, p["cmp_w"], batch=batch, seq=seq, tq=256)
    hist = jnp.zeros((batch, CONV_HALO, CONV_WIDTH), F32)
    y, c = _conv(h, hist, p["conv_dw"], p["conv_db"], p["conv_g"], p["conv_b"], p["conv_pw"],
                 batch=batch, seq=seq, tq=256)
    x1 = _out_ln(gm, nsa, y, x, p["w_out"], p["ln1_g"], p["ln1_b"], tm=512)
    x2 = _peer(x1, p, tm_route=512, tm_dense=512, te=PEER_TE)
    keep = min(WINDOW, seq)
    start = ((seq - 1) // GM_CHUNK) * GM_CHUNK
    states = (kc, vc, ks, vs, kw[:, -keep:], vw[:, -keep:],
              c.reshape(batch, seq, CONV_WIDTH)[:, -(CONV_K - 1):], vn.reshape(batch, seq, GM_WIDTH)[:, start:])
    return x2, states


def _layer_sample(x, p, cos, sin, caches, layer, page_table, win_k, win_v, hist30, *, batch, seq, past_len):
    m = batch * seq
    h = _proj(x, p["w_proj"], cos, sin, tm=m, rope_lo=ROPE_LO, rope_hi=ROPE_HI, name="proj_in_s")
    pad_rows = lambda a: jnp.pad(a.reshape(batch, seq, -1), ((0, 0), (0, GM_CHUNK - seq), (0, 0))).reshape(batch * GM_CHUNK, -1)
    gw, gbs = _gm_weights(p, GM_CHUNK)
    hp = pad_rows(h[:, :2 * GM_WIDTH])
    gm, vn = _gmlp(hp, p["gm_g"], p["gm_b"], gw.astype(BF16), gbs, chunk=GM_CHUNK)
    unpad = lambda a: a.reshape(batch, GM_CHUNK, -1)[:, :seq].reshape(m, -1)
    gm, vn = unpad(gm), unpad(vn)
    kvs = [_kv_rows(h, col, batch, seq) for col in (C_KC, C_VC, C_KS, C_VS, C_KW, C_VW)]
    win_buf = win_k.shape[1]
    page = past_len // page_table.shape[1]
    new_rows = lambda col, rows: jnp.pad(h[:, col:col + KV_WIDTH].reshape(batch, seq, KV_WIDTH),
                                         ((0, 0), (0, rows - seq), (0, 0)))
    flat = lambda a: a.reshape(batch, win_buf, KV_WIDTH)
    wk_all = jnp.concatenate([flat(win_k), new_rows(C_KW, HEAD_DIM)], axis=1)
    wv_all = jnp.concatenate([flat(win_v), new_rows(C_VW, HEAD_DIM)], axis=1)
    nsa = _nsa_sample(h, caches, layer, page_table, new_rows(C_KS, page), new_rows(C_VS, page), wk_all, wv_all,
                      p["cmp_w"], batch=batch, seq=seq, past_len=past_len, win_buf=win_buf)
    keep = min(WINDOW, past_len + seq)
    unflat = lambda a: a[:, :win_buf + seq][:, -keep:].reshape(batch, keep, NSA_KV_HEADS, HEAD_DIM)
    win_k, win_v = unflat(wk_all), unflat(wv_all)
    hist = jnp.pad(hist30, ((0, 0), (CONV_HALO - (CONV_K - 1), 0), (0, 0)))
    y, c = _conv(h, hist, p["conv_dw"], p["conv_db"], p["conv_g"], p["conv_b"], p["conv_pw"],
                 batch=batch, seq=seq, tq=seq)
    x1 = _out_ln(gm, nsa, y, x, p["w_out"], p["ln1_g"], p["ln1_b"], tm=m)
    x2 = _peer(x1, p, tm_route=m, tm_dense=m, te=PEER_TE)
    conv_state = jnp.concatenate([hist30, c.reshape(batch, seq, CONV_WIDTH)], axis=1)[:, -(CONV_K - 1):]
    start = ((seq - 1) // GM_CHUNK) * GM_CHUNK
    states = (kvs[0], kvs[1], kvs[2], kvs[3], win_k, win_v, conv_state, vn.reshape(batch, seq, GM_WIDTH)[:, start:])
    return x2, states


def kernel(x_prompt, x_sample, cache_cmp_k, cache_cmp_v, cache_slc_k, cache_slc_v, cache_win_k, cache_win_v,
           state_conv, page_table, w_in, gm_ln_g, gm_ln_b, gm_ws, gm_bs, cmp_wk, cmp_wv, conv_dw, conv_db,
           conv_ln_g, conv_ln_b, conv_pw, w_out, ln1_g, ln1_b, peer_wq, peer_subkeys, peer_u, peer_v, ln2_g, ln2_b):
    bp, sp, _ = x_prompt.shape
    bs, ss, _ = x_sample.shape
    depth = w_in.shape[0]
    past_len = page_table.shape[1] * cache_cmp_k.shape[2]
    pos_p = jnp.arange(sp, dtype=jnp.int32)
    pos_s = past_len + jnp.arange(ss, dtype=jnp.int32)
    cos_p, sin_p = _rope_tables(pos_p)
    cos_s, sin_s = _rope_tables(jnp.tile(pos_s, bs))
    xp = x_prompt.reshape(bp * sp, D_MODEL)
    xs = x_sample.reshape(bs * ss, D_MODEL)
    p_states, s_states = [], []
    caches = [c.reshape(-1, HEAD_DIM) for c in (cache_cmp_k, cache_cmp_v, cache_slc_k, cache_slc_v)]
    for l in range(depth):
        p = _prep_weights(l, w_in, gm_ln_g, gm_ln_b, gm_ws, gm_bs, cmp_wk, cmp_wv, conv_dw, conv_db, conv_ln_g,
                          conv_ln_b, conv_pw, w_out, ln1_g, ln1_b, peer_wq, peer_subkeys, peer_u, peer_v, ln2_g, ln2_b)
        xp, sp_state = _layer_prompt(xp, p, cos_p, sin_p, batch=bp, seq=sp)
        xs, ss_state = _layer_sample(xs, p, cos_s, sin_s, caches, l, page_table, cache_win_k[l], cache_win_v[l],
                                     state_conv[l], batch=bs, seq=ss, past_len=past_len)
        p_states.append(sp_state)
        s_states.append(ss_state)
    stack = lambda states, i: jnp.stack([st[i] for st in states], axis=0)
    outs = [xp.reshape(bp, sp, D_MODEL), xs.reshape(bs, ss, D_MODEL)]
    outs += [stack(p_states, i) for i in range(8)]
    outs += [stack(s_states, i) for i in range(8)]
    return tuple(outs)
```

```python
import functools

import numpy as np
import jax
import jax.numpy as jnp
from jax import lax
from jax.experimental import pallas as pl
from jax.experimental.pallas import tpu as pltpu

F32 = jnp.float32
BF16 = jnp.bfloat16

D_MODEL = 2048
HEAD_DIM = 128
NSA_HEADS = 8
NSA_KV_HEADS = 2
GQA_R = NSA_HEADS // NSA_KV_HEADS
GM_WIDTH = 512
GM_GROUPS = 4
GM_CHUNK = 128
CONV_WIDTH = 512
CONV_GROUPS = 4
CONV_K = 31
NSA_WIDTH = NSA_HEADS * HEAD_DIM
KV_WIDTH = NSA_KV_HEADS * HEAD_DIM
CMP_BLOCK = 32
SEL_BLOCK = 64
N_SELECT = 16
WINDOW = 512
FORCED_SCORE = 1.0e4
PEER_HEADS = 8
N_KEYS = 128
PEER_TOPK = 16
SUBKEY_DIM = 128
ROPE_THETA = 10000.0
LN_EPS = 1e-5
DEPTH = 2
ALPHA = (2 * DEPTH) ** 0.25

C_GU, C_GV, C_CA, C_CB, C_Q = 0, 512, 1024, 1536, 2048
C_KC, C_KS, C_KW, C_ROPE_END = 3072, 3328, 3584, 4096
C_VC, C_VS, C_VW, C_GATE = 4096, 4352, 4608, 4864
PROJ_WIDTH = 5120
PROJ_TN = 512
ROPE_LO, ROPE_HI = C_Q // PROJ_TN, C_ROPE_END // PROJ_TN

V7X_VMEM_LIMIT = 56 * 1024 * 1024
NEG_INF = float("-inf")


def _cparams(n_axes, vmem=V7X_VMEM_LIMIT):
    return pltpu.CompilerParams(dimension_semantics=("arbitrary",) * n_axes, vmem_limit_bytes=vmem)


def _ln_lanes(z, g, b):
    mu = jnp.mean(z, axis=-1, keepdims=True)
    d = z - mu
    var = jnp.mean(d * d, axis=-1, keepdims=True)
    return d * lax.rsqrt(var + LN_EPS) * g + b


def _dot_nt(a, b):
    return lax.dot_general(a, b, (((1,), (1,)), ((), ())), preferred_element_type=F32)


def _proj_kernel(x_ref, w_ref, cos_ref, sin_ref, o_ref, *rest, rope_lo, rope_hi, tn, kv_cols):
    kv_refs, xb_ref = rest[:-1], rest[-1]
    j = pl.program_id(1)
    tm = x_ref.shape[0]

    @pl.when(j == 0)
    def _():
        xb_ref[...] = x_ref[...].astype(BF16)

    acc = jnp.dot(xb_ref[...], w_ref[...], preferred_element_type=F32)
    is_rope = jnp.logical_and(j >= rope_lo, j < rope_hi)

    def emit(blocks):
        for c, blk in enumerate(blocks):
            o_ref[:, c * HEAD_DIM:(c + 1) * HEAD_DIM] = blk
        for ref, col in zip(kv_refs, kv_cols):
            @pl.when(j == col // tn)
            def _():
                first = (col % tn) // HEAD_DIM
                for g in range(NSA_KV_HEADS):
                    ref[pl.ds(g, tm, stride=NSA_KV_HEADS), :] = blocks[first + g]

    @pl.when(is_rope)
    def _():
        cos, sin = cos_ref[...], sin_ref[...]
        blocks = [acc[:, c * HEAD_DIM:(c + 1) * HEAD_DIM] for c in range(tn // HEAD_DIM)]
        emit([blk * cos + pltpu.roll(blk, HEAD_DIM // 2, 1) * sin for blk in blocks])

    @pl.when(jnp.logical_not(is_rope))
    def _():
        emit([acc[:, c * HEAD_DIM:(c + 1) * HEAD_DIM] for c in range(tn // HEAD_DIM)])


def _proj(x, w, cos, sin, *, tm, rope_lo, rope_hi, name, kv_cols=()):
    m, k = x.shape
    n = w.shape[1]
    tn = PROJ_TN
    nper = cos.shape[0] // tm
    kern = functools.partial(_proj_kernel, rope_lo=rope_lo, rope_hi=rope_hi, tn=tn, kv_cols=tuple(kv_cols))
    kv_spec = pl.BlockSpec((NSA_KV_HEADS * tm, HEAD_DIM), lambda i, j: (i, 0))
    kv_shape = jax.ShapeDtypeStruct((NSA_KV_HEADS * m, HEAD_DIM), F32)
    outs = pl.pallas_call(
        kern,
        grid=(m // tm, n // tn),
        in_specs=[
            pl.BlockSpec((tm, k), lambda i, j: (i, 0)),
            pl.BlockSpec((k, tn), lambda i, j: (0, j)),
            pl.BlockSpec((tm, HEAD_DIM), lambda i, j: (i % nper, 0)),
            pl.BlockSpec((tm, HEAD_DIM), lambda i, j: (i % nper, 0)),
        ],
        out_specs=[pl.BlockSpec((tm, tn), lambda i, j: (i, j))] + [kv_spec] * len(kv_cols),
        out_shape=[jax.ShapeDtypeStruct((m, n), F32)] + [kv_shape] * len(kv_cols),
        scratch_shapes=[pltpu.VMEM((tm, k), BF16)],
        compiler_params=_cparams(2),
        name=name,
    )(x, w, cos, sin)
    return outs if kv_cols else outs[0]


GM_CHUNKS_PER_STEP = 8


def _gmlp_kernel(gu_ref, gv_ref, g_ref, b_ref, w_ref, bs_ref, o_ref, vn_ref, *, chunk):
    gw = GM_WIDTH // GM_GROUPS
    for n in range(gu_ref.shape[0] // chunk):
        rows = slice(n * chunk, (n + 1) * chunk)
        for h in range(GM_GROUPS):
            sl = slice(h * gw, (h + 1) * gw)
            v = jax.nn.gelu(gv_ref[rows, sl])
            vn = _ln_lanes(v, g_ref[:, sl], b_ref[:, sl])
            vn_ref[rows, sl] = vn
            s = jnp.dot(w_ref[h], vn.astype(BF16), preferred_element_type=F32) + bs_ref[:, sl]
            o_ref[rows, sl] = jax.nn.gelu(gu_ref[rows, sl]) * s


def _gmlp(h, g, b, w_tril, bs_full, *, chunk):
    m = h.shape[0]
    tile = chunk * (GM_CHUNKS_PER_STEP if (m // chunk) % GM_CHUNKS_PER_STEP == 0 else 1)
    blk = lambda c: pl.BlockSpec((tile, GM_WIDTH), lambda i, c=c: (i, c))
    full = lambda a: pl.BlockSpec(a.shape, lambda i: (0,) * a.ndim)
    return pl.pallas_call(
        functools.partial(_gmlp_kernel, chunk=chunk),
        grid=(m // tile,),
        in_specs=[blk(C_GU // GM_WIDTH), blk(C_GV // GM_WIDTH), full(g), full(b), full(w_tril), full(bs_full)],
        out_specs=[pl.BlockSpec((tile, GM_WIDTH), lambda i: (i, 0))] * 2,
        out_shape=[jax.ShapeDtypeStruct((m, GM_WIDTH), F32)] * 2,
        compiler_params=_cparams(1),
        name="gmlp",
    )(h, h, g, b, w_tril, bs_full)


CONV_HALO = 32


def _conv_kernel(ca_ref, cb_ref, cah_ref, cbh_ref, hist_ref, dw_ref, db_ref, g_ref, b_ref, pw_ref,
                 y_ref, c_ref, buf_ref, *, tq):
    i = pl.program_id(1)
    c = ca_ref[...] * jax.nn.sigmoid(cb_ref[...])
    c_ref[...] = c
    halo = cah_ref[...] * jax.nn.sigmoid(cbh_ref[...])
    halo = jnp.where(i == 0, hist_ref[0], halo)
    buf_ref[0, 0:CONV_HALO, :] = halo
    buf_ref[0, CONV_HALO:CONV_HALO + tq, :] = c
    sub = 8
    span = tq + CONV_HALO - sub
    for r in range(1, sub):
        buf_ref[r, 0:span, :] = buf_ref[0, r:r + span, :]
    lead = CONV_HALO - (CONV_K - 1)
    acc = jnp.zeros((tq, CONV_WIDTH), F32)
    for k in range(CONV_K):
        off = lead + k
        base = off - off % sub
        acc = acc + dw_ref[k:k + 1, :] * buf_ref[off % sub, base:base + tq, :]
    y = acc + db_ref[...]
    cw = CONV_WIDTH // CONV_GROUPS
    parts = []
    for gi in range(CONV_GROUPS):
        sl = slice(gi * cw, (gi + 1) * cw)
        yn = _ln_lanes(y[:, sl], g_ref[:, sl], b_ref[:, sl])
        parts.append((yn * jax.nn.sigmoid(yn)).astype(BF16))
    y_ref[...] = jnp.dot(jnp.concatenate(parts, axis=-1), pw_ref[...], preferred_element_type=F32)


def _conv(h, hist, dw, db, g, b, pw, *, batch, seq, tq):
    m = h.shape[0]
    nt = seq // tq
    row = lambda bi, i: bi * nt + i
    halo_row = lambda bi, i: jnp.maximum((bi * seq + i * tq) // CONV_HALO - 1, 0)
    full = lambda a: pl.BlockSpec(a.shape, lambda bi, i: (0,) * a.ndim)
    kern = functools.partial(_conv_kernel, tq=tq)
    return pl.pallas_call(
        kern,
        grid=(batch, nt),
        in_specs=[
            pl.BlockSpec((tq, CONV_WIDTH), lambda bi, i: (row(bi, i), C_CA // CONV_WIDTH)),
            pl.BlockSpec((tq, CONV_WIDTH), lambda bi, i: (row(bi, i), C_CB // CONV_WIDTH)),
            pl.BlockSpec((CONV_HALO, CONV_WIDTH), lambda bi, i: (halo_row(bi, i), C_CA // CONV_WIDTH)),
            pl.BlockSpec((CONV_HALO, CONV_WIDTH), lambda bi, i: (halo_row(bi, i), C_CB // CONV_WIDTH)),
            pl.BlockSpec((1, CONV_HALO, CONV_WIDTH), lambda bi, i: (bi, 0, 0)),
            full(dw), full(db), full(g), full(b), full(pw),
        ],
        out_specs=[pl.BlockSpec((tq, CONV_WIDTH), lambda bi, i: (row(bi, i), 0))] * 2,
        out_shape=[jax.ShapeDtypeStruct((m, CONV_WIDTH), F32)] * 2,
        scratch_shapes=[pltpu.VMEM((8, CONV_HALO + tq, CONV_WIDTH), F32)],
        compiler_params=_cparams(2),
        name="conv",
    )(h, h, h, h, hist, dw, db, g, b, pw)


def _masked_softmax(s, mask):
    s = jnp.where(mask, s, NEG_INF)
    m = jnp.max(s, axis=-1, keepdims=True)
    m = jnp.where(m > NEG_INF, m, 0.0)
    e = jnp.exp(s - m)
    return e / jnp.maximum(jnp.sum(e, axis=-1, keepdims=True), 1e-30)


def _select_blocks(imp, tpos, n_blocks):
    tq, lanes = imp.shape
    blk = lax.broadcasted_iota(jnp.int32, (tq, lanes), 1)
    cur = tpos // SEL_BLOCK
    valid = blk * SEL_BLOCK <= tpos
    forced = jnp.logical_or(blk == 0, jnp.logical_or(blk == cur, blk == cur - 1))
    score = jnp.where(valid, jnp.where(forced, FORCED_SCORE, imp), NEG_INF)
    beaten = jnp.zeros((tq, lanes), F32)
    for i in range(n_blocks):
        ci = score[:, i:i + 1]
        earlier = (blk > i).astype(F32)
        beaten = beaten + jnp.where(ci > score, 1.0, jnp.where(ci == score, earlier, 0.0))
    picked = jnp.logical_and(beaten < float(min(N_SELECT, n_blocks)), blk < n_blocks)
    return picked.astype(F32)


SLC_CHUNK = 512


def _online_softmax_step(m_ref, l_ref, acc_ref, r, s, mask, v):
    s = jnp.where(mask, s, NEG_INF)
    m_old = m_ref[r]
    m_new = jnp.maximum(m_old, jnp.max(s, axis=-1, keepdims=True))
    m_safe = jnp.where(m_new > NEG_INF, m_new, 0.0)
    alpha = jnp.exp(m_old - m_safe)
    e = jnp.exp(s - m_safe)
    l_ref[r] = alpha * l_ref[r] + jnp.sum(e, axis=-1, keepdims=True)
    acc_ref[r] = alpha * acc_ref[r] + jnp.dot(e.astype(BF16), v, preferred_element_type=F32)
    m_ref[r] = m_new


VT_CHUNK = 256


def _select_blocks_t(imp_t, tpos_row, n_blocks):
    nb, tq = imp_t.shape
    blk = lax.broadcasted_iota(jnp.int32, (nb, tq), 0)
    cur = tpos_row // SEL_BLOCK
    valid = blk * SEL_BLOCK <= tpos_row
    forced = jnp.logical_or(blk == 0, jnp.logical_or(blk == cur, blk == cur - 1))
    score = jnp.where(valid, jnp.where(forced, FORCED_SCORE, imp_t), NEG_INF)
    beaten = jnp.zeros((nb, tq), F32)
    for i in range(n_blocks):
        ci = score[i:i + 1, :]
        earlier = (blk > i).astype(F32)
        beaten = beaten + jnp.where(ci > score, 1.0, jnp.where(ci == score, earlier, 0.0))
    return (beaten < float(min(N_SELECT, n_blocks))).astype(F32)


def _col_softmax(s_t, mask_t):
    s = jnp.where(mask_t, s_t, NEG_INF)
    m = jnp.max(s, axis=0, keepdims=True)
    m = jnp.where(m > NEG_INF, m, 0.0)
    e = jnp.exp(s - m)
    return e, jnp.sum(e, axis=0, keepdims=True)


def _col_softmax_step(m_ref, l_ref, acc_ref, r, s_t, mask_t, v_t):
    s = jnp.where(mask_t, s_t, NEG_INF)
    m_old = m_ref[r]
    m_new = jnp.maximum(m_old, jnp.max(s, axis=0, keepdims=True))
    m_safe = jnp.where(m_new > NEG_INF, m_new, 0.0)
    alpha = jnp.exp(m_old - m_safe)
    e = jnp.exp(s - m_safe)
    l_ref[r] = alpha * l_ref[r] + jnp.sum(e, axis=0, keepdims=True)
    acc_ref[r] = alpha * acc_ref[r] + jnp.dot(v_t, e.astype(BF16), preferred_element_type=F32)
    m_ref[r] = m_new


def _nsa_prompt_kernel(q_ref, kc_ref, ks_ref, kw_ref, vc_ref, vs_ref, vw_ref, gate_ref, cw_ref, o_ref,
                       kb_ref, vt_ref, kcmp_ref, vcmpt_ref, m_ref, l_ref, acc_ref, *, tq, seq):
    qi = pl.program_id(1)
    t0 = qi * tq
    scale = HEAD_DIM ** -0.5
    n_slc = seq // SEL_BLOCK
    tpos = t0 + lax.broadcasted_iota(jnp.int32, (1, tq), 1)
    row_c = lax.broadcasted_iota(jnp.int32, (2 * n_slc, tq), 0)
    cmp_last = jnp.where(row_c < n_slc, row_c * SEL_BLOCK + CMP_BLOCK - 1, (row_c - n_slc) * SEL_BLOCK + SEL_BLOCK - 1)
    cmp_mask = cmp_last <= tpos
    win_len = tq + WINDOW
    w0 = pl.multiple_of(jnp.maximum(t0 - WINDOW, 0), tq)
    wpos = w0 + lax.broadcasted_iota(jnp.int32, (win_len, tq), 0)
    win_mask = jnp.logical_and(wpos <= tpos, wpos >= tpos - WINDOW)
    gates_t = jax.nn.sigmoid(gate_ref[:, 0:HEAD_DIM]).T

    @pl.when(qi == 0)
    def _():
        kb_ref[0] = ks_ref[0].astype(BF16)
        kb_ref[1] = kw_ref[0].astype(BF16)
        for g in range(NSA_KV_HEADS):
            gs = slice(g * HEAD_DIM, (g + 1) * HEAD_DIM)
            for n, ref in enumerate((vs_ref, vw_ref)):
                for c in range(seq // VT_CHUNK):
                    vt_ref[n, g, c] = ref[0, c * VT_CHUNK:(c + 1) * VT_CHUNK, gs].T.astype(BF16)
            halves = []
            for which, ref in enumerate((kc_ref, vc_ref)):
                x3 = ref[0, :, gs].reshape(n_slc, SEL_BLOCK, HEAD_DIM)
                ev = jnp.sum(x3 * cw_ref[g, which, 0][None], axis=1)
                od = jnp.sum(x3 * cw_ref[g, which, 1][None], axis=1)
                halves.append(jnp.concatenate([ev, od], axis=0))
            kcmp_ref[g] = halves[0].astype(BF16)
            vcmpt_ref[g] = jnp.concatenate([halves[1], jnp.zeros_like(halves[1])], axis=0).T.astype(BF16)

    for g in range(NSA_KV_HEADS):
        gs = slice(g * HEAD_DIM, (g + 1) * HEAD_DIM)
        kcmp, vcmp_t = kcmp_ref[g], vcmpt_ref[g][:, :2 * n_slc]
        qg = jnp.concatenate([(q_ref[:, (g * GQA_R + r) * HEAD_DIM:(g * GQA_R + r + 1) * HEAD_DIM] * scale).astype(BF16)
                              for r in range(GQA_R)], axis=0)
        heads = lambda a: jnp.concatenate([a] * GQA_R, axis=1)
        e, l = _col_softmax(_dot_nt(kcmp, qg), heads(cmp_mask))
        p = e / jnp.maximum(l, 1e-30)
        o_cmp = jnp.dot(vcmp_t, p.astype(BF16), preferred_element_type=F32)
        imp = p[:, 0:tq]
        for r in range(1, GQA_R):
            imp = imp + p[:, r * tq:(r + 1) * tq]
        sel = _select_blocks_t(imp[:n_slc] + imp[n_slc:], tpos, n_slc).astype(BF16)

        m_ref[...] = jnp.full(m_ref.shape, NEG_INF, F32)
        l_ref[...] = jnp.zeros_like(l_ref)
        acc_ref[...] = jnp.zeros_like(acc_ref)
        for j in range(seq // SLC_CHUNK):
            @pl.when(j * SLC_CHUNK < t0 + tq)
            def _():
                kpos = j * SLC_CHUNK + lax.broadcasted_iota(jnp.int32, (SLC_CHUNK, tq), 0)
                key_blk = (j * SLC_CHUNK + lax.broadcasted_iota(jnp.int32, (SLC_CHUNK, n_slc), 0)) // SEL_BLOCK
                expand = (lax.broadcasted_iota(jnp.int32, (SLC_CHUNK, n_slc), 1) == key_blk).astype(BF16)
                sel_keys = jnp.dot(expand, sel, preferred_element_type=F32)
                mask = jnp.logical_and(sel_keys > 0.5, kpos <= tpos)
                k = kb_ref[0, j * SLC_CHUNK:(j + 1) * SLC_CHUNK, gs]
                per = SLC_CHUNK // VT_CHUNK
                v_t = jnp.concatenate([vt_ref[0, g, j * per + c] for c in range(per)], axis=1)
                _col_softmax_step(m_ref, l_ref, acc_ref, 0, _dot_nt(k, qg), heads(mask), v_t)

        kwin = kb_ref[1, pl.ds(w0, win_len), gs]
        wc0 = w0 // VT_CHUNK
        vwin_t = jnp.concatenate([vt_ref[1, g, wc0 + c] for c in range(win_len // VT_CHUNK)], axis=1)
        o_slc = acc_ref[0] / jnp.maximum(l_ref[0], 1e-30)
        e, l = _col_softmax(_dot_nt(kwin, qg), heads(win_mask))
        o_win = jnp.dot(vwin_t, e.astype(BF16), preferred_element_type=F32) / jnp.maximum(l, 1e-30)
        for r in range(GQA_R):
            hh = g * GQA_R + r
            cs = slice(r * tq, (r + 1) * tq)
            o_t = (gates_t[3 * hh:3 * hh + 1] * o_cmp[:, cs] + gates_t[3 * hh + 1:3 * hh + 2] * o_slc[:, cs]
                   + gates_t[3 * hh + 2:3 * hh + 3] * o_win[:, cs])
            o_ref[:, hh * HEAD_DIM:(hh + 1) * HEAD_DIM] = o_t.T


def _nsa_prompt(h, cmp_w, *, batch, seq, tq):
    m = h.shape[0]
    nt = seq // tq
    kv = lambda col: pl.BlockSpec((1, seq, KV_WIDTH), lambda bi, i, col=col: (bi, 0, col // KV_WIDTH))
    h3 = h.reshape(batch, seq, PROJ_WIDTH)
    kern = functools.partial(_nsa_prompt_kernel, tq=tq, seq=seq)
    return pl.pallas_call(
        kern,
        grid=(batch, nt),
        in_specs=[
            pl.BlockSpec((tq, NSA_WIDTH), lambda bi, i: (bi * nt + i, C_Q // NSA_WIDTH)),
            kv(C_KC), kv(C_KS), kv(C_KW), kv(C_VC), kv(C_VS), kv(C_VW),
            pl.BlockSpec((tq, KV_WIDTH), lambda bi, i: (bi * nt + i, C_GATE // KV_WIDTH)),
            pl.BlockSpec(cmp_w.shape, lambda bi, i: (0,) * cmp_w.ndim),
        ],
        out_specs=pl.BlockSpec((tq, NSA_WIDTH), lambda bi, i: (bi * nt + i, 0)),
        out_shape=jax.ShapeDtypeStruct((m, NSA_WIDTH), F32),
        scratch_shapes=[
            pltpu.VMEM((2, seq, KV_WIDTH), BF16),
            pltpu.VMEM((2, NSA_KV_HEADS, seq // VT_CHUNK, HEAD_DIM, VT_CHUNK), BF16),
            pltpu.VMEM((NSA_KV_HEADS, 2 * (seq // SEL_BLOCK), HEAD_DIM), BF16),
            pltpu.VMEM((NSA_KV_HEADS, HEAD_DIM, 4 * (seq // SEL_BLOCK)), BF16),
            pltpu.VMEM((1, 1, GQA_R * tq), F32), pltpu.VMEM((1, 1, GQA_R * tq), F32),
            pltpu.VMEM((1, HEAD_DIM, GQA_R * tq), F32),
        ],
        compiler_params=_cparams(2),
        name="nsa_prompt",
    )(h, h3, h3, h3, h3, h3, h3, h, cmp_w)


def _out_ln_kernel(gm_ref, nsa_ref, y_ref, x_ref, w_ref, g_ref, b_ref, o_ref):
    cat = jnp.concatenate([gm_ref[...].astype(BF16), nsa_ref[...].astype(BF16), y_ref[...].astype(BF16)], axis=-1)
    mix = jnp.dot(cat, w_ref[...], preferred_element_type=F32)
    o_ref[...] = _ln_lanes(ALPHA * x_ref[...] + mix, g_ref[...], b_ref[...])


def _out_ln(gm, nsa, y, x, w, g, b, *, tm):
    m = x.shape[0]
    row = lambda width: pl.BlockSpec((tm, width), lambda i: (i, 0))
    full = lambda a: pl.BlockSpec(a.shape, lambda i: (0,) * a.ndim)
    return pl.pallas_call(
        _out_ln_kernel,
        grid=(m // tm,),
        in_specs=[row(GM_WIDTH), row(NSA_WIDTH), row(CONV_WIDTH), row(D_MODEL), full(w), full(g), full(b)],
        out_specs=row(D_MODEL),
        out_shape=jax.ShapeDtypeStruct((m, D_MODEL), F32),
        compiler_params=_cparams(1),
        name="out_ln",
    )(gm, nsa, y, x, w, g, b)


def _extract_topk(s, exact_ties, want_rank):
    n_rows, tm = s.shape
    rows = lax.broadcasted_iota(jnp.int32, (n_rows, tm), 0)
    krow = lax.broadcasted_iota(jnp.int32, (PEER_TOPK, tm), 0)
    work = s
    rank = jnp.full((n_rows, tm), float(PEER_TOPK), F32) if want_rank else None
    vals = jnp.zeros((PEER_TOPK, tm), F32)
    for r in range(PEER_TOPK):
        mx = jnp.max(work, axis=0, keepdims=True)
        hit = work == mx
        if exact_ties:
            hit = rows == jnp.min(jnp.where(hit, rows, n_rows), axis=0, keepdims=True)
        if want_rank:
            rank = jnp.where(hit, float(r), rank)
        vals = jnp.where(krow == r, mx, vals)
        work = jnp.where(hit, NEG_INF, work)
    taken = jnp.sum((work == NEG_INF).astype(F32), axis=0, keepdims=True)
    return vals, rank, taken


def _pair_topk(v1, v2):
    tm = v1.shape[1]
    krow = lax.broadcasted_iota(jnp.int32, (PEER_TOPK, tm), 0)
    cnt = jnp.zeros((PEER_TOPK, tm), F32)
    front = v1 + v2[0:1]
    m0 = v1[0:1] + v2[0:1]
    z = jnp.zeros((1, tm), F32)
    for _ in range(PEER_TOPK):
        mx = jnp.max(front, axis=0, keepdims=True)
        idx = jnp.min(jnp.where(front == mx, krow, PEER_TOPK), axis=0, keepdims=True)
        hit = krow == idx
        cnt = cnt + hit.astype(F32)
        z = z + jnp.exp(mx - m0)
        chosen = jnp.sum(jnp.where(hit, cnt, 0.0), axis=0, keepdims=True)
        nxt = jnp.full((1, tm), NEG_INF, F32)
        for c in range(1, PEER_TOPK):
            nxt = jnp.where(chosen == float(c), v2[c:c + 1], nxt)
        front = jnp.where(hit, v1 + nxt, front)
    return cnt, z


def _route_kernel(x_ref, wq_ref, sk_ref, r2_ref, cnt_ref, f_ref, e2_ref, xb_ref):
    @pl.when(pl.program_id(1) == 0)
    def _():
        xb_ref[...] = x_ref[...].astype(BF16)

    qp = jnp.dot(xb_ref[...], wq_ref[...], preferred_element_type=F32)
    q1 = qp[:, 0:SUBKEY_DIM].astype(BF16)
    q2 = qp[:, SUBKEY_DIM:2 * SUBKEY_DIM].astype(BF16)
    s1 = _dot_nt(sk_ref[0, 0], q1)
    s2 = _dot_nt(sk_ref[0, 1], q2)

    def emit(v1, v2, cnt, z, cnt_a, rank2):
        r2_ref[0] = rank2.astype(BF16)
        cnt_ref[0] = cnt_a
        f_ref[0] = jnp.exp(s1 - v1[0:1]) / z
        e2_ref[0] = jnp.exp(s2 - v2[0:1]).astype(BF16)

    v1, _, taken1 = _extract_topk(s1, exact_ties=False, want_rank=False)
    v2, rank2, taken2 = _extract_topk(s2, exact_ties=False, want_rank=True)
    cnt, z = _pair_topk(v1, v2)
    cnt_a = jnp.zeros(s1.shape, F32)
    for r in range(PEER_TOPK):
        cnt_a = jnp.where(s1 == v1[r:r + 1], cnt[r:r + 1], cnt_a)
    emit(v1, v2, cnt, z, cnt_a, rank2)

    @pl.when(jnp.max(jnp.maximum(taken1, taken2)) > float(PEER_TOPK))
    def _():
        v1, rank1, _ = _extract_topk(s1, exact_ties=True, want_rank=True)
        v2, rank2, _ = _extract_topk(s2, exact_ties=True, want_rank=True)
        cnt, z = _pair_topk(v1, v2)
        cnt_a = jnp.zeros(s1.shape, F32)
        for r in range(PEER_TOPK):
            cnt_a = jnp.where(rank1 == float(r), cnt[r:r + 1], cnt_a)
        emit(v1, v2, cnt, z, cnt_a, rank2)


def _route(x, wq, subkeys, *, tm):
    m, k = x.shape
    out = lambda dt: jax.ShapeDtypeStruct((PEER_HEADS, N_KEYS, m), dt)
    ospec = pl.BlockSpec((1, N_KEYS, tm), lambda i, hd: (hd, 0, i))
    return pl.pallas_call(
        _route_kernel,
        grid=(m // tm, PEER_HEADS),
        in_specs=[
            pl.BlockSpec((tm, k), lambda i, hd: (i, 0)),
            pl.BlockSpec((k, 2 * SUBKEY_DIM), lambda i, hd: (0, hd)),
            pl.BlockSpec((1, 2, N_KEYS, SUBKEY_DIM), lambda i, hd: (hd, 0, 0, 0)),
        ],
        out_specs=[ospec] * 4,
        out_shape=[out(BF16), out(F32), out(F32), out(BF16)],
        scratch_shapes=[pltpu.VMEM((tm, k), BF16)],
        compiler_params=_cparams(2),
        name="peer_route",
    )(x, wq, subkeys)


PEER_SUB = 256
PEER_TE = 1024


def _peer_dense_kernel(x_ref, u_ref, v_ref, r2_ref, cnt_ref, f_ref, e2_ref, g_ref, b_ref, o_ref, xb_ref, *, te):
    j = pl.program_id(1)

    @pl.when(j == 0)
    def _():
        xb_ref[...] = x_ref[...].astype(BF16)
        o_ref[...] = jnp.zeros_like(o_ref)

    cols = []
    for c in range(te // PEER_SUB):
        act = jax.nn.gelu(_dot_nt(u_ref[c * PEER_SUB:(c + 1) * PEER_SUB, :], xb_ref[...]).astype(BF16))
        rows = []
        for al in range(PEER_SUB // N_KEYS):
            a = c * (PEER_SUB // N_KEYS) + al
            gate = None
            for hd in range(PEER_HEADS):
                cnt = cnt_ref[hd, a:a + 1, :].astype(BF16)
                f = f_ref[hd, a:a + 1, :].astype(BF16)
                term = jnp.where(r2_ref[hd] < cnt, e2_ref[hd], jnp.zeros((), BF16)) * f
                gate = term if gate is None else gate + term
            rows.append(act[al * N_KEYS:(al + 1) * N_KEYS] * gate)
        cols.append(jnp.concatenate(rows, axis=0).T)
    o_ref[...] += jnp.dot(jnp.concatenate(cols, axis=1), v_ref[...], preferred_element_type=F32)

    @pl.when(j == pl.num_programs(1) - 1)
    def _():
        o_ref[...] = _ln_lanes(ALPHA * x_ref[...] + o_ref[...], g_ref[...], b_ref[...])


def _peer_dense(x, u, v, layer, r2, cnt, f, e2, g, b, *, tm, te):
    m = x.shape[0]
    n_exp = u.shape[0] // DEPTH
    first = layer * (n_exp // te)
    by_b = pl.BlockSpec((PEER_HEADS, N_KEYS, tm), lambda i, j: (0, 0, i))
    by_a = pl.BlockSpec((PEER_HEADS, te // N_KEYS, tm), lambda i, j: (0, j, i))
    full = lambda a: pl.BlockSpec(a.shape, lambda i, j: (0,) * a.ndim)
    kern = functools.partial(_peer_dense_kernel, te=te)
    return pl.pallas_call(
        kern,
        grid=(m // tm, n_exp // te),
        in_specs=[
            pl.BlockSpec((tm, D_MODEL), lambda i, j: (i, 0), pipeline_mode=pl.Buffered(1)),
            pl.BlockSpec((te, D_MODEL), lambda i, j: (first + j, 0)),
            pl.BlockSpec((te, D_MODEL), lambda i, j: (first + j, 0)),
            by_b, by_a, by_a, by_b, full(g), full(b),
        ],
        out_specs=pl.BlockSpec((tm, D_MODEL), lambda i, j: (i, 0), pipeline_mode=pl.Buffered(1)),
        out_shape=jax.ShapeDtypeStruct((m, D_MODEL), F32),
        scratch_shapes=[pltpu.VMEM((tm, D_MODEL), BF16)],
        compiler_params=_cparams(2),
        name="peer_dense",
    )(x, u, v, r2, cnt, f, e2, g, b)


PAGES_PER_STEP = 32


def _group_queries(q_ref, g):
    heads = [q_ref[:, (g * GQA_R + r) * HEAD_DIM:(g * GQA_R + r + 1) * HEAD_DIM] for r in range(GQA_R)]
    return jnp.concatenate(heads, axis=0).astype(BF16)


def _page_rows(pages, g):
    n = pages[0].shape[0] // NSA_KV_HEADS
    return jnp.concatenate([r[pl.ds(g, n, stride=NSA_KV_HEADS), :] for r in pages], axis=0)


def _nsa_s_cmp_kernel(pt_ref, q_ref, cw_ref, *refs, seq, past_len, n_real, sel_lanes):
    nps = PAGES_PER_STEP
    k_pages, v_pages = refs[:nps], refs[nps:2 * nps]
    ocmp_ref, sel_ref, ke_ref, ko_ref, ve_ref, vo_ref = refs[2 * nps:]
    c = pl.program_id(1)
    n_pb = ke_ref.shape[0]
    rows = nps * (k_pages[0].shape[0] // NSA_KV_HEADS) // SEL_BLOCK
    off = pl.multiple_of(c * rows, rows)

    def compress(pages, which, even_ref, odd_ref):
        for g in range(NSA_KV_HEADS):
            gs = slice(g * HEAD_DIM, (g + 1) * HEAD_DIM)
            x3 = _page_rows(pages, g).reshape(rows, SEL_BLOCK, HEAD_DIM)
            even_ref[pl.ds(off, rows), gs] = jnp.sum(x3 * cw_ref[g, which, 0][None], axis=1)
            odd_ref[pl.ds(off, rows), gs] = jnp.sum(x3 * cw_ref[g, which, 1][None], axis=1)

    compress(k_pages, 0, ke_ref, ko_ref)
    compress(v_pages, 1, ve_ref, vo_ref)

    @pl.when(c == pl.num_programs(1) - 1)
    def _():
        scale = HEAD_DIM ** -0.5
        nq = GQA_R * seq
        qrow = lax.broadcasted_iota(jnp.int32, (nq, 1), 0)
        qpos = past_len + qrow % seq
        lane = lax.broadcasted_iota(jnp.int32, (nq, 2 * n_pb), 1)
        cmp_last = jnp.where(lane < n_pb, lane * SEL_BLOCK + CMP_BLOCK - 1, (lane - n_pb) * SEL_BLOCK + SEL_BLOCK - 1)
        cmp_mask = cmp_last <= qpos
        tpos = past_len + lax.broadcasted_iota(jnp.int32, (seq, 1), 0)
        for g in range(NSA_KV_HEADS):
            gs = slice(g * HEAD_DIM, (g + 1) * HEAD_DIM)
            kcat = jnp.concatenate([ke_ref[:, gs], ko_ref[:, gs]], axis=0).astype(BF16)
            vcat = jnp.concatenate([ve_ref[:, gs], vo_ref[:, gs]], axis=0).astype(BF16)
            p = _masked_softmax(_dot_nt(_group_queries(q_ref, g), kcat) * scale, cmp_mask)
            ocmp_ref[0, g] = jnp.dot(p.astype(BF16), vcat, preferred_element_type=F32)
            imp = p[0:seq]
            for r in range(1, GQA_R):
                imp = imp + p[r * seq:(r + 1) * seq]
            imp = imp[:, :n_pb] + imp[:, n_pb:]
            imp = jnp.concatenate([imp, jnp.zeros((seq, sel_lanes - n_pb), F32)], axis=1)
            sel = _select_blocks(imp, tpos, n_real)
            bpc = sel_ref.shape[-1]
            for ch in range(sel_ref.shape[2]):
                sel_ref[0, g, ch] = sel[:, ch * bpc:(ch + 1) * bpc]


def _nsa_s_slc_kernel(pt_ref, q_ref, sel_ref, sel_new_ref, ocmp_ref, ksn_ref, vsn_ref, wk_ref, wv_ref, gate_ref, *refs,
                      seq, past_len, win_buf):
    nps = PAGES_PER_STEP
    k_pages, v_pages = refs[:nps], refs[nps:2 * nps]
    o_ref, m_ref, l_ref, acc_ref = refs[2 * nps:]
    c = pl.program_id(1)
    scale = HEAD_DIM ** -0.5
    nq = GQA_R * seq
    chunk = nps * (k_pages[0].shape[0] // NSA_KV_HEADS)
    qrow = lax.broadcasted_iota(jnp.int32, (nq, 1), 0)
    qpos = past_len + qrow % seq

    @pl.when(c == 0)
    def _():
        m_ref[...] = jnp.full(m_ref.shape, NEG_INF, F32)
        l_ref[...] = jnp.zeros_like(l_ref)
        acc_ref[...] = jnp.zeros_like(acc_ref)

    def online_update(g, s, mask, v):
        _online_softmax_step(m_ref, l_ref, acc_ref, g, s, mask, v)

    def key_mask(sel_blocks, first_pos, n_keys):
        bpc = sel_blocks.shape[1]
        key_blk = lax.broadcasted_iota(jnp.int32, (bpc, n_keys), 1) // SEL_BLOCK
        expand = (lax.broadcasted_iota(jnp.int32, (bpc, n_keys), 0) == key_blk).astype(BF16)
        sel_keys = jnp.dot(sel_blocks.astype(BF16), expand, preferred_element_type=F32)
        sel_keys = jnp.concatenate([sel_keys] * GQA_R, axis=0)
        kpos = first_pos + lax.broadcasted_iota(jnp.int32, (nq, n_keys), 1)
        return jnp.logical_and(sel_keys > 0.5, kpos <= qpos)

    for g in range(NSA_KV_HEADS):
        mask = key_mask(sel_ref[0, g, 0], c * chunk, chunk)
        scores = _dot_nt(_group_queries(q_ref, g), _page_rows(k_pages, g).astype(BF16)) * scale
        online_update(g, scores, mask, _page_rows(v_pages, g).astype(BF16))

    @pl.when(c == pl.num_programs(1) - 1)
    def _():
        gates = jax.nn.sigmoid(gate_ref[:, 0:HEAD_DIM])
        n_new = ksn_ref.shape[1]
        n_win = wk_ref.shape[1]
        wpos = past_len - win_buf + lax.broadcasted_iota(jnp.int32, (nq, n_win), 1)
        win_mask = jnp.logical_and(wpos <= qpos, wpos >= qpos - WINDOW)
        for g in range(NSA_KV_HEADS):
            gs = slice(g * HEAD_DIM, (g + 1) * HEAD_DIM)
            qg = _group_queries(q_ref, g)
            online_update(g, _dot_nt(qg, ksn_ref[0, :, gs].astype(BF16)) * scale,
                          key_mask(sel_new_ref[0, g, 0], past_len, n_new), vsn_ref[0, :, gs].astype(BF16))
            o_slc = acc_ref[g] / jnp.maximum(l_ref[g], 1e-30)
            p = _masked_softmax(_dot_nt(qg, wk_ref[0, :, gs].astype(BF16)) * scale, win_mask)
            o_win = jnp.dot(p.astype(BF16), wv_ref[0, :, gs].astype(BF16), preferred_element_type=F32)
            o_cmp = ocmp_ref[0, g]
            for r in range(GQA_R):
                hh = g * GQA_R + r
                rs = slice(r * seq, (r + 1) * seq)
                o_ref[:, hh * HEAD_DIM:(hh + 1) * HEAD_DIM] = (
                    gates[:, 3 * hh:3 * hh + 1] * o_cmp[rs] + gates[:, 3 * hh + 1:3 * hh + 2] * o_slc[rs]
                    + gates[:, 3 * hh + 2:3 * hh + 3] * o_win[rs])


def _nsa_sample(h, caches, layer, page_table, ks_new, vs_new, wk_all, wv_all, cw, *, batch, seq, past_len, win_buf):
    cmp_k, cmp_v, slc_k, slc_v = caches
    nps = PAGES_PER_STEP
    n_pages, page = page_table.shape[1], ks_new.shape[1]
    page_rows = page * NSA_KV_HEADS
    n_pool = cmp_k.shape[0] // (DEPTH * page_rows)
    n_chunks = n_pages // nps
    n_pb = past_len // SEL_BLOCK
    n_real = n_pb + 1
    bpc = nps * page // SEL_BLOCK
    sel_lanes = -(-max(n_real, (n_chunks + 1) * bpc) // 128) * 128
    sel_shape = (NSA_KV_HEADS, n_chunks + 1, seq, bpc)
    sel_chunk = lambda which: pl.BlockSpec((1, NSA_KV_HEADS, 1, seq, bpc), lambda b, c, pt: (b, 0, which(c), 0, 0))
    nq = GQA_R * seq
    page_spec = lambda k: pl.BlockSpec((page_rows, HEAD_DIM),
                                       lambda b, c, pt, k=k: (pt[b, c * nps + k] + layer * n_pool, 0))
    pages = [page_spec(k) for k in range(nps)]
    q_spec = pl.BlockSpec((seq, NSA_WIDTH), lambda b, c, pt: (b, C_Q // NSA_WIDTH))
    per_b = lambda shape: pl.BlockSpec((1,) + shape, lambda b, c, pt: (b,) + (0,) * len(shape))
    ocmp, sel = pl.pallas_call(
        functools.partial(_nsa_s_cmp_kernel, seq=seq, past_len=past_len, n_real=n_real, sel_lanes=sel_lanes),
        grid_spec=pltpu.PrefetchScalarGridSpec(
            num_scalar_prefetch=1, grid=(batch, n_chunks),
            in_specs=[q_spec, pl.BlockSpec(cw.shape, lambda b, c, pt: (0,) * cw.ndim)] + pages + pages,
            out_specs=[per_b((NSA_KV_HEADS, nq, HEAD_DIM)), per_b(sel_shape)],
            scratch_shapes=[pltpu.VMEM((n_pb, KV_WIDTH), F32)] * 4),
        out_shape=[jax.ShapeDtypeStruct((batch, NSA_KV_HEADS, nq, HEAD_DIM), F32),
                   jax.ShapeDtypeStruct((batch,) + sel_shape, F32)],
        compiler_params=_cparams(2),
        name="nsa_sample_cmp",
    )(page_table, h, cw, *([cmp_k] * nps), *([cmp_v] * nps))
    return pl.pallas_call(
        functools.partial(_nsa_s_slc_kernel, seq=seq, past_len=past_len, win_buf=win_buf),
        grid_spec=pltpu.PrefetchScalarGridSpec(
            num_scalar_prefetch=1, grid=(batch, n_chunks),
            in_specs=[q_spec, sel_chunk(lambda c: c), sel_chunk(lambda c: n_chunks),
                      per_b((NSA_KV_HEADS, nq, HEAD_DIM)),
                      per_b(ks_new.shape[1:]), per_b(vs_new.shape[1:]), per_b(wk_all.shape[1:]), per_b(wv_all.shape[1:]),
                      pl.BlockSpec((seq, KV_WIDTH), lambda b, c, pt: (b, C_GATE // KV_WIDTH))] + pages + pages,
            out_specs=pl.BlockSpec((seq, NSA_WIDTH), lambda b, c, pt: (b, 0)),
            scratch_shapes=[pltpu.VMEM((NSA_KV_HEADS, nq, 1), F32), pltpu.VMEM((NSA_KV_HEADS, nq, 1), F32),
                            pltpu.VMEM((NSA_KV_HEADS, nq, HEAD_DIM), F32)]),
        out_shape=jax.ShapeDtypeStruct((batch * seq, NSA_WIDTH), F32),
        compiler_params=_cparams(2),
        name="nsa_sample_slc",
    )(page_table, h, sel, sel, ocmp, ks_new, vs_new, wk_all, wv_all, h, *([slc_k] * nps), *([slc_v] * nps))


def _rope_tables(pos):
    half = HEAD_DIM // 2
    inv = ROPE_THETA ** (-jnp.arange(half, dtype=F32) / half)
    ang = pos.astype(F32)[:, None] * inv[None, :]
    cos, sin = jnp.cos(ang), jnp.sin(ang)
    return jnp.concatenate([cos, cos], -1), jnp.concatenate([-sin, sin], -1)


def _prep_weights(l, w_in, gm_ln_g, gm_ln_b, gm_ws, gm_bs, cmp_wk, cmp_wv, conv_dw, conv_db, conv_ln_g,
                  conv_ln_b, conv_pw, w_out, ln1_g, ln1_b, peer_wq, peer_subkeys, peer_u, peer_v, ln2_g, ln2_b):
    w = w_in[l]
    o = np.cumsum([0, GM_WIDTH, GM_WIDTH, NSA_WIDTH] + [KV_WIDTH] * 6 + [3 * NSA_HEADS, CONV_WIDTH, CONV_WIDTH])
    gu, gv, q, kc, vc, ks, vs, kw, vw, gate, ca, cb = [w[:, o[i]:o[i + 1]] for i in range(12)]
    zeros = lambda n: jnp.zeros((w.shape[0], n), w.dtype)
    w_proj = jnp.concatenate([gu, gv, ca, cb, q, kc, ks, kw, zeros(C_ROPE_END - C_KW - KV_WIDTH), vc, vs, vw, gate,
                              zeros(PROJ_WIDTH - C_GATE - 3 * NSA_HEADS)], axis=1).astype(BF16)
    row = lambda a: a.reshape(1, -1)
    cw = jnp.stack([cmp_wk[l], cmp_wv[l]], axis=1)
    zero = jnp.zeros_like(cw)
    cw = jnp.stack([jnp.concatenate([cw, zero], -1), jnp.concatenate([zero, cw], -1)], axis=2)
    cw = jnp.broadcast_to(cw[..., None], cw.shape + (HEAD_DIM,))
    dw = jnp.concatenate([conv_dw[l], jnp.zeros((CONV_HALO - CONV_K, CONV_WIDTH), F32)], axis=0)
    return dict(
        w_proj=w_proj, gm_g=row(gm_ln_g[l]), gm_b=row(gm_ln_b[l]), gm_ws=gm_ws[l], gm_bs=gm_bs[l], cmp_w=cw,
        conv_dw=dw, conv_db=row(conv_db[l]), conv_g=row(conv_ln_g[l]), conv_b=row(conv_ln_b[l]),
        conv_pw=conv_pw[l].astype(BF16), w_out=w_out[l].astype(BF16), ln1_g=row(ln1_g[l]), ln1_b=row(ln1_b[l]),
        wq=peer_wq[l].astype(BF16), subkeys=peer_subkeys[l].astype(BF16), layer=l,
        u=peer_u.astype(BF16).reshape(-1, D_MODEL), v=peer_v.astype(BF16).reshape(-1, D_MODEL),
        ln2_g=row(ln2_g[l]), ln2_b=row(ln2_b[l]))


def _gm_weights(p, chunk):
    w = jnp.tril(p["gm_ws"][:, :chunk, :chunk])
    bs = jnp.repeat(p["gm_bs"][:, :chunk].T, GM_WIDTH // GM_GROUPS, axis=1)
    return w, bs


def _peer(x, p, *, tm_route, tm_dense, te):
    r2, cnt, f, e2 = _route(x, p["wq"], p["subkeys"], tm=tm_route)
    return _peer_dense(x, p["u"], p["v"], p["layer"], r2, cnt, f, e2, p["ln2_g"], p["ln2_b"], tm=tm_dense, te=te)


def _kv_rows(h, col, batch, seq):
    return h[:, col:col + KV_WIDTH].reshape(batch, seq, NSA_KV_HEADS, HEAD_DIM)


def _layer_prompt(x, p, cos, sin, *, batch, seq):
    m = batch * seq
    h, *kv_out = _proj(x, p["w_proj"], cos, sin, tm=1024, rope_lo=ROPE_LO, rope_hi=ROPE_HI, name="proj_in",
                       kv_cols=(C_KC, C_VC, C_KS, C_VS, C_KW, C_VW))
    kc, vc, ks, vs, kw, vw = [a.reshape(batch, seq, NSA_KV_HEADS, HEAD_DIM) for a in kv_out]
    gw, gbs = _gm_weights(p, GM_CHUNK)
    gm, vn = _gmlp(h, p["gm_g"], p["gm_b"], gw.astype(BF16), gbs, chunk=GM_CHUNK)
    nsa = _nsa_prompt(h, p["cmp_w"], batch=batch, seq=seq, tq=256)
    hist = jnp.zeros((batch, CONV_HALO, CONV_WIDTH), F32)
    y, c = _conv(h, hist, p["conv_dw"], p["conv_db"], p["conv_g"], p["conv_b"], p["conv_pw"],
                 batch=batch, seq=seq, tq=256)
    x1 = _out_ln(gm, nsa, y, x, p["w_out"], p["ln1_g"], p["ln1_b"], tm=512)
    x2 = _peer(x1, p, tm_route=512, tm_dense=512, te=PEER_TE)
    keep = min(WINDOW, seq)
    start = ((seq - 1) // GM_CHUNK) * GM_CHUNK
    states = (kc, vc, ks, vs, kw[:, -keep:], vw[:, -keep:],
              c.reshape(batch, seq, CONV_WIDTH)[:, -(CONV_K - 1):], vn.reshape(batch, seq, GM_WIDTH)[:, start:])
    return x2, states


def _layer_sample(x, p, cos, sin, caches, layer, page_table, win_k, win_v, hist30, *, batch, seq, past_len):
    m = batch * seq
    h = _proj(x, p["w_proj"], cos, sin, tm=m, rope_lo=ROPE_LO, rope_hi=ROPE_HI, name="proj_in_s")
    pad_rows = lambda a: jnp.pad(a.reshape(batch, seq, -1), ((0, 0), (0, GM_CHUNK - seq), (0, 0))).reshape(batch * GM_CHUNK, -1)
    gw, gbs = _gm_weights(p, GM_CHUNK)
    hp = pad_rows(h[:, :2 * GM_WIDTH])
    gm, vn = _gmlp(hp, p["gm_g"], p["gm_b"], gw.astype(BF16), gbs, chunk=GM_CHUNK)
    unpad = lambda a: a.reshape(batch, GM_CHUNK, -1)[:, :seq].reshape(m, -1)
    gm, vn = unpad(gm), unpad(vn)
    kvs = [_kv_rows(h, col, batch, seq) for col in (C_KC, C_VC, C_KS, C_VS, C_KW, C_VW)]
    win_buf = win_k.shape[1]
    page = past_len // page_table.shape[1]
    new_rows = lambda col, rows: jnp.pad(h[:, col:col + KV_WIDTH].reshape(batch, seq, KV_WIDTH),
                                         ((0, 0), (0, rows - seq), (0, 0)))
    flat = lambda a: a.reshape(batch, win_buf, KV_WIDTH)
    wk_all = jnp.concatenate([flat(win_k), new_rows(C_KW, HEAD_DIM)], axis=1)
    wv_all = jnp.concatenate([flat(win_v), new_rows(C_VW, HEAD_DIM)], axis=1)
    nsa = _nsa_sample(h, caches, layer, page_table, new_rows(C_KS, page), new_rows(C_VS, page), wk_all, wv_all,
                      p["cmp_w"], batch=batch, seq=seq, past_len=past_len, win_buf=win_buf)
    keep = min(WINDOW, past_len + seq)
    unflat = lambda a: a[:, :win_buf + seq][:, -keep:].reshape(batch, keep, NSA_KV_HEADS, HEAD_DIM)
    win_k, win_v = unflat(wk_all), unflat(wv_all)
    hist = jnp.pad(hist30, ((0, 0), (CONV_HALO - (CONV_K - 1), 0), (0, 0)))
    y, c = _conv(h, hist, p["conv_dw"], p["conv_db"], p["conv_g"], p["conv_b"], p["conv_pw"],
                 batch=batch, seq=seq, tq=seq)
    x1 = _out_ln(gm, nsa, y, x, p["w_out"], p["ln1_g"], p["ln1_b"], tm=m)
    x2 = _peer(x1, p, tm_route=m, tm_dense=m, te=PEER_TE)
    conv_state = jnp.concatenate([hist30, c.reshape(batch, seq, CONV_WIDTH)], axis=1)[:, -(CONV_K - 1):]
    start = ((seq - 1) // GM_CHUNK) * GM_CHUNK
    states = (kvs[0], kvs[1], kvs[2], kvs[3], win_k, win_v, conv_state, vn.reshape(batch, seq, GM_WIDTH)[:, start:])
    return x2, states


def kernel(x_prompt, x_sample, cache_cmp_k, cache_cmp_v, cache_slc_k, cache_slc_v, cache_win_k, cache_win_v,
           state_conv, page_table, w_in, gm_ln_g, gm_ln_b, gm_ws, gm_bs, cmp_wk, cmp_wv, conv_dw, conv_db,
           conv_ln_g, conv_ln_b, conv_pw, w_out, ln1_g, ln1_b, peer_wq, peer_subkeys, peer_u, peer_v, ln2_g, ln2_b):
    bp, sp, _ = x_prompt.shape
    bs, ss, _ = x_sample.shape
    depth = w_in.shape[0]
    past_len = page_table.shape[1] * cache_cmp_k.shape[2]
    pos_p = jnp.arange(sp, dtype=jnp.int32)
    pos_s = past_len + jnp.arange(ss, dtype=jnp.int32)
    cos_p, sin_p = _rope_tables(pos_p)
    cos_s, sin_s = _rope_tables(jnp.tile(pos_s, bs))
    xp = x_prompt.reshape(bp * sp, D_MODEL)
    xs = x_sample.reshape(bs * ss, D_MODEL)
    p_states, s_states = [], []
    caches = [c.reshape(-1, HEAD_DIM) for c in (cache_cmp_k, cache_cmp_v, cache_slc_k, cache_slc_v)]
    for l in range(depth):
        p = _prep_weights(l, w_in, gm_ln_g, gm_ln_b, gm_ws, gm_bs, cmp_wk, cmp_wv, conv_dw, conv_db, conv_ln_g,
                          conv_ln_b, conv_pw, w_out, ln1_g, ln1_b, peer_wq, peer_subkeys, peer_u, peer_v, ln2_g, ln2_b)
        xp, sp_state = _layer_prompt(xp, p, cos_p, sin_p, batch=bp, seq=sp)
        xs, ss_state = _layer_sample(xs, p, cos_s, sin_s, caches, l, page_table, cache_win_k[l], cache_win_v[l],
                                     state_conv[l], batch=bs, seq=ss, past_len=past_len)
        p_states.append(sp_state)
        s_states.append(ss_state)
    stack = lambda states, i: jnp.stack([st[i] for st in states], axis=0)
    outs = [xp.reshape(bp, sp, D_MODEL), xs.reshape(bs, ss, D_MODEL)]
    outs += [stack(p_states, i) for i in range(8)]
    outs += [stack(s_states, i) for i in range(8)]
    return tuple(outs)
```

```python
import functools

import numpy as np
import jax
import jax.numpy as jnp
from jax import lax
from jax.experimental import pallas as pl
from jax.experimental.pallas import tpu as pltpu

F32 = jnp.float32
BF16 = jnp.bfloat16

D_MODEL = 2048
HEAD_DIM = 128
NSA_HEADS = 8
NSA_KV_HEADS = 2
GQA_R = NSA_HEADS // NSA_KV_HEADS
GM_WIDTH = 512
GM_GROUPS = 4
GM_CHUNK = 128
CONV_WIDTH = 512
CONV_GROUPS = 4
CONV_K = 31
NSA_WIDTH = NSA_HEADS * HEAD_DIM
KV_WIDTH = NSA_KV_HEADS * HEAD_DIM
CMP_BLOCK = 32
SEL_BLOCK = 64
N_SELECT = 16
WINDOW = 512
FORCED_SCORE = 1.0e4
PEER_HEADS = 8
N_KEYS = 128
PEER_TOPK = 16
SUBKEY_DIM = 128
ROPE_THETA = 10000.0
LN_EPS = 1e-5
DEPTH = 2
ALPHA = (2 * DEPTH) ** 0.25

C_GU, C_GV, C_CA, C_CB, C_Q = 0, 512, 1024, 1536, 2048
C_KC, C_KS, C_KW, C_ROPE_END = 3072, 3328, 3584, 4096
C_VC, C_VS, C_VW, C_GATE = 4096, 4352, 4608, 4864
PROJ_WIDTH = 5120
PROJ_TN = 512
ROPE_LO, ROPE_HI = C_Q // PROJ_TN, C_ROPE_END // PROJ_TN

V7X_VMEM_LIMIT = 56 * 1024 * 1024
NEG_INF = float("-inf")


def _cparams(n_axes, vmem=V7X_VMEM_LIMIT):
    return pltpu.CompilerParams(dimension_semantics=("arbitrary",) * n_axes, vmem_limit_bytes=vmem)


def _ln_lanes(z, g, b):
    mu = jnp.mean(z, axis=-1, keepdims=True)
    d = z - mu
    var = jnp.mean(d * d, axis=-1, keepdims=True)
    return d * lax.rsqrt(var + LN_EPS) * g + b


def _dot_nt(a, b):
    return lax.dot_general(a, b, (((1,), (1,)), ((), ())), preferred_element_type=F32)


def _proj_kernel(x_ref, w_ref, cos_ref, sin_ref, o_ref, *rest, rope_lo, rope_hi, tn, kv_cols):
    kv_refs, xb_ref = rest[:-1], rest[-1]
    j = pl.program_id(1)
    tm = x_ref.shape[0]

    @pl.when(j == 0)
    def _():
        xb_ref[...] = x_ref[...].astype(BF16)

    acc = jnp.dot(xb_ref[...], w_ref[...], preferred_element_type=F32)
    is_rope = jnp.logical_and(j >= rope_lo, j < rope_hi)

    def emit(blocks):
        for c, blk in enumerate(blocks):
            o_ref[:, c * HEAD_DIM:(c + 1) * HEAD_DIM] = blk
        for ref, col in zip(kv_refs, kv_cols):
            @pl.when(j == col // tn)
            def _():
                first = (col % tn) // HEAD_DIM
                for g in range(NSA_KV_HEADS):
                    ref[pl.ds(g, tm, stride=NSA_KV_HEADS), :] = blocks[first + g]

    @pl.when(is_rope)
    def _():
        cos, sin = cos_ref[...], sin_ref[...]
        blocks = [acc[:, c * HEAD_DIM:(c + 1) * HEAD_DIM] for c in range(tn // HEAD_DIM)]
        emit([blk * cos + pltpu.roll(blk, HEAD_DIM // 2, 1) * sin for blk in blocks])

    @pl.when(jnp.logical_not(is_rope))
    def _():
        emit([acc[:, c * HEAD_DIM:(c + 1) * HEAD_DIM] for c in range(tn // HEAD_DIM)])


def _proj(x, w, cos, sin, *, tm, rope_lo, rope_hi, name, kv_cols=()):
    m, k = x.shape
    n = w.shape[1]
    tn = PROJ_TN
    nper = cos.shape[0] // tm
    kern = functools.partial(_proj_kernel, rope_lo=rope_lo, rope_hi=rope_hi, tn=tn, kv_cols=tuple(kv_cols))
    kv_spec = pl.BlockSpec((NSA_KV_HEADS * tm, HEAD_DIM), lambda i, j: (i, 0))
    kv_shape = jax.ShapeDtypeStruct((NSA_KV_HEADS * m, HEAD_DIM), F32)
    outs = pl.pallas_call(
        kern,
        grid=(m // tm, n // tn),
        in_specs=[
            pl.BlockSpec((tm, k), lambda i, j: (i, 0)),
            pl.BlockSpec((k, tn), lambda i, j: (0, j)),
            pl.BlockSpec((tm, HEAD_DIM), lambda i, j: (i % nper, 0)),
            pl.BlockSpec((tm, HEAD_DIM), lambda i, j: (i % nper, 0)),
        ],
        out_specs=[pl.BlockSpec((tm, tn), lambda i, j: (i, j))] + [kv_spec] * len(kv_cols),
        out_shape=[jax.ShapeDtypeStruct((m, n), F32)] + [kv_shape] * len(kv_cols),
        scratch_shapes=[pltpu.VMEM((tm, k), BF16)],
        compiler_params=_cparams(2),
        name=name,
    )(x, w, cos, sin)
    return outs if kv_cols else outs[0]


GM_CHUNKS_PER_STEP = 8


def _gmlp_kernel(gu_ref, gv_ref, g_ref, b_ref, w_ref, bs_ref, o_ref, vn_ref, *, chunk):
    gw = GM_WIDTH // GM_GROUPS
    for n in range(gu_ref.shape[0] // chunk):
        rows = slice(n * chunk, (n + 1) * chunk)
        for h in range(GM_GROUPS):
            sl = slice(h * gw, (h + 1) * gw)
            v = jax.nn.gelu(gv_ref[rows, sl])
            vn = _ln_lanes(v, g_ref[:, sl], b_ref[:, sl])
            vn_ref[rows, sl] = vn
            s = jnp.dot(w_ref[h], vn.astype(BF16), preferred_element_type=F32) + bs_ref[:, sl]
            o_ref[rows, sl] = jax.nn.gelu(gu_ref[rows, sl]) * s


def _gmlp(h, g, b, w_tril, bs_full, *, chunk):
    m = h.shape[0]
    tile = chunk * (GM_CHUNKS_PER_STEP if (m // chunk) % GM_CHUNKS_PER_STEP == 0 else 1)
    blk = lambda c: pl.BlockSpec((tile, GM_WIDTH), lambda i, c=c: (i, c))
    full = lambda a: pl.BlockSpec(a.shape, lambda i: (0,) * a.ndim)
    return pl.pallas_call(
        functools.partial(_gmlp_kernel, chunk=chunk),
        grid=(m // tile,),
        in_specs=[blk(C_GU // GM_WIDTH), blk(C_GV // GM_WIDTH), full(g), full(b), full(w_tril), full(bs_full)],
        out_specs=[pl.BlockSpec((tile, GM_WIDTH), lambda i: (i, 0))] * 2,
        out_shape=[jax.ShapeDtypeStruct((m, GM_WIDTH), F32)] * 2,
        compiler_params=_cparams(1),
        name="gmlp",
    )(h, h, g, b, w_tril, bs_full)


CONV_HALO = 32


def _conv_kernel(ca_ref, cb_ref, cah_ref, cbh_ref, hist_ref, dw_ref, db_ref, g_ref, b_ref, pw_ref,
                 y_ref, c_ref, buf_ref, *, tq):
    i = pl.program_id(1)
    c = ca_ref[...] * jax.nn.sigmoid(cb_ref[...])
    c_ref[...] = c
    halo = cah_ref[...] * jax.nn.sigmoid(cbh_ref[...])
    halo = jnp.where(i == 0, hist_ref[0], halo)
    buf_ref[0, 0:CONV_HALO, :] = halo
    buf_ref[0, CONV_HALO:CONV_HALO + tq, :] = c
    sub = 8
    span = tq + CONV_HALO - sub
    for r in range(1, sub):
        buf_ref[r, 0:span, :] = buf_ref[0, r:r + span, :]
    lead = CONV_HALO - (CONV_K - 1)
    acc = jnp.zeros((tq, CONV_WIDTH), F32)
    for k in range(CONV_K):
        off = lead + k
        base = off - off % sub
        acc = acc + dw_ref[k:k + 1, :] * buf_ref[off % sub, base:base + tq, :]
    y = acc + db_ref[...]
    cw = CONV_WIDTH // CONV_GROUPS
    parts = []
    for gi in range(CONV_GROUPS):
        sl = slice(gi * cw, (gi + 1) * cw)
        yn = _ln_lanes(y[:, sl], g_ref[:, sl], b_ref[:, sl])
        parts.append((yn * jax.nn.sigmoid(yn)).astype(BF16))
    y_ref[...] = jnp.dot(jnp.concatenate(parts, axis=-1), pw_ref[...], preferred_element_type=F32)


def _conv(h, hist, dw, db, g, b, pw, *, batch, seq, tq):
    m = h.shape[0]
    nt = seq // tq
    row = lambda bi, i: bi * nt + i
    halo_row = lambda bi, i: jnp.maximum((bi * seq + i * tq) // CONV_HALO - 1, 0)
    full = lambda a: pl.BlockSpec(a.shape, lambda bi, i: (0,) * a.ndim)
    kern = functools.partial(_conv_kernel, tq=tq)
    return pl.pallas_call(
        kern,
        grid=(batch, nt),
        in_specs=[
            pl.BlockSpec((tq, CONV_WIDTH), lambda bi, i: (row(bi, i), C_CA // CONV_WIDTH)),
            pl.BlockSpec((tq, CONV_WIDTH), lambda bi, i: (row(bi, i), C_CB // CONV_WIDTH)),
            pl.BlockSpec((CONV_HALO, CONV_WIDTH), lambda bi, i: (halo_row(bi, i), C_CA // CONV_WIDTH)),
            pl.BlockSpec((CONV_HALO, CONV_WIDTH), lambda bi, i: (halo_row(bi, i), C_CB // CONV_WIDTH)),
            pl.BlockSpec((1, CONV_HALO, CONV_WIDTH), lambda bi, i: (bi, 0, 0)),
            full(dw), full(db), full(g), full(b), full(pw),
        ],
        out_specs=[pl.BlockSpec((tq, CONV_WIDTH), lambda bi, i: (row(bi, i), 0))] * 2,
        out_shape=[jax.ShapeDtypeStruct((m, CONV_WIDTH), F32)] * 2,
        scratch_shapes=[pltpu.VMEM((8, CONV_HALO + tq, CONV_WIDTH), F32)],
        compiler_params=_cparams(2),
        name="conv",
    )(h, h, h, h, hist, dw, db, g, b, pw)


def _masked_softmax(s, mask):
    s = jnp.where(mask, s, NEG_INF)
    m = jnp.max(s, axis=-1, keepdims=True)
    m = jnp.where(m > NEG_INF, m, 0.0)
    e = jnp.exp(s - m)
    return e / jnp.maximum(jnp.sum(e, axis=-1, keepdims=True), 1e-30)


def _select_blocks(imp, tpos, n_blocks):
    tq, lanes = imp.shape
    blk = lax.broadcasted_iota(jnp.int32, (tq, lanes), 1)
    cur = tpos // SEL_BLOCK
    valid = blk * SEL_BLOCK <= tpos
    forced = jnp.logical_or(blk == 0, jnp.logical_or(blk == cur, blk == cur - 1))
    score = jnp.where(valid, jnp.where(forced, FORCED_SCORE, imp), NEG_INF)
    beaten = jnp.zeros((tq, lanes), F32)
    for i in range(n_blocks):
        ci = score[:, i:i + 1]
        earlier = (blk > i).astype(F32)
        beaten = beaten + jnp.where(ci > score, 1.0, jnp.where(ci == score, earlier, 0.0))
    picked = jnp.logical_and(beaten < float(min(N_SELECT, n_blocks)), blk < n_blocks)
    return picked.astype(F32)


SLC_CHUNK = 512


def _online_softmax_step(m_ref, l_ref, acc_ref, r, s, mask, v):
    s = jnp.where(mask, s, NEG_INF)
    m_old = m_ref[r]
    m_new = jnp.maximum(m_old, jnp.max(s, axis=-1, keepdims=True))
    m_safe = jnp.where(m_new > NEG_INF, m_new, 0.0)
    alpha = jnp.exp(m_old - m_safe)
    e = jnp.exp(s - m_safe)
    l_ref[r] = alpha * l_ref[r] + jnp.sum(e, axis=-1, keepdims=True)
    acc_ref[r] = alpha * acc_ref[r] + jnp.dot(e.astype(BF16), v, preferred_element_type=F32)
    m_ref[r] = m_new


VT_CHUNK = 256


def _select_blocks_t(imp_t, tpos_row, n_blocks):
    nb, tq = imp_t.shape
    blk = lax.broadcasted_iota(jnp.int32, (nb, tq), 0)
    cur = tpos_row // SEL_BLOCK
    valid = blk * SEL_BLOCK <= tpos_row
    forced = jnp.logical_or(blk == 0, jnp.logical_or(blk == cur, blk == cur - 1))
    score = jnp.where(valid, jnp.where(forced, FORCED_SCORE, imp_t), NEG_INF)
    beaten = jnp.zeros((nb, tq), F32)
    for i in range(n_blocks):
        ci = score[i:i + 1, :]
        earlier = (blk > i).astype(F32)
        beaten = beaten + jnp.where(ci > score, 1.0, jnp.where(ci == score, earlier, 0.0))
    return (beaten < float(min(N_SELECT, n_blocks))).astype(F32)


def _col_softmax(s_t, mask_t):
    s = jnp.where(mask_t, s_t, NEG_INF)
    m = jnp.max(s, axis=0, keepdims=True)
    m = jnp.where(m > NEG_INF, m, 0.0)
    e = jnp.exp(s - m)
    return e, jnp.sum(e, axis=0, keepdims=True)


def _col_softmax_step(m_ref, l_ref, acc_ref, r, s_t, mask_t, v_t):
    s = jnp.where(mask_t, s_t, NEG_INF)
    m_old = m_ref[r]
    m_new = jnp.maximum(m_old, jnp.max(s, axis=0, keepdims=True))
    m_safe = jnp.where(m_new > NEG_INF, m_new, 0.0)
    alpha = jnp.exp(m_old - m_safe)
    e = jnp.exp(s - m_safe)
    l_ref[r] = alpha * l_ref[r] + jnp.sum(e, axis=0, keepdims=True)
    acc_ref[r] = alpha * acc_ref[r] + jnp.dot(v_t, e.astype(BF16), preferred_element_type=F32)
    m_ref[r] = m_new


def _nsa_prompt_kernel(q_ref, kc_ref, ks_ref, kw_ref, vc_ref, vs_ref, vw_ref, gate_ref, cw_ref, o_ref,
                       kb_ref, vt_ref, kcmp_ref, vcmpt_ref, m_ref, l_ref, acc_ref, *, tq, seq):
    qi = pl.program_id(1)
    t0 = qi * tq
    scale = HEAD_DIM ** -0.5
    n_slc = seq // SEL_BLOCK
    tpos = t0 + lax.broadcasted_iota(jnp.int32, (1, tq), 1)
    row_c = lax.broadcasted_iota(jnp.int32, (2 * n_slc, tq), 0)
    cmp_last = jnp.where(row_c < n_slc, row_c * SEL_BLOCK + CMP_BLOCK - 1, (row_c - n_slc) * SEL_BLOCK + SEL_BLOCK - 1)
    cmp_mask = cmp_last <= tpos
    win_len = tq + WINDOW
    w0 = pl.multiple_of(jnp.maximum(t0 - WINDOW, 0), tq)
    wpos = w0 + lax.broadcasted_iota(jnp.int32, (win_len, tq), 0)
    win_mask = jnp.logical_and(wpos <= tpos, wpos >= tpos - WINDOW)
    gates_t = jax.nn.sigmoid(gate_ref[:, 0:HEAD_DIM]).T

    @pl.when(qi == 0)
    def _():
        kb_ref[0] = ks_ref[0].astype(BF16)
        kb_ref[1] = kw_ref[0].astype(BF16)
        for g in range(NSA_KV_HEADS):
            gs = slice(g * HEAD_DIM, (g + 1) * HEAD_DIM)
            for n, ref in enumerate((vs_ref, vw_ref)):
                for c in range(seq // VT_CHUNK):
                    vt_ref[n, g, c] = ref[0, c * VT_CHUNK:(c + 1) * VT_CHUNK, gs].T.astype(BF16)
            halves = []
            for which, ref in enumerate((kc_ref, vc_ref)):
                x3 = ref[0, :, gs].reshape(n_slc, SEL_BLOCK, HEAD_DIM)
                ev = jnp.sum(x3 * cw_ref[g, which, 0][None], axis=1)
                od = jnp.sum(x3 * cw_ref[g, which, 1][None], axis=1)
                halves.append(jnp.concatenate([ev, od], axis=0))
            kcmp_ref[g] = halves[0].astype(BF16)
            vcmpt_ref[g] = jnp.concatenate([halves[1], jnp.zeros_like(halves[1])], axis=0).T.astype(BF16)

    for g in range(NSA_KV_HEADS):
        gs = slice(g * HEAD_DIM, (g + 1) * HEAD_DIM)
        kcmp, vcmp_t = kcmp_ref[g], vcmpt_ref[g][:, :2 * n_slc]
        qg = jnp.concatenate([(q_ref[:, (g * GQA_R + r) * HEAD_DIM:(g * GQA_R + r + 1) * HEAD_DIM] * scale).astype(BF16)
                              for r in range(GQA_R)], axis=0)
        heads = lambda a: jnp.concatenate([a] * GQA_R, axis=1)
        e, l = _col_softmax(_dot_nt(kcmp, qg), heads(cmp_mask))
        p = e / jnp.maximum(l, 1e-30)
        o_cmp = jnp.dot(vcmp_t, p.astype(BF16), preferred_element_type=F32)
        imp = p[:, 0:tq]
        for r in range(1, GQA_R):
            imp = imp + p[:, r * tq:(r + 1) * tq]
        sel = _select_blocks_t(imp[:n_slc] + imp[n_slc:], tpos, n_slc).astype(BF16)

        m_ref[...] = jnp.full(m_ref.shape, NEG_INF, F32)
        l_ref[...] = jnp.zeros_like(l_ref)
        acc_ref[...] = jnp.zeros_like(acc_ref)
        for j in range(seq // SLC_CHUNK):
            @pl.when(j * SLC_CHUNK < t0 + tq)
            def _():
                kpos = j * SLC_CHUNK + lax.broadcasted_iota(jnp.int32, (SLC_CHUNK, tq), 0)
                key_blk = (j * SLC_CHUNK + lax.broadcasted_iota(jnp.int32, (SLC_CHUNK, n_slc), 0)) // SEL_BLOCK
                expand = (lax.broadcasted_iota(jnp.int32, (SLC_CHUNK, n_slc), 1) == key_blk).astype(BF16)
                sel_keys = jnp.dot(expand, sel, preferred_element_type=F32)
                mask = jnp.logical_and(sel_keys > 0.5, kpos <= tpos)
                k = kb_ref[0, j * SLC_CHUNK:(j + 1) * SLC_CHUNK, gs]
                per = SLC_CHUNK // VT_CHUNK
                v_t = jnp.concatenate([vt_ref[0, g, j * per + c] for c in range(per)], axis=1)
                _col_softmax_step(m_ref, l_ref, acc_ref, 0, _dot_nt(k, qg), heads(mask), v_t)

        kwin = kb_ref[1, pl.ds(w0, win_len), gs]
        wc0 = w0 // VT_CHUNK
        vwin_t = jnp.concatenate([vt_ref[1, g, wc0 + c] for c in range(win_len // VT_CHUNK)], axis=1)
        o_slc = acc_ref[0] / jnp.maximum(l_ref[0], 1e-30)
        e, l = _col_softmax(_dot_nt(kwin, qg), heads(win_mask))
        o_win = jnp.dot(vwin_t, e.astype(BF16), preferred_element_type=F32) / jnp.maximum(l, 1e-30)
        for r in range(GQA_R):
            hh = g * GQA_R + r
            cs = slice(r * tq, (r + 1) * tq)
            o_t = (gates_t[3 * hh:3 * hh + 1] * o_cmp[:, cs] + gates_t[3 * hh + 1:3 * hh + 2] * o_slc[:, cs]
                   + gates_t[3 * hh + 2:3 * hh + 3] * o_win[:, cs])
            o_ref[:, hh * HEAD_DIM:(hh + 1) * HEAD_DIM] = o_t.T


def _nsa_prompt(h, cmp_w, *, batch, seq, tq):
    m = h.shape[0]
    nt = seq // tq
    kv = lambda col: pl.BlockSpec((1, seq, KV_WIDTH), lambda bi, i, col=col: (bi, 0, col // KV_WIDTH))
    h3 = h.reshape(batch, seq, PROJ_WIDTH)
    kern = functools.partial(_nsa_prompt_kernel, tq=tq, seq=seq)
    return pl.pallas_call(
        kern,
        grid=(batch, nt),
        in_specs=[
            pl.BlockSpec((tq, NSA_WIDTH), lambda bi, i: (bi * nt + i, C_Q // NSA_WIDTH)),
            kv(C_KC), kv(C_KS), kv(C_KW), kv(C_VC), kv(C_VS), kv(C_VW),
            pl.BlockSpec((tq, KV_WIDTH), lambda bi, i: (bi * nt + i, C_GATE // KV_WIDTH)),
            pl.BlockSpec(cmp_w.shape, lambda bi, i: (0,) * cmp_w.ndim),
        ],
        out_specs=pl.BlockSpec((tq, NSA_WIDTH), lambda bi, i: (bi * nt + i, 0)),
        out_shape=jax.ShapeDtypeStruct((m, NSA_WIDTH), F32),
        scratch_shapes=[
            pltpu.VMEM((2, seq, KV_WIDTH), BF16),
            pltpu.VMEM((2, NSA_KV_HEADS, seq // VT_CHUNK, HEAD_DIM, VT_CHUNK), BF16),
            pltpu.VMEM((NSA_KV_HEADS, 2 * (seq // SEL_BLOCK), HEAD_DIM), BF16),
            pltpu.VMEM((NSA_KV_HEADS, HEAD_DIM, 4 * (seq // SEL_BLOCK)), BF16),
            pltpu.VMEM((1, 1, GQA_R * tq), F32), pltpu.VMEM((1, 1, GQA_R * tq), F32),
            pltpu.VMEM((1, HEAD_DIM, GQA_R * tq), F32),
        ],
        compiler_params=_cparams(2),
        name="nsa_prompt",
    )(h, h3, h3, h3, h3, h3, h3, h, cmp_w)


def _out_ln_kernel(gm_ref, nsa_ref, y_ref, x_ref, w_ref, g_ref, b_ref, o_ref):
    cat = jnp.concatenate([gm_ref[...].astype(BF16), nsa_ref[...].astype(BF16), y_ref[...].astype(BF16)], axis=-1)
    mix = jnp.dot(cat, w_ref[...], preferred_element_type=F32)
    o_ref[...] = _ln_lanes(ALPHA * x_ref[...] + mix, g_ref[...], b_ref[...])


def _out_ln(gm, nsa, y, x, w, g, b, *, tm):
    m = x.shape[0]
    row = lambda width: pl.BlockSpec((tm, width), lambda i: (i, 0))
    full = lambda a: pl.BlockSpec(a.shape, lambda i: (0,) * a.ndim)
    return pl.pallas_call(
        _out_ln_kernel,
        grid=(m // tm,),
        in_specs=[row(GM_WIDTH), row(NSA_WIDTH), row(CONV_WIDTH), row(D_MODEL), full(w), full(g), full(b)],
        out_specs=row(D_MODEL),
        out_shape=jax.ShapeDtypeStruct((m, D_MODEL), F32),
        compiler_params=_cparams(1),
        name="out_ln",
    )(gm, nsa, y, x, w, g, b)


def _extract_topk(s, exact_ties, want_rank):
    n_rows, tm = s.shape
    rows = lax.broadcasted_iota(jnp.int32, (n_rows, tm), 0)
    krow = lax.broadcasted_iota(jnp.int32, (PEER_TOPK, tm), 0)
    work = s
    rank = jnp.full((n_rows, tm), float(PEER_TOPK), F32) if want_rank else None
    vals = jnp.zeros((PEER_TOPK, tm), F32)
    for r in range(PEER_TOPK):
        mx = jnp.max(work, axis=0, keepdims=True)
        hit = work == mx
        if exact_ties:
            hit = rows == jnp.min(jnp.where(hit, rows, n_rows), axis=0, keepdims=True)
        if want_rank:
            rank = jnp.where(hit, float(r), rank)
        vals = jnp.where(krow == r, mx, vals)
        work = jnp.where(hit, NEG_INF, work)
    taken = jnp.sum((work == NEG_INF).astype(F32), axis=0, keepdims=True)
    return vals, rank, taken


def _pair_topk(v1, v2):
    tm = v1.shape[1]
    krow = lax.broadcasted_iota(jnp.int32, (PEER_TOPK, tm), 0)
    cnt = jnp.zeros((PEER_TOPK, tm), F32)
    front = v1 + v2[0:1]
    m0 = v1[0:1] + v2[0:1]
    z = jnp.zeros((1, tm), F32)
    for _ in range(PEER_TOPK):
        mx = jnp.max(front, axis=0, keepdims=True)
        idx = jnp.min(jnp.where(front == mx, krow, PEER_TOPK), axis=0, keepdims=True)
        hit = krow == idx
        cnt = cnt + hit.astype(F32)
        z = z + jnp.exp(mx - m0)
        chosen = jnp.sum(jnp.where(hit, cnt, 0.0), axis=0, keepdims=True)
        nxt = jnp.full((1, tm), NEG_INF, F32)
        for c in range(1, PEER_TOPK):
            nxt = jnp.where(chosen == float(c), v2[c:c + 1], nxt)
        front = jnp.where(hit, v1 + nxt, front)
    return cnt, z


def _route_kernel(x_ref, wq_ref, sk_ref, r2_ref, cnt_ref, f_ref, e2_ref, xb_ref):
    @pl.when(pl.program_id(1) == 0)
    def _():
        xb_ref[...] = x_ref[...].astype(BF16)

    qp = jnp.dot(xb_ref[...], wq_ref[...], preferred_element_type=F32)
    q1 = qp[:, 0:SUBKEY_DIM].astype(BF16)
    q2 = qp[:, SUBKEY_DIM:2 * SUBKEY_DIM].astype(BF16)
    s1 = _dot_nt(sk_ref[0, 0], q1)
    s2 = _dot_nt(sk_ref[0, 1], q2)

    def emit(v1, v2, cnt, z, cnt_a, rank2):
        r2_ref[0] = rank2.astype(BF16)
        cnt_ref[0] = cnt_a
        f_ref[0] = jnp.exp(s1 - v1[0:1]) / z
        e2_ref[0] = jnp.exp(s2 - v2[0:1]).astype(BF16)

    v1, _, taken1 = _extract_topk(s1, exact_ties=False, want_rank=False)
    v2, rank2, taken2 = _extract_topk(s2, exact_ties=False, want_rank=True)
    cnt, z = _pair_topk(v1, v2)
    cnt_a = jnp.zeros(s1.shape, F32)
    for r in range(PEER_TOPK):
        cnt_a = jnp.where(s1 == v1[r:r + 1], cnt[r:r + 1], cnt_a)
    emit(v1, v2, cnt, z, cnt_a, rank2)

    @pl.when(jnp.max(jnp.maximum(taken1, taken2)) > float(PEER_TOPK))
    def _():
        v1, rank1, _ = _extract_topk(s1, exact_ties=True, want_rank=True)
        v2, rank2, _ = _extract_topk(s2, exact_ties=True, want_rank=True)
        cnt, z = _pair_topk(v1, v2)
        cnt_a = jnp.zeros(s1.shape, F32)
        for r in range(PEER_TOPK):
            cnt_a = jnp.where(rank1 == float(r), cnt[r:r + 1], cnt_a)
        emit(v1, v2, cnt, z, cnt_a, rank2)


def _route(x, wq, subkeys, *, tm):
    m, k = x.shape
    out = lambda dt: jax.ShapeDtypeStruct((PEER_HEADS, N_KEYS, m), dt)
    ospec = pl.BlockSpec((1, N_KEYS, tm), lambda i, hd: (hd, 0, i))
    return pl.pallas_call(
        _route_kernel,
        grid=(m // tm, PEER_HEADS),
        in_specs=[
            pl.BlockSpec((tm, k), lambda i, hd: (i, 0)),
            pl.BlockSpec((k, 2 * SUBKEY_DIM), lambda i, hd: (0, hd)),
            pl.BlockSpec((1, 2, N_KEYS, SUBKEY_DIM), lambda i, hd: (hd, 0, 0, 0)),
        ],
        out_specs=[ospec] * 4,
        out_shape=[out(BF16), out(F32), out(F32), out(BF16)],
        scratch_shapes=[pltpu.VMEM((tm, k), BF16)],
        compiler_params=_cparams(2),
        name="peer_route",
    )(x, wq, subkeys)


PEER_SUB = 256
PEER_TE = 1024


def _peer_dense_kernel(x_ref, u_ref, v_ref, r2_ref, cnt_ref, f_ref, e2_ref, g_ref, b_ref, o_ref, xb_ref, *, te):
    j = pl.program_id(1)

    @pl.when(j == 0)
    def _():
        xb_ref[...] = x_ref[...].astype(BF16)
        o_ref[...] = jnp.zeros_like(o_ref)

    cols = []
    for c in range(te // PEER_SUB):
        act = jax.nn.gelu(_dot_nt(u_ref[c * PEER_SUB:(c + 1) * PEER_SUB, :], xb_ref[...]).astype(BF16))
        rows = []
        for al in range(PEER_SUB // N_KEYS):
            a = c * (PEER_SUB // N_KEYS) + al
            gate = None
            for hd in range(PEER_HEADS):
                cnt = cnt_ref[hd, a:a + 1, :].astype(BF16)
                f = f_ref[hd, a:a + 1, :].astype(BF16)
                term = jnp.where(r2_ref[hd] < cnt, e2_ref[hd], jnp.zeros((), BF16)) * f
                gate = term if gate is None else gate + term
            rows.append(act[al * N_KEYS:(al + 1) * N_KEYS] * gate)
        cols.append(jnp.concatenate(rows, axis=0).T)
    o_ref[...] += jnp.dot(jnp.concatenate(cols, axis=1), v_ref[...], preferred_element_type=F32)

    @pl.when(j == pl.num_programs(1) - 1)
    def _():
        o_ref[...] = _ln_lanes(ALPHA * x_ref[...] + o_ref[...], g_ref[...], b_ref[...])


def _peer_dense(x, u, v, layer, r2, cnt, f, e2, g, b, *, tm, te):
    m = x.shape[0]
    n_exp = u.shape[0] // DEPTH
    first = layer * (n_exp // te)
    by_b = pl.BlockSpec((PEER_HEADS, N_KEYS, tm), lambda i, j: (0, 0, i))
    by_a = pl.BlockSpec((PEER_HEADS, te // N_KEYS, tm), lambda i, j: (0, j, i))
    full = lambda a: pl.BlockSpec(a.shape, lambda i, j: (0,) * a.ndim)
    kern = functools.partial(_peer_dense_kernel, te=te)
    return pl.pallas_call(
        kern,
        grid=(m // tm, n_exp // te),
        in_specs=[
            pl.BlockSpec((tm, D_MODEL), lambda i, j: (i, 0), pipeline_mode=pl.Buffered(1)),
            pl.BlockSpec((te, D_MODEL), lambda i, j: (first + j, 0)),
            pl.BlockSpec((te, D_MODEL), lambda i, j: (first + j, 0)),
            by_b, by_a, by_a, by_b, full(g), full(b),
        ],
        out_specs=pl.BlockSpec((tm, D_MODEL), lambda i, j: (i, 0), pipeline_mode=pl.Buffered(1)),
        out_shape=jax.ShapeDtypeStruct((m, D_MODEL), F32),
        scratch_shapes=[pltpu.VMEM((tm, D_MODEL), BF16)],
        compiler_params=_cparams(2),
        name="peer_dense",
    )(x, u, v, r2, cnt, f, e2, g, b)


PAGES_PER_STEP = 32


def _group_queries(q_ref, g):
    heads = [q_ref[:, (g * GQA_R + r) * HEAD_DIM:(g * GQA_R + r + 1) * HEAD_DIM] for r in range(GQA_R)]
    return jnp.concatenate(heads, axis=0).astype(BF16)


def _page_rows(pages, g):
    n = pages[0].shape[0] // NSA_KV_HEADS
    return jnp.concatenate([r[pl.ds(g, n, stride=NSA_KV_HEADS), :] for r in pages], axis=0)


def _nsa_s_cmp_kernel(pt_ref, q_ref, cw_ref, *refs, seq, past_len, n_real, sel_lanes):
    nps = PAGES_PER_STEP
    k_pages, v_pages = refs[:nps], refs[nps:2 * nps]
    ocmp_ref, sel_ref, ke_ref, ko_ref, ve_ref, vo_ref = refs[2 * nps:]
    c = pl.program_id(1)
    n_pb = ke_ref.shape[0]
    rows = nps * (k_pages[0].shape[0] // NSA_KV_HEADS) // SEL_BLOCK
    off = pl.multiple_of(c * rows, rows)

    def compress(pages, which, even_ref, odd_ref):
        for g in range(NSA_KV_HEADS):
            gs = slice(g * HEAD_DIM, (g + 1) * HEAD_DIM)
            x3 = _page_rows(pages, g).reshape(rows, SEL_BLOCK, HEAD_DIM)
            even_ref[pl.ds(off, rows), gs] = jnp.sum(x3 * cw_ref[g, which, 0][None], axis=1)
            odd_ref[pl.ds(off, rows), gs] = jnp.sum(x3 * cw_ref[g, which, 1][None], axis=1)

    compress(k_pages, 0, ke_ref, ko_ref)
    compress(v_pages, 1, ve_ref, vo_ref)

    @pl.when(c == pl.num_programs(1) - 1)
    def _():
        scale = HEAD_DIM ** -0.5
        nq = GQA_R * seq
        qrow = lax.broadcasted_iota(jnp.int32, (nq, 1), 0)
        qpos = past_len + qrow % seq
        lane = lax.broadcasted_iota(jnp.int32, (nq, 2 * n_pb), 1)
        cmp_last = jnp.where(lane < n_pb, lane * SEL_BLOCK + CMP_BLOCK - 1, (lane - n_pb) * SEL_BLOCK + SEL_BLOCK - 1)
        cmp_mask = cmp_last <= qpos
        tpos = past_len + lax.broadcasted_iota(jnp.int32, (seq, 1), 0)
        for g in range(NSA_KV_HEADS):
            gs = slice(g * HEAD_DIM, (g + 1) * HEAD_DIM)
            kcat = jnp.concatenate([ke_ref[:, gs], ko_ref[:, gs]], axis=0).astype(BF16)
            vcat = jnp.concatenate([ve_ref[:, gs], vo_ref[:, gs]], axis=0).astype(BF16)
            p = _masked_softmax(_dot_nt(_group_queries(q_ref, g), kcat) * scale, cmp_mask)
            ocmp_ref[0, g] = jnp.dot(p.astype(BF16), vcat, preferred_element_type=F32)
            imp = p[0:seq]
            for r in range(1, GQA_R):
                imp = imp + p[r * seq:(r + 1) * seq]
            imp = imp[:, :n_pb] + imp[:, n_pb:]
            imp = jnp.concatenate([imp, jnp.zeros((seq, sel_lanes - n_pb), F32)], axis=1)
            sel = _select_blocks(imp, tpos, n_real)
            bpc = sel_ref.shape[-1]
            for ch in range(sel_ref.shape[2]):
                sel_ref[0, g, ch] = sel[:, ch * bpc:(ch + 1) * bpc]


def _nsa_s_slc_kernel(pt_ref, q_ref, sel_ref, sel_new_ref, ocmp_ref, ksn_ref, vsn_ref, wk_ref, wv_ref, gate_ref, *refs,
                      seq, past_len, win_buf):
    nps = PAGES_PER_STEP
    k_pages, v_pages = refs[:nps], refs[nps:2 * nps]
    o_ref, m_ref, l_ref, acc_ref = refs[2 * nps:]
    c = pl.program_id(1)
    scale = HEAD_DIM ** -0.5
    nq = GQA_R * seq
    chunk = nps * (k_pages[0].shape[0] // NSA_KV_HEADS)
    qrow = lax.broadcasted_iota(jnp.int32, (nq, 1), 0)
    qpos = past_len + qrow % seq

    @pl.when(c == 0)
    def _():
        m_ref[...] = jnp.full(m_ref.shape, NEG_INF, F32)
        l_ref[...] = jnp.zeros_like(l_ref)
        acc_ref[...] = jnp.zeros_like(acc_ref)

    def online_update(g, s, mask, v):
        _online_softmax_step(m_ref, l_ref, acc_ref, g, s, mask, v)

    def key_mask(sel_blocks, first_pos, n_keys):
        bpc = sel_blocks.shape[1]
        key_blk = lax.broadcasted_iota(jnp.int32, (bpc, n_keys), 1) // SEL_BLOCK
        expand = (lax.broadcasted_iota(jnp.int32, (bpc, n_keys), 0) == key_blk).astype(BF16)
        sel_keys = jnp.dot(sel_blocks.astype(BF16), expand, preferred_element_type=F32)
        sel_keys = jnp.concatenate([sel_keys] * GQA_R, axis=0)
        kpos = first_pos + lax.broadcasted_iota(jnp.int32, (nq, n_keys), 1)
        return jnp.logical_and(sel_keys > 0.5, kpos <= qpos)

    for g in range(NSA_KV_HEADS):
        mask = key_mask(sel_ref[0, g, 0], c * chunk, chunk)
        scores = _dot_nt(_group_queries(q_ref, g), _page_rows(k_pages, g).astype(BF16)) * scale
        online_update(g, scores, mask, _page_rows(v_pages, g).astype(BF16))

    @pl.when(c == pl.num_programs(1) - 1)
    def _():
        gates = jax.nn.sigmoid(gate_ref[:, 0:HEAD_DIM])
        n_new = ksn_ref.shape[1]
        n_win = wk_ref.shape[1]
        wpos = past_len - win_buf + lax.broadcasted_iota(jnp.int32, (nq, n_win), 1)
        win_mask = jnp.logical_and(wpos <= qpos, wpos >= qpos - WINDOW)
        for g in range(NSA_KV_HEADS):
            gs = slice(g * HEAD_DIM, (g + 1) * HEAD_DIM)
            qg = _group_queries(q_ref, g)
            online_update(g, _dot_nt(qg, ksn_ref[0, :, gs].astype(BF16)) * scale,
                          key_mask(sel_new_ref[0, g, 0], past_len, n_new), vsn_ref[0, :, gs].astype(BF16))
            o_slc = acc_ref[g] / jnp.maximum(l_ref[g], 1e-30)
            p = _masked_softmax(_dot_nt(qg, wk_ref[0, :, gs].astype(BF16)) * scale, win_mask)
            o_win = jnp.dot(p.astype(BF16), wv_ref[0, :, gs].astype(BF16), preferred_element_type=F32)
            o_cmp = ocmp_ref[0, g]
            for r in range(GQA_R):
                hh = g * GQA_R + r
                rs = slice(r * seq, (r + 1) * seq)
                o_ref[:, hh * HEAD_DIM:(hh + 1) * HEAD_DIM] = (
                    gates[:, 3 * hh:3 * hh + 1] * o_cmp[rs] + gates[:, 3 * hh + 1:3 * hh + 2] * o_slc[rs]
                    + gates[:, 3 * hh + 2:3 * hh + 3] * o_win[rs])


def _nsa_sample(h, caches, layer, page_table, ks_new, vs_new, wk_all, wv_all, cw, *, batch, seq, past_len, win_buf):
    cmp_k, cmp_v, slc_k, slc_v = caches
    nps = PAGES_PER_STEP
    n_pages, page = page_table.shape[1], ks_new.shape[1]
    page_rows = page * NSA_KV_HEADS
    n_pool = cmp_k.shape[0] // (DEPTH * page_rows)
    n_chunks = n_pages // nps
    n_pb = past_len // SEL_BLOCK
    n_real = n_pb + 1
    bpc = nps * page // SEL_BLOCK
    sel_lanes = -(-max(n_real, (n_chunks + 1) * bpc) // 128) * 128
    sel_shape = (NSA_KV_HEADS, n_chunks + 1, seq, bpc)
    sel_chunk = lambda which: pl.BlockSpec((1, NSA_KV_HEADS, 1, seq, bpc), lambda b, c, pt: (b, 0, which(c), 0, 0))
    nq = GQA_R * seq
    page_spec = lambda k: pl.BlockSpec((page_rows, HEAD_DIM),
                                       lambda b, c, pt, k=k: (pt[b, c * nps + k] + layer * n_pool, 0))
    pages = [page_spec(k) for k in range(nps)]
    q_spec = pl.BlockSpec((seq, NSA_WIDTH), lambda b, c, pt: (b, C_Q // NSA_WIDTH))
    per_b = lambda shape: pl.BlockSpec((1,) + shape, lambda b, c, pt: (b,) + (0,) * len(shape))
    ocmp, sel = pl.pallas_call(
        functools.partial(_nsa_s_cmp_kernel, seq=seq, past_len=past_len, n_real=n_real, sel_lanes=sel_lanes),
        grid_spec=pltpu.PrefetchScalarGridSpec(
            num_scalar_prefetch=1, grid=(batch, n_chunks),
            in_specs=[q_spec, pl.BlockSpec(cw.shape, lambda b, c, pt: (0,) * cw.ndim)] + pages + pages,
            out_specs=[per_b((NSA_KV_HEADS, nq, HEAD_DIM)), per_b(sel_shape)],
            scratch_shapes=[pltpu.VMEM((n_pb, KV_WIDTH), F32)] * 4),
        out_shape=[jax.ShapeDtypeStruct((batch, NSA_KV_HEADS, nq, HEAD_DIM), F32),
                   jax.ShapeDtypeStruct((batch,) + sel_shape, F32)],
        compiler_params=_cparams(2),
        name="nsa_sample_cmp",
    )(page_table, h, cw, *([cmp_k] * nps), *([cmp_v] * nps))
    return pl.pallas_call(
        functools.partial(_nsa_s_slc_kernel, seq=seq, past_len=past_len, win_buf=win_buf),
        grid_spec=pltpu.PrefetchScalarGridSpec(
            num_scalar_prefetch=1, grid=(batch, n_chunks),
            in_specs=[q_spec, sel_chunk(lambda c: c), sel_chunk(lambda c: n_chunks),
                      per_b((NSA_KV_HEADS, nq, HEAD_DIM)),
                      per_b(ks_new.shape[1:]), per_b(vs_new.shape[1:]), per_b(wk_all.shape[1:]), per_b(wv_all.shape[1:]),
                      pl.BlockSpec((seq, KV_WIDTH), lambda b, c, pt: (b, C_GATE // KV_WIDTH))] + pages + pages,
            out_specs=pl.BlockSpec((seq, NSA_WIDTH), lambda b, c, pt: (b, 0)),
            scratch_shapes=[pltpu.VMEM((NSA_KV_HEADS, nq, 1), F32), pltpu.VMEM((NSA_KV_HEADS, nq, 1), F32),
                            pltpu.VMEM((NSA_KV_HEADS, nq, HEAD_DIM), F32)]),
        out_shape=jax.ShapeDtypeStruct((batch * seq, NSA_WIDTH), F32),
        compiler_params=_cparams(2),
        name="nsa_sample_slc",
    )(page_table, h, sel, sel, ocmp, ks_new, vs_new, wk_all, wv_all, h, *([slc_k] * nps), *([slc_v] * nps))


def _rope_tables(pos):
    half = HEAD_DIM // 2
    inv = ROPE_THETA ** (-jnp.arange(half, dtype=F32) / half)
    ang = pos.astype(F32)[:, None] * inv[None, :]
    cos, sin = jnp.cos(ang), jnp.sin(ang)
    return jnp.concatenate([cos, cos], -1), jnp.concatenate([-sin, sin], -1)


def _prep_weights(l, w_in, gm_ln_g, gm_ln_b, gm_ws, gm_bs, cmp_wk, cmp_wv, conv_dw, conv_db, conv_ln_g,
                  conv_ln_b, conv_pw, w_out, ln1_g, ln1_b, peer_wq, peer_subkeys, peer_u, peer_v, ln2_g, ln2_b):
    w = w_in[l]
    o = np.cumsum([0, GM_WIDTH, GM_WIDTH, NSA_WIDTH] + [KV_WIDTH] * 6 + [3 * NSA_HEADS, CONV_WIDTH, CONV_WIDTH])
    gu, gv, q, kc, vc, ks, vs, kw, vw, gate, ca, cb = [w[:, o[i]:o[i + 1]] for i in range(12)]
    zeros = lambda n: jnp.zeros((w.shape[0], n), w.dtype)
    w_proj = jnp.concatenate([gu, gv, ca, cb, q, kc, ks, kw, zeros(C_ROPE_END - C_KW - KV_WIDTH), vc, vs, vw, gate,
                              zeros(PROJ_WIDTH - C_GATE - 3 * NSA_HEADS)], axis=1).astype(BF16)
    row = lambda a: a.reshape(1, -1)
    cw = jnp.stack([cmp_wk[l], cmp_wv[l]], axis=1)
    zero = jnp.zeros_like(cw)
    cw = jnp.stack([jnp.concatenate([cw, zero], -1), jnp.concatenate([zero, cw], -1)], axis=2)
    cw = jnp.broadcast_to(cw[..., None], cw.shape + (HEAD_DIM,))
    dw = jnp.concatenate([conv_dw[l], jnp.zeros((CONV_HALO - CONV_K, CONV_WIDTH), F32)], axis=0)
    return dict(
        w_proj=w_proj, gm_g=row(gm_ln_g[l]), gm_b=row(gm_ln_b[l]), gm_ws=gm_ws[l], gm_bs=gm_bs[l], cmp_w=cw,
        conv_dw=dw, conv_db=row(conv_db[l]), conv_g=row(conv_ln_g[l]), conv_b=row(conv_ln_b[l]),
        conv_pw=conv_pw[l].astype(BF16), w_out=w_out[l].astype(BF16), ln1_g=row(ln1_g[l]), ln1_b=row(ln1_b[l]),
        wq=peer_wq[l].astype(BF16), subkeys=peer_subkeys[l].astype(BF16), layer=l,
        u=peer_u.astype(BF16).reshape(-1, D_MODEL), v=peer_v.astype(BF16).reshape(-1, D_MODEL),
        ln2_g=row(ln2_g[l]), ln2_b=row(ln2_b[l]))


def _gm_weights(p, chunk):
    w = jnp.tril(p["gm_ws"][:, :chunk, :chunk])
    bs = jnp.repeat(p["gm_bs"][:, :chunk].T, GM_WIDTH // GM_GROUPS, axis=1)
    return w, bs


def _peer(x, p, *, tm_route, tm_dense, te):
    r2, cnt, f, e2 = _route(x, p["wq"], p["subkeys"], tm=tm_route)
    return _peer_dense(x, p["u"], p["v"], p["layer"], r2, cnt, f, e2, p["ln2_g"], p["ln2_b"], tm=tm_dense, te=te)


def _kv_rows(h, col, batch, seq):
    return h[:, col:col + KV_WIDTH].reshape(batch, seq, NSA_KV_HEADS, HEAD_DIM)


def _layer_prompt(x, p, cos, sin, *, batch, seq):
    m = batch * seq
    h, *kv_out = _proj(x, p["w_proj"], cos, sin, tm=1024, rope_lo=ROPE_LO, rope_hi=ROPE_HI, name="proj_in",
                       kv_cols=(C_KC, C_VC, C_KS, C_VS, C_KW, C_VW))
    kc, vc, ks, vs, kw, vw = [a.reshape(batch, seq, NSA_KV_HEADS, HEAD_DIM) for a in kv_out]
    gw, gbs = _gm_weights(p, GM_CHUNK)
    gm, vn = _gmlp(h, p["gm_g"], p["gm_b"], gw.astype(BF16), gbs, chunk=GM_CHUNK)
    nsa = _nsa_prompt(h, p["cmp_w"], batch=batch, seq=seq, tq=256)
    hist = jnp.zeros((batch, CONV_HALO, CONV_WIDTH), F32)
    y, c = _conv(h, hist, p["conv_dw"], p["conv_db"], p["conv_g"], p["conv_b"], p["conv_pw"],
                 batch=batch, seq=seq, tq=256)
    x1 = _out_ln(gm, nsa, y, x, p["w_out"], p["ln1_g"], p["ln1_b"], tm=512)
    x2 = _peer(x1, p, tm_route=1024, tm_dense=512, te=PEER_TE)
    keep = min(WINDOW, seq)
    start = ((seq - 1) // GM_CHUNK) * GM_CHUNK
    states = (kc, vc, ks, vs, kw[:, -keep:], vw[:, -keep:],
              c.reshape(batch, seq, CONV_WIDTH)[:, -(CONV_K - 1):], vn.reshape(batch, seq, GM_WIDTH)[:, start:])
    return x2, states


def _layer_sample(x, p, cos, sin, caches, layer, page_table, win_k, win_v, hist30, *, batch, seq, past_len):
    m = batch * seq
    h = _proj(x, p["w_proj"], cos, sin, tm=m, rope_lo=ROPE_LO, rope_hi=ROPE_HI, name="proj_in_s")
    pad_rows = lambda a: jnp.pad(a.reshape(batch, seq, -1), ((0, 0), (0, GM_CHUNK - seq), (0, 0))).reshape(batch * GM_CHUNK, -1)
    gw, gbs = _gm_weights(p, GM_CHUNK)
    hp = pad_rows(h[:, :2 * GM_WIDTH])
    gm, vn = _gmlp(hp, p["gm_g"], p["gm_b"], gw.astype(BF16), gbs, chunk=GM_CHUNK)
    unpad = lambda a: a.reshape(batch, GM_CHUNK, -1)[:, :seq].reshape(m, -1)
    gm, vn = unpad(gm), unpad(vn)
    kvs = [_kv_rows(h, col, batch, seq) for col in (C_KC, C_VC, C_KS, C_VS, C_KW, C_VW)]
    win_buf = win_k.shape[1]
    page = past_len // page_table.shape[1]
    new_rows = lambda col, rows: jnp.pad(h[:, col:col + KV_WIDTH].reshape(batch, seq, KV_WIDTH),
                                         ((0, 0), (0, rows - seq), (0, 0)))
    flat = lambda a: a.reshape(batch, win_buf, KV_WIDTH)
    wk_all = jnp.concatenate([flat(win_k), new_rows(C_KW, HEAD_DIM)], axis=1)
    wv_all = jnp.concatenate([flat(win_v), new_rows(C_VW, HEAD_DIM)], axis=1)
    nsa = _nsa_sample(h, caches, layer, page_table, new_rows(C_KS, page), new_rows(C_VS, page), wk_all, wv_all,
                      p["cmp_w"], batch=batch, seq=seq, past_len=past_len, win_buf=win_buf)
    keep = min(WINDOW, past_len + seq)
    unflat = lambda a: a[:, :win_buf + seq][:, -keep:].reshape(batch, keep, NSA_KV_HEADS, HEAD_DIM)
    win_k, win_v = unflat(wk_all), unflat(wv_all)
    hist = jnp.pad(hist30, ((0, 0), (CONV_HALO - (CONV_K - 1), 0), (0, 0)))
    y, c = _conv(h, hist, p["conv_dw"], p["conv_db"], p["conv_g"], p["conv_b"], p["conv_pw"],
                 batch=batch, seq=seq, tq=seq)
    x1 = _out_ln(gm, nsa, y, x, p["w_out"], p["ln1_g"], p["ln1_b"], tm=m)
    x2 = _peer(x1, p, tm_route=m, tm_dense=m, te=PEER_TE)
    conv_state = jnp.concatenate([hist30, c.reshape(batch, seq, CONV_WIDTH)], axis=1)[:, -(CONV_K - 1):]
    start = ((seq - 1) // GM_CHUNK) * GM_CHUNK
    states = (kvs[0], kvs[1], kvs[2], kvs[3], win_k, win_v, conv_state, vn.reshape(batch, seq, GM_WIDTH)[:, start:])
    return x2, states


def kernel(x_prompt, x_sample, cache_cmp_k, cache_cmp_v, cache_slc_k, cache_slc_v, cache_win_k, cache_win_v,
           state_conv, page_table, w_in, gm_ln_g, gm_ln_b, gm_ws, gm_bs, cmp_wk, cmp_wv, conv_dw, conv_db,
           conv_ln_g, conv_ln_b, conv_pw, w_out, ln1_g, ln1_b, peer_wq, peer_subkeys, peer_u, peer_v, ln2_g, ln2_b):
    bp, sp, _ = x_prompt.shape
    bs, ss, _ = x_sample.shape
    depth = w_in.shape[0]
    past_len = page_table.shape[1] * cache_cmp_k.shape[2]
    pos_p = jnp.arange(sp, dtype=jnp.int32)
    pos_s = past_len + jnp.arange(ss, dtype=jnp.int32)
    cos_p, sin_p = _rope_tables(pos_p)
    cos_s, sin_s = _rope_tables(jnp.tile(pos_s, bs))
    xp = x_prompt.reshape(bp * sp, D_MODEL)
    xs = x_sample.reshape(bs * ss, D_MODEL)
    p_states, s_states = [], []
    caches = [c.reshape(-1, HEAD_DIM) for c in (cache_cmp_k, cache_cmp_v, cache_slc_k, cache_slc_v)]
    for l in range(depth):
        p = _prep_weights(l, w_in, gm_ln_g, gm_ln_b, gm_ws, gm_bs, cmp_wk, cmp_wv, conv_dw, conv_db, conv_ln_g,
                          conv_ln_b, conv_pw, w_out, ln1_g, ln1_b, peer_wq, peer_subkeys, peer_u, peer_v, ln2_g, ln2_b)
        xp, sp_state = _layer_prompt(xp, p, cos_p, sin_p, batch=bp, seq=sp)
        xs, ss_state = _layer_sample(xs, p, cos_s, sin_s, caches, l, page_table, cache_win_k[l], cache_win_v[l],
                                     state_conv[l], batch=bs, seq=ss, past_len=past_len)
        p_states.append(sp_state)
        s_states.append(ss_state)
    stack = lambda states, i: jnp.stack([st[i] for st in states], axis=0)
    outs = [xp.reshape(bp, sp, D_MODEL), xs.reshape(bs, ss, D_MODEL)]
    outs += [stack(p_states, i) for i in range(8)]
    outs += [stack(s_states, i) for i in range(8)]
    return tuple(outs)
```

```python
import functools

import numpy as np
import jax
import jax.numpy as jnp
from jax import lax
from jax.experimental import pallas as pl
from jax.experimental.pallas import tpu as pltpu

F32 = jnp.float32
BF16 = jnp.bfloat16

D_MODEL = 2048
HEAD_DIM = 128
NSA_HEADS = 8
NSA_KV_HEADS = 2
GQA_R = NSA_HEADS // NSA_KV_HEADS
GM_WIDTH = 512
GM_GROUPS = 4
GM_CHUNK = 128
CONV_WIDTH = 512
CONV_GROUPS = 4
CONV_K = 31
NSA_WIDTH = NSA_HEADS * HEAD_DIM
KV_WIDTH = NSA_KV_HEADS * HEAD_DIM
CMP_BLOCK = 32
SEL_BLOCK = 64
N_SELECT = 16
WINDOW = 512
FORCED_SCORE = 1.0e4
PEER_HEADS = 8
N_KEYS = 128
PEER_TOPK = 16
SUBKEY_DIM = 128
ROPE_THETA = 10000.0
LN_EPS = 1e-5
DEPTH = 2
ALPHA = (2 * DEPTH) ** 0.25

C_GU, C_GV, C_CA, C_CB, C_Q = 0, 512, 1024, 1536, 2048
C_KC, C_KS, C_KW, C_ROPE_END = 3072, 3328, 3584, 4096
C_VC, C_VS, C_VW, C_GATE = 4096, 4352, 4608, 4864
PROJ_WIDTH = 5120
PROJ_TN = 512
ROPE_LO, ROPE_HI = C_Q // PROJ_TN, C_ROPE_END // PROJ_TN

V7X_VMEM_LIMIT = 56 * 1024 * 1024
NEG_INF = float("-inf")


def _cparams(n_axes, vmem=V7X_VMEM_LIMIT):
    return pltpu.CompilerParams(dimension_semantics=("arbitrary",) * n_axes, vmem_limit_bytes=vmem)


def _ln_lanes(z, g, b):
    mu = jnp.mean(z, axis=-1, keepdims=True)
    d = z - mu
    var = jnp.mean(d * d, axis=-1, keepdims=True)
    return d * lax.rsqrt(var + LN_EPS) * g + b


def _dot_nt(a, b):
    return lax.dot_general(a, b, (((1,), (1,)), ((), ())), preferred_element_type=F32)


def _proj_kernel(x_ref, w_ref, cos_ref, sin_ref, o_ref, *rest, rope_lo, rope_hi, tn, kv_cols):
    kv_refs, xb_ref = rest[:-1], rest[-1]
    j = pl.program_id(1)
    tm = x_ref.shape[0]

    @pl.when(j == 0)
    def _():
        xb_ref[...] = x_ref[...].astype(BF16)

    acc = jnp.dot(xb_ref[...], w_ref[...], preferred_element_type=F32)
    is_rope = jnp.logical_and(j >= rope_lo, j < rope_hi)

    def emit(blocks):
        for c, blk in enumerate(blocks):
            o_ref[:, c * HEAD_DIM:(c + 1) * HEAD_DIM] = blk
        for ref, col in zip(kv_refs, kv_cols):
            @pl.when(j == col // tn)
            def _():
                first = (col % tn) // HEAD_DIM
                for g in range(NSA_KV_HEADS):
                    ref[pl.ds(g, tm, stride=NSA_KV_HEADS), :] = blocks[first + g]

    @pl.when(is_rope)
    def _():
        cos, sin = cos_ref[...], sin_ref[...]
        blocks = [acc[:, c * HEAD_DIM:(c + 1) * HEAD_DIM] for c in range(tn // HEAD_DIM)]
        emit([blk * cos + pltpu.roll(blk, HEAD_DIM // 2, 1) * sin for blk in blocks])

    @pl.when(jnp.logical_not(is_rope))
    def _():
        emit([acc[:, c * HEAD_DIM:(c + 1) * HEAD_DIM] for c in range(tn // HEAD_DIM)])


def _proj(x, w, cos, sin, *, tm, rope_lo, rope_hi, name, kv_cols=()):
    m, k = x.shape
    n = w.shape[1]
    tn = PROJ_TN
    nper = cos.shape[0] // tm
    kern = functools.partial(_proj_kernel, rope_lo=rope_lo, rope_hi=rope_hi, tn=tn, kv_cols=tuple(kv_cols))
    kv_spec = pl.BlockSpec((NSA_KV_HEADS * tm, HEAD_DIM), lambda i, j: (i, 0))
    kv_shape = jax.ShapeDtypeStruct((NSA_KV_HEADS * m, HEAD_DIM), F32)
    outs = pl.pallas_call(
        kern,
        grid=(m // tm, n // tn),
        in_specs=[
            pl.BlockSpec((tm, k), lambda i, j: (i, 0)),
            pl.BlockSpec((k, tn), lambda i, j: (0, j)),
            pl.BlockSpec((tm, HEAD_DIM), lambda i, j: (i % nper, 0)),
            pl.BlockSpec((tm, HEAD_DIM), lambda i, j: (i % nper, 0)),
        ],
        out_specs=[pl.BlockSpec((tm, tn), lambda i, j: (i, j))] + [kv_spec] * len(kv_cols),
        out_shape=[jax.ShapeDtypeStruct((m, n), F32)] + [kv_shape] * len(kv_cols),
        scratch_shapes=[pltpu.VMEM((tm, k), BF16)],
        compiler_params=_cparams(2),
        name=name,
    )(x, w, cos, sin)
    return outs if kv_cols else outs[0]


GM_CHUNKS_PER_STEP = 8


def _gmlp_kernel(gu_ref, gv_ref, g_ref, b_ref, w_ref, bs_ref, o_ref, vn_ref, *, chunk):
    gw = GM_WIDTH // GM_GROUPS
    for n in range(gu_ref.shape[0] // chunk):
        rows = slice(n * chunk, (n + 1) * chunk)
        for h in range(GM_GROUPS):
            sl = slice(h * gw, (h + 1) * gw)
            v = jax.nn.gelu(gv_ref[rows, sl])
            vn = _ln_lanes(v, g_ref[:, sl], b_ref[:, sl])
            vn_ref[rows, sl] = vn
            s = jnp.dot(w_ref[h], vn.astype(BF16), preferred_element_type=F32) + bs_ref[:, sl]
            o_ref[rows, sl] = jax.nn.gelu(gu_ref[rows, sl]) * s


def _gmlp(h, g, b, w_tril, bs_full, *, chunk):
    m = h.shape[0]
    tile = chunk * (GM_CHUNKS_PER_STEP if (m // chunk) % GM_CHUNKS_PER_STEP == 0 else 1)
    blk = lambda c: pl.BlockSpec((tile, GM_WIDTH), lambda i, c=c: (i, c))
    full = lambda a: pl.BlockSpec(a.shape, lambda i: (0,) * a.ndim)
    return pl.pallas_call(
        functools.partial(_gmlp_kernel, chunk=chunk),
        grid=(m // tile,),
        in_specs=[blk(C_GU // GM_WIDTH), blk(C_GV // GM_WIDTH), full(g), full(b), full(w_tril), full(bs_full)],
        out_specs=[pl.BlockSpec((tile, GM_WIDTH), lambda i: (i, 0))] * 2,
        out_shape=[jax.ShapeDtypeStruct((m, GM_WIDTH), F32)] * 2,
        compiler_params=_cparams(1),
        name="gmlp",
    )(h, h, g, b, w_tril, bs_full)


CONV_HALO = 32


def _conv_kernel(ca_ref, cb_ref, cah_ref, cbh_ref, hist_ref, dw_ref, db_ref, g_ref, b_ref, pw_ref,
                 y_ref, c_ref, buf_ref, *, tq):
    i = pl.program_id(1)
    c = ca_ref[...] * jax.nn.sigmoid(cb_ref[...])
    c_ref[...] = c
    halo = cah_ref[...] * jax.nn.sigmoid(cbh_ref[...])
    halo = jnp.where(i == 0, hist_ref[0], halo)
    buf_ref[0, 0:CONV_HALO, :] = halo
    buf_ref[0, CONV_HALO:CONV_HALO + tq, :] = c
    sub = 8
    span = tq + CONV_HALO - sub
    for r in range(1, sub):
        buf_ref[r, 0:span, :] = buf_ref[0, r:r + span, :]
    lead = CONV_HALO - (CONV_K - 1)
    acc = jnp.zeros((tq, CONV_WIDTH), F32)
    for k in range(CONV_K):
        off = lead + k
        base = off - off % sub
        acc = acc + dw_ref[k:k + 1, :] * buf_ref[off % sub, base:base + tq, :]
    y = acc + db_ref[...]
    cw = CONV_WIDTH // CONV_GROUPS
    parts = []
    for gi in range(CONV_GROUPS):
        sl = slice(gi * cw, (gi + 1) * cw)
        yn = _ln_lanes(y[:, sl], g_ref[:, sl], b_ref[:, sl])
        parts.append((yn * jax.nn.sigmoid(yn)).astype(BF16))
    y_ref[...] = jnp.dot(jnp.concatenate(parts, axis=-1), pw_ref[...], preferred_element_type=F32)


def _conv(h, hist, dw, db, g, b, pw, *, batch, seq, tq):
    m = h.shape[0]
    nt = seq // tq
    row = lambda bi, i: bi * nt + i
    halo_row = lambda bi, i: jnp.maximum((bi * seq + i * tq) // CONV_HALO - 1, 0)
    full = lambda a: pl.BlockSpec(a.shape, lambda bi, i: (0,) * a.ndim)
    kern = functools.partial(_conv_kernel, tq=tq)
    return pl.pallas_call(
        kern,
        grid=(batch, nt),
        in_specs=[
            pl.BlockSpec((tq, CONV_WIDTH), lambda bi, i: (row(bi, i), C_CA // CONV_WIDTH)),
            pl.BlockSpec((tq, CONV_WIDTH), lambda bi, i: (row(bi, i), C_CB // CONV_WIDTH)),
            pl.BlockSpec((CONV_HALO, CONV_WIDTH), lambda bi, i: (halo_row(bi, i), C_CA // CONV_WIDTH)),
            pl.BlockSpec((CONV_HALO, CONV_WIDTH), lambda bi, i: (halo_row(bi, i), C_CB // CONV_WIDTH)),
            pl.BlockSpec((1, CONV_HALO, CONV_WIDTH), lambda bi, i: (bi, 0, 0)),
            full(dw), full(db), full(g), full(b), full(pw),
        ],
        out_specs=[pl.BlockSpec((tq, CONV_WIDTH), lambda bi, i: (row(bi, i), 0))] * 2,
        out_shape=[jax.ShapeDtypeStruct((m, CONV_WIDTH), F32)] * 2,
        scratch_shapes=[pltpu.VMEM((8, CONV_HALO + tq, CONV_WIDTH), F32)],
        compiler_params=_cparams(2),
        name="conv",
    )(h, h, h, h, hist, dw, db, g, b, pw)


def _masked_softmax(s, mask):
    s = jnp.where(mask, s, NEG_INF)
    m = jnp.max(s, axis=-1, keepdims=True)
    m = jnp.where(m > NEG_INF, m, 0.0)
    e = jnp.exp(s - m)
    return e / jnp.maximum(jnp.sum(e, axis=-1, keepdims=True), 1e-30)


def _select_blocks(imp, tpos, n_blocks):
    tq, lanes = imp.shape
    blk = lax.broadcasted_iota(jnp.int32, (tq, lanes), 1)
    cur = tpos // SEL_BLOCK
    valid = blk * SEL_BLOCK <= tpos
    forced = jnp.logical_or(blk == 0, jnp.logical_or(blk == cur, blk == cur - 1))
    score = jnp.where(valid, jnp.where(forced, FORCED_SCORE, imp), NEG_INF)
    beaten = jnp.zeros((tq, lanes), F32)
    for i in range(n_blocks):
        ci = score[:, i:i + 1]
        earlier = (blk > i).astype(F32)
        beaten = beaten + jnp.where(ci > score, 1.0, jnp.where(ci == score, earlier, 0.0))
    picked = jnp.logical_and(beaten < float(min(N_SELECT, n_blocks)), blk < n_blocks)
    return picked.astype(F32)


SLC_CHUNK = 512


def _online_softmax_step(m_ref, l_ref, acc_ref, r, s, mask, v):
    s = jnp.where(mask, s, NEG_INF)
    m_old = m_ref[r]
    m_new = jnp.maximum(m_old, jnp.max(s, axis=-1, keepdims=True))
    m_safe = jnp.where(m_new > NEG_INF, m_new, 0.0)
    alpha = jnp.exp(m_old - m_safe)
    e = jnp.exp(s - m_safe)
    l_ref[r] = alpha * l_ref[r] + jnp.sum(e, axis=-1, keepdims=True)
    acc_ref[r] = alpha * acc_ref[r] + jnp.dot(e.astype(BF16), v, preferred_element_type=F32)
    m_ref[r] = m_new


VT_CHUNK = 256


def _select_blocks_t(imp_t, tpos_row, n_blocks):
    nb, tq = imp_t.shape
    blk = lax.broadcasted_iota(jnp.int32, (nb, tq), 0)
    cur = tpos_row // SEL_BLOCK
    valid = blk * SEL_BLOCK <= tpos_row
    forced = jnp.logical_or(blk == 0, jnp.logical_or(blk == cur, blk == cur - 1))
    score = jnp.where(valid, jnp.where(forced, FORCED_SCORE, imp_t), NEG_INF)
    beaten = jnp.zeros((nb, tq), F32)
    for i in range(n_blocks):
        ci = score[i:i + 1, :]
        earlier = (blk > i).astype(F32)
        beaten = beaten + jnp.where(ci > score, 1.0, jnp.where(ci == score, earlier, 0.0))
    return (beaten < float(min(N_SELECT, n_blocks))).astype(F32)


def _col_softmax(s_t, mask_t):
    s = jnp.where(mask_t, s_t, NEG_INF)
    m = jnp.max(s, axis=0, keepdims=True)
    m = jnp.where(m > NEG_INF, m, 0.0)
    e = jnp.exp(s - m)
    return e, jnp.sum(e, axis=0, keepdims=True)


def _col_softmax_step(m_ref, l_ref, acc_ref, r, s_t, mask_t, v_t):
    s = jnp.where(mask_t, s_t, NEG_INF)
    m_old = m_ref[r]
    m_new = jnp.maximum(m_old, jnp.max(s, axis=0, keepdims=True))
    m_safe = jnp.where(m_new > NEG_INF, m_new, 0.0)
    alpha = jnp.exp(m_old - m_safe)
    e = jnp.exp(s - m_safe)
    l_ref[r] = alpha * l_ref[r] + jnp.sum(e, axis=0, keepdims=True)
    acc_ref[r] = alpha * acc_ref[r] + jnp.dot(v_t, e.astype(BF16), preferred_element_type=F32)
    m_ref[r] = m_new


def _nsa_prompt_kernel(q_ref, kc_ref, ks_ref, kw_ref, vc_ref, vs_ref, vw_ref, gate_ref, cw_ref, o_ref,
                       kb_ref, vt_ref, kcmp_ref, vcmpt_ref, m_ref, l_ref, acc_ref, *, tq, seq):
    qi = pl.program_id(1)
    t0 = qi * tq
    scale = HEAD_DIM ** -0.5
    n_slc = seq // SEL_BLOCK
    tpos = t0 + lax.broadcasted_iota(jnp.int32, (1, tq), 1)
    row_c = lax.broadcasted_iota(jnp.int32, (2 * n_slc, tq), 0)
    cmp_last = jnp.where(row_c < n_slc, row_c * SEL_BLOCK + CMP_BLOCK - 1, (row_c - n_slc) * SEL_BLOCK + SEL_BLOCK - 1)
    cmp_mask = cmp_last <= tpos
    win_len = tq + WINDOW
    w0 = pl.multiple_of(jnp.maximum(t0 - WINDOW, 0), tq)
    wpos = w0 + lax.broadcasted_iota(jnp.int32, (win_len, tq), 0)
    win_mask = jnp.logical_and(wpos <= tpos, wpos >= tpos - WINDOW)
    gates_t = jax.nn.sigmoid(gate_ref[:, 0:HEAD_DIM]).T

    @pl.when(qi == 0)
    def _():
        kb_ref[0] = ks_ref[0].astype(BF16)
        kb_ref[1] = kw_ref[0].astype(BF16)
        for g in range(NSA_KV_HEADS):
            gs = slice(g * HEAD_DIM, (g + 1) * HEAD_DIM)
            for n, ref in enumerate((vs_ref, vw_ref)):
                for c in range(seq // VT_CHUNK):
                    vt_ref[n, g, c] = ref[0, c * VT_CHUNK:(c + 1) * VT_CHUNK, gs].T.astype(BF16)
            halves = []
            for which, ref in enumerate((kc_ref, vc_ref)):
                x3 = ref[0, :, gs].reshape(n_slc, SEL_BLOCK, HEAD_DIM)
                ev = jnp.sum(x3 * cw_ref[g, which, 0][None], axis=1)
                od = jnp.sum(x3 * cw_ref[g, which, 1][None], axis=1)
                halves.append(jnp.concatenate([ev, od], axis=0))
            kcmp_ref[g] = halves[0].astype(BF16)
            vcmpt_ref[g] = jnp.concatenate([halves[1], jnp.zeros_like(halves[1])], axis=0).T.astype(BF16)

    for g in range(NSA_KV_HEADS):
        gs = slice(g * HEAD_DIM, (g + 1) * HEAD_DIM)
        kcmp, vcmp_t = kcmp_ref[g], vcmpt_ref[g][:, :2 * n_slc]
        qg = jnp.concatenate([(q_ref[:, (g * GQA_R + r) * HEAD_DIM:(g * GQA_R + r + 1) * HEAD_DIM] * scale).astype(BF16)
                              for r in range(GQA_R)], axis=0)
        heads = lambda a: jnp.concatenate([a] * GQA_R, axis=1)
        e, l = _col_softmax(_dot_nt(kcmp, qg), heads(cmp_mask))
        p = e / jnp.maximum(l, 1e-30)
        o_cmp = jnp.dot(vcmp_t, p.astype(BF16), preferred_element_type=F32)
        imp = p[:, 0:tq]
        for r in range(1, GQA_R):
            imp = imp + p[:, r * tq:(r + 1) * tq]
        sel = _select_blocks_t(imp[:n_slc] + imp[n_slc:], tpos, n_slc).astype(BF16)

        m_ref[...] = jnp.full(m_ref.shape, NEG_INF, F32)
        l_ref[...] = jnp.zeros_like(l_ref)
        acc_ref[...] = jnp.zeros_like(acc_ref)
        for j in range(seq // SLC_CHUNK):
            @pl.when(j * SLC_CHUNK < t0 + tq)
            def _():
                kpos = j * SLC_CHUNK + lax.broadcasted_iota(jnp.int32, (SLC_CHUNK, tq), 0)
                key_blk = (j * SLC_CHUNK + lax.broadcasted_iota(jnp.int32, (SLC_CHUNK, n_slc), 0)) // SEL_BLOCK
                expand = (lax.broadcasted_iota(jnp.int32, (SLC_CHUNK, n_slc), 1) == key_blk).astype(BF16)
                sel_keys = jnp.dot(expand, sel, preferred_element_type=F32)
                mask = jnp.logical_and(sel_keys > 0.5, kpos <= tpos)
                k = kb_ref[0, j * SLC_CHUNK:(j + 1) * SLC_CHUNK, gs]
                per = SLC_CHUNK // VT_CHUNK
                v_t = jnp.concatenate([vt_ref[0, g, j * per + c] for c in range(per)], axis=1)
                _col_softmax_step(m_ref, l_ref, acc_ref, 0, _dot_nt(k, qg), heads(mask), v_t)

        kwin = kb_ref[1, pl.ds(w0, win_len), gs]
        wc0 = w0 // VT_CHUNK
        vwin_t = jnp.concatenate([vt_ref[1, g, wc0 + c] for c in range(win_len // VT_CHUNK)], axis=1)
        o_slc = acc_ref[0] / jnp.maximum(l_ref[0], 1e-30)
        e, l = _col_softmax(_dot_nt(kwin, qg), heads(win_mask))
        o_win = jnp.dot(vwin_t, e.astype(BF16), preferred_element_type=F32) / jnp.maximum(l, 1e-30)
        for r in range(GQA_R):
            hh = g * GQA_R + r
            cs = slice(r * tq, (r + 1) * tq)
            o_t = (gates_t[3 * hh:3 * hh + 1] * o_cmp[:, cs] + gates_t[3 * hh + 1:3 * hh + 2] * o_slc[:, cs]
                   + gates_t[3 * hh + 2:3 * hh + 3] * o_win[:, cs])
            o_ref[:, hh * HEAD_DIM:(hh + 1) * HEAD_DIM] = o_t.T


def _nsa_prompt(h, cmp_w, *, batch, seq, tq):
    m = h.shape[0]
    nt = seq // tq
    kv = lambda col: pl.BlockSpec((1, seq, KV_WIDTH), lambda bi, i, col=col: (bi, 0, col // KV_WIDTH))
    h3 = h.reshape(batch, seq, PROJ_WIDTH)
    kern = functools.partial(_nsa_prompt_kernel, tq=tq, seq=seq)
    return pl.pallas_call(
        kern,
        grid=(batch, nt),
        in_specs=[
            pl.BlockSpec((tq, NSA_WIDTH), lambda bi, i: (bi * nt + i, C_Q // NSA_WIDTH)),
            kv(C_KC), kv(C_KS), kv(C_KW), kv(C_VC), kv(C_VS), kv(C_VW),
            pl.BlockSpec((tq, KV_WIDTH), lambda bi, i: (bi * nt + i, C_GATE // KV_WIDTH)),
            pl.BlockSpec(cmp_w.shape, lambda bi, i: (0,) * cmp_w.ndim),
        ],
        out_specs=pl.BlockSpec((tq, NSA_WIDTH), lambda bi, i: (bi * nt + i, 0)),
        out_shape=jax.ShapeDtypeStruct((m, NSA_WIDTH), F32),
        scratch_shapes=[
            pltpu.VMEM((2, seq, KV_WIDTH), BF16),
            pltpu.VMEM((2, NSA_KV_HEADS, seq // VT_CHUNK, HEAD_DIM, VT_CHUNK), BF16),
            pltpu.VMEM((NSA_KV_HEADS, 2 * (seq // SEL_BLOCK), HEAD_DIM), BF16),
            pltpu.VMEM((NSA_KV_HEADS, HEAD_DIM, 4 * (seq // SEL_BLOCK)), BF16),
            pltpu.VMEM((1, 1, GQA_R * tq), F32), pltpu.VMEM((1, 1, GQA_R * tq), F32),
            pltpu.VMEM((1, HEAD_DIM, GQA_R * tq), F32),
        ],
        compiler_params=_cparams(2),
        name="nsa_prompt",
    )(h, h3, h3, h3, h3, h3, h3, h, cmp_w)


def _out_ln_kernel(gm_ref, nsa_ref, y_ref, x_ref, w_ref, g_ref, b_ref, o_ref):
    cat = jnp.concatenate([gm_ref[...].astype(BF16), nsa_ref[...].astype(BF16), y_ref[...].astype(BF16)], axis=-1)
    mix = jnp.dot(cat, w_ref[...], preferred_element_type=F32)
    o_ref[...] = _ln_lanes(ALPHA * x_ref[...] + mix, g_ref[...], b_ref[...])


def _out_ln(gm, nsa, y, x, w, g, b, *, tm):
    m = x.shape[0]
    row = lambda width: pl.BlockSpec((tm, width), lambda i: (i, 0))
    full = lambda a: pl.BlockSpec(a.shape, lambda i: (0,) * a.ndim)
    return pl.pallas_call(
        _out_ln_kernel,
        grid=(m // tm,),
        in_specs=[row(GM_WIDTH), row(NSA_WIDTH), row(CONV_WIDTH), row(D_MODEL), full(w), full(g), full(b)],
        out_specs=row(D_MODEL),
        out_shape=jax.ShapeDtypeStruct((m, D_MODEL), F32),
        compiler_params=_cparams(1),
        name="out_ln",
    )(gm, nsa, y, x, w, g, b)


def _extract_topk(s, exact_ties, want_rank):
    n_rows, tm = s.shape
    rows = lax.broadcasted_iota(jnp.int32, (n_rows, tm), 0)
    krow = lax.broadcasted_iota(jnp.int32, (PEER_TOPK, tm), 0)
    work = s
    rank = jnp.full((n_rows, tm), float(PEER_TOPK), F32) if want_rank else None
    vals = jnp.zeros((PEER_TOPK, tm), F32)
    for r in range(PEER_TOPK):
        mx = jnp.max(work, axis=0, keepdims=True)
        hit = work == mx
        if exact_ties:
            hit = rows == jnp.min(jnp.where(hit, rows, n_rows), axis=0, keepdims=True)
        if want_rank:
            rank = jnp.where(hit, float(r), rank)
        vals = jnp.where(krow == r, mx, vals)
        work = jnp.where(hit, NEG_INF, work)
    taken = jnp.sum((work == NEG_INF).astype(F32), axis=0, keepdims=True)
    return vals, rank, taken


def _pair_topk(v1, v2):
    tm = v1.shape[1]
    krow = lax.broadcasted_iota(jnp.int32, (PEER_TOPK, tm), 0)
    cnt = jnp.zeros((PEER_TOPK, tm), F32)
    front = v1 + v2[0:1]
    m0 = v1[0:1] + v2[0:1]
    z = jnp.zeros((1, tm), F32)
    for _ in range(PEER_TOPK):
        mx = jnp.max(front, axis=0, keepdims=True)
        idx = jnp.min(jnp.where(front == mx, krow, PEER_TOPK), axis=0, keepdims=True)
        hit = krow == idx
        cnt = cnt + hit.astype(F32)
        z = z + jnp.exp(mx - m0)
        chosen = jnp.sum(jnp.where(hit, cnt, 0.0), axis=0, keepdims=True)
        nxt = jnp.full((1, tm), NEG_INF, F32)
        for c in range(1, PEER_TOPK):
            nxt = jnp.where(chosen == float(c), v2[c:c + 1], nxt)
        front = jnp.where(hit, v1 + nxt, front)
    return cnt, z


def _route_kernel(x_ref, wq_ref, sk_ref, r2_ref, cnt_ref, f_ref, e2_ref, xb_ref):
    @pl.when(pl.program_id(1) == 0)
    def _():
        xb_ref[...] = x_ref[...].astype(BF16)

    qp = jnp.dot(xb_ref[...], wq_ref[...], preferred_element_type=F32)
    q1 = qp[:, 0:SUBKEY_DIM].astype(BF16)
    q2 = qp[:, SUBKEY_DIM:2 * SUBKEY_DIM].astype(BF16)
    s1 = _dot_nt(sk_ref[0, 0], q1)
    s2 = _dot_nt(sk_ref[0, 1], q2)

    def emit(v1, v2, cnt, z, cnt_a, rank2):
        r2_ref[0] = rank2.astype(BF16)
        cnt_ref[0] = cnt_a
        f_ref[0] = jnp.exp(s1 - v1[0:1]) / z
        e2_ref[0] = jnp.exp(s2 - v2[0:1]).astype(BF16)

    v1, _, taken1 = _extract_topk(s1, exact_ties=False, want_rank=False)
    v2, rank2, taken2 = _extract_topk(s2, exact_ties=False, want_rank=True)
    cnt, z = _pair_topk(v1, v2)
    cnt_a = jnp.zeros(s1.shape, F32)
    for r in range(PEER_TOPK):
        cnt_a = jnp.where(s1 == v1[r:r + 1], cnt[r:r + 1], cnt_a)
    emit(v1, v2, cnt, z, cnt_a, rank2)

    @pl.when(jnp.max(jnp.maximum(taken1, taken2)) > float(PEER_TOPK))
    def _():
        v1, rank1, _ = _extract_topk(s1, exact_ties=True, want_rank=True)
        v2, rank2, _ = _extract_topk(s2, exact_ties=True, want_rank=True)
        cnt, z = _pair_topk(v1, v2)
        cnt_a = jnp.zeros(s1.shape, F32)
        for r in range(PEER_TOPK):
            cnt_a = jnp.where(rank1 == float(r), cnt[r:r + 1], cnt_a)
        emit(v1, v2, cnt, z, cnt_a, rank2)


def _route(x, wq, subkeys, *, tm):
    m, k = x.shape
    out = lambda dt: jax.ShapeDtypeStruct((PEER_HEADS, N_KEYS, m), dt)
    ospec = pl.BlockSpec((1, N_KEYS, tm), lambda i, hd: (hd, 0, i))
    return pl.pallas_call(
        _route_kernel,
        grid=(m // tm, PEER_HEADS),
        in_specs=[
            pl.BlockSpec((tm, k), lambda i, hd: (i, 0)),
            pl.BlockSpec((k, 2 * SUBKEY_DIM), lambda i, hd: (0, hd)),
            pl.BlockSpec((1, 2, N_KEYS, SUBKEY_DIM), lambda i, hd: (hd, 0, 0, 0)),
        ],
        out_specs=[ospec] * 4,
        out_shape=[out(BF16), out(F32), out(F32), out(BF16)],
        scratch_shapes=[pltpu.VMEM((tm, k), BF16)],
        compiler_params=_cparams(2),
        name="peer_route",
    )(x, wq, subkeys)


BF16_SUBLANES = 16


def _row_as_bf16_block(row):
    tile = jnp.broadcast_to(row, (BF16_SUBLANES, row.shape[1])).astype(BF16)
    return pltpu.repeat(tile, N_KEYS // BF16_SUBLANES, axis=0)


PEER_SUB = 256
PEER_TE = 1024


def _peer_dense_kernel(x_ref, u_ref, v_ref, r2_ref, cnt_ref, f_ref, e2_ref, g_ref, b_ref, o_ref, xb_ref, *, te):
    j = pl.program_id(1)

    @pl.when(j == 0)
    def _():
        xb_ref[...] = x_ref[...].astype(BF16)
        o_ref[...] = jnp.zeros_like(o_ref)

    cols = []
    for c in range(te // PEER_SUB):
        act = jax.nn.gelu(_dot_nt(u_ref[c * PEER_SUB:(c + 1) * PEER_SUB, :], xb_ref[...]).astype(BF16))
        rows = []
        for al in range(PEER_SUB // N_KEYS):
            a = c * (PEER_SUB // N_KEYS) + al
            gate = None
            for hd in range(PEER_HEADS):
                cnt = _row_as_bf16_block(cnt_ref[hd, a:a + 1, :])
                f = _row_as_bf16_block(f_ref[hd, a:a + 1, :])
                term = jnp.where(r2_ref[hd] < cnt, e2_ref[hd], jnp.zeros((), BF16)) * f
                gate = term if gate is None else gate + term
            rows.append(act[al * N_KEYS:(al + 1) * N_KEYS] * gate)
        cols.append(jnp.concatenate(rows, axis=0).T)
    o_ref[...] += jnp.dot(jnp.concatenate(cols, axis=1), v_ref[...], preferred_element_type=F32)

    @pl.when(j == pl.num_programs(1) - 1)
    def _():
        o_ref[...] = _ln_lanes(ALPHA * x_ref[...] + o_ref[...], g_ref[...], b_ref[...])


def _peer_dense(x, u, v, layer, r2, cnt, f, e2, g, b, *, tm, te):
    m = x.shape[0]
    n_exp = u.shape[0] // DEPTH
    first = layer * (n_exp // te)
    by_b = pl.BlockSpec((PEER_HEADS, N_KEYS, tm), lambda i, j: (0, 0, i))
    by_a = pl.BlockSpec((PEER_HEADS, te // N_KEYS, tm), lambda i, j: (0, j, i))
    full = lambda a: pl.BlockSpec(a.shape, lambda i, j: (0,) * a.ndim)
    kern = functools.partial(_peer_dense_kernel, te=te)
    return pl.pallas_call(
        kern,
        grid=(m // tm, n_exp // te),
        in_specs=[
            pl.BlockSpec((tm, D_MODEL), lambda i, j: (i, 0), pipeline_mode=pl.Buffered(1)),
            pl.BlockSpec((te, D_MODEL), lambda i, j: (first + j, 0)),
            pl.BlockSpec((te, D_MODEL), lambda i, j: (first + j, 0)),
            by_b, by_a, by_a, by_b, full(g), full(b),
        ],
        out_specs=pl.BlockSpec((tm, D_MODEL), lambda i, j: (i, 0), pipeline_mode=pl.Buffered(1)),
        out_shape=jax.ShapeDtypeStruct((m, D_MODEL), F32),
        scratch_shapes=[pltpu.VMEM((tm, D_MODEL), BF16)],
        compiler_params=_cparams(2),
        name="peer_dense",
    )(x, u, v, r2, cnt, f, e2, g, b)


PAGES_PER_STEP = 32


def _group_queries(q_ref, g):
    heads = [q_ref[:, (g * GQA_R + r) * HEAD_DIM:(g * GQA_R + r + 1) * HEAD_DIM] for r in range(GQA_R)]
    return jnp.concatenate(heads, axis=0).astype(BF16)


def _page_rows(pages, g):
    n = pages[0].shape[0] // NSA_KV_HEADS
    return jnp.concatenate([r[pl.ds(g, n, stride=NSA_KV_HEADS), :] for r in pages], axis=0)


def _nsa_s_cmp_kernel(pt_ref, q_ref, cw_ref, *refs, seq, past_len, n_real, sel_lanes):
    nps = PAGES_PER_STEP
    k_pages, v_pages = refs[:nps], refs[nps:2 * nps]
    ocmp_ref, sel_ref, ke_ref, ko_ref, ve_ref, vo_ref = refs[2 * nps:]
    c = pl.program_id(1)
    n_pb = ke_ref.shape[0]
    rows = nps * (k_pages[0].shape[0] // NSA_KV_HEADS) // SEL_BLOCK
    off = pl.multiple_of(c * rows, rows)

    def compress(pages, which, even_ref, odd_ref):
        for g in range(NSA_KV_HEADS):
            gs = slice(g * HEAD_DIM, (g + 1) * HEAD_DIM)
            x3 = _page_rows(pages, g).reshape(rows, SEL_BLOCK, HEAD_DIM)
            even_ref[pl.ds(off, rows), gs] = jnp.sum(x3 * cw_ref[g, which, 0][None], axis=1)
            odd_ref[pl.ds(off, rows), gs] = jnp.sum(x3 * cw_ref[g, which, 1][None], axis=1)

    compress(k_pages, 0, ke_ref, ko_ref)
    compress(v_pages, 1, ve_ref, vo_ref)

    @pl.when(c == pl.num_programs(1) - 1)
    def _():
        scale = HEAD_DIM ** -0.5
        nq = GQA_R * seq
        qrow = lax.broadcasted_iota(jnp.int32, (nq, 1), 0)
        qpos = past_len + qrow % seq
        lane = lax.broadcasted_iota(jnp.int32, (nq, 2 * n_pb), 1)
        cmp_last = jnp.where(lane < n_pb, lane * SEL_BLOCK + CMP_BLOCK - 1, (lane - n_pb) * SEL_BLOCK + SEL_BLOCK - 1)
        cmp_mask = cmp_last <= qpos
        tpos = past_len + lax.broadcasted_iota(jnp.int32, (seq, 1), 0)
        for g in range(NSA_KV_HEADS):
            gs = slice(g * HEAD_DIM, (g + 1) * HEAD_DIM)
            kcat = jnp.concatenate([ke_ref[:, gs], ko_ref[:, gs]], axis=0).astype(BF16)
            vcat = jnp.concatenate([ve_ref[:, gs], vo_ref[:, gs]], axis=0).astype(BF16)
            p = _masked_softmax(_dot_nt(_group_queries(q_ref, g), kcat) * scale, cmp_mask)
            ocmp_ref[0, g] = jnp.dot(p.astype(BF16), vcat, preferred_element_type=F32)
            imp = p[0:seq]
            for r in range(1, GQA_R):
                imp = imp + p[r * seq:(r + 1) * seq]
            imp = imp[:, :n_pb] + imp[:, n_pb:]
            imp = jnp.concatenate([imp, jnp.zeros((seq, sel_lanes - n_pb), F32)], axis=1)
            sel = _select_blocks(imp, tpos, n_real)
            bpc = sel_ref.shape[-1]
            for ch in range(sel_ref.shape[2]):
                sel_ref[0, g, ch] = sel[:, ch * bpc:(ch + 1) * bpc]


def _nsa_s_slc_kernel(pt_ref, q_ref, sel_ref, sel_new_ref, ocmp_ref, ksn_ref, vsn_ref, wk_ref, wv_ref, gate_ref, *refs,
                      seq, past_len, win_buf):
    nps = PAGES_PER_STEP
    k_pages, v_pages = refs[:nps], refs[nps:2 * nps]
    o_ref, m_ref, l_ref, acc_ref = refs[2 * nps:]
    c = pl.program_id(1)
    scale = HEAD_DIM ** -0.5
    nq = GQA_R * seq
    chunk = nps * (k_pages[0].shape[0] // NSA_KV_HEADS)
    qrow = lax.broadcasted_iota(jnp.int32, (nq, 1), 0)
    qpos = past_len + qrow % seq

    @pl.when(c == 0)
    def _():
        m_ref[...] = jnp.full(m_ref.shape, NEG_INF, F32)
        l_ref[...] = jnp.zeros_like(l_ref)
        acc_ref[...] = jnp.zeros_like(acc_ref)

    def online_update(g, s, mask, v):
        _online_softmax_step(m_ref, l_ref, acc_ref, g, s, mask, v)

    def key_mask(sel_blocks, first_pos, n_keys):
        bpc = sel_blocks.shape[1]
        key_blk = lax.broadcasted_iota(jnp.int32, (bpc, n_keys), 1) // SEL_BLOCK
        expand = (lax.broadcasted_iota(jnp.int32, (bpc, n_keys), 0) == key_blk).astype(BF16)
        sel_keys = jnp.dot(sel_blocks.astype(BF16), expand, preferred_element_type=F32)
        sel_keys = jnp.concatenate([sel_keys] * GQA_R, axis=0)
        kpos = first_pos + lax.broadcasted_iota(jnp.int32, (nq, n_keys), 1)
        return jnp.logical_and(sel_keys > 0.5, kpos <= qpos)

    for g in range(NSA_KV_HEADS):
        mask = key_mask(sel_ref[0, g, 0], c * chunk, chunk)
        scores = _dot_nt(_group_queries(q_ref, g), _page_rows(k_pages, g).astype(BF16)) * scale
        online_update(g, scores, mask, _page_rows(v_pages, g).astype(BF16))

    @pl.when(c == pl.num_programs(1) - 1)
    def _():
        gates = jax.nn.sigmoid(gate_ref[:, 0:HEAD_DIM])
        n_new = ksn_ref.shape[1]
        n_win = wk_ref.shape[1]
        wpos = past_len - win_buf + lax.broadcasted_iota(jnp.int32, (nq, n_win), 1)
        win_mask = jnp.logical_and(wpos <= qpos, wpos >= qpos - WINDOW)
        for g in range(NSA_KV_HEADS):
            gs = slice(g * HEAD_DIM, (g + 1) * HEAD_DIM)
            qg = _group_queries(q_ref, g)
            online_update(g, _dot_nt(qg, ksn_ref[0, :, gs].astype(BF16)) * scale,
                          key_mask(sel_new_ref[0, g, 0], past_len, n_new), vsn_ref[0, :, gs].astype(BF16))
            o_slc = acc_ref[g] / jnp.maximum(l_ref[g], 1e-30)
            p = _masked_softmax(_dot_nt(qg, wk_ref[0, :, gs].astype(BF16)) * scale, win_mask)
            o_win = jnp.dot(p.astype(BF16), wv_ref[0, :, gs].astype(BF16), preferred_element_type=F32)
            o_cmp = ocmp_ref[0, g]
            for r in range(GQA_R):
                hh = g * GQA_R + r
                rs = slice(r * seq, (r + 1) * seq)
                o_ref[:, hh * HEAD_DIM:(hh + 1) * HEAD_DIM] = (
                    gates[:, 3 * hh:3 * hh + 1] * o_cmp[rs] + gates[:, 3 * hh + 1:3 * hh + 2] * o_slc[rs]
                    + gates[:, 3 * hh + 2:3 * hh + 3] * o_win[rs])


def _nsa_sample(h, caches, layer, page_table, ks_new, vs_new, wk_all, wv_all, cw, *, batch, seq, past_len, win_buf):
    cmp_k, cmp_v, slc_k, slc_v = caches
    nps = PAGES_PER_STEP
    n_pages, page = page_table.shape[1], ks_new.shape[1]
    page_rows = page * NSA_KV_HEADS
    n_pool = cmp_k.shape[0] // (DEPTH * page_rows)
    n_chunks = n_pages // nps
    n_pb = past_len // SEL_BLOCK
    n_real = n_pb + 1
    bpc = nps * page // SEL_BLOCK
    sel_lanes = -(-max(n_real, (n_chunks + 1) * bpc) // 128) * 128
    sel_shape = (NSA_KV_HEADS, n_chunks + 1, seq, bpc)
    sel_chunk = lambda which: pl.BlockSpec((1, NSA_KV_HEADS, 1, seq, bpc), lambda b, c, pt: (b, 0, which(c), 0, 0))
    nq = GQA_R * seq
    page_spec = lambda k: pl.BlockSpec((page_rows, HEAD_DIM),
                                       lambda b, c, pt, k=k: (pt[b, c * nps + k] + layer * n_pool, 0))
    pages = [page_spec(k) for k in range(nps)]
    q_spec = pl.BlockSpec((seq, NSA_WIDTH), lambda b, c, pt: (b, C_Q // NSA_WIDTH))
    per_b = lambda shape: pl.BlockSpec((1,) + shape, lambda b, c, pt: (b,) + (0,) * len(shape))
    ocmp, sel = pl.pallas_call(
        functools.partial(_nsa_s_cmp_kernel, seq=seq, past_len=past_len, n_real=n_real, sel_lanes=sel_lanes),
        grid_spec=pltpu.PrefetchScalarGridSpec(
            num_scalar_prefetch=1, grid=(batch, n_chunks),
            in_specs=[q_spec, pl.BlockSpec(cw.shape, lambda b, c, pt: (0,) * cw.ndim)] + pages + pages,
            out_specs=[per_b((NSA_KV_HEADS, nq, HEAD_DIM)), per_b(sel_shape)],
            scratch_shapes=[pltpu.VMEM((n_pb, KV_WIDTH), F32)] * 4),
        out_shape=[jax.ShapeDtypeStruct((batch, NSA_KV_HEADS, nq, HEAD_DIM), F32),
                   jax.ShapeDtypeStruct((batch,) + sel_shape, F32)],
        compiler_params=_cparams(2),
        name="nsa_sample_cmp",
    )(page_table, h, cw, *([cmp_k] * nps), *([cmp_v] * nps))
    return pl.pallas_call(
        functools.partial(_nsa_s_slc_kernel, seq=seq, past_len=past_len, win_buf=win_buf),
        grid_spec=pltpu.PrefetchScalarGridSpec(
            num_scalar_prefetch=1, grid=(batch, n_chunks),
            in_specs=[q_spec, sel_chunk(lambda c: c), sel_chunk(lambda c: n_chunks),
                      per_b((NSA_KV_HEADS, nq, HEAD_DIM)),
                      per_b(ks_new.shape[1:]), per_b(vs_new.shape[1:]), per_b(wk_all.shape[1:]), per_b(wv_all.shape[1:]),
                      pl.BlockSpec((seq, KV_WIDTH), lambda b, c, pt: (b, C_GATE // KV_WIDTH))] + pages + pages,
            out_specs=pl.BlockSpec((seq, NSA_WIDTH), lambda b, c, pt: (b, 0)),
            scratch_shapes=[pltpu.VMEM((NSA_KV_HEADS, nq, 1), F32), pltpu.VMEM((NSA_KV_HEADS, nq, 1), F32),
                            pltpu.VMEM((NSA_KV_HEADS, nq, HEAD_DIM), F32)]),
        out_shape=jax.ShapeDtypeStruct((batch * seq, NSA_WIDTH), F32),
        compiler_params=_cparams(2),
        name="nsa_sample_slc",
    )(page_table, h, sel, sel, ocmp, ks_new, vs_new, wk_all, wv_all, h, *([slc_k] * nps), *([slc_v] * nps))


def _rope_tables(pos):
    half = HEAD_DIM // 2
    inv = ROPE_THETA ** (-jnp.arange(half, dtype=F32) / half)
    ang = pos.astype(F32)[:, None] * inv[None, :]
    cos, sin = jnp.cos(ang), jnp.sin(ang)
    return jnp.concatenate([cos, cos], -1), jnp.concatenate([-sin, sin], -1)


def _prep_weights(l, w_in, gm_ln_g, gm_ln_b, gm_ws, gm_bs, cmp_wk, cmp_wv, conv_dw, conv_db, conv_ln_g,
                  conv_ln_b, conv_pw, w_out, ln1_g, ln1_b, peer_wq, peer_subkeys, peer_u, peer_v, ln2_g, ln2_b):
    w = w_in[l]
    o = np.cumsum([0, GM_WIDTH, GM_WIDTH, NSA_WIDTH] + [KV_WIDTH] * 6 + [3 * NSA_HEADS, CONV_WIDTH, CONV_WIDTH])
    gu, gv, q, kc, vc, ks, vs, kw, vw, gate, ca, cb = [w[:, o[i]:o[i + 1]] for i in range(12)]
    zeros = lambda n: jnp.zeros((w.shape[0], n), w.dtype)
    w_proj = jnp.concatenate([gu, gv, ca, cb, q, kc, ks, kw, zeros(C_ROPE_END - C_KW - KV_WIDTH), vc, vs, vw, gate,
                              zeros(PROJ_WIDTH - C_GATE - 3 * NSA_HEADS)], axis=1).astype(BF16)
    row = lambda a: a.reshape(1, -1)
    cw = jnp.stack([cmp_wk[l], cmp_wv[l]], axis=1)
    zero = jnp.zeros_like(cw)
    cw = jnp.stack([jnp.concatenate([cw, zero], -1), jnp.concatenate([zero, cw], -1)], axis=2)
    cw = jnp.broadcast_to(cw[..., None], cw.shape + (HEAD_DIM,))
    dw = jnp.concatenate([conv_dw[l], jnp.zeros((CONV_HALO - CONV_K, CONV_WIDTH), F32)], axis=0)
    return dict(
        w_proj=w_proj, gm_g=row(gm_ln_g[l]), gm_b=row(gm_ln_b[l]), gm_ws=gm_ws[l], gm_bs=gm_bs[l], cmp_w=cw,
        conv_dw=dw, conv_db=row(conv_db[l]), conv_g=row(conv_ln_g[l]), conv_b=row(conv_ln_b[l]),
        conv_pw=conv_pw[l].astype(BF16), w_out=w_out[l].astype(BF16), ln1_g=row(ln1_g[l]), ln1_b=row(ln1_b[l]),
        wq=peer_wq[l].astype(BF16), subkeys=peer_subkeys[l].astype(BF16), layer=l,
        u=peer_u.astype(BF16).reshape(-1, D_MODEL), v=peer_v.astype(BF16).reshape(-1, D_MODEL),
        ln2_g=row(ln2_g[l]), ln2_b=row(ln2_b[l]))


def _gm_weights(p, chunk):
    w = jnp.tril(p["gm_ws"][:, :chunk, :chunk])
    bs = jnp.repeat(p["gm_bs"][:, :chunk].T, GM_WIDTH // GM_GROUPS, axis=1)
    return w, bs


def _peer(x, p, *, tm_route, tm_dense, te):
    r2, cnt, f, e2 = _route(x, p["wq"], p["subkeys"], tm=tm_route)
    return _peer_dense(x, p["u"], p["v"], p["layer"], r2, cnt, f, e2, p["ln2_g"], p["ln2_b"], tm=tm_dense, te=te)


def _kv_rows(h, col, batch, seq):
    return h[:, col:col + KV_WIDTH].reshape(batch, seq, NSA_KV_HEADS, HEAD_DIM)


def _layer_prompt(x, p, cos, sin, *, batch, seq):
    m = batch * seq
    h, *kv_out = _proj(x, p["w_proj"], cos, sin, tm=1024, rope_lo=ROPE_LO, rope_hi=ROPE_HI, name="proj_in",
                       kv_cols=(C_KC, C_VC, C_KS, C_VS, C_KW, C_VW))
    kc, vc, ks, vs, kw, vw = [a.reshape(batch, seq, NSA_KV_HEADS, HEAD_DIM) for a in kv_out]
    gw, gbs = _gm_weights(p, GM_CHUNK)
    gm, vn = _gmlp(h, p["gm_g"], p["gm_b"], gw.astype(BF16), gbs, chunk=GM_CHUNK)
    nsa = _nsa_prompt(h, p["cmp_w"], batch=batch, seq=seq, tq=256)
    hist = jnp.zeros((batch, CONV_HALO, CONV_WIDTH), F32)
    y, c = _conv(h, hist, p["conv_dw"], p["conv_db"], p["conv_g"], p["conv_b"], p["conv_pw"],
                 batch=batch, seq=seq, tq=256)
    x1 = _out_ln(gm, nsa, y, x, p["w_out"], p["ln1_g"], p["ln1_b"], tm=512)
    x2 = _peer(x1, p, tm_route=1024, tm_dense=512, te=PEER_TE)
    keep = min(WINDOW, seq)
    start = ((seq - 1) // GM_CHUNK) * GM_CHUNK
    states = (kc, vc, ks, vs, kw[:, -keep:], vw[:, -keep:],
              c.reshape(batch, seq, CONV_WIDTH)[:, -(CONV_K - 1):], vn.reshape(batch, seq, GM_WIDTH)[:, start:])
    return x2, states


def _layer_sample(x, p, cos, sin, caches, layer, page_table, win_k, win_v, hist30, *, batch, seq, past_len):
    m = batch * seq
    h = _proj(x, p["w_proj"], cos, sin, tm=m, rope_lo=ROPE_LO, rope_hi=ROPE_HI, name="proj_in_s")
    pad_rows = lambda a: jnp.pad(a.reshape(batch, seq, -1), ((0, 0), (0, GM_CHUNK - seq), (0, 0))).reshape(batch * GM_CHUNK, -1)
    gw, gbs = _gm_weights(p, GM_CHUNK)
    hp = pad_rows(h[:, :2 * GM_WIDTH])
    gm, vn = _gmlp(hp, p["gm_g"], p["gm_b"], gw.astype(BF16), gbs, chunk=GM_CHUNK)
    unpad = lambda a: a.reshape(batch, GM_CHUNK, -1)[:, :seq].reshape(m, -1)
    gm, vn = unpad(gm), unpad(vn)
    kvs = [_kv_rows(h, col, batch, seq) for col in (C_KC, C_VC, C_KS, C_VS, C_KW, C_VW)]
    win_buf = win_k.shape[1]
    page = past_len // page_table.shape[1]
    new_rows = lambda col, rows: jnp.pad(h[:, col:col + KV_WIDTH].reshape(batch, seq, KV_WIDTH),
                                         ((0, 0), (0, rows - seq), (0, 0)))
    flat = lambda a: a.reshape(batch, win_buf, KV_WIDTH)
    wk_all = jnp.concatenate([flat(win_k), new_rows(C_KW, HEAD_DIM)], axis=1)
    wv_all = jnp.concatenate([flat(win_v), new_rows(C_VW, HEAD_DIM)], axis=1)
    nsa = _nsa_sample(h, caches, layer, page_table, new_rows(C_KS, page), new_rows(C_VS, page), wk_all, wv_all,
                      p["cmp_w"], batch=batch, seq=seq, past_len=past_len, win_buf=win_buf)
    keep = min(WINDOW, past_len + seq)
    unflat = lambda a: a[:, :win_buf + seq][:, -keep:].reshape(batch, keep, NSA_KV_HEADS, HEAD_DIM)
    win_k, win_v = unflat(wk_all), unflat(wv_all)
    hist = jnp.pad(hist30, ((0, 0), (CONV_HALO - (CONV_K - 1), 0), (0, 0)))
    y, c = _conv(h, hist, p["conv_dw"], p["conv_db"], p["conv_g"], p["conv_b"], p["conv_pw"],
                 batch=batch, seq=seq, tq=seq)
    x1 = _out_ln(gm, nsa, y, x, p["w_out"], p["ln1_g"], p["ln1_b"], tm=m)
    x2 = _peer(x1, p, tm_route=m, tm_dense=m, te=PEER_TE)
    conv_state = jnp.concatenate([hist30, c.reshape(batch, seq, CONV_WIDTH)], axis=1)[:, -(CONV_K - 1):]
    start = ((seq - 1) // GM_CHUNK) * GM_CHUNK
    states = (kvs[0], kvs[1], kvs[2], kvs[3], win_k, win_v, conv_state, vn.reshape(batch, seq, GM_WIDTH)[:, start:])
    return x2, states


def kernel(x_prompt, x_sample, cache_cmp_k, cache_cmp_v, cache_slc_k, cache_slc_v, cache_win_k, cache_win_v,
           state_conv, page_table, w_in, gm_ln_g, gm_ln_b, gm_ws, gm_bs, cmp_wk, cmp_wv, conv_dw, conv_db,
           conv_ln_g, conv_ln_b, conv_pw, w_out, ln1_g, ln1_b, peer_wq, peer_subkeys, peer_u, peer_v, ln2_g, ln2_b):
    bp, sp, _ = x_prompt.shape
    bs, ss, _ = x_sample.shape
    depth = w_in.shape[0]
    past_len = page_table.shape[1] * cache_cmp_k.shape[2]
    pos_p = jnp.arange(sp, dtype=jnp.int32)
    pos_s = past_len + jnp.arange(ss, dtype=jnp.int32)
    cos_p, sin_p = _rope_tables(pos_p)
    cos_s, sin_s = _rope_tables(jnp.tile(pos_s, bs))
    xp = x_prompt.reshape(bp * sp, D_MODEL)
    xs = x_sample.reshape(bs * ss, D_MODEL)
    p_states, s_states = [], []
    caches = [c.reshape(-1, HEAD_DIM) for c in (cache_cmp_k, cache_cmp_v, cache_slc_k, cache_slc_v)]
    for l in range(depth):
        p = _prep_weights(l, w_in, gm_ln_g, gm_ln_b, gm_ws, gm_bs, cmp_wk, cmp_wv, conv_dw, conv_db, conv_ln_g,
                          conv_ln_b, conv_pw, w_out, ln1_g, ln1_b, peer_wq, peer_subkeys, peer_u, peer_v, ln2_g, ln2_b)
        xp, sp_state = _layer_prompt(xp, p, cos_p, sin_p, batch=bp, seq=sp)
        xs, ss_state = _layer_sample(xs, p, cos_s, sin_s, caches, l, page_table, cache_win_k[l], cache_win_v[l],
                                     state_conv[l], batch=bs, seq=ss, past_len=past_len)
        p_states.append(sp_state)
        s_states.append(ss_state)
    stack = lambda states, i: jnp.stack([st[i] for st in states], axis=0)
    outs = [xp.reshape(bp, sp, D_MODEL), xs.reshape(bs, ss, D_MODEL)]
    outs += [stack(p_states, i) for i in range(8)]
    outs += [stack(s_states, i) for i in range(8)]
    return tuple(outs)
```
